```python
import math
import jax, jax.numpy as jnp
from jax import lax
import numpy as np

D_MODEL = 1024
BATCH = 8
SEQ = 16384
DEPTH = 4

CHUNK = 64
N_MIXERS = 2
N_A = (DEPTH + 1) // 2
N_B = DEPTH // 2
N_HEADS_A = 16
HEAD_DIM_A = D_MODEL // N_HEADS_A
Q_BLOCK = 128
D_RNN = D_MODEL
N_BLOCKS_B = 8
BLOCK_B = D_RNN // N_BLOCKS_B
CONV_B = 4
LRU_C = 8.0
D_FF = 2816
CONV_F = 3
D_PLE = 256
LN_EPS = 1e-5
ALPHA = (2.0 * DEPTH) ** 0.25
BETA = (8.0 * DEPTH) ** -0.25

kernel_name = "fox_rglru_deepnorm_convffn_hybrid"


def layer_norm(x, g, b):
    xf = x.astype(jnp.float32)
    mu = jnp.mean(xf, axis=-1, keepdims=True)
    xc = xf - mu
    var = jnp.mean(xc * xc, axis=-1, keepdims=True)
    y = xc * lax.rsqrt(var + LN_EPS) * g.astype(jnp.float32) + b.astype(jnp.float32)
    return y.astype(x.dtype)


def causal_dwconv(x, w, b):
    k, c = w.shape
    y = lax.conv_general_dilated(
        x, w[:, None, :].astype(x.dtype), window_strides=(1,), padding=[(k - 1, 0)],
        dimension_numbers=("NWC", "WIO", "NWC"), feature_group_count=c)
    return y + b.astype(x.dtype)


def forgetting_attention(x, w_in, b_f, w_out):
    bsz, s, _ = x.shape
    h, dh = N_HEADS_A, HEAD_DIM_A
    proj = x @ w_in
    q, k, v, fg = jnp.split(proj, [D_MODEL, 2 * D_MODEL, 3 * D_MODEL], axis=-1)
    q = q.reshape(bsz, s, h, dh).transpose(0, 2, 1, 3)
    k = k.reshape(bsz, s, h, dh).transpose(0, 2, 1, 3)
    v = v.reshape(bsz, s, h, dh).transpose(0, 2, 1, 3)
    log_f = jax.nn.log_sigmoid((fg + b_f).astype(jnp.float32))
    c = jnp.cumsum(log_f, axis=1).transpose(0, 2, 1)
    n_blk = s // Q_BLOCK
    qb = q.reshape(bsz, h, n_blk, Q_BLOCK, dh).transpose(2, 0, 1, 3, 4)
    cb = c.reshape(bsz, h, n_blk, Q_BLOCK).transpose(2, 0, 1, 3)
    pos_k = jnp.arange(s)
    scale = 1.0 / math.sqrt(dh)

    def one_block(args):
        q_i, c_i, i = args
        logits = jnp.einsum("bhqd,bhkd->bhqk", q_i, k).astype(jnp.float32) * scale
        logits = logits + c_i[..., None] - c[:, :, None, :]
        pos_q = i * Q_BLOCK + jnp.arange(Q_BLOCK)
        mask = pos_k[None, :] <= pos_q[:, None]
        logits = jnp.where(mask, logits, -jnp.inf)
        probs = jax.nn.softmax(logits, axis=-1)
        return jnp.einsum("bhqk,bhkd->bhqd", probs.astype(v.dtype), v)

    o = lax.map(one_block, (qb, cb, jnp.arange(n_blk)))
    o = o.transpose(1, 0, 3, 2, 4).reshape(bsz, s, D_MODEL)
    return o @ w_out


def rglru_block(x, w_in, conv_w, conv_b, w_a, b_a, w_i, b_i, lam, w_out):
    bsz, s, _ = x.shape
    proj = x @ w_in
    xb, gb = jnp.split(proj, [D_RNN], axis=-1)
    xb = causal_dwconv(xb, conv_w, conv_b)
    xh = xb.reshape(bsz, s, N_BLOCKS_B, BLOCK_B)
    r = jax.nn.sigmoid(jnp.einsum("bsnc,ncd->bsnd", xh, w_a) + b_a).reshape(bsz, s, D_RNN)
    ig = jax.nn.sigmoid(jnp.einsum("bsnc,ncd->bsnd", xh, w_i) + b_i).reshape(bsz, s, D_RNN)
    log_a = -LRU_C * r.astype(jnp.float32) * jax.nn.softplus(-lam.astype(jnp.float32))
    a = jnp.exp(log_a)
    mult = jnp.sqrt(-jnp.expm1(2.0 * log_a))
    u = mult * (ig * xb).astype(jnp.float32)

    def combine(left, right):
        a1, b1 = left
        a2, b2 = right
        return a1 * a2, a2 * b1 + b2

    _, hseq = lax.associative_scan(combine, (a, u), axis=1)
    y = hseq.astype(x.dtype) * jax.nn.gelu(gb)
    return y @ w_out


def conv_ffn(x, w_up, conv_w, conv_b, w_down):
    hdn = x @ w_up
    hdn = causal_dwconv(hdn, conv_w, conv_b)
    val, gate = jnp.split(hdn, [D_FF], axis=-1)
    return (jax.nn.gelu(gate) * val) @ w_down


def _fwd_setup_inputs(seed: int = 0) -> dict:
    key = jax.random.key(seed)
    ks = jax.random.split(key, 32)
    f32 = jnp.float32
    nrm = lambda k, shape, sc: jax.random.normal(k, shape, f32) * sc
    x = nrm(ks[0], (BATCH, SEQ, D_MODEL), 1.0)
    p = nrm(ks[1], (DEPTH, BATCH, SEQ, D_PLE), 1.0)
    col_scale = jnp.concatenate([
        jnp.ones((2 * D_MODEL,), f32),
        jnp.full((D_MODEL,), BETA, f32),
        jnp.full((N_HEADS_A,), 0.1, f32)])
    a_w_in = nrm(ks[2], (N_A, D_MODEL, 3 * D_MODEL + N_HEADS_A), D_MODEL ** -0.5) * col_scale
    a_b_f = jnp.linspace(1.0, 6.0, N_HEADS_A, dtype=f32)[None, :] + nrm(ks[3], (N_A, N_HEADS_A), 0.1)
    a_w_out = nrm(ks[4], (N_A, D_MODEL, D_MODEL), D_MODEL ** -0.5 * BETA)
    b_w_in = nrm(ks[5], (N_B, D_MODEL, 2 * D_RNN), D_MODEL ** -0.5)
    b_conv_w = nrm(ks[6], (N_B, CONV_B, D_RNN), CONV_B ** -0.5)
    b_conv_b = nrm(ks[7], (N_B, D_RNN), 0.02)
    b_w_a = nrm(ks[8], (N_B, N_BLOCKS_B, BLOCK_B, BLOCK_B), BLOCK_B ** -0.5)
    b_b_a = nrm(ks[9], (N_B, N_BLOCKS_B, BLOCK_B), 0.02)
    b_w_i = nrm(ks[10], (N_B, N_BLOCKS_B, BLOCK_B, BLOCK_B), BLOCK_B ** -0.5)
    b_b_i = nrm(ks[11], (N_B, N_BLOCKS_B, BLOCK_B), 0.02)
    a_pow_c = jax.random.uniform(ks[12], (N_B, D_RNN), f32, 0.9, 0.999)
    a_base = a_pow_c ** (1.0 / LRU_C)
    b_lam = jnp.log(a_base) - jnp.log1p(-a_base)
    b_w_out = nrm(ks[13], (N_B, D_RNN, D_MODEL), D_RNN ** -0.5 * BETA)
    f_w_up = nrm(ks[14], (DEPTH, D_MODEL, 2 * D_FF), D_MODEL ** -0.5)
    f_conv_w = nrm(ks[15], (DEPTH, CONV_F, 2 * D_FF), CONV_F ** -0.5)
    f_conv_b = nrm(ks[16], (DEPTH, 2 * D_FF), 0.02)
    f_w_down = nrm(ks[17], (DEPTH, D_FF, D_MODEL), D_FF ** -0.5 * BETA)
    ln1_g = 1.0 + nrm(ks[18], (DEPTH, D_MODEL), 0.02)
    ln1_b = nrm(ks[19], (DEPTH, D_MODEL), 0.02)
    ln2_g = 1.0 + nrm(ks[20], (DEPTH, D_MODEL), 0.02)
    ln2_b = nrm(ks[21], (DEPTH, D_MODEL), 0.02)
    ple_w = nrm(ks[22], (DEPTH, D_PLE, D_MODEL), D_PLE ** -0.5 * BETA)
    ple_gate_w = nrm(ks[23], (DEPTH, D_MODEL, D_MODEL), D_MODEL ** -0.5)
    ple_gate_b = nrm(ks[24], (DEPTH, D_MODEL), 0.02)
    return {"x": x, "p": p,
            "a_w_in": a_w_in, "a_b_f": a_b_f, "a_w_out": a_w_out,
            "b_w_in": b_w_in, "b_conv_w": b_conv_w, "b_conv_b": b_conv_b,
            "b_w_a": b_w_a, "b_b_a": b_b_a, "b_w_i": b_w_i, "b_b_i": b_b_i,
            "b_lam": b_lam, "b_w_out": b_w_out,
            "f_w_up": f_w_up, "f_conv_w": f_conv_w, "f_conv_b": f_conv_b, "f_w_down": f_w_down,
            "ln1_g": ln1_g, "ln1_b": ln1_b, "ln2_g": ln2_g, "ln2_b": ln2_b,
            "ple_w": ple_w, "ple_gate_w": ple_gate_w, "ple_gate_b": ple_gate_b}


def _fwd_reference(x, p, a_w_in, a_b_f, a_w_out,
              b_w_in, b_conv_w, b_conv_b, b_w_a, b_b_a, b_w_i, b_b_i, b_lam, b_w_out,
              f_w_up, f_conv_w, f_conv_b, f_w_down,
              ln1_g, ln1_b, ln2_g, ln2_b,
              ple_w, ple_gate_w, ple_gate_b):
    for i in range(DEPTH):
        j = i // N_MIXERS
        if i % N_MIXERS == 0:
            m = forgetting_attention(x, a_w_in[j], a_b_f[j], a_w_out[j])
        else:
            m = rglru_block(x, b_w_in[j], b_conv_w[j], b_conv_b[j], b_w_a[j], b_b_a[j],
                            b_w_i[j], b_b_i[j], b_lam[j], b_w_out[j])
        x = layer_norm(ALPHA * x + m, ln1_g[i], ln1_b[i])
        ff = conv_ffn(x, f_w_up[i], f_conv_w[i], f_conv_b[i], f_w_down[i])
        x = layer_norm(ALPHA * x + ff, ln2_g[i], ln2_b[i])
        gate = jax.nn.sigmoid(x @ ple_gate_w[i] + ple_gate_b[i])
        x = x + gate * (p[i] @ ple_w[i])
    return x


import jax as _jax
import jax.numpy as _jnp

TWIN_FORMAT = 'train_step'
FWD_PARAMS = ['x', 'p', 'a_w_in', 'a_b_f', 'a_w_out', 'b_w_in', 'b_conv_w', 'b_conv_b', 'b_w_a', 'b_b_a', 'b_w_i', 'b_b_i', 'b_lam', 'b_w_out', 'f_w_up', 'f_conv_w', 'f_conv_b', 'f_w_down', 'ln1_g', 'ln1_b', 'ln2_g', 'ln2_b', 'ple_w', 'ple_gate_w', 'ple_gate_b']
TWIN_WEIGHTS = ['a_w_in', 'a_b_f', 'a_w_out', 'b_w_in', 'b_conv_w', 'b_conv_b', 'b_w_a', 'b_b_a', 'b_w_i', 'b_b_i', 'b_lam', 'b_w_out', 'f_w_up', 'f_conv_w', 'f_conv_b', 'f_w_down', 'ln1_g', 'ln1_b', 'ln2_g', 'ln2_b', 'ple_w', 'ple_gate_w', 'ple_gate_b']
TWIN_DIFF_INPUT = 'x'
TWIN_INPUTS = ['x', 'p', 'a_w_in', 'a_b_f', 'a_w_out', 'b_w_in', 'b_conv_w', 'b_conv_b', 'b_w_a', 'b_b_a', 'b_w_i', 'b_b_i', 'b_lam', 'b_w_out', 'f_w_up', 'f_conv_w', 'f_conv_b', 'f_w_down', 'ln1_g', 'ln1_b', 'ln2_g', 'ln2_b', 'ple_w', 'ple_gate_w', 'ple_gate_b', 'loss_target', 'm_a_w_in', 'm_a_b_f', 'm_a_w_out', 'm_b_w_in', 'm_b_conv_w', 'm_b_conv_b', 'm_b_w_a', 'm_b_b_a', 'm_b_w_i', 'm_b_b_i', 'm_b_lam', 'm_b_w_out', 'm_f_w_up', 'm_f_conv_w', 'm_f_conv_b', 'm_f_w_down', 'm_ln1_g', 'm_ln1_b', 'm_ln2_g', 'm_ln2_b', 'm_ple_w', 'm_ple_gate_w', 'm_ple_gate_b', 'v_a_w_in', 'v_a_b_f', 'v_a_w_out', 'v_b_w_in', 'v_b_conv_w', 'v_b_conv_b', 'v_b_w_a', 'v_b_b_a', 'v_b_w_i', 'v_b_b_i', 'v_b_lam', 'v_b_w_out', 'v_f_w_up', 'v_f_conv_w', 'v_f_conv_b', 'v_f_w_down', 'v_ln1_g', 'v_ln1_b', 'v_ln2_g', 'v_ln2_b', 'v_ple_w', 'v_ple_gate_w', 'v_ple_gate_b']
TWIN_OUTPUTS = ['loss', 'grad_x', 'grad_a_w_in', 'grad_a_b_f', 'grad_a_w_out', 'grad_b_w_in', 'grad_b_conv_w', 'grad_b_conv_b', 'grad_b_w_a', 'grad_b_b_a', 'grad_b_w_i', 'grad_b_b_i', 'grad_b_lam', 'grad_b_w_out', 'grad_f_w_up', 'grad_f_conv_w', 'grad_f_conv_b', 'grad_f_w_down', 'grad_ln1_g', 'grad_ln1_b', 'grad_ln2_g', 'grad_ln2_b', 'grad_ple_w', 'grad_ple_gate_w', 'grad_ple_gate_b', 'delta_a_w_in', 'delta_a_b_f', 'delta_a_w_out', 'delta_b_w_in', 'delta_b_conv_w', 'delta_b_conv_b', 'delta_b_w_a', 'delta_b_b_a', 'delta_b_w_i', 'delta_b_b_i', 'delta_b_lam', 'delta_b_w_out', 'delta_f_w_up', 'delta_f_conv_w', 'delta_f_conv_b', 'delta_f_w_down', 'delta_ln1_g', 'delta_ln1_b', 'delta_ln2_g', 'delta_ln2_b', 'delta_ple_w', 'delta_ple_gate_w', 'delta_ple_gate_b', 'new_m_a_w_in', 'new_m_a_b_f', 'new_m_a_w_out', 'new_m_b_w_in', 'new_m_b_conv_w', 'new_m_b_conv_b', 'new_m_b_w_a', 'new_m_b_b_a', 'new_m_b_w_i', 'new_m_b_b_i', 'new_m_b_lam', 'new_m_b_w_out', 'new_m_f_w_up', 'new_m_f_conv_w', 'new_m_f_conv_b', 'new_m_f_w_down', 'new_m_ln1_g', 'new_m_ln1_b', 'new_m_ln2_g', 'new_m_ln2_b', 'new_m_ple_w', 'new_m_ple_gate_w', 'new_m_ple_gate_b', 'new_v_a_w_in', 'new_v_a_b_f', 'new_v_a_w_out', 'new_v_b_w_in', 'new_v_b_conv_w', 'new_v_b_conv_b', 'new_v_b_w_a', 'new_v_b_b_a', 'new_v_b_w_i', 'new_v_b_b_i', 'new_v_b_lam', 'new_v_b_w_out', 'new_v_f_w_up', 'new_v_f_conv_w', 'new_v_f_conv_b', 'new_v_f_w_down', 'new_v_ln1_g', 'new_v_ln1_b', 'new_v_ln2_g', 'new_v_ln2_b', 'new_v_ple_w', 'new_v_ple_gate_w', 'new_v_ple_gate_b']
TWIN_LEAF_KINDS = {'loss': 'loss', 'grad_x': 'grad_x', 'grad_a_w_in': 'grad_w', 'grad_a_b_f': 'grad_w', 'grad_a_w_out': 'grad_w', 'grad_b_w_in': 'grad_w', 'grad_b_conv_w': 'grad_w', 'grad_b_conv_b': 'grad_w', 'grad_b_w_a': 'grad_w', 'grad_b_b_a': 'grad_w', 'grad_b_w_i': 'grad_w', 'grad_b_b_i': 'grad_w', 'grad_b_lam': 'grad_w', 'grad_b_w_out': 'grad_w', 'grad_f_w_up': 'grad_w', 'grad_f_conv_w': 'grad_w', 'grad_f_conv_b': 'grad_w', 'grad_f_w_down': 'grad_w', 'grad_ln1_g': 'grad_w', 'grad_ln1_b': 'grad_w', 'grad_ln2_g': 'grad_w', 'grad_ln2_b': 'grad_w', 'grad_ple_w': 'grad_w', 'grad_ple_gate_w': 'grad_w', 'grad_ple_gate_b': 'grad_w', 'delta_a_w_in': 'delta_w', 'delta_a_b_f': 'delta_w', 'delta_a_w_out': 'delta_w', 'delta_b_w_in': 'delta_w', 'delta_b_conv_w': 'delta_w', 'delta_b_conv_b': 'delta_w', 'delta_b_w_a': 'delta_w', 'delta_b_b_a': 'delta_w', 'delta_b_w_i': 'delta_w', 'delta_b_b_i': 'delta_w', 'delta_b_lam': 'delta_w', 'delta_b_w_out': 'delta_w', 'delta_f_w_up': 'delta_w', 'delta_f_conv_w': 'delta_w', 'delta_f_conv_b': 'delta_w', 'delta_f_w_down': 'delta_w', 'delta_ln1_g': 'delta_w', 'delta_ln1_b': 'delta_w', 'delta_ln2_g': 'delta_w', 'delta_ln2_b': 'delta_w', 'delta_ple_w': 'delta_w', 'delta_ple_gate_w': 'delta_w', 'delta_ple_gate_b': 'delta_w', 'new_m_a_w_in': 'new_m', 'new_m_a_b_f': 'new_m', 'new_m_a_w_out': 'new_m', 'new_m_b_w_in': 'new_m', 'new_m_b_conv_w': 'new_m', 'new_m_b_conv_b': 'new_m', 'new_m_b_w_a': 'new_m', 'new_m_b_b_a': 'new_m', 'new_m_b_w_i': 'new_m', 'new_m_b_b_i': 'new_m', 'new_m_b_lam': 'new_m', 'new_m_b_w_out': 'new_m', 'new_m_f_w_up': 'new_m', 'new_m_f_conv_w': 'new_m', 'new_m_f_conv_b': 'new_m', 'new_m_f_w_down': 'new_m', 'new_m_ln1_g': 'new_m', 'new_m_ln1_b': 'new_m', 'new_m_ln2_g': 'new_m', 'new_m_ln2_b': 'new_m', 'new_m_ple_w': 'new_m', 'new_m_ple_gate_w': 'new_m', 'new_m_ple_gate_b': 'new_m', 'new_v_a_w_in': 'new_v', 'new_v_a_b_f': 'new_v', 'new_v_a_w_out': 'new_v', 'new_v_b_w_in': 'new_v', 'new_v_b_conv_w': 'new_v', 'new_v_b_conv_b': 'new_v', 'new_v_b_w_a': 'new_v', 'new_v_b_b_a': 'new_v', 'new_v_b_w_i': 'new_v', 'new_v_b_b_i': 'new_v', 'new_v_b_lam': 'new_v', 'new_v_b_w_out': 'new_v', 'new_v_f_w_up': 'new_v', 'new_v_f_conv_w': 'new_v', 'new_v_f_conv_b': 'new_v', 'new_v_f_w_down': 'new_v', 'new_v_ln1_g': 'new_v', 'new_v_ln1_b': 'new_v', 'new_v_ln2_g': 'new_v', 'new_v_ln2_b': 'new_v', 'new_v_ple_w': 'new_v', 'new_v_ple_gate_w': 'new_v', 'new_v_ple_gate_b': 'new_v'}


def _forward(args):
    return _fwd_reference(*[args[k] for k in FWD_PARAMS])


def _output_shape():
    def fwd():
        inp = _fwd_setup_inputs(0)
        return _fwd_reference(*[inp[k] for k in FWD_PARAMS])
    out = _jax.eval_shape(fwd)
    return out.shape, out.dtype

N_MICROBATCH = 1
ADAM_LR = 0.001
ADAM_B1 = 0.9
ADAM_B2 = 0.999
ADAM_EPS = 1e-08
ADAM_WD = 0.01
ADAM_STEP = 10
PER_EXAMPLE_BATCH_AXIS = {'x': 0, 'p': 1, 'loss_target': 0}
SHARED_INPUTS = []
_WEIGHT_DTYPES = {'a_w_in': _jnp.float32, 'a_b_f': _jnp.float32, 'a_w_out': _jnp.float32, 'b_w_in': _jnp.float32, 'b_conv_w': _jnp.float32, 'b_conv_b': _jnp.float32, 'b_w_a': _jnp.float32, 'b_b_a': _jnp.float32, 'b_w_i': _jnp.float32, 'b_b_i': _jnp.float32, 'b_lam': _jnp.float32, 'b_w_out': _jnp.float32, 'f_w_up': _jnp.float32, 'f_conv_w': _jnp.float32, 'f_conv_b': _jnp.float32, 'f_w_down': _jnp.float32, 'ln1_g': _jnp.float32, 'ln1_b': _jnp.float32, 'ln2_g': _jnp.float32, 'ln2_b': _jnp.float32, 'ple_w': _jnp.float32, 'ple_gate_w': _jnp.float32, 'ple_gate_b': _jnp.float32}
MOMENT_SCALE = {'a_w_in': 1.902662e-02, 'a_b_f': 4.033810e-02, 'a_w_out': 3.004395e-02, 'b_w_in': 5.306107e-02, 'b_conv_w': 5.838722e-02, 'b_conv_b': 1.291743e+00, 'b_w_a': 2.673838e-02, 'b_b_a': 2.002526e-02, 'b_w_i': 4.952980e-02, 'b_b_i': 1.962117e-02, 'b_lam': 3.361665e-02, 'b_w_out': 1.746031e-01, 'f_w_up': 3.617982e-02, 'f_conv_w': 3.632833e-02, 'f_conv_b': 6.399600e-02, 'f_w_down': 1.408686e-01, 'ln1_g': 3.739433e+00, 'ln1_b': 2.599995e+00, 'ln2_g': 6.458220e+01, 'ln2_b': 4.486699e+00, 'ple_w': 3.067125e-01, 'ple_gate_w': 4.885048e-02, 'ple_gate_b': 1.128461e+00}


def _to_microbatches(a, axis):
    t = _jnp.moveaxis(a, axis, 0)
    t = t.reshape((N_MICROBATCH, t.shape[0] // N_MICROBATCH) + t.shape[1:])
    return _jnp.moveaxis(t, 1, axis + 1)


def setup_inputs(seed: int = 0) -> dict:
    inp = _fwd_setup_inputs(seed)
    key = _jax.random.fold_in(_jax.random.key(seed), 7919)
    shape, _ = _output_shape()
    out = dict(inp)
    out["loss_target"] = _jax.random.normal(_jax.random.fold_in(key, 0), shape, _jnp.float32)
    for i, name in enumerate(TWIN_WEIGHTS):
        w = inp[name].astype(_jnp.float32)
        if MOMENT_SCALE is None:
            s = _jnp.sqrt(_jnp.mean(_jnp.square(w)) + 1e-30)
        else:
            s = MOMENT_SCALE[name]
        km, kv = _jax.random.split(_jax.random.fold_in(key, i + 1))
        out[name] = w
        out["m_" + name] = s * _jax.random.normal(km, w.shape, _jnp.float32)
        out["v_" + name] = (s * s) * _jax.random.uniform(kv, w.shape, _jnp.float32, 0.5, 1.5)
    if N_MICROBATCH > 1:
        for name, axis in PER_EXAMPLE_BATCH_AXIS.items():
            out[name] = _to_microbatches(out[name], axis)
    return {'x': out['x'], 'p': out['p'], 'a_w_in': out['a_w_in'], 'a_b_f': out['a_b_f'], 'a_w_out': out['a_w_out'], 'b_w_in': out['b_w_in'], 'b_conv_w': out['b_conv_w'], 'b_conv_b': out['b_conv_b'], 'b_w_a': out['b_w_a'], 'b_b_a': out['b_b_a'], 'b_w_i': out['b_w_i'], 'b_b_i': out['b_b_i'], 'b_lam': out['b_lam'], 'b_w_out': out['b_w_out'], 'f_w_up': out['f_w_up'], 'f_conv_w': out['f_conv_w'], 'f_conv_b': out['f_conv_b'], 'f_w_down': out['f_w_down'], 'ln1_g': out['ln1_g'], 'ln1_b': out['ln1_b'], 'ln2_g': out['ln2_g'], 'ln2_b': out['ln2_b'], 'ple_w': out['ple_w'], 'ple_gate_w': out['ple_gate_w'], 'ple_gate_b': out['ple_gate_b'], 'loss_target': out['loss_target'], 'm_a_w_in': out['m_a_w_in'], 'm_a_b_f': out['m_a_b_f'], 'm_a_w_out': out['m_a_w_out'], 'm_b_w_in': out['m_b_w_in'], 'm_b_conv_w': out['m_b_conv_w'], 'm_b_conv_b': out['m_b_conv_b'], 'm_b_w_a': out['m_b_w_a'], 'm_b_b_a': out['m_b_b_a'], 'm_b_w_i': out['m_b_w_i'], 'm_b_b_i': out['m_b_b_i'], 'm_b_lam': out['m_b_lam'], 'm_b_w_out': out['m_b_w_out'], 'm_f_w_up': out['m_f_w_up'], 'm_f_conv_w': out['m_f_conv_w'], 'm_f_conv_b': out['m_f_conv_b'], 'm_f_w_down': out['m_f_w_down'], 'm_ln1_g': out['m_ln1_g'], 'm_ln1_b': out['m_ln1_b'], 'm_ln2_g': out['m_ln2_g'], 'm_ln2_b': out['m_ln2_b'], 'm_ple_w': out['m_ple_w'], 'm_ple_gate_w': out['m_ple_gate_w'], 'm_ple_gate_b': out['m_ple_gate_b'], 'v_a_w_in': out['v_a_w_in'], 'v_a_b_f': out['v_a_b_f'], 'v_a_w_out': out['v_a_w_out'], 'v_b_w_in': out['v_b_w_in'], 'v_b_conv_w': out['v_b_conv_w'], 'v_b_conv_b': out['v_b_conv_b'], 'v_b_w_a': out['v_b_w_a'], 'v_b_b_a': out['v_b_b_a'], 'v_b_w_i': out['v_b_w_i'], 'v_b_b_i': out['v_b_b_i'], 'v_b_lam': out['v_b_lam'], 'v_b_w_out': out['v_b_w_out'], 'v_f_w_up': out['v_f_w_up'], 'v_f_conv_w': out['v_f_conv_w'], 'v_f_conv_b': out['v_f_conv_b'], 'v_f_w_down': out['v_f_w_down'], 'v_ln1_g': out['v_ln1_g'], 'v_ln1_b': out['v_ln1_b'], 'v_ln2_g': out['v_ln2_g'], 'v_ln2_b': out['v_ln2_b'], 'v_ple_w': out['v_ple_w'], 'v_ple_gate_w': out['v_ple_gate_w'], 'v_ple_gate_b': out['v_ple_gate_b']}


def _loss(weights, diff, rest, loss_target):
    with _jax.named_scope("forward"):
        args = {**rest, TWIN_DIFF_INPUT: diff, **{k: w.astype(_WEIGHT_DTYPES[k]) for k, w in weights.items()}}
        y = _forward(args)
    with _jax.named_scope("loss_head"):
        err = _jnp.square(y.astype(_jnp.float32) - loss_target)
        return 0.5 * _jnp.sum(_jnp.mean(err, axis=-1)) if err.ndim else 0.5 * err


def _adamw(w, g, m, v):
    m = ADAM_B1 * m + (1.0 - ADAM_B1) * g
    v = ADAM_B2 * v + (1.0 - ADAM_B2) * _jnp.square(g)
    m_hat = m / (1.0 - ADAM_B1 ** ADAM_STEP)
    v_hat = v / (1.0 - ADAM_B2 ** ADAM_STEP)
    delta = -ADAM_LR * (m_hat / (_jnp.sqrt(v_hat) + ADAM_EPS) + ADAM_WD * w)
    return delta, m, v


def reference(x, p, a_w_in, a_b_f, a_w_out, b_w_in, b_conv_w, b_conv_b, b_w_a, b_b_a, b_w_i, b_b_i, b_lam, b_w_out, f_w_up, f_conv_w, f_conv_b, f_w_down, ln1_g, ln1_b, ln2_g, ln2_b, ple_w, ple_gate_w, ple_gate_b, loss_target, m_a_w_in, m_a_b_f, m_a_w_out, m_b_w_in, m_b_conv_w, m_b_conv_b, m_b_w_a, m_b_b_a, m_b_w_i, m_b_b_i, m_b_lam, m_b_w_out, m_f_w_up, m_f_conv_w, m_f_conv_b, m_f_w_down, m_ln1_g, m_ln1_b, m_ln2_g, m_ln2_b, m_ple_w, m_ple_gate_w, m_ple_gate_b, v_a_w_in, v_a_b_f, v_a_w_out, v_b_w_in, v_b_conv_w, v_b_conv_b, v_b_w_a, v_b_b_a, v_b_w_i, v_b_b_i, v_b_lam, v_b_w_out, v_f_w_up, v_f_conv_w, v_f_conv_b, v_f_w_down, v_ln1_g, v_ln1_b, v_ln2_g, v_ln2_b, v_ple_w, v_ple_gate_w, v_ple_gate_b):
    given = dict(x=x, p=p, a_w_in=a_w_in, a_b_f=a_b_f, a_w_out=a_w_out, b_w_in=b_w_in, b_conv_w=b_conv_w, b_conv_b=b_conv_b, b_w_a=b_w_a, b_b_a=b_b_a, b_w_i=b_w_i, b_b_i=b_b_i, b_lam=b_lam, b_w_out=b_w_out, f_w_up=f_w_up, f_conv_w=f_conv_w, f_conv_b=f_conv_b, f_w_down=f_w_down, ln1_g=ln1_g, ln1_b=ln1_b, ln2_g=ln2_g, ln2_b=ln2_b, ple_w=ple_w, ple_gate_w=ple_gate_w, ple_gate_b=ple_gate_b, loss_target=loss_target, m_a_w_in=m_a_w_in, m_a_b_f=m_a_b_f, m_a_w_out=m_a_w_out, m_b_w_in=m_b_w_in, m_b_conv_w=m_b_conv_w, m_b_conv_b=m_b_conv_b, m_b_w_a=m_b_w_a, m_b_b_a=m_b_b_a, m_b_w_i=m_b_w_i, m_b_b_i=m_b_b_i, m_b_lam=m_b_lam, m_b_w_out=m_b_w_out, m_f_w_up=m_f_w_up, m_f_conv_w=m_f_conv_w, m_f_conv_b=m_f_conv_b, m_f_w_down=m_f_w_down, m_ln1_g=m_ln1_g, m_ln1_b=m_ln1_b, m_ln2_g=m_ln2_g, m_ln2_b=m_ln2_b, m_ple_w=m_ple_w, m_ple_gate_w=m_ple_gate_w, m_ple_gate_b=m_ple_gate_b, v_a_w_in=v_a_w_in, v_a_b_f=v_a_b_f, v_a_w_out=v_a_w_out, v_b_w_in=v_b_w_in, v_b_conv_w=v_b_conv_w, v_b_conv_b=v_b_conv_b, v_b_w_a=v_b_w_a, v_b_b_a=v_b_b_a, v_b_w_i=v_b_w_i, v_b_b_i=v_b_b_i, v_b_lam=v_b_lam, v_b_w_out=v_b_w_out, v_f_w_up=v_f_w_up, v_f_conv_w=v_f_conv_w, v_f_conv_b=v_f_conv_b, v_f_w_down=v_f_w_down, v_ln1_g=v_ln1_g, v_ln1_b=v_ln1_b, v_ln2_g=v_ln2_g, v_ln2_b=v_ln2_b, v_ple_w=v_ple_w, v_ple_gate_w=v_ple_gate_w, v_ple_gate_b=v_ple_gate_b)
    weights = {n: given[n] for n in TWIN_WEIGHTS}
    shared = {n: given[n] for n in SHARED_INPUTS}
    per_example = {n: given[n] for n in ['x', 'p']}
    grad_fn = _jax.value_and_grad(_loss, argnums=(0, 1))

    def one_microbatch(ex, loss_target):
        ex = dict(ex)
        diff = ex.pop(TWIN_DIFF_INPUT)
        return grad_fn(weights, diff, {**shared, **ex}, loss_target)

    if N_MICROBATCH == 1:
        loss, (grad_w, grad_x) = one_microbatch(per_example, given["loss_target"])
    else:
        def body(carry, xs):
            loss_sum, grad_sum = carry
            l_k, (gw_k, gx_k) = one_microbatch(xs[0], xs[1])
            with _jax.named_scope("update"):
                return (loss_sum + l_k, _jax.tree.map(_jnp.add, grad_sum, gw_k)), gx_k

        init = (_jnp.zeros((), _jnp.float32), _jax.tree.map(_jnp.zeros_like, weights))
        (loss, grad_w), grad_x = _jax.lax.scan(body, init, (per_example, given["loss_target"]))
    with _jax.named_scope("update"):
        delta_w, new_m, new_v = {}, {}, {}
        for n in TWIN_WEIGHTS:
            delta_w[n], new_m[n], new_v[n] = _adamw(weights[n], grad_w[n], given["m_" + n], given["v_" + n])
    return (loss, grad_x, *[grad_w[n] for n in TWIN_WEIGHTS], *[delta_w[n] for n in TWIN_WEIGHTS],
            *[new_m[n] for n in TWIN_WEIGHTS], *[new_v[n] for n in TWIN_WEIGHTS])
```

```python
import functools
import math

import jax
import jax.numpy as jnp
from jax import lax
from jax.experimental import pallas as pl
from jax.experimental.pallas import tpu as pltpu

F32 = jnp.float32
BF16 = jnp.bfloat16

D_MODEL = 1024
DEPTH = 4
N_HEADS = 16
HEAD_DIM = 64
N_PAIRS = N_HEADS // 2
N_BLOCKS_B = 8
BLOCK_B = 128
CONV_B = 4
LRU_C = 8.0
D_FF = 2816
CONV_F = 3
D_PLE = 256
LN_EPS = 1e-5
ALPHA = (2.0 * DEPTH) ** 0.25
ADAM_LR, ADAM_B1, ADAM_B2, ADAM_EPS, ADAM_WD, ADAM_STEP = 0.001, 0.9, 0.999, 1e-08, 0.01, 10
N_DEV = 8
MESH_AXES = ("x", "y", "c")

LANES = 128
SUBLANES = 8
VMEM_LIMIT_BYTES = 56 * 1024 * 1024
ATTN_BLOCK = 256
ROW_TILE = 256
NEG_BIG = -1e30


def _cparams(*sem):
    return pltpu.CompilerParams(dimension_semantics=sem if sem else None, vmem_limit_bytes=VMEM_LIMIT_BYTES)


def _pick(n, pref):
    if n <= pref:
        return n
    best = None
    for t in range(LANES, pref + 1, LANES):
        if n % t == 0:
            best = t
    assert best is not None, (n, pref)
    return best


def _sigmoid(x):
    return 1.0 / (1.0 + jnp.exp(-x))


def _log1p(x):
    u = 1.0 + x
    d = u - 1.0
    return jnp.where(d == 0.0, x, jnp.log(u) * (x / jnp.where(d == 0.0, 1.0, d)))


def _expm1(x):
    u = jnp.exp(x)
    lu = jnp.log(u)
    return jnp.where(u == 1.0, x, (u - 1.0) * (x / jnp.where(u == 1.0, 1.0, lu)))


def _softplus(x):
    return jnp.maximum(x, 0.0) + _log1p(jnp.exp(-jnp.abs(x)))


_GELU_C = math.sqrt(2.0 / math.pi)


def _gelu(x):
    return 0.5 * x * (1.0 + jnp.tanh(_GELU_C * (x + 0.044715 * x * x * x)))


def _gelu_and_grad(x):
    t = jnp.tanh(_GELU_C * (x + 0.044715 * x * x * x))
    du = _GELU_C * (1.0 + 3.0 * 0.044715 * x * x)
    return 0.5 * x * (1.0 + t), 0.5 * (1.0 + t) + 0.5 * x * (1.0 - t * t) * du


_DOT_DIMS = {"nn": (((1,), (0,)), ((), ())), "nt": (((1,), (1,)), ((), ())), "tn": (((0,), (0,)), ((), ()))}


def _mm(a, b, mode, out_dtype, name, add=None, add_scale=1.0, tm=512, tn=1408, tk=1408):
    if mode == "nn":
        (m, k), (k2, n) = a.shape, b.shape
    elif mode == "nt":
        (m, k), (n, k2) = a.shape, b.shape
    else:
        (k, m), (k2, n) = a.shape, b.shape
    assert k == k2 and a.dtype == BF16 and b.dtype == BF16, (a.shape, b.shape, a.dtype, b.dtype)
    tm, tn, tk = _pick(m, tm), _pick(n, tn), _pick(k, tk)
    nk = k // tk
    dims = _DOT_DIMS[mode]

    def body(*refs):
        if add is None:
            a_ref, b_ref, o_ref, acc_ref = refs
        else:
            a_ref, b_ref, add_ref, o_ref, acc_ref = refs
        kk = pl.program_id(2)

        @pl.when(kk == 0)
        def _():
            acc_ref[...] = jnp.zeros_like(acc_ref)

        acc_ref[...] += lax.dot_general(a_ref[...], b_ref[...], dims, preferred_element_type=F32)

        @pl.when(kk == nk - 1)
        def _():
            r = acc_ref[...]
            if add is not None:
                r = r + add_scale * add_ref[...]
            o_ref[...] = r.astype(out_dtype)

    a_spec = (pl.BlockSpec((tk, tm), lambda j, i, kk: (kk, i)) if mode == "tn"
              else pl.BlockSpec((tm, tk), lambda j, i, kk: (i, kk)))
    b_spec = (pl.BlockSpec((tn, tk), lambda j, i, kk: (j, kk)) if mode == "nt"
              else pl.BlockSpec((tk, tn), lambda j, i, kk: (kk, j)))
    o_spec = pl.BlockSpec((tm, tn), lambda j, i, kk: (i, j))
    in_specs, args = [a_spec, b_spec], [a, b]
    if add is not None:
        assert add.shape == (m, n) and add.dtype == F32
        in_specs.append(o_spec)
        args.append(add)
    return pl.pallas_call(
        body, name=name, grid=(n // tn, m // tm, nk),
        in_specs=in_specs, out_specs=o_spec,
        out_shape=jax.ShapeDtypeStruct((m, n), out_dtype),
        scratch_shapes=[pltpu.VMEM((tm, tn), F32)],
        compiler_params=_cparams("parallel", "parallel", "arbitrary"),
    )(*args)


def _ln_fwd(x, m, g, b, name):
    t, d = x.shape
    bt = _pick(t, ROW_TILE)

    def body(x_ref, m_ref, g_ref, b_ref, y_ref, yb_ref, z_ref):
        z = ALPHA * x_ref[...] + m_ref[...]
        mu = jnp.mean(z, axis=-1, keepdims=True)
        zc = z - mu
        var = jnp.mean(zc * zc, axis=-1, keepdims=True)
        y = zc * lax.rsqrt(var + LN_EPS) * g_ref[...] + b_ref[...]
        y_ref[...] = y
        yb_ref[...] = y.astype(BF16)
        z_ref[...] = z

    row = pl.BlockSpec((bt, d), lambda i: (i, 0))
    vec = pl.BlockSpec((1, d), lambda i: (0, 0))
    return pl.pallas_call(
        body, name=name, grid=(t // bt,), in_specs=[row, row, vec, vec], out_specs=[row, row, row],
        out_shape=[jax.ShapeDtypeStruct((t, d), F32), jax.ShapeDtypeStruct((t, d), BF16),
                   jax.ShapeDtypeStruct((t, d), F32)],
        compiler_params=_cparams("parallel"),
    )(x, m, g.reshape(1, d), b.reshape(1, d))


def _ln_bwd(dy, z, g, name):
    t, d = dy.shape
    bt = _pick(t, ROW_TILE)

    def body(dy_ref, z_ref, g_ref, dz_ref, dzb_ref, dg_ref, db_ref):
        @pl.when(pl.program_id(0) == 0)
        def _():
            dg_ref[...] = jnp.zeros_like(dg_ref)
            db_ref[...] = jnp.zeros_like(db_ref)

        z = z_ref[...]
        dyv = dy_ref[...]
        mu = jnp.mean(z, axis=-1, keepdims=True)
        zc = z - mu
        var = jnp.mean(zc * zc, axis=-1, keepdims=True)
        rstd = lax.rsqrt(var + LN_EPS)
        xhat = zc * rstd
        dxh = dyv * g_ref[...]
        m1 = jnp.mean(dxh, axis=-1, keepdims=True)
        m2 = jnp.mean(dxh * xhat, axis=-1, keepdims=True)
        dz = rstd * (dxh - m1 - xhat * m2)
        dz_ref[...] = dz
        dzb_ref[...] = dz.astype(BF16)
        dg_ref[...] += jnp.sum(dyv * xhat, axis=0, keepdims=True)
        db_ref[...] += jnp.sum(dyv, axis=0, keepdims=True)

    row = pl.BlockSpec((bt, d), lambda i: (i, 0))
    vec = pl.BlockSpec((1, d), lambda i: (0, 0))
    return pl.pallas_call(
        body, name=name, grid=(t // bt,), in_specs=[row, row, vec], out_specs=[row, row, vec, vec],
        out_shape=[jax.ShapeDtypeStruct((t, d), F32), jax.ShapeDtypeStruct((t, d), BF16),
                   jax.ShapeDtypeStruct((1, d), F32), jax.ShapeDtypeStruct((1, d), F32)],
        compiler_params=_cparams("arbitrary"),
    )(dy, z, g.reshape(1, d))


def _rows_iota(shape):
    return lax.broadcasted_iota(jnp.int32, shape, 0)


def _shift_down(x, halo, s):
    if s == 0:
        return x
    rolled = pltpu.roll(x, s, axis=0)
    first = jnp.where(_rows_iota((SUBLANES, x.shape[1])) < s, pltpu.roll(halo, s, axis=0), rolled[:SUBLANES])
    return jnp.concatenate([first, rolled[SUBLANES:]], axis=0)


def _shift_up(x, halo, s):
    if s == 0:
        return x
    n = x.shape[0]
    rolled = pltpu.roll(x, n - s, axis=0)
    last = jnp.where(_rows_iota((SUBLANES, x.shape[1])) < SUBLANES - s, rolled[n - SUBLANES:],
                     pltpu.roll(halo, SUBLANES - s, axis=0))
    return jnp.concatenate([rolled[:n - SUBLANES], last], axis=0)


def _prev_halo_spec(bt, cb, col_off=0):
    r = bt // SUBLANES
    return pl.BlockSpec((SUBLANES, cb), lambda i, j: (jnp.maximum(i * r - 1, 0), j + col_off))


def _next_halo_spec(bt, cb, t, col_off=0):
    r = bt // SUBLANES
    last = t // SUBLANES - 1
    return pl.BlockSpec((SUBLANES, cb), lambda i, j: (jnp.minimum((i + 1) * r, last), j + col_off))


def _causal_conv(x, halo, w_ref, ksize):
    acc = None
    for j in range(ksize):
        term = w_ref[j:j + 1, :] * _shift_down(x, halo, ksize - 1 - j)
        acc = term if acc is None else acc + term
    return acc


def _anticausal_conv(y, halo, w_ref, ksize):
    acc = None
    for j in range(ksize):
        term = w_ref[j:j + 1, :] * _shift_up(y, halo, ksize - 1 - j)
        acc = term if acc is None else acc + term
    return acc


def _ffn_mid_fwd(h, cw, cb_, name):
    t = h.shape[0]
    bt, cb = _pick(t, ROW_TILE), _pick(D_FF, 1408)
    nc = D_FF // cb

    def body(hv_ref, hvh_ref, hg_ref, hgh_ref, wv_ref, wg_ref, bv_ref, bg_ref, y_ref):
        keep = (pl.program_id(0) > 0).astype(F32)
        val = _causal_conv(hv_ref[...], hvh_ref[...] * keep, wv_ref, CONV_F) + bv_ref[...]
        gate = _causal_conv(hg_ref[...], hgh_ref[...] * keep, wg_ref, CONV_F) + bg_ref[...]
        y_ref[...] = (_gelu(gate) * val).astype(BF16)

    tile_v = pl.BlockSpec((bt, cb), lambda i, j: (i, j))
    tile_g = pl.BlockSpec((bt, cb), lambda i, j: (i, j + nc))
    wv = pl.BlockSpec((CONV_F, cb), lambda i, j: (0, j))
    wg = pl.BlockSpec((CONV_F, cb), lambda i, j: (0, j + nc))
    bv = pl.BlockSpec((1, cb), lambda i, j: (0, j))
    bg = pl.BlockSpec((1, cb), lambda i, j: (0, j + nc))
    cb2 = cb_.reshape(1, 2 * D_FF)
    return pl.pallas_call(
        body, name=name, grid=(t // bt, nc),
        in_specs=[tile_v, _prev_halo_spec(bt, cb), tile_g, _prev_halo_spec(bt, cb, nc), wv, wg, bv, bg],
        out_specs=tile_v, out_shape=jax.ShapeDtypeStruct((t, D_FF), BF16),
        compiler_params=_cparams("parallel", "parallel"),
    )(h, h, h, h, cw, cw, cb2, cb2)


def _ffn_mid_bwd_a(h, dy, cw, cb_, name):
    t = h.shape[0]
    bt, cb = _pick(t, ROW_TILE), _pick(D_FF, 1408)
    nc = D_FF // cb

    def body(hv_ref, hvh_ref, hg_ref, hgh_ref, dy_ref, wv_ref, wg_ref, bv_ref, bg_ref,
             dv_ref, dg_ref, dwv_ref, dwg_ref, dbv_ref, dbg_ref):
        @pl.when(pl.program_id(1) == 0)
        def _():
            for r in (dwv_ref, dwg_ref, dbv_ref, dbg_ref):
                r[...] = jnp.zeros_like(r)

        keep = (pl.program_id(1) > 0).astype(F32)
        hv, hvh = hv_ref[...], hvh_ref[...] * keep
        hg, hgh = hg_ref[...], hgh_ref[...] * keep
        val = _causal_conv(hv, hvh, wv_ref, CONV_F) + bv_ref[...]
        gate = _causal_conv(hg, hgh, wg_ref, CONV_F) + bg_ref[...]
        gl, glg = _gelu_and_grad(gate)
        dyv = dy_ref[...]
        dval = dyv * gl
        dgate = dyv * val * glg
        dv_ref[...] = dval
        dg_ref[...] = dgate
        dbv_ref[...] += jnp.sum(dval, axis=0, keepdims=True)
        dbg_ref[...] += jnp.sum(dgate, axis=0, keepdims=True)
        for j in range(CONV_F):
            s = CONV_F - 1 - j
            dwv_ref[j:j + 1, :] += jnp.sum(dval * _shift_down(hv, hvh, s), axis=0, keepdims=True)
            dwg_ref[j:j + 1, :] += jnp.sum(dgate * _shift_down(hg, hgh, s), axis=0, keepdims=True)

    tile_v = pl.BlockSpec((bt, cb), lambda j, i: (i, j))
    tile_g = pl.BlockSpec((bt, cb), lambda j, i: (i, j + nc))
    r = bt // SUBLANES
    halo_v = pl.BlockSpec((SUBLANES, cb), lambda j, i: (jnp.maximum(i * r - 1, 0), j))
    halo_g = pl.BlockSpec((SUBLANES, cb), lambda j, i: (jnp.maximum(i * r - 1, 0), j + nc))
    wv = pl.BlockSpec((CONV_F, cb), lambda j, i: (0, j))
    wg = pl.BlockSpec((CONV_F, cb), lambda j, i: (0, j + nc))
    bv = pl.BlockSpec((1, cb), lambda j, i: (0, j))
    bg = pl.BlockSpec((1, cb), lambda j, i: (0, j + nc))
    cb2 = cb_.reshape(1, 2 * D_FF)
    dv, dg, dwv, dwg, dbv, dbg = pl.pallas_call(
        body, name=name, grid=(nc, t // bt),
        in_specs=[tile_v, halo_v, tile_g, halo_g, tile_v, wv, wg, bv, bg],
        out_specs=[tile_v, tile_v, wv, wv, bv, bv],
        out_shape=[jax.ShapeDtypeStruct((t, D_FF), F32), jax.ShapeDtypeStruct((t, D_FF), F32),
                   jax.ShapeDtypeStruct((CONV_F, D_FF), F32), jax.ShapeDtypeStruct((CONV_F, D_FF), F32),
                   jax.ShapeDtypeStruct((1, D_FF), F32), jax.ShapeDtypeStruct((1, D_FF), F32)],
        compiler_params=_cparams("parallel", "arbitrary"),
    )(h, h, h, h, dy, cw, cw, cb2, cb2)
    return dv, dg, jnp.concatenate([dwv, dwg], axis=1), jnp.concatenate([dbv, dbg], axis=1)


def _conv_bwd_x(dy, cw, name, out_dtype):
    t, c = dy.shape
    ksize = cw.shape[0]
    bt, cb = _pick(t, ROW_TILE), _pick(c, 1408)

    def body(dy_ref, halo_ref, w_ref, o_ref):
        keep = (pl.program_id(0) < t // bt - 1).astype(F32)
        o_ref[...] = _anticausal_conv(dy_ref[...], halo_ref[...] * keep, w_ref, ksize).astype(out_dtype)

    tile = pl.BlockSpec((bt, cb), lambda i, j: (i, j))
    return pl.pallas_call(
        body, name=name, grid=(t // bt, c // cb),
        in_specs=[tile, _next_halo_spec(bt, cb, t), pl.BlockSpec((ksize, cb), lambda i, j: (0, j))],
        out_specs=tile, out_shape=jax.ShapeDtypeStruct((t, c), out_dtype),
        compiler_params=_cparams("parallel", "parallel"),
    )(dy, dy, cw)


def _ple_fwd(x2, g, e, bg, name):
    t, d = x2.shape
    bt = _pick(t, ROW_TILE)

    def body(x_ref, g_ref, e_ref, b_ref, y_ref, yb_ref):
        y = x_ref[...] + _sigmoid(g_ref[...] + b_ref[...]) * e_ref[...]
        y_ref[...] = y
        yb_ref[...] = y.astype(BF16)

    row = pl.BlockSpec((bt, d), lambda i: (i, 0))
    vec = pl.BlockSpec((1, d), lambda i: (0, 0))
    return pl.pallas_call(
        body, name=name, grid=(t // bt,), in_specs=[row, row, row, vec], out_specs=[row, row],
        out_shape=[jax.ShapeDtypeStruct((t, d), F32), jax.ShapeDtypeStruct((t, d), BF16)],
        compiler_params=_cparams("parallel"),
    )(x2, g, e, bg.reshape(1, d))


def _ple_bwd(dx3, g, e, bg, name):
    t, d = dx3.shape
    bt = _pick(t, ROW_TILE)

    def body(dx_ref, g_ref, e_ref, b_ref, dg_ref, de_ref, db_ref):
        @pl.when(pl.program_id(0) == 0)
        def _():
            db_ref[...] = jnp.zeros_like(db_ref)

        dx = dx_ref[...]
        gate = _sigmoid(g_ref[...] + b_ref[...])
        dg = dx * e_ref[...] * gate * (1.0 - gate)
        dg_ref[...] = dg.astype(BF16)
        de_ref[...] = (dx * gate).astype(BF16)
        db_ref[...] += jnp.sum(dg, axis=0, keepdims=True)

    row = pl.BlockSpec((bt, d), lambda i: (i, 0))
    vec = pl.BlockSpec((1, d), lambda i: (0, 0))
    return pl.pallas_call(
        body, name=name, grid=(t // bt,), in_specs=[row, row, row, vec], out_specs=[row, row, vec],
        out_shape=[jax.ShapeDtypeStruct((t, d), BF16), jax.ShapeDtypeStruct((t, d), BF16),
                   jax.ShapeDtypeStruct((1, d), F32)],
        compiler_params=_cparams("arbitrary"),
    )(dx3, g, e, bg.reshape(1, d))


def _loss_head(y, target, name):
    t, d = y.shape
    bt = _pick(t, ROW_TILE)

    def body(y_ref, t_ref, dy_ref, l_ref, acc_ref):
        @pl.when(pl.program_id(0) == 0)
        def _():
            acc_ref[...] = jnp.zeros_like(acc_ref)

        err = y_ref[...] - t_ref[...]
        dy_ref[...] = err * (1.0 / d)
        acc_ref[...] += jnp.sum(err * err, axis=0, keepdims=True)

        @pl.when(pl.program_id(0) == t // bt - 1)
        def _():
            l_ref[...] = jnp.full(l_ref.shape, (0.5 / d) * jnp.sum(acc_ref[...]), F32)

    row = pl.BlockSpec((bt, d), lambda i: (i, 0))
    dy, l = pl.pallas_call(
        body, name=name, grid=(t // bt,), in_specs=[row, row],
        out_specs=[row, pl.BlockSpec((SUBLANES, LANES), lambda i: (0, 0))],
        out_shape=[jax.ShapeDtypeStruct((t, d), F32), jax.ShapeDtypeStruct((SUBLANES, LANES), F32)],
        scratch_shapes=[pltpu.VMEM((1, d), F32)],
        compiler_params=_cparams("arbitrary"),
    )(y, target)
    return dy, l[0, 0]


def _split3(x):
    hi = x.astype(BF16)
    r1 = x - hi.astype(F32)
    mid = r1.astype(BF16)
    lo = (r1 - mid.astype(F32)).astype(BF16)
    return hi, mid, lo


def _tri_dot(x, tri):
    hi, mid, lo = _split3(x)
    dims = (((1,), (0,)), ((), ()))
    return (lax.dot_general(hi, tri, dims, preferred_element_type=F32)
            + lax.dot_general(mid, tri, dims, preferred_element_type=F32)
            + lax.dot_general(lo, tri, dims, preferred_element_type=F32))


def _fgate_fwd(fg_rows, b_f, name):
    hh, t = fg_rows.shape
    bt = _pick(t, 512)

    def body(fg_ref, b_ref, c_ref, carry_ref):
        @pl.when(pl.program_id(0) == 0)
        def _():
            carry_ref[...] = jnp.zeros_like(carry_ref)

        xx = fg_ref[...] + b_ref[...]
        logf = jnp.minimum(xx, 0.0) - _log1p(jnp.exp(-jnp.abs(xx)))
        r = lax.broadcasted_iota(jnp.int32, (bt, bt), 0)
        c = lax.broadcasted_iota(jnp.int32, (bt, bt), 1)
        tri = (r <= c).astype(BF16)
        cs = _tri_dot(logf, tri) + carry_ref[...]
        c_ref[...] = cs
        carry_ref[...] = cs[:, bt - 1:bt]

    return pl.pallas_call(
        body, name=name, grid=(t // bt,),
        in_specs=[pl.BlockSpec((hh, bt), lambda i: (0, i)), pl.BlockSpec((hh, 1), lambda i: (0, 0))],
        out_specs=pl.BlockSpec((hh, bt), lambda i: (0, i)),
        out_shape=jax.ShapeDtypeStruct((hh, t), F32),
        scratch_shapes=[pltpu.VMEM((hh, 1), F32)],
        compiler_params=_cparams("arbitrary"),
    )(fg_rows, b_f.reshape(hh, 1))


def _fgate_bwd(dck_rows, dcq_rows, fg_rows, b_f, name):
    hh, t = fg_rows.shape
    bt = _pick(t, 512)
    nb = t // bt

    def body(dc_ref, dcq_ref, fg_ref, b_ref, dfg_ref, db_ref, carry_ref):
        @pl.when(pl.program_id(0) == 0)
        def _():
            carry_ref[...] = jnp.zeros_like(carry_ref)
            db_ref[...] = jnp.zeros_like(db_ref)

        r = lax.broadcasted_iota(jnp.int32, (bt, bt), 0)
        c = lax.broadcasted_iota(jnp.int32, (bt, bt), 1)
        tri = (r >= c).astype(BF16)
        dlogf = _tri_dot(dc_ref[...] + dcq_ref[...], tri) + carry_ref[...]
        carry_ref[...] = dlogf[:, 0:1]
        xx = fg_ref[...] + b_ref[...]
        dfg = dlogf * _sigmoid(-xx)
        dfg_ref[...] = dfg
        db_ref[...] += jnp.sum(dfg, axis=1, keepdims=True)

    blk = pl.BlockSpec((hh, bt), lambda i: (0, nb - 1 - i))
    vec = pl.BlockSpec((hh, 1), lambda i: (0, 0))
    return pl.pallas_call(
        body, name=name, grid=(nb,), in_specs=[blk, blk, blk, vec], out_specs=[blk, vec],
        out_shape=[jax.ShapeDtypeStruct((hh, t), F32), jax.ShapeDtypeStruct((hh, 1), F32)],
        scratch_shapes=[pltpu.VMEM((hh, 1), F32)],
        compiler_params=_cparams("arbitrary"),
    )(dck_rows, dcq_rows, fg_rows, b_f.reshape(hh, 1))


def _rows_to_cols(r):
    hh, t = r.shape
    return jnp.repeat(r.reshape(hh // 2, 2, t).transpose(0, 2, 1), HEAD_DIM, axis=-1)


def _cols_to_rows(c):
    npairs, t, _ = c.shape
    return c[:, :, ::HEAD_DIM].transpose(0, 2, 1).reshape(2 * npairs, t)


def _rows_to_pairs(r):
    hh, t = r.shape
    return jnp.pad(r.reshape(hh // 2, 2, t), ((0, 0), (0, SUBLANES - 2), (0, 0)))


def _head_masks(shape):
    first = lax.broadcasted_iota(jnp.int32, shape, 1) < HEAD_DIM
    return first


def _attn_fwd(qkv, c_rows, name):
    t = qkv.shape[0]
    bq = _pick(t, ATTN_BLOCK)
    nq = t // bq
    scale = 1.0 / math.sqrt(HEAD_DIM)

    def body(q_ref, k_ref, v_ref, c_ref, o_ref, of_ref, lse_ref, acc_ref, m_ref, l_ref):
        hp, i = pl.program_id(0), pl.program_id(1)
        first = _head_masks((bq, LANES))
        q2 = q_ref[...]
        zero = jnp.zeros_like(q2)
        qs = (jnp.where(first, q2, zero), jnp.where(first, zero, q2))
        q0 = pl.multiple_of(i * bq, bq)
        crefs = [c_ref[a:a + 1, pl.ds(q0, LANES)][:, 0:1] for a in (0, 1)]
        m_ref[...] = jnp.full(m_ref.shape, NEG_BIG, F32)
        l_ref[...] = jnp.zeros_like(l_ref)
        acc_ref[...] = jnp.zeros_like(acc_ref)
        causal = _rows_iota((bq, bq)) >= lax.broadcasted_iota(jnp.int32, (bq, bq), 1)

        def step(j, masked):
            k0 = pl.multiple_of(j * bq, bq)
            kb = k_ref[pl.ds(k0, bq), :]
            vb = v_ref[pl.ds(k0, bq), :]
            for a in (0, 1):
                s = lax.dot_general(qs[a], kb, _DOT_DIMS["nt"], preferred_element_type=F32) * scale
                s = s + (crefs[a] - c_ref[a:a + 1, pl.ds(k0, bq)])
                if masked:
                    s = jnp.where(causal, s, NEG_BIG)
                m_old = m_ref[a]
                m_new = jnp.maximum(m_old, jnp.max(s, axis=-1, keepdims=True))
                p = jnp.exp(s - m_new)
                alpha = jnp.exp(m_old - m_new)
                l_ref[a] = alpha * l_ref[a] + jnp.sum(p, axis=-1, keepdims=True)
                acc_ref[a] = alpha * acc_ref[a] + lax.dot_general(
                    p.astype(BF16), vb, _DOT_DIMS["nn"], preferred_element_type=F32)
                m_ref[a] = m_new

        def loop_body(j, carry):
            step(j, False)
            return carry

        lax.fori_loop(0, i, loop_body, 0)
        step(i, True)
        o = jnp.where(first, acc_ref[0] / l_ref[0], acc_ref[1] / l_ref[1])
        o_ref[...] = o.astype(BF16)
        of_ref[...] = o
        lse_ref[...] = jnp.where(first, m_ref[0] + jnp.log(l_ref[0]), m_ref[1] + jnp.log(l_ref[1]))

    return pl.pallas_call(
        body, name=name, grid=(N_PAIRS, nq),
        in_specs=[pl.BlockSpec((bq, LANES), lambda hp, i: (i, hp)),
                  pl.BlockSpec((t, LANES), lambda hp, i: (0, N_PAIRS + hp)),
                  pl.BlockSpec((t, LANES), lambda hp, i: (0, 2 * N_PAIRS + hp)),
                  pl.BlockSpec((None, SUBLANES, t), lambda hp, i: (hp, 0, 0))],
        out_specs=[pl.BlockSpec((bq, LANES), lambda hp, i: (i, hp)),
                   pl.BlockSpec((bq, LANES), lambda hp, i: (i, hp)),
                   pl.BlockSpec((None, bq, LANES), lambda hp, i: (hp, i, 0))],
        out_shape=[jax.ShapeDtypeStruct((t, D_MODEL), BF16), jax.ShapeDtypeStruct((t, D_MODEL), F32),
                   jax.ShapeDtypeStruct((N_PAIRS, t, LANES), F32)],
        scratch_shapes=[pltpu.VMEM((2, bq, LANES), F32), pltpu.VMEM((2, bq, 1), F32), pltpu.VMEM((2, bq, 1), F32)],
        compiler_params=_cparams("parallel", "arbitrary"),
    )(qkv, qkv, qkv, c_rows)


def _attn_delta(do, o, name):
    t = do.shape[0]
    bt = _pick(t, ROW_TILE)

    def body(do_ref, o_ref, d_ref, dob_ref):
        dob = do_ref[...].astype(BF16)
        prod = dob.astype(F32) * o_ref[...]
        first = _head_masks(prod.shape)
        da = jnp.sum(jnp.where(first, prod, 0.0), axis=-1, keepdims=True)
        db = jnp.sum(jnp.where(first, 0.0, prod), axis=-1, keepdims=True)
        d_ref[...] = jnp.where(first, da, db)
        dob_ref[...] = dob

    tile = pl.BlockSpec((bt, LANES), lambda i, hp: (i, hp))
    return pl.pallas_call(
        body, name=name, grid=(t // bt, N_PAIRS), in_specs=[tile, tile],
        out_specs=[pl.BlockSpec((None, bt, LANES), lambda i, hp: (hp, i, 0)), tile],
        out_shape=[jax.ShapeDtypeStruct((N_PAIRS, t, LANES), F32), jax.ShapeDtypeStruct((t, D_MODEL), BF16)],
        compiler_params=_cparams("parallel", "parallel"),
    )(do, o)


def _attn_bwd_dq(qkv, dob, c_rows, lse_cols, d_cols, name):
    t = qkv.shape[0]
    bq = _pick(t, ATTN_BLOCK)
    nq = t // bq
    scale = 1.0 / math.sqrt(HEAD_DIM)

    def body(q_ref, k_ref, v_ref, do_ref, c_ref, lse_ref, d_ref, dq_ref, rs_ref, acc_ref, rsa_ref):
        hp, i = pl.program_id(0), pl.program_id(1)
        first = _head_masks((bq, LANES))
        q2, do2 = q_ref[...], do_ref[...]
        zero = jnp.zeros_like(q2)
        qs = (jnp.where(first, q2, zero), jnp.where(first, zero, q2))
        dos = (jnp.where(first, do2, zero), jnp.where(first, zero, do2))
        q0 = pl.multiple_of(i * bq, bq)
        crefs = [c_ref[a:a + 1, pl.ds(q0, LANES)][:, 0:1] for a in (0, 1)]
        lses = (lse_ref[:, 0:1], lse_ref[:, HEAD_DIM:HEAD_DIM + 1])
        dds = (d_ref[:, 0:1], d_ref[:, HEAD_DIM:HEAD_DIM + 1])
        acc_ref[...] = jnp.zeros_like(acc_ref)
        rsa_ref[...] = jnp.zeros_like(rsa_ref)
        causal = _rows_iota((bq, bq)) >= lax.broadcasted_iota(jnp.int32, (bq, bq), 1)

        def step(j, masked):
            k0 = pl.multiple_of(j * bq, bq)
            kb = k_ref[pl.ds(k0, bq), :]
            vb = v_ref[pl.ds(k0, bq), :]
            for a in (0, 1):
                s = lax.dot_general(qs[a], kb, _DOT_DIMS["nt"], preferred_element_type=F32) * scale
                s = s + (crefs[a] - c_ref[a:a + 1, pl.ds(k0, bq)])
                p = jnp.exp(s - lses[a])
                if masked:
                    p = jnp.where(causal, p, 0.0)
                dp = lax.dot_general(dos[a], vb, _DOT_DIMS["nt"], preferred_element_type=F32)
                ds = p * (dp - dds[a])
                rsa_ref[a] += jnp.sum(ds, axis=-1, keepdims=True)
                acc_ref[a] += lax.dot_general(ds.astype(BF16), kb, _DOT_DIMS["nn"], preferred_element_type=F32)

        def loop_body(j, carry):
            step(j, False)
            return carry

        lax.fori_loop(0, i, loop_body, 0)
        step(i, True)
        dq_ref[...] = (jnp.where(first, acc_ref[0], acc_ref[1]) * scale).astype(BF16)
        rs_ref[...] = jnp.where(first, rsa_ref[0], rsa_ref[1])

    colblk = pl.BlockSpec((None, bq, LANES), lambda hp, i: (hp, i, 0))
    return pl.pallas_call(
        body, name=name, grid=(N_PAIRS, nq),
        in_specs=[pl.BlockSpec((bq, LANES), lambda hp, i: (i, hp)),
                  pl.BlockSpec((t, LANES), lambda hp, i: (0, N_PAIRS + hp)),
                  pl.BlockSpec((t, LANES), lambda hp, i: (0, 2 * N_PAIRS + hp)),
                  pl.BlockSpec((bq, LANES), lambda hp, i: (i, hp)),
                  pl.BlockSpec((None, SUBLANES, t), lambda hp, i: (hp, 0, 0)), colblk, colblk],
        out_specs=[pl.BlockSpec((bq, LANES), lambda hp, i: (i, hp)), colblk],
        out_shape=[jax.ShapeDtypeStruct((t, D_MODEL), BF16), jax.ShapeDtypeStruct((N_PAIRS, t, LANES), F32)],
        scratch_shapes=[pltpu.VMEM((2, bq, LANES), F32), pltpu.VMEM((2, bq, 1), F32)],
        compiler_params=_cparams("parallel", "arbitrary"),
    )(qkv, qkv, qkv, dob, c_rows, lse_cols, d_cols)


def _attn_bwd_dkv(qkv, dob, c_rows, c_cols, lse_rows, d_rows, name):
    t = qkv.shape[0]
    bk = _pick(t, ATTN_BLOCK)
    nk = t // bk
    scale = 1.0 / math.sqrt(HEAD_DIM)

    def body(q_ref, do_ref, k_ref, v_ref, c_ref, cc_ref, lse_ref, d_ref, dk_ref, dv_ref, dc_ref,
             dka_ref, dva_ref, dca_ref):
        hp, j = pl.program_id(0), pl.program_id(1)
        first = _head_masks((bk, LANES))
        k2, v2 = k_ref[...], v_ref[...]
        zero = jnp.zeros_like(k2)
        ks = (jnp.where(first, k2, zero), jnp.where(first, zero, k2))
        vs = (jnp.where(first, v2, zero), jnp.where(first, zero, v2))
        ccols = (cc_ref[:, 0:1], cc_ref[:, HEAD_DIM:HEAD_DIM + 1])
        dka_ref[...] = jnp.zeros_like(dka_ref)
        dva_ref[...] = jnp.zeros_like(dva_ref)
        dca_ref[...] = jnp.zeros_like(dca_ref)
        causal = _rows_iota((bk, bk)) <= lax.broadcasted_iota(jnp.int32, (bk, bk), 1)

        def step(i, masked):
            q0 = pl.multiple_of(i * bk, bk)
            qb = q_ref[pl.ds(q0, bk), :]
            dob_ = do_ref[pl.ds(q0, bk), :]
            for a in (0, 1):
                cref = c_ref[a:a + 1, pl.ds(q0, LANES)][:, 0:1]
                st = lax.dot_general(ks[a], qb, _DOT_DIMS["nt"], preferred_element_type=F32) * scale
                st = st + (cref - ccols[a])
                pt = jnp.exp(st - lse_ref[a:a + 1, pl.ds(q0, bk)])
                if masked:
                    pt = jnp.where(causal, pt, 0.0)
                dpt = lax.dot_general(vs[a], dob_, _DOT_DIMS["nt"], preferred_element_type=F32)
                dst = pt * (dpt - d_ref[a:a + 1, pl.ds(q0, bk)])
                dva_ref[a] += lax.dot_general(pt.astype(BF16), dob_, _DOT_DIMS["nn"], preferred_element_type=F32)
                dka_ref[a] += lax.dot_general(dst.astype(BF16), qb, _DOT_DIMS["nn"], preferred_element_type=F32)
                dca_ref[a] += jnp.sum(dst, axis=-1, keepdims=True)

        def loop_body(i, carry):
            step(i, False)
            return carry

        step(j, True)
        lax.fori_loop(j + 1, nk, loop_body, 0)
        dk_ref[...] = (jnp.where(first, dka_ref[0], dka_ref[1]) * scale).astype(BF16)
        dv_ref[...] = jnp.where(first, dva_ref[0], dva_ref[1]).astype(BF16)
        dc_ref[...] = jnp.where(first, -dca_ref[0], -dca_ref[1])

    rows = pl.BlockSpec((None, SUBLANES, t), lambda hp, j: (hp, 0, 0))
    kv_out = pl.BlockSpec((bk, LANES), lambda hp, j: (j, hp))
    colblk = pl.BlockSpec((None, bk, LANES), lambda hp, j: (hp, j, 0))
    return pl.pallas_call(
        body, name=name, grid=(N_PAIRS, nk),
        in_specs=[pl.BlockSpec((t, LANES), lambda hp, j: (0, hp)),
                  pl.BlockSpec((t, LANES), lambda hp, j: (0, hp)),
                  pl.BlockSpec((bk, LANES), lambda hp, j: (j, N_PAIRS + hp)),
                  pl.BlockSpec((bk, LANES), lambda hp, j: (j, 2 * N_PAIRS + hp)),
                  rows, colblk, rows, rows],
        out_specs=[kv_out, kv_out, colblk],
        out_shape=[jax.ShapeDtypeStruct((t, D_MODEL), BF16), jax.ShapeDtypeStruct((t, D_MODEL), BF16),
                   jax.ShapeDtypeStruct((N_PAIRS, t, LANES), F32)],
        scratch_shapes=[pltpu.VMEM((2, bk, LANES), F32), pltpu.VMEM((2, bk, LANES), F32),
                        pltpu.VMEM((2, bk, 1), F32)],
        compiler_params=_cparams("parallel", "arbitrary"),
    )(qkv, dob, qkv, qkv, c_rows, c_cols, lse_rows, d_rows)


def _scan(a, u, name, reverse=False):
    t, c = a.shape
    bt, cb = _pick(t, ROW_TILE), _pick(c, 1024)
    nt = t // bt
    ngroups = bt // SUBLANES

    def body(a_ref, u_ref, h_ref, carry_ref, as_ref, us_ref):
        @pl.when(pl.program_id(1) == 0)
        def _():
            carry_ref[...] = jnp.zeros_like(carry_ref)

        av, uv = a_ref[...], u_ref[...]
        sub = _rows_iota((bt, cb)) % SUBLANES
        for s in (1, 2, 4):
            if reverse:
                a_sh, u_sh = pltpu.roll(av, bt - s, axis=0), pltpu.roll(uv, bt - s, axis=0)
                valid = sub < SUBLANES - s
            else:
                a_sh, u_sh = pltpu.roll(av, s, axis=0), pltpu.roll(uv, s, axis=0)
                valid = sub >= s
            uv = jnp.where(valid, uv + av * u_sh, uv)
            av = jnp.where(valid, av * a_sh, av)
        as_ref[...] = av
        us_ref[...] = uv
        edge = 0 if reverse else SUBLANES - 1
        pick = _rows_iota((SUBLANES, cb)) == edge

        def group(gi, carry):
            g = (ngroups - 1 - gi) if reverse else gi
            r0 = pl.multiple_of(g * SUBLANES, SUBLANES)
            h8 = us_ref[pl.ds(r0, SUBLANES), :] + as_ref[pl.ds(r0, SUBLANES), :] * carry
            h_ref[pl.ds(r0, SUBLANES), :] = h8
            return jnp.sum(jnp.where(pick, h8, 0.0), axis=0, keepdims=True)

        carry_ref[...] = lax.fori_loop(0, ngroups, group, carry_ref[...])

    if reverse:
        tile = pl.BlockSpec((bt, cb), lambda j, i: (nt - 1 - i, j))
    else:
        tile = pl.BlockSpec((bt, cb), lambda j, i: (i, j))
    return pl.pallas_call(
        body, name=name, grid=(c // cb, nt), in_specs=[tile, tile], out_specs=tile,
        out_shape=jax.ShapeDtypeStruct((t, c), F32),
        scratch_shapes=[pltpu.VMEM((1, cb), F32), pltpu.VMEM((bt, cb), F32), pltpu.VMEM((bt, cb), F32)],
        compiler_params=_cparams("parallel", "arbitrary"),
    )(a, u)


def _rg_conv_fwd(proj, cw, cb_, name):
    t = proj.shape[0]
    bt, cb = _pick(t, ROW_TILE), D_MODEL

    def body(x_ref, halo_ref, w_ref, b_ref, o_ref):
        keep = (pl.program_id(0) > 0).astype(F32)
        o_ref[...] = _causal_conv(x_ref[...], halo_ref[...] * keep, w_ref, CONV_B) + b_ref[...]

    tile = pl.BlockSpec((bt, cb), lambda i, j: (i, j))
    return pl.pallas_call(
        body, name=name, grid=(t // bt, 1),
        in_specs=[tile, _prev_halo_spec(bt, cb), pl.BlockSpec((CONV_B, cb), lambda i, j: (0, 0)),
                  pl.BlockSpec((1, cb), lambda i, j: (0, 0))],
        out_specs=tile, out_shape=jax.ShapeDtypeStruct((t, D_MODEL), F32),
        compiler_params=_cparams("parallel", "parallel"),
    )(proj, proj, cw, cb_.reshape(1, D_MODEL))


def _conv_bwd_w(x, dy, ksize, name):
    t, c = dy.shape
    bt = _pick(t, ROW_TILE)
    r = bt // SUBLANES

    def body(x_ref, halo_ref, dy_ref, dw_ref, db_ref):
        @pl.when(pl.program_id(0) == 0)
        def _():
            dw_ref[...] = jnp.zeros_like(dw_ref)
            db_ref[...] = jnp.zeros_like(db_ref)

        keep = (pl.program_id(0) > 0).astype(F32)
        xv, halo, dyv = x_ref[...], halo_ref[...] * keep, dy_ref[...]
        db_ref[...] += jnp.sum(dyv, axis=0, keepdims=True)
        for j in range(ksize):
            dw_ref[j:j + 1, :] += jnp.sum(dyv * _shift_down(xv, halo, ksize - 1 - j), axis=0, keepdims=True)

    tile = pl.BlockSpec((bt, c), lambda i: (i, 0))
    return pl.pallas_call(
        body, name=name, grid=(t // bt,),
        in_specs=[tile, pl.BlockSpec((SUBLANES, c), lambda i: (jnp.maximum(i * r - 1, 0), 0)), tile],
        out_specs=[pl.BlockSpec((ksize, c), lambda i: (0, 0)), pl.BlockSpec((1, c), lambda i: (0, 0))],
        out_shape=[jax.ShapeDtypeStruct((ksize, c), F32), jax.ShapeDtypeStruct((1, c), F32)],
        compiler_params=_cparams("arbitrary"),
    )(x, x, dy)


def _rg_gate_math(xc, wa_ref, wi_ref, ba_ref, bi_ref, lam_ref):
    xb = xc.astype(BF16)
    ra = lax.dot_general(xb, wa_ref[...], _DOT_DIMS["nn"], preferred_element_type=F32) + ba_ref[...]
    ia = lax.dot_general(xb, wi_ref[...], _DOT_DIMS["nn"], preferred_element_type=F32) + bi_ref[...]
    r, ig = _sigmoid(ra), _sigmoid(ia)
    sp = _softplus(-lam_ref[...])
    log_a = -LRU_C * r * sp
    a = jnp.exp(log_a)
    mult = jnp.sqrt(-_expm1(2.0 * log_a))
    return xb, r, ig, sp, a, mult


def _rg_gate_specs(bt, time_first):
    if time_first:
        tile = pl.BlockSpec((bt, BLOCK_B), lambda i, n: (i, n))
        w = pl.BlockSpec((None, BLOCK_B, BLOCK_B), lambda i, n: (n, 0, 0))
        v = pl.BlockSpec((None, 1, BLOCK_B), lambda i, n: (n, 0, 0))
    else:
        tile = pl.BlockSpec((bt, BLOCK_B), lambda n, i: (i, n))
        w = pl.BlockSpec((None, BLOCK_B, BLOCK_B), lambda n, i: (n, 0, 0))
        v = pl.BlockSpec((None, 1, BLOCK_B), lambda n, i: (n, 0, 0))
    return tile, w, v


def _rg_gate_fwd(xc, wa, ba, wi, bi, lam, name):
    t = xc.shape[0]
    bt = _pick(t, 512)

    def body(x_ref, wa_ref, wi_ref, ba_ref, bi_ref, lam_ref, a_ref, u_ref):
        xcv = x_ref[...]
        _, _, ig, _, a, mult = _rg_gate_math(xcv, wa_ref, wi_ref, ba_ref, bi_ref, lam_ref)
        a_ref[...] = a
        u_ref[...] = mult * (ig * xcv)

    tile, w, v = _rg_gate_specs(bt, True)
    return pl.pallas_call(
        body, name=name, grid=(t // bt, N_BLOCKS_B), in_specs=[tile, w, w, v, v, v], out_specs=[tile, tile],
        out_shape=[jax.ShapeDtypeStruct((t, D_MODEL), F32), jax.ShapeDtypeStruct((t, D_MODEL), F32)],
        compiler_params=_cparams("parallel", "parallel"),
    )(xc, wa, wi, ba, bi, lam)


def _rg_gate_bwd(xc, g, h, wa, ba, wi, bi, lam, name):
    t = xc.shape[0]
    bt = _pick(t, 512)
    rr = bt // SUBLANES

    def body(x_ref, g_ref, h_ref, hh_ref, wa_ref, wi_ref, ba_ref, bi_ref, lam_ref,
             dx_ref, dwa_ref, dwi_ref, dba_ref, dbi_ref, dlam_ref):
        @pl.when(pl.program_id(1) == 0)
        def _():
            for ref in (dwa_ref, dwi_ref, dba_ref, dbi_ref, dlam_ref):
                ref[...] = jnp.zeros_like(ref)

        keep = (pl.program_id(1) > 0).astype(F32)
        xcv, gv = x_ref[...], g_ref[...]
        xb, r, ig, sp, a, mult = _rg_gate_math(xcv, wa_ref, wi_ref, ba_ref, bi_ref, lam_ref)
        h_prev = _shift_down(h_ref[...], hh_ref[...] * keep, 1)
        da = gv * h_prev
        dmult = gv * ig * xcv
        dig = gv * mult * xcv
        dxc = gv * mult * ig
        dlog_a = da * a - dmult * (a * a / mult)
        dr = dlog_a * (-LRU_C * sp)
        dsp = jnp.sum(dlog_a * (-LRU_C * r), axis=0, keepdims=True)
        dlam_ref[...] += dsp * (-_sigmoid(-lam_ref[...]))
        dra = dr * r * (1.0 - r)
        dia = dig * ig * (1.0 - ig)
        dba_ref[...] += jnp.sum(dra, axis=0, keepdims=True)
        dbi_ref[...] += jnp.sum(dia, axis=0, keepdims=True)
        drab, diab = dra.astype(BF16), dia.astype(BF16)
        dwa_ref[...] += lax.dot_general(xb, drab, _DOT_DIMS["tn"], preferred_element_type=F32)
        dwi_ref[...] += lax.dot_general(xb, diab, _DOT_DIMS["tn"], preferred_element_type=F32)
        dxc = dxc + lax.dot_general(drab, wa_ref[...], _DOT_DIMS["nt"], preferred_element_type=F32)
        dxc = dxc + lax.dot_general(diab, wi_ref[...], _DOT_DIMS["nt"], preferred_element_type=F32)
        dx_ref[...] = dxc

    tile, w, v = _rg_gate_specs(bt, False)
    halo = pl.BlockSpec((SUBLANES, BLOCK_B), lambda n, i: (jnp.maximum(i * rr - 1, 0), n))
    wshape = jax.ShapeDtypeStruct((N_BLOCKS_B, BLOCK_B, BLOCK_B), F32)
    vshape = jax.ShapeDtypeStruct((N_BLOCKS_B, 1, BLOCK_B), F32)
    return pl.pallas_call(
        body, name=name, grid=(N_BLOCKS_B, t // bt),
        in_specs=[tile, tile, tile, halo, w, w, v, v, v],
        out_specs=[tile, w, w, v, v, v],
        out_shape=[jax.ShapeDtypeStruct((t, D_MODEL), F32), wshape, wshape, vshape, vshape, vshape],
        compiler_params=_cparams("parallel", "arbitrary"),
    )(xc, g, h, h, wa, wi, ba, bi, lam)


def _rg_out_fwd(h, proj, name):
    t = h.shape[0]
    bt = _pick(t, ROW_TILE)

    def body(h_ref, g_ref, y_ref):
        y_ref[...] = (h_ref[...] * _gelu(g_ref[...])).astype(BF16)

    tile = pl.BlockSpec((bt, D_MODEL), lambda i: (i, 0))
    return pl.pallas_call(
        body, name=name, grid=(t // bt,), in_specs=[tile, pl.BlockSpec((bt, D_MODEL), lambda i: (i, 1))],
        out_specs=tile, out_shape=jax.ShapeDtypeStruct((t, D_MODEL), BF16),
        compiler_params=_cparams("parallel"),
    )(h, proj)


def _rg_out_bwd(dy, h, proj, name):
    t = h.shape[0]
    bt = _pick(t, ROW_TILE)

    def body(dy_ref, h_ref, g_ref, dh_ref, dg_ref):
        gl, glg = _gelu_and_grad(g_ref[...])
        dyv = dy_ref[...]
        dh_ref[...] = dyv * gl
        dg_ref[...] = (dyv * h_ref[...] * glg).astype(BF16)

    tile = pl.BlockSpec((bt, D_MODEL), lambda i: (i, 0))
    return pl.pallas_call(
        body, name=name, grid=(t // bt,),
        in_specs=[tile, tile, pl.BlockSpec((bt, D_MODEL), lambda i: (i, 1))], out_specs=[tile, tile],
        out_shape=[jax.ShapeDtypeStruct((t, D_MODEL), F32), jax.ShapeDtypeStruct((t, D_MODEL), BF16)],
        compiler_params=_cparams("parallel"),
    )(dy, h, proj)


def _shift_up_one(a, name):
    t, c = a.shape
    bt = _pick(t, ROW_TILE)

    def body(a_ref, halo_ref, o_ref):
        o_ref[...] = _shift_up(a_ref[...], halo_ref[...], 1)

    tile = pl.BlockSpec((bt, c), lambda i, j: (i, j))
    return pl.pallas_call(
        body, name=name, grid=(t // bt, 1), in_specs=[tile, _next_halo_spec(bt, c, t)], out_specs=tile,
        out_shape=jax.ShapeDtypeStruct((t, c), F32), compiler_params=_cparams("parallel", "parallel"),
    )(a, a)


ADAM_ROWS = 64


def _adamw(recv, w, m, v, name):
    _, r, c = recv.shape
    br = ADAM_ROWS
    assert r % br == 0

    def body(r_ref, w_ref, m_ref, v_ref, g_ref, d_ref, nm_ref, nv_ref):
        g = r_ref[0]
        for s in range(1, N_DEV):
            g = g + r_ref[s]
        m_new = ADAM_B1 * m_ref[...] + (1.0 - ADAM_B1) * g
        v_new = ADAM_B2 * v_ref[...] + (1.0 - ADAM_B2) * (g * g)
        m_hat = m_new / (1.0 - ADAM_B1 ** ADAM_STEP)
        v_hat = v_new / (1.0 - ADAM_B2 ** ADAM_STEP)
        g_ref[...] = g
        d_ref[...] = -ADAM_LR * (m_hat / (jnp.sqrt(v_hat) + ADAM_EPS) + ADAM_WD * w_ref[...])
        nm_ref[...] = m_new
        nv_ref[...] = v_new

    tile = pl.BlockSpec((br, c), lambda i: (i, 0))
    shape = jax.ShapeDtypeStruct((r, c), F32)
    return pl.pallas_call(
        body, name=name, grid=(r // br,),
        in_specs=[pl.BlockSpec((N_DEV, br, c), lambda i: (0, i, 0)), tile, tile, tile],
        out_specs=[tile] * 4, out_shape=[shape] * 4, compiler_params=_cparams("parallel"),
    )(recv, w, m, v)


def _exchange(src, scatter, name):
    slab = src.shape[1:] if scatter else src.shape

    def body(src_ref, out_ref, send_sems, recv_sems, local_sem):
        pos = [lax.axis_index(ax) for ax in MESH_AXES]
        me = 4 * pos[0] + 2 * pos[1] + pos[2]

        def peer_of(k):
            p = [(1 - pos[b]) if (k >> (2 - b)) & 1 else pos[b] for b in range(3)]
            return tuple(p), 4 * p[0] + 2 * p[1] + p[2]

        def copy(k):
            peer, peer_idx = peer_of(k)
            return pltpu.make_async_remote_copy(
                src_ref=src_ref.at[peer_idx] if scatter else src_ref, dst_ref=out_ref.at[me],
                send_sem=send_sems.at[k - 1], recv_sem=recv_sems.at[k - 1],
                device_id=peer, device_id_type=pl.DeviceIdType.MESH)

        def arrival(k):
            peer, peer_idx = peer_of(k)
            return pltpu.make_async_remote_copy(
                src_ref=src_ref.at[me] if scatter else src_ref, dst_ref=out_ref.at[peer_idx],
                send_sem=send_sems.at[k - 1], recv_sem=recv_sems.at[k - 1],
                device_id=peer, device_id_type=pl.DeviceIdType.MESH)

        mine = pltpu.make_async_copy(src_ref.at[me] if scatter else src_ref, out_ref.at[me], local_sem)
        mine.start()
        sends = [copy(k) for k in range(1, N_DEV)]
        for cp in sends:
            cp.start()
        for k in range(1, N_DEV):
            arrival(k).wait_recv()
        for cp in sends:
            cp.wait_send()
        mine.wait()

    hbm = pl.BlockSpec(memory_space=pltpu.HBM)
    return pl.pallas_call(
        body, name=name, in_specs=[hbm], out_specs=hbm,
        out_shape=jax.ShapeDtypeStruct((N_DEV,) + tuple(slab), src.dtype),
        scratch_shapes=[pltpu.SemaphoreType.DMA((N_DEV - 1,)), pltpu.SemaphoreType.DMA((N_DEV - 1,)),
                        pltpu.SemaphoreType.DMA],
        compiler_params=pltpu.CompilerParams(has_side_effects=True),
    )(src)


WEIGHTS = ['a_w_in', 'a_b_f', 'a_w_out', 'b_w_in', 'b_conv_w', 'b_conv_b', 'b_w_a', 'b_b_a', 'b_w_i', 'b_b_i',
           'b_lam', 'b_w_out', 'f_w_up', 'f_conv_w', 'f_conv_b', 'f_w_down', 'ln1_g', 'ln1_b', 'ln2_g', 'ln2_b',
           'ple_w', 'ple_gate_w', 'ple_gate_b']
SHARD_AXIS = {'a_w_in': 2, 'a_b_f': None, 'a_w_out': 1, 'b_w_in': 2, 'b_conv_w': 2, 'b_conv_b': 1, 'b_w_a': None,
              'b_b_a': None, 'b_w_i': None, 'b_b_i': None, 'b_lam': 1, 'b_w_out': 1, 'f_w_up': 2, 'f_conv_w': 2,
              'f_conv_b': None, 'f_w_down': 1, 'ln1_g': None, 'ln1_b': None, 'ln2_g': None, 'ln2_b': None,
              'ple_w': 2, 'ple_gate_w': 1, 'ple_gate_b': None}
MATMUL_WEIGHTS = ['a_w_in', 'a_w_out', 'b_w_in', 'b_w_out', 'f_w_up', 'f_w_down', 'ple_w', 'ple_gate_w']
SMALL_SHARDED = ['b_conv_w', 'b_conv_b', 'b_lam', 'f_conv_w']
PACK_COLS = 1024


def _to_shards(full, axis):
    shp = full.shape
    split = full.reshape(shp[:axis] + (N_DEV, shp[axis] // N_DEV) + shp[axis + 1:])
    return jnp.moveaxis(split, axis, 0)


def _from_shards(pieces, axis):
    moved = jnp.moveaxis(pieces, 0, axis)
    shp = moved.shape
    return moved.reshape(shp[:axis] + (shp[axis] * shp[axis + 1],) + shp[axis + 2:])


def _pack_rows(flat, row_mult):
    n = flat.shape[-1]
    rows = -(-n // PACK_COLS)
    rows = -(-rows // row_mult) * row_mult
    pad = [(0, 0)] * (flat.ndim - 1) + [(0, rows * PACK_COLS - n)]
    return jnp.pad(flat, pad).reshape(flat.shape[:-1] + (rows, PACK_COLS))


def _gather_weights(local, names, dtype, row_mult, name):
    flat = jnp.concatenate([local[n].astype(dtype).reshape(-1) for n in names])
    gathered = _exchange(_pack_rows(flat, row_mult), False, name)
    gathered = gathered.reshape(N_DEV, -1)
    out, off = {}, 0
    for n in names:
        size = local[n].size
        pieces = gathered[:, off:off + size].reshape((N_DEV,) + local[n].shape)
        out[n] = _from_shards(pieces, SHARD_AXIS[n])
        off += size
    return out


def _mixer_a_fwd(tag, xb, w):
    qkv = _mm(xb, w["wqkv"], "nn", BF16, f"{tag}_qkv")
    fg = _mm(xb, w["wf"], "nn", F32, f"{tag}_fgproj")
    fg_rows = fg[:, :N_HEADS].T
    c_rows = _fgate_fwd(fg_rows, w["b_f"], f"{tag}_fgate")
    c_pairs = _rows_to_pairs(c_rows)
    o, o_f32, lse_cols = _attn_fwd(qkv, c_pairs, f"{tag}_attn")
    m = _mm(o, w["wout"], "nn", F32, f"{tag}_oproj")
    return m, dict(qkv=qkv, fg_rows=fg_rows, c_rows=c_rows, c_pairs=c_pairs, o=o, o_f32=o_f32, lse_cols=lse_cols)


def _mixer_a_bwd(tag, dz, dzb, xb, w, s):
    t = xb.shape[0]
    do = _mm(dzb, w["wout"], "nt", F32, f"{tag}_b_do")
    g_wout = _mm(s["o"], dzb, "tn", F32, f"{tag}_b_dwout")
    d_cols, dob = _attn_delta(do, s["o_f32"], f"{tag}_b_delta")
    dq, dcq_cols = _attn_bwd_dq(s["qkv"], dob, s["c_pairs"], s["lse_cols"], d_cols, f"{tag}_b_dq")
    dk, dv, dc_cols = _attn_bwd_dkv(s["qkv"], dob, s["c_pairs"], _rows_to_cols(s["c_rows"]),
                                    _rows_to_pairs(_cols_to_rows(s["lse_cols"])),
                                    _rows_to_pairs(_cols_to_rows(d_cols)), f"{tag}_b_dkv")
    dfg_rows, db_f = _fgate_bwd(_cols_to_rows(dc_cols), _cols_to_rows(dcq_cols), s["fg_rows"], w["b_f"],
                                f"{tag}_b_fgate")
    dqkv = jnp.concatenate([dq, dk, dv], axis=1)
    dfg = jnp.pad(dfg_rows.T, ((0, 0), (0, LANES - N_HEADS))).astype(BF16)
    dx = _mm(dqkv, w["wqkv"], "nt", F32, f"{tag}_b_dx_qkv", add=dz, add_scale=ALPHA)
    dx = _mm(dfg, w["wf"], "nt", F32, f"{tag}_b_dx_fg", add=dx)
    g_wqkv = _mm(xb, dqkv, "tn", F32, f"{tag}_b_dwqkv")
    g_wf = _mm(xb, dfg, "tn", F32, f"{tag}_b_dwf")[:, :N_HEADS]
    grads = dict(a_w_in=jnp.concatenate([g_wqkv, g_wf], axis=1), a_b_f=db_f.reshape(N_HEADS), a_w_out=g_wout)
    return dx, grads


def _mixer_b_fwd(tag, xb, w):
    proj = _mm(xb, w["win"], "nn", F32, f"{tag}_proj")
    xc = _rg_conv_fwd(proj, w["conv_w"], w["conv_b"], f"{tag}_conv")
    a, u = _rg_gate_fwd(xc, w["wa"], w["ba"], w["wi"], w["bi"], w["lam"], f"{tag}_gate")
    h = _scan(a, u, f"{tag}_scan")
    y = _rg_out_fwd(h, proj, f"{tag}_out")
    m = _mm(y, w["wout"], "nn", F32, f"{tag}_oproj")
    return m, dict(proj=proj, xc=xc, a=a, h=h, y=y)


def _mixer_b_bwd(tag, dz, dzb, xb, w, s):
    dy = _mm(dzb, w["wout"], "nt", F32, f"{tag}_b_dy")
    g_wout = _mm(s["y"], dzb, "tn", F32, f"{tag}_b_dwout")
    dh, dgate = _rg_out_bwd(dy, s["h"], s["proj"], f"{tag}_b_out")
    g = _scan(_shift_up_one(s["a"], f"{tag}_b_shift"), dh, f"{tag}_b_scan", reverse=True)
    dxc, g_wa, g_wi, g_ba, g_bi, g_lam = _rg_gate_bwd(
        s["xc"], g, s["h"], w["wa"], w["ba"], w["wi"], w["bi"], w["lam"], f"{tag}_b_gate")
    dxp = _conv_bwd_x(dxc, w["conv_w"], f"{tag}_b_convx", BF16)
    g_cw, g_cb = _conv_bwd_w(s["proj"], dxc, CONV_B, f"{tag}_b_convw")
    dproj = jnp.concatenate([dxp, dgate], axis=1)
    dx = _mm(dproj, w["win"], "nt", F32, f"{tag}_b_dx", add=dz, add_scale=ALPHA)
    g_win = _mm(xb, dproj, "tn", F32, f"{tag}_b_dwin")
    grads = dict(b_w_in=g_win, b_conv_w=g_cw, b_conv_b=g_cb.reshape(D_MODEL), b_w_a=g_wa,
                 b_b_a=g_ba.reshape(N_BLOCKS_B, BLOCK_B), b_w_i=g_wi, b_b_i=g_bi.reshape(N_BLOCKS_B, BLOCK_B),
                 b_lam=g_lam.reshape(D_MODEL), b_w_out=g_wout)
    return dx, grads


def _layer_fwd(i, x, xb, pb, w):
    tag = f"L{i}"
    mix = _mixer_a_fwd if i % 2 == 0 else _mixer_b_fwd
    m, sm = mix(tag, xb, w)
    x1, x1b, z1 = _ln_fwd(x, m, w["ln1_g"], w["ln1_b"], f"{tag}_ln1")
    h = _mm(x1b, w["wup"], "nn", F32, f"{tag}_ffn_up")
    y = _ffn_mid_fwd(h, w["fconv_w"], w["fconv_b"], f"{tag}_ffn_mid")
    ff = _mm(y, w["wdown"], "nn", F32, f"{tag}_ffn_down")
    x2, x2b, z2 = _ln_fwd(x1, ff, w["ln2_g"], w["ln2_b"], f"{tag}_ln2")
    gl = _mm(x2b, w["wg"], "nn", F32, f"{tag}_ple_gate")
    e = _mm(pb, w["wp"], "nn", F32, f"{tag}_ple_emb")
    x3, x3b = _ple_fwd(x2, gl, e, w["bg"], f"{tag}_ple")
    saved = dict(mixer=sm, xb=xb, x1b=x1b, z1=z1, h=h, y=y, x2b=x2b, z2=z2, gl=gl, e=e, pb=pb)
    return x3, x3b, saved


def _layer_bwd(i, dx3, w, s):
    tag = f"L{i}"
    dgl, de, g_bg = _ple_bwd(dx3, s["gl"], s["e"], w["bg"], f"{tag}_b_ple")
    g_wg = _mm(s["x2b"], dgl, "tn", F32, f"{tag}_b_dwg")
    g_wp = _mm(s["pb"], de, "tn", F32, f"{tag}_b_dwp")
    dx2 = _mm(dgl, w["wg"], "nt", F32, f"{tag}_b_dx2", add=dx3)
    dz2, dz2b, g_ln2g, g_ln2b = _ln_bwd(dx2, s["z2"], w["ln2_g"], f"{tag}_b_ln2")
    dy = _mm(dz2b, w["wdown"], "nt", F32, f"{tag}_b_dy")
    g_wdown = _mm(s["y"], dz2b, "tn", F32, f"{tag}_b_dwdown")
    dval, dgate, g_fcw, g_fcb = _ffn_mid_bwd_a(s["h"], dy, w["fconv_w"], w["fconv_b"], f"{tag}_b_ffn_mid")
    dhv = _conv_bwd_x(dval, w["fconv_w"][:, :D_FF], f"{tag}_b_convx_v", BF16)
    dhg = _conv_bwd_x(dgate, w["fconv_w"][:, D_FF:], f"{tag}_b_convx_g", BF16)
    dx1 = _mm(dhv, w["wup"][:, :D_FF], "nt", F32, f"{tag}_b_dx1_v", add=dz2, add_scale=ALPHA)
    dx1 = _mm(dhg, w["wup"][:, D_FF:], "nt", F32, f"{tag}_b_dx1_g", add=dx1)
    g_wup = jnp.concatenate([_mm(s["x1b"], dhv, "tn", F32, f"{tag}_b_dwup_v"),
                             _mm(s["x1b"], dhg, "tn", F32, f"{tag}_b_dwup_g")], axis=1)
    dz1, dz1b, g_ln1g, g_ln1b = _ln_bwd(dx1, s["z1"], w["ln1_g"], f"{tag}_b_ln1")
    mix_bwd = _mixer_a_bwd if i % 2 == 0 else _mixer_b_bwd
    dx, g_mix = mix_bwd(tag, dz1, dz1b, s["xb"], w, s["mixer"])
    grads = dict(f_w_up=g_wup, f_conv_w=g_fcw, f_conv_b=g_fcb.reshape(2 * D_FF), f_w_down=g_wdown,
                 ln1_g=g_ln1g.reshape(D_MODEL), ln1_b=g_ln1b.reshape(D_MODEL), ln2_g=g_ln2g.reshape(D_MODEL),
                 ln2_b=g_ln2b.reshape(D_MODEL), ple_w=g_wp, ple_gate_w=g_wg, ple_gate_b=g_bg.reshape(D_MODEL))
    return dx, g_mix, grads


def _layer_weights(i, full, rep):
    j = i // 2
    w = dict(ln1_g=rep["ln1_g"][i], ln1_b=rep["ln1_b"][i], ln2_g=rep["ln2_g"][i], ln2_b=rep["ln2_b"][i],
             wup=full["f_w_up"][i], fconv_w=full["f_conv_w"][i], fconv_b=rep["f_conv_b"][i],
             wdown=full["f_w_down"][i], wp=full["ple_w"][i], wg=full["ple_gate_w"][i], bg=rep["ple_gate_b"][i])
    if i % 2 == 0:
        w_in = full["a_w_in"][j]
        w.update(wqkv=w_in[:, :3 * D_MODEL],
                 wf=jnp.pad(w_in[:, 3 * D_MODEL:], ((0, 0), (0, LANES - N_HEADS))),
                 b_f=rep["a_b_f"][j], wout=full["a_w_out"][j])
    else:
        w.update(win=full["b_w_in"][j], conv_w=full["b_conv_w"][j], conv_b=full["b_conv_b"][j],
                 wa=rep["b_w_a"][j].astype(BF16), wi=rep["b_w_i"][j].astype(BF16),
                 ba=rep["b_b_a"][j].reshape(N_BLOCKS_B, 1, BLOCK_B), bi=rep["b_b_i"][j].reshape(N_BLOCKS_B, 1, BLOCK_B),
                 lam=full["b_lam"][j].reshape(N_BLOCKS_B, 1, BLOCK_B), wout=full["b_w_out"][j])
    return w


def _fwd_bwd(x, p, target, full, rep):
    weights = [_layer_weights(i, full, rep) for i in range(DEPTH)]
    xb = x.astype(BF16)
    pb = p.astype(BF16)
    saved = []
    for i in range(DEPTH):
        x, xb, s = _layer_fwd(i, x, xb, pb[i], weights[i])
        saved.append(s)
    dx, loss_local = _loss_head(x, target, "loss_head")

    per_layer = {n: [None] * (DEPTH if n.startswith(("f_", "ln", "ple")) else DEPTH // 2) for n in WEIGHTS}
    for i in reversed(range(DEPTH)):
        dx, g_mix, g_layer = _layer_bwd(i, dx, weights[i], saved[i])
        for n, g in g_layer.items():
            per_layer[n][i] = g
        for n, g in g_mix.items():
            per_layer[n][i // 2] = g
    return loss_local, dx, {n: jnp.stack(per_layer[n]) for n in WEIGHTS}


def _train_step(x, p, target, local, moments_m, moments_v):
    full = _gather_weights(local, MATMUL_WEIGHTS, BF16, 16, "gather_matmul_weights")
    full.update(_gather_weights(local, SMALL_SHARDED, F32, SUBLANES, "gather_small_weights"))
    rep = {n: local[n] for n in WEIGHTS if SHARD_AXIS[n] is None}
    loss_local, dx, grads_full = _fwd_bwd(x, p, target, full, rep)

    pieces = []
    for n in WEIGHTS:
        g = grads_full[n].astype(F32)
        if SHARD_AXIS[n] is None:
            pieces.append(jnp.broadcast_to(g.reshape(1, -1), (N_DEV, g.size)))
        else:
            pieces.append(_to_shards(g, SHARD_AXIS[n]).reshape(N_DEV, -1))
    slabs = _pack_rows(jnp.concatenate(pieces, axis=1), ADAM_ROWS)
    recv = _exchange(slabs, True, "reduce_scatter_grads")

    def pack_local(d):
        return _pack_rows(jnp.concatenate([d[n].astype(F32).reshape(-1) for n in WEIGHTS]), ADAM_ROWS)

    outs = _adamw(recv, pack_local(local), pack_local(moments_m), pack_local(moments_v), "adamw")
    unpacked = []
    for packed in outs:
        flat, off, d = packed.reshape(-1), 0, {}
        for n in WEIGHTS:
            d[n] = flat[off:off + local[n].size].reshape(local[n].shape)
            off += local[n].size
        unpacked.append(d)
    return loss_local, dx, unpacked


def kernel(x, p, a_w_in, a_b_f, a_w_out, b_w_in, b_conv_w, b_conv_b, b_w_a, b_b_a, b_w_i, b_b_i, b_lam, b_w_out, f_w_up, f_conv_w, f_conv_b, f_w_down, ln1_g, ln1_b, ln2_g, ln2_b, ple_w, ple_gate_w, ple_gate_b, loss_target, m_a_w_in, m_a_b_f, m_a_w_out, m_b_w_in, m_b_conv_w, m_b_conv_b, m_b_w_a, m_b_b_a, m_b_w_i, m_b_b_i, m_b_lam, m_b_w_out, m_f_w_up, m_f_conv_w, m_f_conv_b, m_f_w_down, m_ln1_g, m_ln1_b, m_ln2_g, m_ln2_b, m_ple_w, m_ple_gate_w, m_ple_gate_b, v_a_w_in, v_a_b_f, v_a_w_out, v_b_w_in, v_b_conv_w, v_b_conv_b, v_b_w_a, v_b_b_a, v_b_w_i, v_b_b_i, v_b_lam, v_b_w_out, v_f_w_up, v_f_conv_w, v_f_conv_b, v_f_w_down, v_ln1_g, v_ln1_b, v_ln2_g, v_ln2_b, v_ple_w, v_ple_gate_w, v_ple_gate_b):
    given = dict(locals())
    local = {n: given[n] for n in WEIGHTS}
    mom_m = {n: given["m_" + n] for n in WEIGHTS}
    mom_v = {n: given["v_" + n] for n in WEIGHTS}
    t = x.shape[1]
    loss_local, dx, (grad, delta, new_m, new_v) = _train_step(
        x.reshape(t, D_MODEL), p.reshape(DEPTH, t, D_PLE), loss_target.reshape(t, D_MODEL), local, mom_m, mom_v)
    loss = lax.psum(loss_local, MESH_AXES)
    return (loss, dx.reshape(1, t, D_MODEL), *[grad[n] for n in WEIGHTS], *[delta[n] for n in WEIGHTS],
            *[new_m[n] for n in WEIGHTS], *[new_v[n] for n in WEIGHTS])
```

```python
import functools
import math

import jax
import jax.numpy as jnp
from jax import lax
from jax.experimental import pallas as pl
from jax.experimental.pallas import tpu as pltpu

F32 = jnp.float32
BF16 = jnp.bfloat16

D_MODEL = 1024
DEPTH = 4
N_HEADS = 16
HEAD_DIM = 64
N_PAIRS = N_HEADS // 2
N_BLOCKS_B = 8
BLOCK_B = 128
CONV_B = 4
LRU_C = 8.0
D_FF = 2816
CONV_F = 3
D_PLE = 256
LN_EPS = 1e-5
ALPHA = (2.0 * DEPTH) ** 0.25
ADAM_LR, ADAM_B1, ADAM_B2, ADAM_EPS, ADAM_WD, ADAM_STEP = 0.001, 0.9, 0.999, 1e-08, 0.01, 10
N_DEV = 8
MESH_AXES = ("x", "y", "c")

LANES = 128
SUBLANES = 8
VMEM_LIMIT_BYTES = 56 * 1024 * 1024
ATTN_BLOCK = 512
ROW_TILE = 256
NEG_BIG = -1e30


def _cparams(*sem):
    return pltpu.CompilerParams(dimension_semantics=sem if sem else None, vmem_limit_bytes=VMEM_LIMIT_BYTES)


def _pick(n, pref):
    if n <= pref:
        return n
    best = None
    for t in range(LANES, pref + 1, LANES):
        if n % t == 0:
            best = t
    assert best is not None, (n, pref)
    return best


def _sigmoid(x):
    return 1.0 / (1.0 + jnp.exp(-x))


def _log1p(x):
    u = 1.0 + x
    d = u - 1.0
    return jnp.where(d == 0.0, x, jnp.log(u) * (x / jnp.where(d == 0.0, 1.0, d)))


def _expm1(x):
    u = jnp.exp(x)
    lu = jnp.log(u)
    return jnp.where(u == 1.0, x, (u - 1.0) * (x / jnp.where(u == 1.0, 1.0, lu)))


def _softplus(x):
    return jnp.maximum(x, 0.0) + _log1p(jnp.exp(-jnp.abs(x)))


_GELU_C = math.sqrt(2.0 / math.pi)


def _gelu(x):
    return 0.5 * x * (1.0 + jnp.tanh(_GELU_C * (x + 0.044715 * x * x * x)))


def _gelu_and_grad(x):
    t = jnp.tanh(_GELU_C * (x + 0.044715 * x * x * x))
    du = _GELU_C * (1.0 + 3.0 * 0.044715 * x * x)
    return 0.5 * x * (1.0 + t), 0.5 * (1.0 + t) + 0.5 * x * (1.0 - t * t) * du


_DOT_DIMS = {"nn": (((1,), (0,)), ((), ())), "nt": (((1,), (1,)), ((), ())), "tn": (((0,), (0,)), ((), ()))}


def _mm(a, b, mode, out_dtype, name, add=None, add_scale=1.0, tm=512, tn=1408, tk=1408):
    if mode == "nn":
        (m, k), (k2, n) = a.shape, b.shape
    elif mode == "nt":
        (m, k), (n, k2) = a.shape, b.shape
    else:
        (k, m), (k2, n) = a.shape, b.shape
    assert k == k2 and a.dtype == BF16 and b.dtype == BF16, (a.shape, b.shape, a.dtype, b.dtype)
    tm, tn, tk = _pick(m, tm), _pick(n, tn), _pick(k, tk)
    nk = k // tk
    dims = _DOT_DIMS[mode]

    def body(*refs):
        if add is None:
            a_ref, b_ref, o_ref, acc_ref = refs
        else:
            a_ref, b_ref, add_ref, o_ref, acc_ref = refs
        kk = pl.program_id(2)

        @pl.when(kk == 0)
        def _():
            acc_ref[...] = jnp.zeros_like(acc_ref)

        acc_ref[...] += lax.dot_general(a_ref[...], b_ref[...], dims, preferred_element_type=F32)

        @pl.when(kk == nk - 1)
        def _():
            r = acc_ref[...]
            if add is not None:
                r = r + add_scale * add_ref[...]
            o_ref[...] = r.astype(out_dtype)

    a_spec = (pl.BlockSpec((tk, tm), lambda j, i, kk: (kk, i)) if mode == "tn"
              else pl.BlockSpec((tm, tk), lambda j, i, kk: (i, kk)))
    b_spec = (pl.BlockSpec((tn, tk), lambda j, i, kk: (j, kk)) if mode == "nt"
              else pl.BlockSpec((tk, tn), lambda j, i, kk: (kk, j)))
    o_spec = pl.BlockSpec((tm, tn), lambda j, i, kk: (i, j))
    in_specs, args = [a_spec, b_spec], [a, b]
    if add is not None:
        assert add.shape == (m, n) and add.dtype == F32
        in_specs.append(o_spec)
        args.append(add)
    return pl.pallas_call(
        body, name=name, grid=(n // tn, m // tm, nk),
        in_specs=in_specs, out_specs=o_spec,
        out_shape=jax.ShapeDtypeStruct((m, n), out_dtype),
        scratch_shapes=[pltpu.VMEM((tm, tn), F32)],
        compiler_params=_cparams("parallel", "parallel", "arbitrary"),
    )(*args)


def _ln_fwd(x, m, g, b, name):
    t, d = x.shape
    bt = _pick(t, ROW_TILE)

    def body(x_ref, m_ref, g_ref, b_ref, y_ref, yb_ref, z_ref):
        z = ALPHA * x_ref[...] + m_ref[...]
        mu = jnp.mean(z, axis=-1, keepdims=True)
        zc = z - mu
        var = jnp.mean(zc * zc, axis=-1, keepdims=True)
        y = zc * lax.rsqrt(var + LN_EPS) * g_ref[...] + b_ref[...]
        y_ref[...] = y
        yb_ref[...] = y.astype(BF16)
        z_ref[...] = z

    row = pl.BlockSpec((bt, d), lambda i: (i, 0))
    vec = pl.BlockSpec((1, d), lambda i: (0, 0))
    return pl.pallas_call(
        body, name=name, grid=(t // bt,), in_specs=[row, row, vec, vec], out_specs=[row, row, row],
        out_shape=[jax.ShapeDtypeStruct((t, d), F32), jax.ShapeDtypeStruct((t, d), BF16),
                   jax.ShapeDtypeStruct((t, d), F32)],
        compiler_params=_cparams("parallel"),
    )(x, m, g.reshape(1, d), b.reshape(1, d))


def _ln_bwd(dy, z, g, name):
    t, d = dy.shape
    bt = _pick(t, ROW_TILE)

    def body(dy_ref, z_ref, g_ref, dz_ref, dzb_ref, dg_ref, db_ref):
        @pl.when(pl.program_id(0) == 0)
        def _():
            dg_ref[...] = jnp.zeros_like(dg_ref)
            db_ref[...] = jnp.zeros_like(db_ref)

        z = z_ref[...]
        dyv = dy_ref[...]
        mu = jnp.mean(z, axis=-1, keepdims=True)
        zc = z - mu
        var = jnp.mean(zc * zc, axis=-1, keepdims=True)
        rstd = lax.rsqrt(var + LN_EPS)
        xhat = zc * rstd
        dxh = dyv * g_ref[...]
        m1 = jnp.mean(dxh, axis=-1, keepdims=True)
        m2 = jnp.mean(dxh * xhat, axis=-1, keepdims=True)
        dz = rstd * (dxh - m1 - xhat * m2)
        dz_ref[...] = dz
        dzb_ref[...] = dz.astype(BF16)
        dg_ref[...] += jnp.sum(dyv * xhat, axis=0, keepdims=True)
        db_ref[...] += jnp.sum(dyv, axis=0, keepdims=True)

    row = pl.BlockSpec((bt, d), lambda i: (i, 0))
    vec = pl.BlockSpec((1, d), lambda i: (0, 0))
    return pl.pallas_call(
        body, name=name, grid=(t // bt,), in_specs=[row, row, vec], out_specs=[row, row, vec, vec],
        out_shape=[jax.ShapeDtypeStruct((t, d), F32), jax.ShapeDtypeStruct((t, d), BF16),
                   jax.ShapeDtypeStruct((1, d), F32), jax.ShapeDtypeStruct((1, d), F32)],
        compiler_params=_cparams("arbitrary"),
    )(dy, z, g.reshape(1, d))


def _rows_iota(shape):
    return lax.broadcasted_iota(jnp.int32, shape, 0)


def _shift_down(x, halo, s):
    if s == 0:
        return x
    rolled = pltpu.roll(x, s, axis=0)
    first = jnp.where(_rows_iota((SUBLANES, x.shape[1])) < s, pltpu.roll(halo, s, axis=0), rolled[:SUBLANES])
    return jnp.concatenate([first, rolled[SUBLANES:]], axis=0)


def _shift_up(x, halo, s):
    if s == 0:
        return x
    n = x.shape[0]
    rolled = pltpu.roll(x, n - s, axis=0)
    last = jnp.where(_rows_iota((SUBLANES, x.shape[1])) < SUBLANES - s, rolled[n - SUBLANES:],
                     pltpu.roll(halo, SUBLANES - s, axis=0))
    return jnp.concatenate([rolled[:n - SUBLANES], last], axis=0)


def _prev_halo_spec(bt, cb, col_off=0):
    r = bt // SUBLANES
    return pl.BlockSpec((SUBLANES, cb), lambda i, j: (jnp.maximum(i * r - 1, 0), j + col_off))


def _next_halo_spec(bt, cb, t, col_off=0):
    r = bt // SUBLANES
    last = t // SUBLANES - 1
    return pl.BlockSpec((SUBLANES, cb), lambda i, j: (jnp.minimum((i + 1) * r, last), j + col_off))


def _causal_conv(x, halo, w_ref, ksize):
    acc = None
    for j in range(ksize):
        term = w_ref[j:j + 1, :] * _shift_down(x, halo, ksize - 1 - j)
        acc = term if acc is None else acc + term
    return acc


def _anticausal_conv(y, halo, w_ref, ksize):
    acc = None
    for j in range(ksize):
        term = w_ref[j:j + 1, :] * _shift_up(y, halo, ksize - 1 - j)
        acc = term if acc is None else acc + term
    return acc


def _ffn_mid_fwd(h, cw, cb_, name):
    t = h.shape[0]
    bt, cb = _pick(t, ROW_TILE), _pick(D_FF, 1408)
    nc = D_FF // cb

    def body(hv_ref, hvh_ref, hg_ref, hgh_ref, wv_ref, wg_ref, bv_ref, bg_ref, y_ref):
        keep = (pl.program_id(0) > 0).astype(F32)
        val = _causal_conv(hv_ref[...], hvh_ref[...] * keep, wv_ref, CONV_F) + bv_ref[...]
        gate = _causal_conv(hg_ref[...], hgh_ref[...] * keep, wg_ref, CONV_F) + bg_ref[...]
        y_ref[...] = (_gelu(gate) * val).astype(BF16)

    tile_v = pl.BlockSpec((bt, cb), lambda i, j: (i, j))
    tile_g = pl.BlockSpec((bt, cb), lambda i, j: (i, j + nc))
    wv = pl.BlockSpec((CONV_F, cb), lambda i, j: (0, j))
    wg = pl.BlockSpec((CONV_F, cb), lambda i, j: (0, j + nc))
    bv = pl.BlockSpec((1, cb), lambda i, j: (0, j))
    bg = pl.BlockSpec((1, cb), lambda i, j: (0, j + nc))
    cb2 = cb_.reshape(1, 2 * D_FF)
    return pl.pallas_call(
        body, name=name, grid=(t // bt, nc),
        in_specs=[tile_v, _prev_halo_spec(bt, cb), tile_g, _prev_halo_spec(bt, cb, nc), wv, wg, bv, bg],
        out_specs=tile_v, out_shape=jax.ShapeDtypeStruct((t, D_FF), BF16),
        compiler_params=_cparams("parallel", "parallel"),
    )(h, h, h, h, cw, cw, cb2, cb2)


def _ffn_mid_bwd_a(h, dy, cw, cb_, name):
    t = h.shape[0]
    bt, cb = _pick(t, ROW_TILE), _pick(D_FF, 1408)
    nc = D_FF // cb

    def body(hv_ref, hvh_ref, hg_ref, hgh_ref, dy_ref, wv_ref, wg_ref, bv_ref, bg_ref,
             dv_ref, dg_ref, dwv_ref, dwg_ref, dbv_ref, dbg_ref):
        @pl.when(pl.program_id(1) == 0)
        def _():
            for r in (dwv_ref, dwg_ref, dbv_ref, dbg_ref):
                r[...] = jnp.zeros_like(r)

        keep = (pl.program_id(1) > 0).astype(F32)
        hv, hvh = hv_ref[...], hvh_ref[...] * keep
        hg, hgh = hg_ref[...], hgh_ref[...] * keep
        val = _causal_conv(hv, hvh, wv_ref, CONV_F) + bv_ref[...]
        gate = _causal_conv(hg, hgh, wg_ref, CONV_F) + bg_ref[...]
        gl, glg = _gelu_and_grad(gate)
        dyv = dy_ref[...]
        dval = dyv * gl
        dgate = dyv * val * glg
        dv_ref[...] = dval
        dg_ref[...] = dgate
        dbv_ref[...] += jnp.sum(dval, axis=0, keepdims=True)
        dbg_ref[...] += jnp.sum(dgate, axis=0, keepdims=True)
        for j in range(CONV_F):
            s = CONV_F - 1 - j
            dwv_ref[j:j + 1, :] += jnp.sum(dval * _shift_down(hv, hvh, s), axis=0, keepdims=True)
            dwg_ref[j:j + 1, :] += jnp.sum(dgate * _shift_down(hg, hgh, s), axis=0, keepdims=True)

    tile_v = pl.BlockSpec((bt, cb), lambda j, i: (i, j))
    tile_g = pl.BlockSpec((bt, cb), lambda j, i: (i, j + nc))
    r = bt // SUBLANES
    halo_v = pl.BlockSpec((SUBLANES, cb), lambda j, i: (jnp.maximum(i * r - 1, 0), j))
    halo_g = pl.BlockSpec((SUBLANES, cb), lambda j, i: (jnp.maximum(i * r - 1, 0), j + nc))
    wv = pl.BlockSpec((CONV_F, cb), lambda j, i: (0, j))
    wg = pl.BlockSpec((CONV_F, cb), lambda j, i: (0, j + nc))
    bv = pl.BlockSpec((1, cb), lambda j, i: (0, j))
    bg = pl.BlockSpec((1, cb), lambda j, i: (0, j + nc))
    cb2 = cb_.reshape(1, 2 * D_FF)
    dv, dg, dwv, dwg, dbv, dbg = pl.pallas_call(
        body, name=name, grid=(nc, t // bt),
        in_specs=[tile_v, halo_v, tile_g, halo_g, tile_v, wv, wg, bv, bg],
        out_specs=[tile_v, tile_v, wv, wv, bv, bv],
        out_shape=[jax.ShapeDtypeStruct((t, D_FF), F32), jax.ShapeDtypeStruct((t, D_FF), F32),
                   jax.ShapeDtypeStruct((CONV_F, D_FF), F32), jax.ShapeDtypeStruct((CONV_F, D_FF), F32),
                   jax.ShapeDtypeStruct((1, D_FF), F32), jax.ShapeDtypeStruct((1, D_FF), F32)],
        compiler_params=_cparams("parallel", "arbitrary"),
    )(h, h, h, h, dy, cw, cw, cb2, cb2)
    return dv, dg, jnp.concatenate([dwv, dwg], axis=1), jnp.concatenate([dbv, dbg], axis=1)


def _conv_bwd_x(dy, cw, name, out_dtype):
    t, c = dy.shape
    ksize = cw.shape[0]
    bt, cb = _pick(t, ROW_TILE), _pick(c, 1408)

    def body(dy_ref, halo_ref, w_ref, o_ref):
        keep = (pl.program_id(0) < t // bt - 1).astype(F32)
        o_ref[...] = _anticausal_conv(dy_ref[...], halo_ref[...] * keep, w_ref, ksize).astype(out_dtype)

    tile = pl.BlockSpec((bt, cb), lambda i, j: (i, j))
    return pl.pallas_call(
        body, name=name, grid=(t // bt, c // cb),
        in_specs=[tile, _next_halo_spec(bt, cb, t), pl.BlockSpec((ksize, cb), lambda i, j: (0, j))],
        out_specs=tile, out_shape=jax.ShapeDtypeStruct((t, c), out_dtype),
        compiler_params=_cparams("parallel", "parallel"),
    )(dy, dy, cw)


def _ple_fwd(x2, g, e, bg, name):
    t, d = x2.shape
    bt = _pick(t, ROW_TILE)

    def body(x_ref, g_ref, e_ref, b_ref, y_ref, yb_ref):
        y = x_ref[...] + _sigmoid(g_ref[...] + b_ref[...]) * e_ref[...]
        y_ref[...] = y
        yb_ref[...] = y.astype(BF16)

    row = pl.BlockSpec((bt, d), lambda i: (i, 0))
    vec = pl.BlockSpec((1, d), lambda i: (0, 0))
    return pl.pallas_call(
        body, name=name, grid=(t // bt,), in_specs=[row, row, row, vec], out_specs=[row, row],
        out_shape=[jax.ShapeDtypeStruct((t, d), F32), jax.ShapeDtypeStruct((t, d), BF16)],
        compiler_params=_cparams("parallel"),
    )(x2, g, e, bg.reshape(1, d))


def _ple_bwd(dx3, g, e, bg, name):
    t, d = dx3.shape
    bt = _pick(t, ROW_TILE)

    def body(dx_ref, g_ref, e_ref, b_ref, dg_ref, de_ref, db_ref):
        @pl.when(pl.program_id(0) == 0)
        def _():
            db_ref[...] = jnp.zeros_like(db_ref)

        dx = dx_ref[...]
        gate = _sigmoid(g_ref[...] + b_ref[...])
        dg = dx * e_ref[...] * gate * (1.0 - gate)
        dg_ref[...] = dg.astype(BF16)
        de_ref[...] = (dx * gate).astype(BF16)
        db_ref[...] += jnp.sum(dg, axis=0, keepdims=True)

    row = pl.BlockSpec((bt, d), lambda i: (i, 0))
    vec = pl.BlockSpec((1, d), lambda i: (0, 0))
    return pl.pallas_call(
        body, name=name, grid=(t // bt,), in_specs=[row, row, row, vec], out_specs=[row, row, vec],
        out_shape=[jax.ShapeDtypeStruct((t, d), BF16), jax.ShapeDtypeStruct((t, d), BF16),
                   jax.ShapeDtypeStruct((1, d), F32)],
        compiler_params=_cparams("arbitrary"),
    )(dx3, g, e, bg.reshape(1, d))


def _loss_head(y, target, name):
    t, d = y.shape
    bt = _pick(t, ROW_TILE)

    def body(y_ref, t_ref, dy_ref, l_ref, acc_ref):
        @pl.when(pl.program_id(0) == 0)
        def _():
            acc_ref[...] = jnp.zeros_like(acc_ref)

        err = y_ref[...] - t_ref[...]
        dy_ref[...] = err * (1.0 / d)
        acc_ref[...] += jnp.sum(err * err, axis=0, keepdims=True)

        @pl.when(pl.program_id(0) == t // bt - 1)
        def _():
            l_ref[...] = jnp.full(l_ref.shape, (0.5 / d) * jnp.sum(acc_ref[...]), F32)

    row = pl.BlockSpec((bt, d), lambda i: (i, 0))
    dy, l = pl.pallas_call(
        body, name=name, grid=(t // bt,), in_specs=[row, row],
        out_specs=[row, pl.BlockSpec((SUBLANES, LANES), lambda i: (0, 0))],
        out_shape=[jax.ShapeDtypeStruct((t, d), F32), jax.ShapeDtypeStruct((SUBLANES, LANES), F32)],
        scratch_shapes=[pltpu.VMEM((1, d), F32)],
        compiler_params=_cparams("arbitrary"),
    )(y, target)
    return dy, l[0, 0]


def _split3(x):
    hi = x.astype(BF16)
    r1 = x - hi.astype(F32)
    mid = r1.astype(BF16)
    lo = (r1 - mid.astype(F32)).astype(BF16)
    return hi, mid, lo


def _tri_dot(x, tri):
    hi, mid, lo = _split3(x)
    dims = (((1,), (0,)), ((), ()))
    return (lax.dot_general(hi, tri, dims, preferred_element_type=F32)
            + lax.dot_general(mid, tri, dims, preferred_element_type=F32)
            + lax.dot_general(lo, tri, dims, preferred_element_type=F32))


def _fgate_fwd(fg_rows, b_f, name):
    hh, t = fg_rows.shape
    bt = _pick(t, 512)

    def body(fg_ref, b_ref, c_ref, carry_ref):
        @pl.when(pl.program_id(0) == 0)
        def _():
            carry_ref[...] = jnp.zeros_like(carry_ref)

        xx = fg_ref[...] + b_ref[...]
        logf = jnp.minimum(xx, 0.0) - _log1p(jnp.exp(-jnp.abs(xx)))
        r = lax.broadcasted_iota(jnp.int32, (bt, bt), 0)
        c = lax.broadcasted_iota(jnp.int32, (bt, bt), 1)
        tri = (r <= c).astype(BF16)
        cs = _tri_dot(logf, tri) + carry_ref[...]
        c_ref[...] = cs
        carry_ref[...] = cs[:, bt - 1:bt]

    return pl.pallas_call(
        body, name=name, grid=(t // bt,),
        in_specs=[pl.BlockSpec((hh, bt), lambda i: (0, i)), pl.BlockSpec((hh, 1), lambda i: (0, 0))],
        out_specs=pl.BlockSpec((hh, bt), lambda i: (0, i)),
        out_shape=jax.ShapeDtypeStruct((hh, t), F32),
        scratch_shapes=[pltpu.VMEM((hh, 1), F32)],
        compiler_params=_cparams("arbitrary"),
    )(fg_rows, b_f.reshape(hh, 1))


def _fgate_bwd(dck_rows, dcq_rows, fg_rows, b_f, name):
    hh, t = fg_rows.shape
    bt = _pick(t, 512)
    nb = t // bt

    def body(dc_ref, dcq_ref, fg_ref, b_ref, dfg_ref, db_ref, carry_ref):
        @pl.when(pl.program_id(0) == 0)
        def _():
            carry_ref[...] = jnp.zeros_like(carry_ref)
            db_ref[...] = jnp.zeros_like(db_ref)

        r = lax.broadcasted_iota(jnp.int32, (bt, bt), 0)
        c = lax.broadcasted_iota(jnp.int32, (bt, bt), 1)
        tri = (r >= c).astype(BF16)
        dlogf = _tri_dot(dc_ref[...] + dcq_ref[...], tri) + carry_ref[...]
        carry_ref[...] = dlogf[:, 0:1]
        xx = fg_ref[...] + b_ref[...]
        dfg = dlogf * _sigmoid(-xx)
        dfg_ref[...] = dfg
        db_ref[...] += jnp.sum(dfg, axis=1, keepdims=True)

    blk = pl.BlockSpec((hh, bt), lambda i: (0, nb - 1 - i))
    vec = pl.BlockSpec((hh, 1), lambda i: (0, 0))
    return pl.pallas_call(
        body, name=name, grid=(nb,), in_specs=[blk, blk, blk, vec], out_specs=[blk, vec],
        out_shape=[jax.ShapeDtypeStruct((hh, t), F32), jax.ShapeDtypeStruct((hh, 1), F32)],
        scratch_shapes=[pltpu.VMEM((hh, 1), F32)],
        compiler_params=_cparams("arbitrary"),
    )(dck_rows, dcq_rows, fg_rows, b_f.reshape(hh, 1))


def _rows_to_cols(r):
    hh, t = r.shape
    return jnp.repeat(r.reshape(hh // 2, 2, t).transpose(0, 2, 1), HEAD_DIM, axis=-1)


def _cols_to_rows(c):
    npairs, t, _ = c.shape
    return c[:, :, ::HEAD_DIM].transpose(0, 2, 1).reshape(2 * npairs, t)


def _rows_to_pairs(r):
    hh, t = r.shape
    return jnp.pad(r.reshape(hh // 2, 2, t), ((0, 0), (0, SUBLANES - 2), (0, 0)))


def _head_masks(shape):
    first = lax.broadcasted_iota(jnp.int32, shape, 1) < HEAD_DIM
    return first


def _attn_fwd(qkv, c_rows, name):
    t = qkv.shape[0]
    bq = _pick(t, ATTN_BLOCK)
    nq = t // bq
    scale = 1.0 / math.sqrt(HEAD_DIM)

    def body(q_ref, k_ref, v_ref, c_ref, o_ref, of_ref, lse_ref, acc_ref, m_ref, l_ref):
        hp, i = pl.program_id(0), pl.program_id(1)
        first = _head_masks((bq, LANES))
        q2 = q_ref[...]
        zero = jnp.zeros_like(q2)
        qs = (jnp.where(first, q2, zero), jnp.where(first, zero, q2))
        q0 = pl.multiple_of(i * bq, bq)
        crefs = [c_ref[a:a + 1, pl.ds(q0, LANES)][:, 0:1] for a in (0, 1)]
        m_ref[...] = jnp.full(m_ref.shape, NEG_BIG, F32)
        l_ref[...] = jnp.zeros_like(l_ref)
        acc_ref[...] = jnp.zeros_like(acc_ref)
        causal = _rows_iota((bq, bq)) >= lax.broadcasted_iota(jnp.int32, (bq, bq), 1)

        def step(j, masked):
            k0 = pl.multiple_of(j * bq, bq)
            kb = k_ref[pl.ds(k0, bq), :]
            vb = v_ref[pl.ds(k0, bq), :]
            for a in (0, 1):
                s = lax.dot_general(qs[a], kb, _DOT_DIMS["nt"], preferred_element_type=F32) * scale
                s = s + (crefs[a] - c_ref[a:a + 1, pl.ds(k0, bq)])
                if masked:
                    s = jnp.where(causal, s, NEG_BIG)
                m_old = m_ref[a]
                m_new = jnp.maximum(m_old, jnp.max(s, axis=-1, keepdims=True))
                p = jnp.exp(s - m_new)
                alpha = jnp.exp(m_old - m_new)
                l_ref[a] = alpha * l_ref[a] + jnp.sum(p, axis=-1, keepdims=True)
                acc_ref[a] = alpha * acc_ref[a] + lax.dot_general(
                    p.astype(BF16), vb, _DOT_DIMS["nn"], preferred_element_type=F32)
                m_ref[a] = m_new

        def loop_body(j, carry):
            step(j, False)
            return carry

        lax.fori_loop(0, i, loop_body, 0)
        step(i, True)
        o = jnp.where(first, acc_ref[0] / l_ref[0], acc_ref[1] / l_ref[1])
        o_ref[...] = o.astype(BF16)
        of_ref[...] = o
        lse_ref[...] = jnp.where(first, m_ref[0] + jnp.log(l_ref[0]), m_ref[1] + jnp.log(l_ref[1]))

    return pl.pallas_call(
        body, name=name, grid=(N_PAIRS, nq),
        in_specs=[pl.BlockSpec((bq, LANES), lambda hp, i: (i, hp)),
                  pl.BlockSpec((t, LANES), lambda hp, i: (0, N_PAIRS + hp)),
                  pl.BlockSpec((t, LANES), lambda hp, i: (0, 2 * N_PAIRS + hp)),
                  pl.BlockSpec((None, SUBLANES, t), lambda hp, i: (hp, 0, 0))],
        out_specs=[pl.BlockSpec((bq, LANES), lambda hp, i: (i, hp)),
                   pl.BlockSpec((bq, LANES), lambda hp, i: (i, hp)),
                   pl.BlockSpec((None, bq, LANES), lambda hp, i: (hp, i, 0))],
        out_shape=[jax.ShapeDtypeStruct((t, D_MODEL), BF16), jax.ShapeDtypeStruct((t, D_MODEL), F32),
                   jax.ShapeDtypeStruct((N_PAIRS, t, LANES), F32)],
        scratch_shapes=[pltpu.VMEM((2, bq, LANES), F32), pltpu.VMEM((2, bq, 1), F32), pltpu.VMEM((2, bq, 1), F32)],
        compiler_params=_cparams("parallel", "arbitrary"),
    )(qkv, qkv, qkv, c_rows)


def _attn_delta(do, o, name):
    t = do.shape[0]
    bt = _pick(t, ROW_TILE)

    def body(do_ref, o_ref, d_ref, dob_ref):
        dob = do_ref[...].astype(BF16)
        prod = dob.astype(F32) * o_ref[...]
        first = _head_masks(prod.shape)
        da = jnp.sum(jnp.where(first, prod, 0.0), axis=-1, keepdims=True)
        db = jnp.sum(jnp.where(first, 0.0, prod), axis=-1, keepdims=True)
        d_ref[...] = jnp.where(first, da, db)
        dob_ref[...] = dob

    tile = pl.BlockSpec((bt, LANES), lambda i, hp: (i, hp))
    return pl.pallas_call(
        body, name=name, grid=(t // bt, N_PAIRS), in_specs=[tile, tile],
        out_specs=[pl.BlockSpec((None, bt, LANES), lambda i, hp: (hp, i, 0)), tile],
        out_shape=[jax.ShapeDtypeStruct((N_PAIRS, t, LANES), F32), jax.ShapeDtypeStruct((t, D_MODEL), BF16)],
        compiler_params=_cparams("parallel", "parallel"),
    )(do, o)


def _attn_bwd_dq(qkv, dob, c_rows, lse_cols, d_cols, name):
    t = qkv.shape[0]
    bq = _pick(t, ATTN_BLOCK)
    nq = t // bq
    scale = 1.0 / math.sqrt(HEAD_DIM)

    def body(q_ref, k_ref, v_ref, do_ref, c_ref, lse_ref, d_ref, dq_ref, rs_ref, acc_ref, rsa_ref):
        hp, i = pl.program_id(0), pl.program_id(1)
        first = _head_masks((bq, LANES))
        q2, do2 = q_ref[...], do_ref[...]
        zero = jnp.zeros_like(q2)
        qs = (jnp.where(first, q2, zero), jnp.where(first, zero, q2))
        dos = (jnp.where(first, do2, zero), jnp.where(first, zero, do2))
        q0 = pl.multiple_of(i * bq, bq)
        crefs = [c_ref[a:a + 1, pl.ds(q0, LANES)][:, 0:1] for a in (0, 1)]
        lses = (lse_ref[:, 0:1], lse_ref[:, HEAD_DIM:HEAD_DIM + 1])
        dds = (d_ref[:, 0:1], d_ref[:, HEAD_DIM:HEAD_DIM + 1])
        acc_ref[...] = jnp.zeros_like(acc_ref)
        rsa_ref[...] = jnp.zeros_like(rsa_ref)
        causal = _rows_iota((bq, bq)) >= lax.broadcasted_iota(jnp.int32, (bq, bq), 1)

        def step(j, masked):
            k0 = pl.multiple_of(j * bq, bq)
            kb = k_ref[pl.ds(k0, bq), :]
            vb = v_ref[pl.ds(k0, bq), :]
            for a in (0, 1):
                s = lax.dot_general(qs[a], kb, _DOT_DIMS["nt"], preferred_element_type=F32) * scale
                s = s + (crefs[a] - c_ref[a:a + 1, pl.ds(k0, bq)])
                p = jnp.exp(s - lses[a])
                if masked:
                    p = jnp.where(causal, p, 0.0)
                dp = lax.dot_general(dos[a], vb, _DOT_DIMS["nt"], preferred_element_type=F32)
                ds = p * (dp - dds[a])
                rsa_ref[a] += jnp.sum(ds, axis=-1, keepdims=True)
                acc_ref[a] += lax.dot_general(ds.astype(BF16), kb, _DOT_DIMS["nn"], preferred_element_type=F32)

        def loop_body(j, carry):
            step(j, False)
            return carry

        lax.fori_loop(0, i, loop_body, 0)
        step(i, True)
        dq_ref[...] = (jnp.where(first, acc_ref[0], acc_ref[1]) * scale).astype(BF16)
        rs_ref[...] = jnp.where(first, rsa_ref[0], rsa_ref[1])

    colblk = pl.BlockSpec((None, bq, LANES), lambda hp, i: (hp, i, 0))
    return pl.pallas_call(
        body, name=name, grid=(N_PAIRS, nq),
        in_specs=[pl.BlockSpec((bq, LANES), lambda hp, i: (i, hp)),
                  pl.BlockSpec((t, LANES), lambda hp, i: (0, N_PAIRS + hp)),
                  pl.BlockSpec((t, LANES), lambda hp, i: (0, 2 * N_PAIRS + hp)),
                  pl.BlockSpec((bq, LANES), lambda hp, i: (i, hp)),
                  pl.BlockSpec((None, SUBLANES, t), lambda hp, i: (hp, 0, 0)), colblk, colblk],
        out_specs=[pl.BlockSpec((bq, LANES), lambda hp, i: (i, hp)), colblk],
        out_shape=[jax.ShapeDtypeStruct((t, D_MODEL), BF16), jax.ShapeDtypeStruct((N_PAIRS, t, LANES), F32)],
        scratch_shapes=[pltpu.VMEM((2, bq, LANES), F32), pltpu.VMEM((2, bq, 1), F32)],
        compiler_params=_cparams("parallel", "arbitrary"),
    )(qkv, qkv, qkv, dob, c_rows, lse_cols, d_cols)


def _attn_bwd_dkv(qkv, dob, c_rows, c_cols, lse_rows, d_rows, name):
    t = qkv.shape[0]
    bk = _pick(t, ATTN_BLOCK)
    nk = t // bk
    scale = 1.0 / math.sqrt(HEAD_DIM)

    def body(q_ref, do_ref, k_ref, v_ref, c_ref, cc_ref, lse_ref, d_ref, dk_ref, dv_ref, dc_ref,
             dka_ref, dva_ref, dca_ref):
        hp, j = pl.program_id(0), pl.program_id(1)
        first = _head_masks((bk, LANES))
        k2, v2 = k_ref[...], v_ref[...]
        zero = jnp.zeros_like(k2)
        ks = (jnp.where(first, k2, zero), jnp.where(first, zero, k2))
        vs = (jnp.where(first, v2, zero), jnp.where(first, zero, v2))
        ccols = (cc_ref[:, 0:1], cc_ref[:, HEAD_DIM:HEAD_DIM + 1])
        dka_ref[...] = jnp.zeros_like(dka_ref)
        dva_ref[...] = jnp.zeros_like(dva_ref)
        dca_ref[...] = jnp.zeros_like(dca_ref)
        causal = _rows_iota((bk, bk)) <= lax.broadcasted_iota(jnp.int32, (bk, bk), 1)

        def step(i, masked):
            q0 = pl.multiple_of(i * bk, bk)
            qb = q_ref[pl.ds(q0, bk), :]
            dob_ = do_ref[pl.ds(q0, bk), :]
            for a in (0, 1):
                cref = c_ref[a:a + 1, pl.ds(q0, LANES)][:, 0:1]
                st = lax.dot_general(ks[a], qb, _DOT_DIMS["nt"], preferred_element_type=F32) * scale
                st = st + (cref - ccols[a])
                pt = jnp.exp(st - lse_ref[a:a + 1, pl.ds(q0, bk)])
                if masked:
                    pt = jnp.where(causal, pt, 0.0)
                dpt = lax.dot_general(vs[a], dob_, _DOT_DIMS["nt"], preferred_element_type=F32)
                dst = pt * (dpt - d_ref[a:a + 1, pl.ds(q0, bk)])
                dva_ref[a] += lax.dot_general(pt.astype(BF16), dob_, _DOT_DIMS["nn"], preferred_element_type=F32)
                dka_ref[a] += lax.dot_general(dst.astype(BF16), qb, _DOT_DIMS["nn"], preferred_element_type=F32)
                dca_ref[a] += jnp.sum(dst, axis=-1, keepdims=True)

        def loop_body(i, carry):
            step(i, False)
            return carry

        step(j, True)
        lax.fori_loop(j + 1, nk, loop_body, 0)
        dk_ref[...] = (jnp.where(first, dka_ref[0], dka_ref[1]) * scale).astype(BF16)
        dv_ref[...] = jnp.where(first, dva_ref[0], dva_ref[1]).astype(BF16)
        dc_ref[...] = jnp.where(first, -dca_ref[0], -dca_ref[1])

    rows = pl.BlockSpec((None, SUBLANES, t), lambda hp, j: (hp, 0, 0))
    kv_out = pl.BlockSpec((bk, LANES), lambda hp, j: (j, hp))
    colblk = pl.BlockSpec((None, bk, LANES), lambda hp, j: (hp, j, 0))
    return pl.pallas_call(
        body, name=name, grid=(N_PAIRS, nk),
        in_specs=[pl.BlockSpec((t, LANES), lambda hp, j: (0, hp)),
                  pl.BlockSpec((t, LANES), lambda hp, j: (0, hp)),
                  pl.BlockSpec((bk, LANES), lambda hp, j: (j, N_PAIRS + hp)),
                  pl.BlockSpec((bk, LANES), lambda hp, j: (j, 2 * N_PAIRS + hp)),
                  rows, colblk, rows, rows],
        out_specs=[kv_out, kv_out, colblk],
        out_shape=[jax.ShapeDtypeStruct((t, D_MODEL), BF16), jax.ShapeDtypeStruct((t, D_MODEL), BF16),
                   jax.ShapeDtypeStruct((N_PAIRS, t, LANES), F32)],
        scratch_shapes=[pltpu.VMEM((2, bk, LANES), F32), pltpu.VMEM((2, bk, LANES), F32),
                        pltpu.VMEM((2, bk, 1), F32)],
        compiler_params=_cparams("parallel", "arbitrary"),
    )(qkv, dob, qkv, qkv, c_rows, c_cols, lse_rows, d_rows)


AUG_C = HEAD_DIM
AUG_ONE = HEAD_DIM + 3


def _attn_prep(qkv, c_cols, name):
    t = qkv.shape[0]
    bt = _pick(t, ATTN_BLOCK)
    scale = 1.0 / math.sqrt(HEAD_DIM)

    def body(q_ref, k_ref, c_ref, qh_ref, kh_ref):
        lane = lax.broadcasted_iota(jnp.int32, (bt, LANES), 1)
        q2 = q_ref[...].astype(F32) * scale
        k2 = k_ref[...].astype(F32)
        c2 = c_ref[...]
        parts = [p.astype(F32) for p in _split3(c2 - c2[0:1, :])]
        swapped = [pltpu.roll(p, HEAD_DIM, axis=1) for p in parts]
        for a in (0, 1):
            qa = q2 if a == 0 else pltpu.roll(q2, HEAD_DIM, axis=1)
            ka = k2 if a == 0 else pltpu.roll(k2, HEAD_DIM, axis=1)
            hi, mid, lo = swapped if a == 0 else parts
            kaug = jnp.where(lane < HEAD_DIM, ka,
                             jnp.where(lane == AUG_C, hi,
                                       jnp.where(lane == AUG_C + 1, mid,
                                                 jnp.where(lane == AUG_C + 2, lo,
                                                           jnp.where(lane == AUG_ONE, 1.0, 0.0)))))
            qaug = jnp.where(lane < HEAD_DIM, qa, jnp.where(lane < AUG_ONE, -1.0, 0.0))
            qh_ref[:, a * LANES:(a + 1) * LANES] = qaug.astype(BF16)
            kh_ref[:, a * LANES:(a + 1) * LANES] = kaug.astype(BF16)

    out = pl.BlockSpec((bt, 2 * LANES), lambda i, hp: (i, hp))
    shape = jax.ShapeDtypeStruct((t, N_HEADS * LANES), BF16)
    return pl.pallas_call(
        body, name=name, grid=(t // bt, N_PAIRS),
        in_specs=[pl.BlockSpec((bt, LANES), lambda i, hp: (i, hp)),
                  pl.BlockSpec((bt, LANES), lambda i, hp: (i, N_PAIRS + hp)),
                  pl.BlockSpec((None, bt, LANES), lambda i, hp: (hp, i, 0))],
        out_specs=[out, out], out_shape=[shape, shape],
        compiler_params=_cparams("parallel", "parallel"),
    )(qkv, qkv, c_cols)


def _block_scalar(c_ref, a, start):
    return c_ref[a:a + 1, pl.ds(start, LANES)][:, 0:1]


def _attn_fwd_t(qh, kh, vt, c_pairs, name):
    t = qh.shape[0]
    bq = _pick(t, ATTN_BLOCK)
    nq = t // bq

    def body(q_ref, k_ref, vt_ref, c_ref, ot_ref, otb_ref, lse_ref, acc_ref):
        i = pl.program_id(1)
        q0 = pl.multiple_of(i * bq, bq)
        qs = (q_ref[:, 0:LANES], q_ref[:, LANES:2 * LANES])
        cq = [_block_scalar(c_ref, a, q0) for a in (0, 1)]
        acc_ref[...] = jnp.zeros_like(acc_ref)
        keep = _rows_iota((bq, bq)) <= lax.broadcasted_iota(jnp.int32, (bq, bq), 1)

        def step(j, carry, masked):
            k0 = pl.multiple_of(j * bq, bq)
            kb = k_ref[pl.ds(k0, bq), :]
            new = []
            for a in (0, 1):
                m_old, l_old = carry[2 * a], carry[2 * a + 1]
                st = lax.dot_general(kb[:, a * LANES:(a + 1) * LANES], qs[a], _DOT_DIMS["nt"],
                                     preferred_element_type=F32)
                if masked:
                    st = jnp.where(keep, st, NEG_BIG)
                sigma = cq[a] - _block_scalar(c_ref, a, k0)
                m_new = jnp.maximum(m_old, jnp.max(st, axis=0, keepdims=True) + sigma)
                pt = jnp.exp(st - (m_new - sigma))
                alpha = jnp.exp(m_old - m_new)
                l_new = alpha * l_old + jnp.sum(pt, axis=0, keepdims=True)
                vta = vt_ref[a * HEAD_DIM:(a + 1) * HEAD_DIM, pl.ds(k0, bq)]
                acc_ref[a] = alpha * acc_ref[a] + lax.dot_general(
                    vta, pt.astype(BF16), _DOT_DIMS["nn"], preferred_element_type=F32)
                new += [m_new, l_new]
            return tuple(new)

        neg = jnp.full((1, bq), NEG_BIG, F32)
        zero = jnp.zeros((1, bq), F32)
        carry = lax.fori_loop(0, i, lambda j, c: step(j, c, False), (neg, zero, neg, zero))
        m_a, l_a, m_b, l_b = step(i, carry, True)
        ot = jnp.concatenate([acc_ref[0] / l_a, acc_ref[1] / l_b], axis=0)
        ot_ref[...] = ot
        otb_ref[...] = ot.astype(BF16)
        lse_ref[...] = jnp.zeros_like(lse_ref)
        lse_ref[0:1, :] = m_a + jnp.log(l_a)
        lse_ref[1:2, :] = m_b + jnp.log(l_b)

    rows = pl.BlockSpec((None, SUBLANES, bq), lambda hp, i: (hp, 0, i))
    otile = pl.BlockSpec((LANES, bq), lambda hp, i: (hp, i))
    return pl.pallas_call(
        body, name=name, grid=(N_PAIRS, nq),
        in_specs=[pl.BlockSpec((bq, 2 * LANES), lambda hp, i: (i, hp)),
                  pl.BlockSpec((t, 2 * LANES), lambda hp, i: (0, hp)),
                  pl.BlockSpec((LANES, t), lambda hp, i: (hp, 0)),
                  pl.BlockSpec((None, SUBLANES, t), lambda hp, i: (hp, 0, 0))],
        out_specs=[otile, otile, rows],
        out_shape=[jax.ShapeDtypeStruct((D_MODEL, t), F32), jax.ShapeDtypeStruct((D_MODEL, t), BF16),
                   jax.ShapeDtypeStruct((N_PAIRS, SUBLANES, t), F32)],
        scratch_shapes=[pltpu.VMEM((2, HEAD_DIM, bq), F32)],
        compiler_params=_cparams("parallel", "arbitrary"),
    )(qh, kh, vt, c_pairs)


def _attn_delta_t(dot, ot, name):
    t = dot.shape[1]
    bt = _pick(t, 512)

    def body(do_ref, o_ref, d_ref, dob_ref):
        dob = do_ref[...].astype(BF16)
        prod = dob.astype(F32) * o_ref[...]
        d_ref[...] = jnp.zeros_like(d_ref)
        d_ref[0:1, :] = jnp.sum(prod[0:HEAD_DIM], axis=0, keepdims=True)
        d_ref[1:2, :] = jnp.sum(prod[HEAD_DIM:], axis=0, keepdims=True)
        dob_ref[...] = dob

    tile = pl.BlockSpec((LANES, bt), lambda hp, i: (hp, i))
    return pl.pallas_call(
        body, name=name, grid=(N_PAIRS, t // bt), in_specs=[tile, tile],
        out_specs=[pl.BlockSpec((None, SUBLANES, bt), lambda hp, i: (hp, 0, i)), tile],
        out_shape=[jax.ShapeDtypeStruct((N_PAIRS, SUBLANES, t), F32), jax.ShapeDtypeStruct((D_MODEL, t), BF16)],
        compiler_params=_cparams("parallel", "parallel"),
    )(dot, ot)


def _attn_dq_t(qh, kh, kt, qkv, dotb, c_pairs, lse_pairs, d_pairs, name):
    t = qh.shape[0]
    bq = _pick(t, ATTN_BLOCK)
    nq = t // bq
    scale = 1.0 / math.sqrt(HEAD_DIM)

    def body(q_ref, k_ref, kt_ref, v_ref, do_ref, c_ref, lse_ref, d_ref, dq_ref, rs_ref, acc_ref):
        i = pl.program_id(1)
        q0 = pl.multiple_of(i * bq, bq)
        qs = (q_ref[:, 0:LANES], q_ref[:, LANES:2 * LANES])
        do2 = do_ref[...]
        top = _rows_iota((LANES, bq)) < HEAD_DIM
        zero = jnp.zeros_like(do2)
        dos = (jnp.where(top, do2, zero), jnp.where(top, zero, do2))
        cq = [_block_scalar(c_ref, a, q0) for a in (0, 1)]
        lses = (lse_ref[0:1, :], lse_ref[1:2, :])
        dds = (d_ref[0:1, :], d_ref[1:2, :])
        acc_ref[...] = jnp.zeros_like(acc_ref)
        keep = _rows_iota((bq, bq)) <= lax.broadcasted_iota(jnp.int32, (bq, bq), 1)

        def step(j, masked):
            k0 = pl.multiple_of(j * bq, bq)
            kb = k_ref[pl.ds(k0, bq), :]
            vb = v_ref[pl.ds(k0, bq), :]
            for a in (0, 1):
                st = lax.dot_general(kb[:, a * LANES:(a + 1) * LANES], qs[a], _DOT_DIMS["nt"],
                                     preferred_element_type=F32)
                sigma = cq[a] - _block_scalar(c_ref, a, k0)
                pt = jnp.exp(st - (lses[a] - sigma))
                if masked:
                    pt = jnp.where(keep, pt, 0.0)
                dpt = lax.dot_general(vb, dos[a], _DOT_DIMS["nn"], preferred_element_type=F32)
                dst = pt * (dpt - dds[a])
                acc_ref[a] += lax.dot_general(kt_ref[a * LANES:(a + 1) * LANES, pl.ds(k0, bq)], dst.astype(BF16),
                                              _DOT_DIMS["nn"], preferred_element_type=F32)

        def loop_body(j, carry):
            step(j, False)
            return carry

        lax.fori_loop(0, i, loop_body, 0)
        step(i, True)
        dq_ref[...] = (jnp.concatenate([acc_ref[0, 0:HEAD_DIM, :], acc_ref[1, 0:HEAD_DIM, :]], axis=0)
                       * scale).astype(BF16)
        rs_ref[...] = jnp.zeros_like(rs_ref)
        rs_ref[0:1, :] = acc_ref[0, AUG_ONE:AUG_ONE + 1, :]
        rs_ref[1:2, :] = acc_ref[1, AUG_ONE:AUG_ONE + 1, :]

    rows_full = pl.BlockSpec((None, SUBLANES, t), lambda hp, i: (hp, 0, 0))
    rows = pl.BlockSpec((None, SUBLANES, bq), lambda hp, i: (hp, 0, i))
    ttile = pl.BlockSpec((LANES, bq), lambda hp, i: (hp, i))
    return pl.pallas_call(
        body, name=name, grid=(N_PAIRS, nq),
        in_specs=[pl.BlockSpec((bq, 2 * LANES), lambda hp, i: (i, hp)),
                  pl.BlockSpec((t, 2 * LANES), lambda hp, i: (0, hp)),
                  pl.BlockSpec((2 * LANES, t), lambda hp, i: (hp, 0)),
                  pl.BlockSpec((t, LANES), lambda hp, i: (0, 2 * N_PAIRS + hp)),
                  ttile, rows_full, rows, rows],
        out_specs=[ttile, rows],
        out_shape=[jax.ShapeDtypeStruct((D_MODEL, t), BF16), jax.ShapeDtypeStruct((N_PAIRS, SUBLANES, t), F32)],
        scratch_shapes=[pltpu.VMEM((2, LANES, bq), F32)],
        compiler_params=_cparams("parallel", "arbitrary"),
    )(qh, kh, kt, qkv, dotb, c_pairs, lse_pairs, d_pairs)


def _attn_dkv_t(qh, kh, qkv, dob, dotb, c_pairs, lse_pairs, d_pairs, name):
    t = qh.shape[0]
    bk = _pick(t, ATTN_BLOCK)
    nk = t // bk

    def body(k_ref, v_ref, q_ref, do_ref, dot_ref, c_ref, lse_ref, d_ref, dk_ref, dv_ref, dc_ref, dka_ref, dva_ref):
        j = pl.program_id(1)
        k0 = pl.multiple_of(j * bk, bk)
        ks = (k_ref[:, 0:LANES], k_ref[:, LANES:2 * LANES])
        vb = v_ref[...]
        ck = [_block_scalar(c_ref, a, k0) for a in (0, 1)]
        dka_ref[...] = jnp.zeros_like(dka_ref)
        dva_ref[...] = jnp.zeros_like(dva_ref)
        keep = _rows_iota((bk, bk)) <= lax.broadcasted_iota(jnp.int32, (bk, bk), 1)
        top = _rows_iota((LANES, bk)) < HEAD_DIM

        def step(i, masked):
            q0 = pl.multiple_of(i * bk, bk)
            qb = q_ref[pl.ds(q0, bk), :]
            dob_ = do_ref[pl.ds(q0, bk), :]
            dot2 = dot_ref[:, pl.ds(q0, bk)]
            zero = jnp.zeros_like(dot2)
            for a in (0, 1):
                qa = qb[:, a * LANES:(a + 1) * LANES]
                st = lax.dot_general(ks[a], qa, _DOT_DIMS["nt"], preferred_element_type=F32)
                sigma = _block_scalar(c_ref, a, q0) - ck[a]
                pt = jnp.exp(st - (lse_ref[a:a + 1, pl.ds(q0, bk)] - sigma))
                if masked:
                    pt = jnp.where(keep, pt, 0.0)
                dota = jnp.where(top, dot2, zero) if a == 0 else jnp.where(top, zero, dot2)
                dpt = lax.dot_general(vb, dota, _DOT_DIMS["nn"], preferred_element_type=F32)
                dst = pt * (dpt - d_ref[a:a + 1, pl.ds(q0, bk)])
                dva_ref[a] += lax.dot_general(pt.astype(BF16), dob_, _DOT_DIMS["nn"], preferred_element_type=F32)
                dka_ref[a] += lax.dot_general(dst.astype(BF16), qa, _DOT_DIMS["nn"], preferred_element_type=F32)

        def loop_body(i, carry):
            step(i, False)
            return carry

        step(j, True)
        lax.fori_loop(j + 1, nk, loop_body, 0)
        first = lax.broadcasted_iota(jnp.int32, (bk, LANES), 1) < HEAD_DIM
        dka, dkb = dka_ref[0], dka_ref[1]
        dk_ref[...] = jnp.where(first, dka, pltpu.roll(dkb, HEAD_DIM, axis=1)).astype(BF16)
        dv_ref[...] = jnp.where(first, dva_ref[0], dva_ref[1]).astype(BF16)
        dc_ref[...] = jnp.where(first, dka[:, AUG_C:AUG_C + 1], dkb[:, AUG_C:AUG_C + 1])

    rows = pl.BlockSpec((None, SUBLANES, t), lambda hp, j: (hp, 0, 0))
    kv_out = pl.BlockSpec((bk, LANES), lambda hp, j: (j, hp))
    return pl.pallas_call(
        body, name=name, grid=(N_PAIRS, nk),
        in_specs=[pl.BlockSpec((bk, 2 * LANES), lambda hp, j: (j, hp)),
                  pl.BlockSpec((bk, LANES), lambda hp, j: (j, 2 * N_PAIRS + hp)),
                  pl.BlockSpec((t, 2 * LANES), lambda hp, j: (0, hp)),
                  pl.BlockSpec((t, LANES), lambda hp, j: (0, hp)),
                  pl.BlockSpec((LANES, t), lambda hp, j: (hp, 0)),
                  rows, rows, rows],
        out_specs=[kv_out, kv_out, pl.BlockSpec((None, bk, LANES), lambda hp, j: (hp, j, 0))],
        out_shape=[jax.ShapeDtypeStruct((t, D_MODEL), BF16), jax.ShapeDtypeStruct((t, D_MODEL), BF16),
                   jax.ShapeDtypeStruct((N_PAIRS, t, LANES), F32)],
        scratch_shapes=[pltpu.VMEM((2, bk, LANES), F32), pltpu.VMEM((2, bk, LANES), F32)],
        compiler_params=_cparams("parallel", "arbitrary"),
    )(kh, qkv, qh, dob, dotb, c_pairs, lse_pairs, d_pairs)


def _scan(a, u, name, reverse=False):
    t, c = a.shape
    bt, cb = _pick(t, ROW_TILE), _pick(c, 1024)
    nt = t // bt
    ngroups = bt // SUBLANES

    def body(a_ref, u_ref, h_ref, carry_ref, as_ref, us_ref):
        @pl.when(pl.program_id(1) == 0)
        def _():
            carry_ref[...] = jnp.zeros_like(carry_ref)

        av, uv = a_ref[...], u_ref[...]
        sub = _rows_iota((bt, cb)) % SUBLANES
        for s in (1, 2, 4):
            if reverse:
                a_sh, u_sh = pltpu.roll(av, bt - s, axis=0), pltpu.roll(uv, bt - s, axis=0)
                valid = sub < SUBLANES - s
            else:
                a_sh, u_sh = pltpu.roll(av, s, axis=0), pltpu.roll(uv, s, axis=0)
                valid = sub >= s
            uv = jnp.where(valid, uv + av * u_sh, uv)
            av = jnp.where(valid, av * a_sh, av)
        as_ref[...] = av
        us_ref[...] = uv
        edge = 0 if reverse else SUBLANES - 1
        pick = _rows_iota((SUBLANES, cb)) == edge

        def group(gi, carry):
            g = (ngroups - 1 - gi) if reverse else gi
            r0 = pl.multiple_of(g * SUBLANES, SUBLANES)
            h8 = us_ref[pl.ds(r0, SUBLANES), :] + as_ref[pl.ds(r0, SUBLANES), :] * carry
            h_ref[pl.ds(r0, SUBLANES), :] = h8
            return jnp.sum(jnp.where(pick, h8, 0.0), axis=0, keepdims=True)

        carry_ref[...] = lax.fori_loop(0, ngroups, group, carry_ref[...])

    if reverse:
        tile = pl.BlockSpec((bt, cb), lambda j, i: (nt - 1 - i, j))
    else:
        tile = pl.BlockSpec((bt, cb), lambda j, i: (i, j))
    return pl.pallas_call(
        body, name=name, grid=(c // cb, nt), in_specs=[tile, tile], out_specs=tile,
        out_shape=jax.ShapeDtypeStruct((t, c), F32),
        scratch_shapes=[pltpu.VMEM((1, cb), F32), pltpu.VMEM((bt, cb), F32), pltpu.VMEM((bt, cb), F32)],
        compiler_params=_cparams("parallel", "arbitrary"),
    )(a, u)


def _rg_conv_fwd(proj, cw, cb_, name):
    t = proj.shape[0]
    bt, cb = _pick(t, ROW_TILE), D_MODEL

    def body(x_ref, halo_ref, w_ref, b_ref, o_ref):
        keep = (pl.program_id(0) > 0).astype(F32)
        o_ref[...] = _causal_conv(x_ref[...], halo_ref[...] * keep, w_ref, CONV_B) + b_ref[...]

    tile = pl.BlockSpec((bt, cb), lambda i, j: (i, j))
    return pl.pallas_call(
        body, name=name, grid=(t // bt, 1),
        in_specs=[tile, _prev_halo_spec(bt, cb), pl.BlockSpec((CONV_B, cb), lambda i, j: (0, 0)),
                  pl.BlockSpec((1, cb), lambda i, j: (0, 0))],
        out_specs=tile, out_shape=jax.ShapeDtypeStruct((t, D_MODEL), F32),
        compiler_params=_cparams("parallel", "parallel"),
    )(proj, proj, cw, cb_.reshape(1, D_MODEL))


def _conv_bwd_w(x, dy, ksize, name):
    t, c = dy.shape
    bt = _pick(t, ROW_TILE)
    r = bt // SUBLANES

    def body(x_ref, halo_ref, dy_ref, dw_ref, db_ref):
        @pl.when(pl.program_id(0) == 0)
        def _():
            dw_ref[...] = jnp.zeros_like(dw_ref)
            db_ref[...] = jnp.zeros_like(db_ref)

        keep = (pl.program_id(0) > 0).astype(F32)
        xv, halo, dyv = x_ref[...], halo_ref[...] * keep, dy_ref[...]
        db_ref[...] += jnp.sum(dyv, axis=0, keepdims=True)
        for j in range(ksize):
            dw_ref[j:j + 1, :] += jnp.sum(dyv * _shift_down(xv, halo, ksize - 1 - j), axis=0, keepdims=True)

    tile = pl.BlockSpec((bt, c), lambda i: (i, 0))
    return pl.pallas_call(
        body, name=name, grid=(t // bt,),
        in_specs=[tile, pl.BlockSpec((SUBLANES, c), lambda i: (jnp.maximum(i * r - 1, 0), 0)), tile],
        out_specs=[pl.BlockSpec((ksize, c), lambda i: (0, 0)), pl.BlockSpec((1, c), lambda i: (0, 0))],
        out_shape=[jax.ShapeDtypeStruct((ksize, c), F32), jax.ShapeDtypeStruct((1, c), F32)],
        compiler_params=_cparams("arbitrary"),
    )(x, x, dy)


def _rg_gate_math(xc, wa_ref, wi_ref, ba_ref, bi_ref, lam_ref):
    xb = xc.astype(BF16)
    ra = lax.dot_general(xb, wa_ref[...], _DOT_DIMS["nn"], preferred_element_type=F32) + ba_ref[...]
    ia = lax.dot_general(xb, wi_ref[...], _DOT_DIMS["nn"], preferred_element_type=F32) + bi_ref[...]
    r, ig = _sigmoid(ra), _sigmoid(ia)
    sp = _softplus(-lam_ref[...])
    log_a = -LRU_C * r * sp
    a = jnp.exp(log_a)
    mult = jnp.sqrt(-_expm1(2.0 * log_a))
    return xb, r, ig, sp, a, mult


def _rg_gate_specs(bt, time_first):
    if time_first:
        tile = pl.BlockSpec((bt, BLOCK_B), lambda i, n: (i, n))
        w = pl.BlockSpec((None, BLOCK_B, BLOCK_B), lambda i, n: (n, 0, 0))
        v = pl.BlockSpec((None, 1, BLOCK_B), lambda i, n: (n, 0, 0))
    else:
        tile = pl.BlockSpec((bt, BLOCK_B), lambda n, i: (i, n))
        w = pl.BlockSpec((None, BLOCK_B, BLOCK_B), lambda n, i: (n, 0, 0))
        v = pl.BlockSpec((None, 1, BLOCK_B), lambda n, i: (n, 0, 0))
    return tile, w, v


def _rg_gate_fwd(xc, wa, ba, wi, bi, lam, name):
    t = xc.shape[0]
    bt = _pick(t, 512)

    def body(x_ref, wa_ref, wi_ref, ba_ref, bi_ref, lam_ref, a_ref, u_ref):
        xcv = x_ref[...]
        _, _, ig, _, a, mult = _rg_gate_math(xcv, wa_ref, wi_ref, ba_ref, bi_ref, lam_ref)
        a_ref[...] = a
        u_ref[...] = mult * (ig * xcv)

    tile, w, v = _rg_gate_specs(bt, True)
    return pl.pallas_call(
        body, name=name, grid=(t // bt, N_BLOCKS_B), in_specs=[tile, w, w, v, v, v], out_specs=[tile, tile],
        out_shape=[jax.ShapeDtypeStruct((t, D_MODEL), F32), jax.ShapeDtypeStruct((t, D_MODEL), F32)],
        compiler_params=_cparams("parallel", "parallel"),
    )(xc, wa, wi, ba, bi, lam)


def _rg_gate_bwd(xc, g, h, wa, ba, wi, bi, lam, name):
    t = xc.shape[0]
    bt = _pick(t, 512)
    rr = bt // SUBLANES

    def body(x_ref, g_ref, h_ref, hh_ref, wa_ref, wi_ref, ba_ref, bi_ref, lam_ref,
             dx_ref, dwa_ref, dwi_ref, dba_ref, dbi_ref, dlam_ref):
        @pl.when(pl.program_id(1) == 0)
        def _():
            for ref in (dwa_ref, dwi_ref, dba_ref, dbi_ref, dlam_ref):
                ref[...] = jnp.zeros_like(ref)

        keep = (pl.program_id(1) > 0).astype(F32)
        xcv, gv = x_ref[...], g_ref[...]
        xb, r, ig, sp, a, mult = _rg_gate_math(xcv, wa_ref, wi_ref, ba_ref, bi_ref, lam_ref)
        h_prev = _shift_down(h_ref[...], hh_ref[...] * keep, 1)
        da = gv * h_prev
        dmult = gv * ig * xcv
        dig = gv * mult * xcv
        dxc = gv * mult * ig
        dlog_a = da * a - dmult * (a * a / mult)
        dr = dlog_a * (-LRU_C * sp)
        dsp = jnp.sum(dlog_a * (-LRU_C * r), axis=0, keepdims=True)
        dlam_ref[...] += dsp * (-_sigmoid(-lam_ref[...]))
        dra = dr * r * (1.0 - r)
        dia = dig * ig * (1.0 - ig)
        dba_ref[...] += jnp.sum(dra, axis=0, keepdims=True)
        dbi_ref[...] += jnp.sum(dia, axis=0, keepdims=True)
        drab, diab = dra.astype(BF16), dia.astype(BF16)
        dwa_ref[...] += lax.dot_general(xb, drab, _DOT_DIMS["tn"], preferred_element_type=F32)
        dwi_ref[...] += lax.dot_general(xb, diab, _DOT_DIMS["tn"], preferred_element_type=F32)
        dxc = dxc + lax.dot_general(drab, wa_ref[...], _DOT_DIMS["nt"], preferred_element_type=F32)
        dxc = dxc + lax.dot_general(diab, wi_ref[...], _DOT_DIMS["nt"], preferred_element_type=F32)
        dx_ref[...] = dxc

    tile, w, v = _rg_gate_specs(bt, False)
    halo = pl.BlockSpec((SUBLANES, BLOCK_B), lambda n, i: (jnp.maximum(i * rr - 1, 0), n))
    wshape = jax.ShapeDtypeStruct((N_BLOCKS_B, BLOCK_B, BLOCK_B), F32)
    vshape = jax.ShapeDtypeStruct((N_BLOCKS_B, 1, BLOCK_B), F32)
    return pl.pallas_call(
        body, name=name, grid=(N_BLOCKS_B, t // bt),
        in_specs=[tile, tile, tile, halo, w, w, v, v, v],
        out_specs=[tile, w, w, v, v, v],
        out_shape=[jax.ShapeDtypeStruct((t, D_MODEL), F32), wshape, wshape, vshape, vshape, vshape],
        compiler_params=_cparams("parallel", "arbitrary"),
    )(xc, g, h, h, wa, wi, ba, bi, lam)


def _rg_out_fwd(h, proj, name):
    t = h.shape[0]
    bt = _pick(t, ROW_TILE)

    def body(h_ref, g_ref, y_ref):
        y_ref[...] = (h_ref[...] * _gelu(g_ref[...])).astype(BF16)

    tile = pl.BlockSpec((bt, D_MODEL), lambda i: (i, 0))
    return pl.pallas_call(
        body, name=name, grid=(t // bt,), in_specs=[tile, pl.BlockSpec((bt, D_MODEL), lambda i: (i, 1))],
        out_specs=tile, out_shape=jax.ShapeDtypeStruct((t, D_MODEL), BF16),
        compiler_params=_cparams("parallel"),
    )(h, proj)


def _rg_out_bwd(dy, h, proj, name):
    t = h.shape[0]
    bt = _pick(t, ROW_TILE)

    def body(dy_ref, h_ref, g_ref, dh_ref, dg_ref):
        gl, glg = _gelu_and_grad(g_ref[...])
        dyv = dy_ref[...]
        dh_ref[...] = dyv * gl
        dg_ref[...] = (dyv * h_ref[...] * glg).astype(BF16)

    tile = pl.BlockSpec((bt, D_MODEL), lambda i: (i, 0))
    return pl.pallas_call(
        body, name=name, grid=(t // bt,),
        in_specs=[tile, tile, pl.BlockSpec((bt, D_MODEL), lambda i: (i, 1))], out_specs=[tile, tile],
        out_shape=[jax.ShapeDtypeStruct((t, D_MODEL), F32), jax.ShapeDtypeStruct((t, D_MODEL), BF16)],
        compiler_params=_cparams("parallel"),
    )(dy, h, proj)


def _shift_up_one(a, name):
    t, c = a.shape
    bt = _pick(t, ROW_TILE)

    def body(a_ref, halo_ref, o_ref):
        o_ref[...] = _shift_up(a_ref[...], halo_ref[...], 1)

    tile = pl.BlockSpec((bt, c), lambda i, j: (i, j))
    return pl.pallas_call(
        body, name=name, grid=(t // bt, 1), in_specs=[tile, _next_halo_spec(bt, c, t)], out_specs=tile,
        out_shape=jax.ShapeDtypeStruct((t, c), F32), compiler_params=_cparams("parallel", "parallel"),
    )(a, a)


ADAM_ROWS = 64


def _adamw(recv, w, m, v, name):
    _, r, c = recv.shape
    br = ADAM_ROWS
    assert r % br == 0

    def body(r_ref, w_ref, m_ref, v_ref, g_ref, d_ref, nm_ref, nv_ref):
        g = r_ref[0]
        for s in range(1, N_DEV):
            g = g + r_ref[s]
        m_new = ADAM_B1 * m_ref[...] + (1.0 - ADAM_B1) * g
        v_new = ADAM_B2 * v_ref[...] + (1.0 - ADAM_B2) * (g * g)
        m_hat = m_new / (1.0 - ADAM_B1 ** ADAM_STEP)
        v_hat = v_new / (1.0 - ADAM_B2 ** ADAM_STEP)
        g_ref[...] = g
        d_ref[...] = -ADAM_LR * (m_hat / (jnp.sqrt(v_hat) + ADAM_EPS) + ADAM_WD * w_ref[...])
        nm_ref[...] = m_new
        nv_ref[...] = v_new

    tile = pl.BlockSpec((br, c), lambda i: (i, 0))
    shape = jax.ShapeDtypeStruct((r, c), F32)
    return pl.pallas_call(
        body, name=name, grid=(r // br,),
        in_specs=[pl.BlockSpec((N_DEV, br, c), lambda i: (0, i, 0)), tile, tile, tile],
        out_specs=[tile] * 4, out_shape=[shape] * 4, compiler_params=_cparams("parallel"),
    )(recv, w, m, v)


def _exchange(src, scatter, name):
    slab = src.shape[1:] if scatter else src.shape

    def body(src_ref, out_ref, send_sems, recv_sems, local_sem):
        pos = [lax.axis_index(ax) for ax in MESH_AXES]
        me = 4 * pos[0] + 2 * pos[1] + pos[2]

        def peer_of(k):
            p = [(1 - pos[b]) if (k >> (2 - b)) & 1 else pos[b] for b in range(3)]
            return tuple(p), 4 * p[0] + 2 * p[1] + p[2]

        def copy(k):
            peer, peer_idx = peer_of(k)
            return pltpu.make_async_remote_copy(
                src_ref=src_ref.at[peer_idx] if scatter else src_ref, dst_ref=out_ref.at[me],
                send_sem=send_sems.at[k - 1], recv_sem=recv_sems.at[k - 1],
                device_id=peer, device_id_type=pl.DeviceIdType.MESH)

        def arrival(k):
            peer, peer_idx = peer_of(k)
            return pltpu.make_async_remote_copy(
                src_ref=src_ref.at[me] if scatter else src_ref, dst_ref=out_ref.at[peer_idx],
                send_sem=send_sems.at[k - 1], recv_sem=recv_sems.at[k - 1],
                device_id=peer, device_id_type=pl.DeviceIdType.MESH)

        mine = pltpu.make_async_copy(src_ref.at[me] if scatter else src_ref, out_ref.at[me], local_sem)
        mine.start()
        sends = [copy(k) for k in range(1, N_DEV)]
        for cp in sends:
            cp.start()
        for k in range(1, N_DEV):
            arrival(k).wait_recv()
        for cp in sends:
            cp.wait_send()
        mine.wait()

    hbm = pl.BlockSpec(memory_space=pltpu.HBM)
    return pl.pallas_call(
        body, name=name, in_specs=[hbm], out_specs=hbm,
        out_shape=jax.ShapeDtypeStruct((N_DEV,) + tuple(slab), src.dtype),
        scratch_shapes=[pltpu.SemaphoreType.DMA((N_DEV - 1,)), pltpu.SemaphoreType.DMA((N_DEV - 1,)),
                        pltpu.SemaphoreType.DMA],
        compiler_params=pltpu.CompilerParams(has_side_effects=True),
    )(src)


WEIGHTS = ['a_w_in', 'a_b_f', 'a_w_out', 'b_w_in', 'b_conv_w', 'b_conv_b', 'b_w_a', 'b_b_a', 'b_w_i', 'b_b_i',
           'b_lam', 'b_w_out', 'f_w_up', 'f_conv_w', 'f_conv_b', 'f_w_down', 'ln1_g', 'ln1_b', 'ln2_g', 'ln2_b',
           'ple_w', 'ple_gate_w', 'ple_gate_b']
SHARD_AXIS = {'a_w_in': 2, 'a_b_f': None, 'a_w_out': 1, 'b_w_in': 2, 'b_conv_w': 2, 'b_conv_b': 1, 'b_w_a': None,
              'b_b_a': None, 'b_w_i': None, 'b_b_i': None, 'b_lam': 1, 'b_w_out': 1, 'f_w_up': 2, 'f_conv_w': 2,
              'f_conv_b': None, 'f_w_down': 1, 'ln1_g': None, 'ln1_b': None, 'ln2_g': None, 'ln2_b': None,
              'ple_w': 2, 'ple_gate_w': 1, 'ple_gate_b': None}
MATMUL_WEIGHTS = ['a_w_in', 'a_w_out', 'b_w_in', 'b_w_out', 'f_w_up', 'f_w_down', 'ple_w', 'ple_gate_w']
SMALL_SHARDED = ['b_conv_w', 'b_conv_b', 'b_lam', 'f_conv_w']
PACK_COLS = 1024


def _to_shards(full, axis):
    return jnp.stack(jnp.split(full, N_DEV, axis=axis))


def _from_shards(pieces, axis):
    return jnp.concatenate([pieces[d] for d in range(N_DEV)], axis=axis)


def _pack_rows(flat, row_mult):
    n = flat.shape[-1]
    rows = -(-n // PACK_COLS)
    rows = -(-rows // row_mult) * row_mult
    pad = [(0, 0)] * (flat.ndim - 1) + [(0, rows * PACK_COLS - n)]
    return jnp.pad(flat, pad).reshape(flat.shape[:-1] + (rows, PACK_COLS))


def _gather_weights(local, names, dtype, row_mult, name):
    flat = jnp.concatenate([local[n].astype(dtype).reshape(-1) for n in names])
    gathered = _exchange(_pack_rows(flat, row_mult), False, name)
    gathered = gathered.reshape(N_DEV, -1)
    out, off = {}, 0
    for n in names:
        size = local[n].size
        pieces = gathered[:, off:off + size].reshape((N_DEV,) + local[n].shape)
        out[n] = _from_shards(pieces, SHARD_AXIS[n])
        off += size
    return out


def _mixer_a_fwd(tag, xb, w):
    qkv = _mm(xb, w["wqkv"], "nn", BF16, f"{tag}_qkv")
    fg = _mm(xb, w["wf"], "nn", F32, f"{tag}_fgproj")
    fg_rows = fg[:, :N_HEADS].T
    c_rows = _fgate_fwd(fg_rows, w["b_f"], f"{tag}_fgate")
    c_pairs = _rows_to_pairs(c_rows)
    qh, kh = _attn_prep(qkv, _rows_to_cols(c_rows), f"{tag}_attn_prep")
    ot, otb, lse_pairs = _attn_fwd_t(qh, kh, qkv[:, 2 * D_MODEL:].T, c_pairs, f"{tag}_attn")
    m = _mm(otb, w["wout"], "tn", F32, f"{tag}_oproj")
    return m, dict(qkv=qkv, qh=qh, kh=kh, fg_rows=fg_rows, c_pairs=c_pairs, ot=ot, otb=otb, lse_pairs=lse_pairs)


def _mixer_a_bwd(tag, dz, dzb, xb, w, s):
    t = xb.shape[0]
    dot = _mm(w["wout"], dzb, "nt", F32, f"{tag}_b_do")
    g_wout = _mm(s["otb"], dzb, "nn", F32, f"{tag}_b_dwout")
    d_pairs, dotb = _attn_delta_t(dot, s["ot"], f"{tag}_b_delta")
    dqt, dcq_pairs = _attn_dq_t(s["qh"], s["kh"], s["kh"].T, s["qkv"], dotb, s["c_pairs"], s["lse_pairs"],
                                d_pairs, f"{tag}_b_dq")
    dk, dv, dc_cols = _attn_dkv_t(s["qh"], s["kh"], s["qkv"], dotb.T, dotb, s["c_pairs"], s["lse_pairs"],
                                  d_pairs, f"{tag}_b_dkv")
    dfg_rows, db_f = _fgate_bwd(_cols_to_rows(dc_cols), dcq_pairs[:, :2, :].reshape(N_HEADS, t), s["fg_rows"],
                                w["b_f"], f"{tag}_b_fgate")
    dqkv = jnp.concatenate([dqt.T, dk, dv], axis=1)
    dfg = jnp.pad(dfg_rows.T, ((0, 0), (0, LANES - N_HEADS))).astype(BF16)
    dx = _mm(dqkv, w["wqkv"], "nt", F32, f"{tag}_b_dx_qkv", add=dz, add_scale=ALPHA)
    dx = _mm(dfg, w["wf"], "nt", F32, f"{tag}_b_dx_fg", add=dx)
    g_wqkv = _mm(xb, dqkv, "tn", F32, f"{tag}_b_dwqkv")
    g_wf = _mm(xb, dfg, "tn", F32, f"{tag}_b_dwf")[:, :N_HEADS]
    grads = dict(a_w_in=jnp.concatenate([g_wqkv, g_wf], axis=1), a_b_f=db_f.reshape(N_HEADS), a_w_out=g_wout)
    return dx, grads


def _mixer_b_fwd(tag, xb, w):
    proj = _mm(xb, w["win"], "nn", F32, f"{tag}_proj")
    xc = _rg_conv_fwd(proj, w["conv_w"], w["conv_b"], f"{tag}_conv")
    a, u = _rg_gate_fwd(xc, w["wa"], w["ba"], w["wi"], w["bi"], w["lam"], f"{tag}_gate")
    h = _scan(a, u, f"{tag}_scan")
    y = _rg_out_fwd(h, proj, f"{tag}_out")
    m = _mm(y, w["wout"], "nn", F32, f"{tag}_oproj")
    return m, dict(proj=proj, xc=xc, a=a, h=h, y=y)


def _mixer_b_bwd(tag, dz, dzb, xb, w, s):
    dy = _mm(dzb, w["wout"], "nt", F32, f"{tag}_b_dy")
    g_wout = _mm(s["y"], dzb, "tn", F32, f"{tag}_b_dwout")
    dh, dgate = _rg_out_bwd(dy, s["h"], s["proj"], f"{tag}_b_out")
    g = _scan(_shift_up_one(s["a"], f"{tag}_b_shift"), dh, f"{tag}_b_scan", reverse=True)
    dxc, g_wa, g_wi, g_ba, g_bi, g_lam = _rg_gate_bwd(
        s["xc"], g, s["h"], w["wa"], w["ba"], w["wi"], w["bi"], w["lam"], f"{tag}_b_gate")
    dxp = _conv_bwd_x(dxc, w["conv_w"], f"{tag}_b_convx", BF16)
    g_cw, g_cb = _conv_bwd_w(s["proj"], dxc, CONV_B, f"{tag}_b_convw")
    dproj = jnp.concatenate([dxp, dgate], axis=1)
    dx = _mm(dproj, w["win"], "nt", F32, f"{tag}_b_dx", add=dz, add_scale=ALPHA)
    g_win = _mm(xb, dproj, "tn", F32, f"{tag}_b_dwin")
    grads = dict(b_w_in=g_win, b_conv_w=g_cw, b_conv_b=g_cb.reshape(D_MODEL), b_w_a=g_wa,
                 b_b_a=g_ba.reshape(N_BLOCKS_B, BLOCK_B), b_w_i=g_wi, b_b_i=g_bi.reshape(N_BLOCKS_B, BLOCK_B),
                 b_lam=g_lam.reshape(D_MODEL), b_w_out=g_wout)
    return dx, grads


def _layer_fwd(i, x, xb, pb, w):
    tag = f"L{i}"
    mix = _mixer_a_fwd if i % 2 == 0 else _mixer_b_fwd
    m, sm = mix(tag, xb, w)
    x1, x1b, z1 = _ln_fwd(x, m, w["ln1_g"], w["ln1_b"], f"{tag}_ln1")
    h = _mm(x1b, w["wup"], "nn", F32, f"{tag}_ffn_up")
    y = _ffn_mid_fwd(h, w["fconv_w"], w["fconv_b"], f"{tag}_ffn_mid")
    ff = _mm(y, w["wdown"], "nn", F32, f"{tag}_ffn_down")
    x2, x2b, z2 = _ln_fwd(x1, ff, w["ln2_g"], w["ln2_b"], f"{tag}_ln2")
    gl = _mm(x2b, w["wg"], "nn", F32, f"{tag}_ple_gate")
    e = _mm(pb, w["wp"], "nn", F32, f"{tag}_ple_emb")
    x3, x3b = _ple_fwd(x2, gl, e, w["bg"], f"{tag}_ple")
    saved = dict(mixer=sm, xb=xb, x1b=x1b, z1=z1, h=h, y=y, x2b=x2b, z2=z2, gl=gl, e=e, pb=pb)
    return x3, x3b, saved


def _layer_bwd(i, dx3, w, s):
    tag = f"L{i}"
    dgl, de, g_bg = _ple_bwd(dx3, s["gl"], s["e"], w["bg"], f"{tag}_b_ple")
    g_wg = _mm(s["x2b"], dgl, "tn", F32, f"{tag}_b_dwg")
    g_wp = _mm(s["pb"], de, "tn", F32, f"{tag}_b_dwp")
    dx2 = _mm(dgl, w["wg"], "nt", F32, f"{tag}_b_dx2", add=dx3)
    dz2, dz2b, g_ln2g, g_ln2b = _ln_bwd(dx2, s["z2"], w["ln2_g"], f"{tag}_b_ln2")
    dy = _mm(dz2b, w["wdown"], "nt", F32, f"{tag}_b_dy")
    g_wdown = _mm(s["y"], dz2b, "tn", F32, f"{tag}_b_dwdown")
    dval, dgate, g_fcw, g_fcb = _ffn_mid_bwd_a(s["h"], dy, w["fconv_w"], w["fconv_b"], f"{tag}_b_ffn_mid")
    dhv = _conv_bwd_x(dval, w["fconv_w"][:, :D_FF], f"{tag}_b_convx_v", BF16)
    dhg = _conv_bwd_x(dgate, w["fconv_w"][:, D_FF:], f"{tag}_b_convx_g", BF16)
    dx1 = _mm(dhv, w["wup"][:, :D_FF], "nt", F32, f"{tag}_b_dx1_v", add=dz2, add_scale=ALPHA)
    dx1 = _mm(dhg, w["wup"][:, D_FF:], "nt", F32, f"{tag}_b_dx1_g", add=dx1)
    g_wup = jnp.concatenate([_mm(s["x1b"], dhv, "tn", F32, f"{tag}_b_dwup_v"),
                             _mm(s["x1b"], dhg, "tn", F32, f"{tag}_b_dwup_g")], axis=1)
    dz1, dz1b, g_ln1g, g_ln1b = _ln_bwd(dx1, s["z1"], w["ln1_g"], f"{tag}_b_ln1")
    mix_bwd = _mixer_a_bwd if i % 2 == 0 else _mixer_b_bwd
    dx, g_mix = mix_bwd(tag, dz1, dz1b, s["xb"], w, s["mixer"])
    grads = dict(f_w_up=g_wup, f_conv_w=g_fcw, f_conv_b=g_fcb.reshape(2 * D_FF), f_w_down=g_wdown,
                 ln1_g=g_ln1g.reshape(D_MODEL), ln1_b=g_ln1b.reshape(D_MODEL), ln2_g=g_ln2g.reshape(D_MODEL),
                 ln2_b=g_ln2b.reshape(D_MODEL), ple_w=g_wp, ple_gate_w=g_wg, ple_gate_b=g_bg.reshape(D_MODEL))
    return dx, g_mix, grads


def _layer_weights(i, full, rep):
    j = i // 2
    w = dict(ln1_g=rep["ln1_g"][i], ln1_b=rep["ln1_b"][i], ln2_g=rep["ln2_g"][i], ln2_b=rep["ln2_b"][i],
             wup=full["f_w_up"][i], fconv_w=full["f_conv_w"][i], fconv_b=rep["f_conv_b"][i],
             wdown=full["f_w_down"][i], wp=full["ple_w"][i], wg=full["ple_gate_w"][i], bg=rep["ple_gate_b"][i])
    if i % 2 == 0:
        w_in = full["a_w_in"][j]
        w.update(wqkv=w_in[:, :3 * D_MODEL],
                 wf=jnp.pad(w_in[:, 3 * D_MODEL:], ((0, 0), (0, LANES - N_HEADS))),
                 b_f=rep["a_b_f"][j], wout=full["a_w_out"][j])
    else:
        w.update(win=full["b_w_in"][j], conv_w=full["b_conv_w"][j], conv_b=full["b_conv_b"][j],
                 wa=rep["b_w_a"][j].astype(BF16), wi=rep["b_w_i"][j].astype(BF16),
                 ba=rep["b_b_a"][j].reshape(N_BLOCKS_B, 1, BLOCK_B), bi=rep["b_b_i"][j].reshape(N_BLOCKS_B, 1, BLOCK_B),
                 lam=full["b_lam"][j].reshape(N_BLOCKS_B, 1, BLOCK_B), wout=full["b_w_out"][j])
    return w


def _fwd_bwd(x, p, target, full, rep):
    weights = [_layer_weights(i, full, rep) for i in range(DEPTH)]
    xb = x.astype(BF16)
    pb = p.astype(BF16)
    saved = []
    for i in range(DEPTH):
        x, xb, s = _layer_fwd(i, x, xb, pb[i], weights[i])
        saved.append(s)
    dx, loss_local = _loss_head(x, target, "loss_head")

    per_layer = {n: [None] * (DEPTH if n.startswith(("f_", "ln", "ple")) else DEPTH // 2) for n in WEIGHTS}
    for i in reversed(range(DEPTH)):
        dx, g_mix, g_layer = _layer_bwd(i, dx, weights[i], saved[i])
        for n, g in g_layer.items():
            per_layer[n][i] = g
        for n, g in g_mix.items():
            per_layer[n][i // 2] = g
    return loss_local, dx, {n: jnp.stack(per_layer[n]) for n in WEIGHTS}


def _train_step(x, p, target, local, moments_m, moments_v):
    full = _gather_weights(local, MATMUL_WEIGHTS, BF16, 16, "gather_matmul_weights")
    full.update(_gather_weights(local, SMALL_SHARDED, F32, SUBLANES, "gather_small_weights"))
    rep = {n: local[n] for n in WEIGHTS if SHARD_AXIS[n] is None}
    loss_local, dx, grads_full = _fwd_bwd(x, p, target, full, rep)

    pieces = []
    for n in WEIGHTS:
        g = grads_full[n].astype(F32)
        if SHARD_AXIS[n] is None:
            pieces.append(jnp.broadcast_to(g.reshape(1, -1), (N_DEV, g.size)))
        else:
            pieces.append(_to_shards(g, SHARD_AXIS[n]).reshape(N_DEV, -1))
    slabs = _pack_rows(jnp.concatenate(pieces, axis=1), ADAM_ROWS)
    recv = _exchange(slabs, True, "reduce_scatter_grads")

    def pack_local(d):
        return _pack_rows(jnp.concatenate([d[n].astype(F32).reshape(-1) for n in WEIGHTS]), ADAM_ROWS)

    outs = _adamw(recv, pack_local(local), pack_local(moments_m), pack_local(moments_v), "adamw")
    unpacked = []
    for packed in outs:
        flat, off, d = packed.reshape(-1), 0, {}
        for n in WEIGHTS:
            d[n] = flat[off:off + local[n].size].reshape(local[n].shape)
            off += local[n].size
        unpacked.append(d)
    return loss_local, dx, unpacked


def kernel(x, p, a_w_in, a_b_f, a_w_out, b_w_in, b_conv_w, b_conv_b, b_w_a, b_b_a, b_w_i, b_b_i, b_lam, b_w_out, f_w_up, f_conv_w, f_conv_b, f_w_down, ln1_g, ln1_b, ln2_g, ln2_b, ple_w, ple_gate_w, ple_gate_b, loss_target, m_a_w_in, m_a_b_f, m_a_w_out, m_b_w_in, m_b_conv_w, m_b_conv_b, m_b_w_a, m_b_b_a, m_b_w_i, m_b_b_i, m_b_lam, m_b_w_out, m_f_w_up, m_f_conv_w, m_f_conv_b, m_f_w_down, m_ln1_g, m_ln1_b, m_ln2_g, m_ln2_b, m_ple_w, m_ple_gate_w, m_ple_gate_b, v_a_w_in, v_a_b_f, v_a_w_out, v_b_w_in, v_b_conv_w, v_b_conv_b, v_b_w_a, v_b_b_a, v_b_w_i, v_b_b_i, v_b_lam, v_b_w_out, v_f_w_up, v_f_conv_w, v_f_conv_b, v_f_w_down, v_ln1_g, v_ln1_b, v_ln2_g, v_ln2_b, v_ple_w, v_ple_gate_w, v_ple_gate_b):
    given = dict(locals())
    local = {n: given[n] for n in WEIGHTS}
    mom_m = {n: given["m_" + n] for n in WEIGHTS}
    mom_v = {n: given["v_" + n] for n in WEIGHTS}
    t = x.shape[1]
    loss_local, dx, (grad, delta, new_m, new_v) = _train_step(
        x.reshape(t, D_MODEL), p.reshape(DEPTH, t, D_PLE), loss_target.reshape(t, D_MODEL), local, mom_m, mom_v)
    loss = lax.psum(loss_local, MESH_AXES)
    return (loss, dx.reshape(1, t, D_MODEL), *[grad[n] for n in WEIGHTS], *[delta[n] for n in WEIGHTS],
            *[new_m[n] for n in WEIGHTS], *[new_v[n] for n in WEIGHTS])
```

```python
import math

import jax
import jax.numpy as jnp
from jax import lax
from jax.experimental import pallas as pl
from jax.experimental.pallas import tpu as pltpu

F32 = jnp.float32
BF16 = jnp.bfloat16

D_MODEL = 1024
DEPTH = 4
N_HEADS = 16
HEAD_DIM = 64
N_PAIRS = N_HEADS // 2
N_BLOCKS_B = 8
BLOCK_B = 128
CONV_B = 4
LRU_C = 8.0
D_FF = 2816
CONV_F = 3
D_PLE = 256
LN_EPS = 1e-5
ALPHA = (2.0 * DEPTH) ** 0.25
ADAM_LR, ADAM_B1, ADAM_B2, ADAM_EPS, ADAM_WD, ADAM_STEP = 0.001, 0.9, 0.999, 1e-08, 0.01, 10
N_DEV = 8
MESH_AXES = ("x", "y", "c")

LANES = 128
SUBLANES = 8
VMEM_LIMIT_BYTES = 56 * 1024 * 1024
ATTN_BLOCK = 512
ROW_TILE = 256
NEG_BIG = -1e30


def _cparams(*sem):
    return pltpu.CompilerParams(dimension_semantics=sem if sem else None, vmem_limit_bytes=VMEM_LIMIT_BYTES)


def _pick(n, pref):
    if n <= pref:
        return n
    best = None
    for t in range(LANES, pref + 1, LANES):
        if n % t == 0:
            best = t
    assert best is not None, (n, pref)
    return best


def _sigmoid(x):
    return 1.0 / (1.0 + jnp.exp(-x))


def _log1p(x):
    u = 1.0 + x
    d = u - 1.0
    return jnp.where(d == 0.0, x, jnp.log(u) * (x / jnp.where(d == 0.0, 1.0, d)))


def _expm1(x):
    u = jnp.exp(x)
    lu = jnp.log(u)
    return jnp.where(u == 1.0, x, (u - 1.0) * (x / jnp.where(u == 1.0, 1.0, lu)))


def _softplus(x):
    return jnp.maximum(x, 0.0) + _log1p(jnp.exp(-jnp.abs(x)))


_GELU_C = math.sqrt(2.0 / math.pi)


def _gelu(x):
    return 0.5 * x * (1.0 + jnp.tanh(_GELU_C * (x + 0.044715 * x * x * x)))


def _gelu_and_grad(x):
    t = jnp.tanh(_GELU_C * (x + 0.044715 * x * x * x))
    du = _GELU_C * (1.0 + 3.0 * 0.044715 * x * x)
    return 0.5 * x * (1.0 + t), 0.5 * (1.0 + t) + 0.5 * x * (1.0 - t * t) * du


_DOT_DIMS = {"nn": (((1,), (0,)), ((), ())), "nt": (((1,), (1,)), ((), ())), "tn": (((0,), (0,)), ((), ()))}


def _mm(a, b, mode, out_dtype, name, add=None, add_scale=1.0, tm=512, tn=1408, tk=1408):
    if mode == "nn":
        (m, k), (k2, n) = a.shape, b.shape
    elif mode == "nt":
        (m, k), (n, k2) = a.shape, b.shape
    else:
        (k, m), (k2, n) = a.shape, b.shape
    assert k == k2 and a.dtype == BF16 and b.dtype == BF16, (a.shape, b.shape, a.dtype, b.dtype)
    tm, tn, tk = _pick(m, tm), _pick(n, tn), _pick(k, tk)
    nk = k // tk
    dims = _DOT_DIMS[mode]

    def body(*refs):
        if add is None:
            a_ref, b_ref, o_ref, acc_ref = refs
        else:
            a_ref, b_ref, add_ref, o_ref, acc_ref = refs
        kk = pl.program_id(2)

        @pl.when(kk == 0)
        def _():
            acc_ref[...] = jnp.zeros_like(acc_ref)

        acc_ref[...] += lax.dot_general(a_ref[...], b_ref[...], dims, preferred_element_type=F32)

        @pl.when(kk == nk - 1)
        def _():
            r = acc_ref[...]
            if add is not None:
                r = r + add_scale * add_ref[...]
            o_ref[...] = r.astype(out_dtype)

    a_spec = (pl.BlockSpec((tk, tm), lambda j, i, kk: (kk, i)) if mode == "tn"
              else pl.BlockSpec((tm, tk), lambda j, i, kk: (i, kk)))
    b_spec = (pl.BlockSpec((tn, tk), lambda j, i, kk: (j, kk)) if mode == "nt"
              else pl.BlockSpec((tk, tn), lambda j, i, kk: (kk, j)))
    o_spec = pl.BlockSpec((tm, tn), lambda j, i, kk: (i, j))
    in_specs, args = [a_spec, b_spec], [a, b]
    if add is not None:
        assert add.shape == (m, n) and add.dtype == F32
        in_specs.append(o_spec)
        args.append(add)
    return pl.pallas_call(
        body, name=name, grid=(n // tn, m // tm, nk),
        in_specs=in_specs, out_specs=o_spec,
        out_shape=jax.ShapeDtypeStruct((m, n), out_dtype),
        scratch_shapes=[pltpu.VMEM((tm, tn), F32)],
        compiler_params=_cparams("parallel", "parallel", "arbitrary"),
    )(*args)


def _ln_fwd(x, m, g, b, name):
    t, d = x.shape
    bt = _pick(t, ROW_TILE)

    def body(x_ref, m_ref, g_ref, b_ref, y_ref, yb_ref, z_ref):
        z = ALPHA * x_ref[...] + m_ref[...]
        mu = jnp.mean(z, axis=-1, keepdims=True)
        zc = z - mu
        var = jnp.mean(zc * zc, axis=-1, keepdims=True)
        y = zc * lax.rsqrt(var + LN_EPS) * g_ref[...] + b_ref[...]
        y_ref[...] = y
        yb_ref[...] = y.astype(BF16)
        z_ref[...] = z

    row = pl.BlockSpec((bt, d), lambda i: (i, 0))
    vec = pl.BlockSpec((1, d), lambda i: (0, 0))
    return pl.pallas_call(
        body, name=name, grid=(t // bt,), in_specs=[row, row, vec, vec], out_specs=[row, row, row],
        out_shape=[jax.ShapeDtypeStruct((t, d), F32), jax.ShapeDtypeStruct((t, d), BF16),
                   jax.ShapeDtypeStruct((t, d), F32)],
        compiler_params=_cparams("parallel"),
    )(x, m, g.reshape(1, d), b.reshape(1, d))


def _ln_bwd(dy, z, g, name):
    t, d = dy.shape
    bt = _pick(t, ROW_TILE)

    def body(dy_ref, z_ref, g_ref, dz_ref, dzb_ref, dg_ref, db_ref):
        @pl.when(pl.program_id(0) == 0)
        def _():
            dg_ref[...] = jnp.zeros_like(dg_ref)
            db_ref[...] = jnp.zeros_like(db_ref)

        z = z_ref[...]
        dyv = dy_ref[...]
        mu = jnp.mean(z, axis=-1, keepdims=True)
        zc = z - mu
        var = jnp.mean(zc * zc, axis=-1, keepdims=True)
        rstd = lax.rsqrt(var + LN_EPS)
        xhat = zc * rstd
        dxh = dyv * g_ref[...]
        m1 = jnp.mean(dxh, axis=-1, keepdims=True)
        m2 = jnp.mean(dxh * xhat, axis=-1, keepdims=True)
        dz = rstd * (dxh - m1 - xhat * m2)
        dz_ref[...] = dz
        dzb_ref[...] = dz.astype(BF16)
        dg_ref[...] += jnp.sum(dyv * xhat, axis=0, keepdims=True)
        db_ref[...] += jnp.sum(dyv, axis=0, keepdims=True)

    row = pl.BlockSpec((bt, d), lambda i: (i, 0))
    vec = pl.BlockSpec((1, d), lambda i: (0, 0))
    return pl.pallas_call(
        body, name=name, grid=(t // bt,), in_specs=[row, row, vec], out_specs=[row, row, vec, vec],
        out_shape=[jax.ShapeDtypeStruct((t, d), F32), jax.ShapeDtypeStruct((t, d), BF16),
                   jax.ShapeDtypeStruct((1, d), F32), jax.ShapeDtypeStruct((1, d), F32)],
        compiler_params=_cparams("arbitrary"),
    )(dy, z, g.reshape(1, d))


def _rows_iota(shape):
    return lax.broadcasted_iota(jnp.int32, shape, 0)


def _shift_down(x, halo, s):
    if s == 0:
        return x
    rolled = pltpu.roll(x, s, axis=0)
    first = jnp.where(_rows_iota((SUBLANES, x.shape[1])) < s, pltpu.roll(halo, s, axis=0), rolled[:SUBLANES])
    return jnp.concatenate([first, rolled[SUBLANES:]], axis=0)


def _shift_up(x, halo, s):
    if s == 0:
        return x
    n = x.shape[0]
    rolled = pltpu.roll(x, n - s, axis=0)
    last = jnp.where(_rows_iota((SUBLANES, x.shape[1])) < SUBLANES - s, rolled[n - SUBLANES:],
                     pltpu.roll(halo, SUBLANES - s, axis=0))
    return jnp.concatenate([rolled[:n - SUBLANES], last], axis=0)


def _prev_halo_spec(bt, cb, col_off=0):
    r = bt // SUBLANES
    return pl.BlockSpec((SUBLANES, cb), lambda i, j: (jnp.maximum(i * r - 1, 0), j + col_off))


def _next_halo_spec(bt, cb, t, col_off=0):
    r = bt // SUBLANES
    last = t // SUBLANES - 1
    return pl.BlockSpec((SUBLANES, cb), lambda i, j: (jnp.minimum((i + 1) * r, last), j + col_off))


def _causal_conv(x, halo, w_ref, ksize):
    acc = None
    for j in range(ksize):
        term = w_ref[j:j + 1, :] * _shift_down(x, halo, ksize - 1 - j)
        acc = term if acc is None else acc + term
    return acc


def _anticausal_conv(y, halo, w_ref, ksize):
    acc = None
    for j in range(ksize):
        term = w_ref[j:j + 1, :] * _shift_up(y, halo, ksize - 1 - j)
        acc = term if acc is None else acc + term
    return acc


def _ffn_mid_fwd(h, cw, cb_, name):
    t = h.shape[0]
    bt, cb = _pick(t, ROW_TILE), _pick(D_FF, 1408)
    nc = D_FF // cb

    def body(hv_ref, hvh_ref, hg_ref, hgh_ref, wv_ref, wg_ref, bv_ref, bg_ref, y_ref):
        keep = (pl.program_id(0) > 0).astype(F32)
        val = _causal_conv(hv_ref[...], hvh_ref[...] * keep, wv_ref, CONV_F) + bv_ref[...]
        gate = _causal_conv(hg_ref[...], hgh_ref[...] * keep, wg_ref, CONV_F) + bg_ref[...]
        y_ref[...] = (_gelu(gate) * val).astype(BF16)

    tile_v = pl.BlockSpec((bt, cb), lambda i, j: (i, j))
    tile_g = pl.BlockSpec((bt, cb), lambda i, j: (i, j + nc))
    wv = pl.BlockSpec((CONV_F, cb), lambda i, j: (0, j))
    wg = pl.BlockSpec((CONV_F, cb), lambda i, j: (0, j + nc))
    bv = pl.BlockSpec((1, cb), lambda i, j: (0, j))
    bg = pl.BlockSpec((1, cb), lambda i, j: (0, j + nc))
    cb2 = cb_.reshape(1, 2 * D_FF)
    return pl.pallas_call(
        body, name=name, grid=(t // bt, nc),
        in_specs=[tile_v, _prev_halo_spec(bt, cb), tile_g, _prev_halo_spec(bt, cb, nc), wv, wg, bv, bg],
        out_specs=tile_v, out_shape=jax.ShapeDtypeStruct((t, D_FF), BF16),
        compiler_params=_cparams("parallel", "parallel"),
    )(h, h, h, h, cw, cw, cb2, cb2)


def _ffn_mid_bwd_a(h, dy, cw, cb_, name):
    t = h.shape[0]
    bt, cb = _pick(t, ROW_TILE), _pick(D_FF, 1408)
    nc = D_FF // cb

    def body(hv_ref, hvh_ref, hg_ref, hgh_ref, dy_ref, wv_ref, wg_ref, bv_ref, bg_ref,
             dv_ref, dg_ref, dwv_ref, dwg_ref, dbv_ref, dbg_ref):
        @pl.when(pl.program_id(1) == 0)
        def _():
            for r in (dwv_ref, dwg_ref, dbv_ref, dbg_ref):
                r[...] = jnp.zeros_like(r)

        keep = (pl.program_id(1) > 0).astype(F32)
        hv, hvh = hv_ref[...], hvh_ref[...] * keep
        hg, hgh = hg_ref[...], hgh_ref[...] * keep
        val = _causal_conv(hv, hvh, wv_ref, CONV_F) + bv_ref[...]
        gate = _causal_conv(hg, hgh, wg_ref, CONV_F) + bg_ref[...]
        gl, glg = _gelu_and_grad(gate)
        dyv = dy_ref[...]
        dval = dyv * gl
        dgate = dyv * val * glg
        dv_ref[...] = dval
        dg_ref[...] = dgate
        dbv_ref[...] += jnp.sum(dval, axis=0, keepdims=True)
        dbg_ref[...] += jnp.sum(dgate, axis=0, keepdims=True)
        for j in range(CONV_F):
            s = CONV_F - 1 - j
            dwv_ref[j:j + 1, :] += jnp.sum(dval * _shift_down(hv, hvh, s), axis=0, keepdims=True)
            dwg_ref[j:j + 1, :] += jnp.sum(dgate * _shift_down(hg, hgh, s), axis=0, keepdims=True)

    tile_v = pl.BlockSpec((bt, cb), lambda j, i: (i, j))
    tile_g = pl.BlockSpec((bt, cb), lambda j, i: (i, j + nc))
    r = bt // SUBLANES
    halo_v = pl.BlockSpec((SUBLANES, cb), lambda j, i: (jnp.maximum(i * r - 1, 0), j))
    halo_g = pl.BlockSpec((SUBLANES, cb), lambda j, i: (jnp.maximum(i * r - 1, 0), j + nc))
    wv = pl.BlockSpec((CONV_F, cb), lambda j, i: (0, j))
    wg = pl.BlockSpec((CONV_F, cb), lambda j, i: (0, j + nc))
    bv = pl.BlockSpec((1, cb), lambda j, i: (0, j))
    bg = pl.BlockSpec((1, cb), lambda j, i: (0, j + nc))
    cb2 = cb_.reshape(1, 2 * D_FF)
    dv, dg, dwv, dwg, dbv, dbg = pl.pallas_call(
        body, name=name, grid=(nc, t // bt),
        in_specs=[tile_v, halo_v, tile_g, halo_g, tile_v, wv, wg, bv, bg],
        out_specs=[tile_v, tile_v, wv, wv, bv, bv],
        out_shape=[jax.ShapeDtypeStruct((t, D_FF), F32), jax.ShapeDtypeStruct((t, D_FF), F32),
                   jax.ShapeDtypeStruct((CONV_F, D_FF), F32), jax.ShapeDtypeStruct((CONV_F, D_FF), F32),
                   jax.ShapeDtypeStruct((1, D_FF), F32), jax.ShapeDtypeStruct((1, D_FF), F32)],
        compiler_params=_cparams("parallel", "arbitrary"),
    )(h, h, h, h, dy, cw, cw, cb2, cb2)
    return dv, dg, jnp.concatenate([dwv, dwg], axis=1), jnp.concatenate([dbv, dbg], axis=1)


def _conv_bwd_x(dy, cw, name, out_dtype):
    t, c = dy.shape
    ksize = cw.shape[0]
    bt, cb = _pick(t, ROW_TILE), _pick(c, 1408)

    def body(dy_ref, halo_ref, w_ref, o_ref):
        keep = (pl.program_id(0) < t // bt - 1).astype(F32)
        o_ref[...] = _anticausal_conv(dy_ref[...], halo_ref[...] * keep, w_ref, ksize).astype(out_dtype)

    tile = pl.BlockSpec((bt, cb), lambda i, j: (i, j))
    return pl.pallas_call(
        body, name=name, grid=(t // bt, c // cb),
        in_specs=[tile, _next_halo_spec(bt, cb, t), pl.BlockSpec((ksize, cb), lambda i, j: (0, j))],
        out_specs=tile, out_shape=jax.ShapeDtypeStruct((t, c), out_dtype),
        compiler_params=_cparams("parallel", "parallel"),
    )(dy, dy, cw)


def _ple_fwd(x2, g, e, bg, name):
    t, d = x2.shape
    bt = _pick(t, ROW_TILE)

    def body(x_ref, g_ref, e_ref, b_ref, y_ref, yb_ref):
        y = x_ref[...] + _sigmoid(g_ref[...] + b_ref[...]) * e_ref[...]
        y_ref[...] = y
        yb_ref[...] = y.astype(BF16)

    row = pl.BlockSpec((bt, d), lambda i: (i, 0))
    vec = pl.BlockSpec((1, d), lambda i: (0, 0))
    return pl.pallas_call(
        body, name=name, grid=(t // bt,), in_specs=[row, row, row, vec], out_specs=[row, row],
        out_shape=[jax.ShapeDtypeStruct((t, d), F32), jax.ShapeDtypeStruct((t, d), BF16)],
        compiler_params=_cparams("parallel"),
    )(x2, g, e, bg.reshape(1, d))


def _ple_bwd(dx3, g, e, bg, name):
    t, d = dx3.shape
    bt = _pick(t, ROW_TILE)

    def body(dx_ref, g_ref, e_ref, b_ref, dg_ref, de_ref, db_ref):
        @pl.when(pl.program_id(0) == 0)
        def _():
            db_ref[...] = jnp.zeros_like(db_ref)

        dx = dx_ref[...]
        gate = _sigmoid(g_ref[...] + b_ref[...])
        dg = dx * e_ref[...] * gate * (1.0 - gate)
        dg_ref[...] = dg.astype(BF16)
        de_ref[...] = (dx * gate).astype(BF16)
        db_ref[...] += jnp.sum(dg, axis=0, keepdims=True)

    row = pl.BlockSpec((bt, d), lambda i: (i, 0))
    vec = pl.BlockSpec((1, d), lambda i: (0, 0))
    return pl.pallas_call(
        body, name=name, grid=(t // bt,), in_specs=[row, row, row, vec], out_specs=[row, row, vec],
        out_shape=[jax.ShapeDtypeStruct((t, d), BF16), jax.ShapeDtypeStruct((t, d), BF16),
                   jax.ShapeDtypeStruct((1, d), F32)],
        compiler_params=_cparams("arbitrary"),
    )(dx3, g, e, bg.reshape(1, d))


def _loss_head(y, target, name):
    t, d = y.shape
    bt = _pick(t, ROW_TILE)

    def body(y_ref, t_ref, dy_ref, l_ref, acc_ref):
        @pl.when(pl.program_id(0) == 0)
        def _():
            acc_ref[...] = jnp.zeros_like(acc_ref)

        err = y_ref[...] - t_ref[...]
        dy_ref[...] = err * (1.0 / d)
        acc_ref[...] += jnp.sum(err * err, axis=0, keepdims=True)

        @pl.when(pl.program_id(0) == t // bt - 1)
        def _():
            l_ref[...] = jnp.full(l_ref.shape, (0.5 / d) * jnp.sum(acc_ref[...]), F32)

    row = pl.BlockSpec((bt, d), lambda i: (i, 0))
    dy, l = pl.pallas_call(
        body, name=name, grid=(t // bt,), in_specs=[row, row],
        out_specs=[row, pl.BlockSpec((SUBLANES, LANES), lambda i: (0, 0))],
        out_shape=[jax.ShapeDtypeStruct((t, d), F32), jax.ShapeDtypeStruct((SUBLANES, LANES), F32)],
        scratch_shapes=[pltpu.VMEM((1, d), F32)],
        compiler_params=_cparams("arbitrary"),
    )(y, target)
    return dy, l[0, 0]


def _split3(x):
    hi = x.astype(BF16)
    r1 = x - hi.astype(F32)
    mid = r1.astype(BF16)
    lo = (r1 - mid.astype(F32)).astype(BF16)
    return hi, mid, lo


def _tri_dot(x, tri):
    hi, mid, lo = _split3(x)
    dims = (((1,), (0,)), ((), ()))
    return (lax.dot_general(hi, tri, dims, preferred_element_type=F32)
            + lax.dot_general(mid, tri, dims, preferred_element_type=F32)
            + lax.dot_general(lo, tri, dims, preferred_element_type=F32))


def _fgate_fwd(fg_rows, b_f, name):
    hh, t = fg_rows.shape
    bt = _pick(t, 512)

    def body(fg_ref, b_ref, c_ref, carry_ref):
        @pl.when(pl.program_id(0) == 0)
        def _():
            carry_ref[...] = jnp.zeros_like(carry_ref)

        xx = fg_ref[...] + b_ref[...]
        logf = jnp.minimum(xx, 0.0) - _log1p(jnp.exp(-jnp.abs(xx)))
        r = lax.broadcasted_iota(jnp.int32, (bt, bt), 0)
        c = lax.broadcasted_iota(jnp.int32, (bt, bt), 1)
        tri = (r <= c).astype(BF16)
        cs = _tri_dot(logf, tri) + carry_ref[...]
        c_ref[...] = cs
        carry_ref[...] = cs[:, bt - 1:bt]

    return pl.pallas_call(
        body, name=name, grid=(t // bt,),
        in_specs=[pl.BlockSpec((hh, bt), lambda i: (0, i)), pl.BlockSpec((hh, 1), lambda i: (0, 0))],
        out_specs=pl.BlockSpec((hh, bt), lambda i: (0, i)),
        out_shape=jax.ShapeDtypeStruct((hh, t), F32),
        scratch_shapes=[pltpu.VMEM((hh, 1), F32)],
        compiler_params=_cparams("arbitrary"),
    )(fg_rows, b_f.reshape(hh, 1))


def _fgate_bwd(dck_rows, dcq_rows, fg_rows, b_f, name):
    hh, t = fg_rows.shape
    bt = _pick(t, 512)
    nb = t // bt

    def body(dc_ref, dcq_ref, fg_ref, b_ref, dfg_ref, db_ref, carry_ref):
        @pl.when(pl.program_id(0) == 0)
        def _():
            carry_ref[...] = jnp.zeros_like(carry_ref)
            db_ref[...] = jnp.zeros_like(db_ref)

        r = lax.broadcasted_iota(jnp.int32, (bt, bt), 0)
        c = lax.broadcasted_iota(jnp.int32, (bt, bt), 1)
        tri = (r >= c).astype(BF16)
        dlogf = _tri_dot(dc_ref[...] + dcq_ref[...], tri) + carry_ref[...]
        carry_ref[...] = dlogf[:, 0:1]
        xx = fg_ref[...] + b_ref[...]
        dfg = dlogf * _sigmoid(-xx)
        dfg_ref[...] = dfg
        db_ref[...] += jnp.sum(dfg, axis=1, keepdims=True)

    blk = pl.BlockSpec((hh, bt), lambda i: (0, nb - 1 - i))
    vec = pl.BlockSpec((hh, 1), lambda i: (0, 0))
    return pl.pallas_call(
        body, name=name, grid=(nb,), in_specs=[blk, blk, blk, vec], out_specs=[blk, vec],
        out_shape=[jax.ShapeDtypeStruct((hh, t), F32), jax.ShapeDtypeStruct((hh, 1), F32)],
        scratch_shapes=[pltpu.VMEM((hh, 1), F32)],
        compiler_params=_cparams("arbitrary"),
    )(dck_rows, dcq_rows, fg_rows, b_f.reshape(hh, 1))


def _rows_to_cols(r):
    hh, t = r.shape
    return jnp.repeat(r.reshape(hh // 2, 2, t).transpose(0, 2, 1), HEAD_DIM, axis=-1)


def _rows_to_pairs(r):
    hh, t = r.shape
    return jnp.pad(r.reshape(hh // 2, 2, t), ((0, 0), (0, SUBLANES - 2), (0, 0)))


AUG_C = HEAD_DIM
AUG_ONE = HEAD_DIM + 3


def _attn_prep(qkv, c_cols, name):
    t = qkv.shape[0]
    bt = _pick(t, ATTN_BLOCK)
    scale = 1.0 / math.sqrt(HEAD_DIM)

    def body(q_ref, k_ref, c_ref, qh_ref, kh_ref):
        lane = lax.broadcasted_iota(jnp.int32, (bt, LANES), 1)
        q2 = q_ref[...].astype(F32)
        k2 = k_ref[...].astype(F32) * scale
        c2 = c_ref[...]
        parts = [p.astype(F32) for p in _split3(c2 - c2[0:1, :])]
        swapped = [pltpu.roll(p, HEAD_DIM, axis=1) for p in parts]
        for a in (0, 1):
            qa = q2 if a == 0 else pltpu.roll(q2, HEAD_DIM, axis=1)
            ka = k2 if a == 0 else pltpu.roll(k2, HEAD_DIM, axis=1)
            hi, mid, lo = swapped if a == 0 else parts
            kaug = jnp.where(lane < HEAD_DIM, ka,
                             jnp.where(lane == AUG_C, hi,
                                       jnp.where(lane == AUG_C + 1, mid,
                                                 jnp.where(lane == AUG_C + 2, lo,
                                                           jnp.where(lane == AUG_ONE, 1.0, 0.0)))))
            qaug = jnp.where(lane < HEAD_DIM, qa, jnp.where(lane < AUG_ONE, -1.0, 0.0))
            qh_ref[:, a * LANES:(a + 1) * LANES] = qaug.astype(BF16)
            kh_ref[:, a * LANES:(a + 1) * LANES] = kaug.astype(BF16)

    out = pl.BlockSpec((bt, 2 * LANES), lambda i, hp: (i, hp))
    shape = jax.ShapeDtypeStruct((t, N_HEADS * LANES), BF16)
    return pl.pallas_call(
        body, name=name, grid=(t // bt, N_PAIRS),
        in_specs=[pl.BlockSpec((bt, LANES), lambda i, hp: (i, hp)),
                  pl.BlockSpec((bt, LANES), lambda i, hp: (i, N_PAIRS + hp)),
                  pl.BlockSpec((None, bt, LANES), lambda i, hp: (hp, i, 0))],
        out_specs=[out, out], out_shape=[shape, shape],
        compiler_params=_cparams("parallel", "parallel"),
    )(qkv, qkv, c_cols)


def _block_scalar(c_ref, a, start):
    return c_ref[a:a + 1, pl.ds(start, LANES)][:, 0:1]


def _attn_fwd_t(qh, kh, vt, c_pairs, name):
    t = qh.shape[0]
    bq = _pick(t, ATTN_BLOCK)
    nq = t // bq

    def body(q_ref, k_ref, vt_ref, c_ref, ot_ref, otb_ref, lse_ref, acc_ref):
        i = pl.program_id(1)
        q0 = pl.multiple_of(i * bq, bq)
        qs = (q_ref[:, 0:LANES], q_ref[:, LANES:2 * LANES])
        cq = [_block_scalar(c_ref, a, q0) for a in (0, 1)]
        acc_ref[...] = jnp.zeros_like(acc_ref)
        keep = _rows_iota((bq, bq)) <= lax.broadcasted_iota(jnp.int32, (bq, bq), 1)

        def step(j, carry, masked):
            k0 = pl.multiple_of(j * bq, bq)
            kb = k_ref[pl.ds(k0, bq), :]
            new = []
            for a in (0, 1):
                m_old, l_old = carry[2 * a], carry[2 * a + 1]
                st = lax.dot_general(kb[:, a * LANES:(a + 1) * LANES], qs[a], _DOT_DIMS["nt"],
                                     preferred_element_type=F32)
                if masked:
                    st = jnp.where(keep, st, NEG_BIG)
                sigma = cq[a] - _block_scalar(c_ref, a, k0)
                m_new = jnp.maximum(m_old, jnp.max(st, axis=0, keepdims=True) + sigma)
                pt = jnp.exp(st - (m_new - sigma))
                alpha = jnp.exp(m_old - m_new)
                l_new = alpha * l_old + jnp.sum(pt, axis=0, keepdims=True)
                vta = vt_ref[a * HEAD_DIM:(a + 1) * HEAD_DIM, pl.ds(k0, bq)]
                acc_ref[a] = alpha * acc_ref[a] + lax.dot_general(
                    vta, pt.astype(BF16), _DOT_DIMS["nn"], preferred_element_type=F32)
                new += [m_new, l_new]
            return tuple(new)

        neg = jnp.full((1, bq), NEG_BIG, F32)
        zero = jnp.zeros((1, bq), F32)
        carry = lax.fori_loop(0, i, lambda j, c: step(j, c, False), (neg, zero, neg, zero))
        m_a, l_a, m_b, l_b = step(i, carry, True)
        ot = jnp.concatenate([acc_ref[0] / l_a, acc_ref[1] / l_b], axis=0)
        ot_ref[...] = ot
        otb_ref[...] = ot.astype(BF16)
        lse_ref[...] = jnp.zeros_like(lse_ref)
        lse_ref[0:1, :] = m_a + jnp.log(l_a)
        lse_ref[1:2, :] = m_b + jnp.log(l_b)

    rows = pl.BlockSpec((None, SUBLANES, bq), lambda hp, i: (hp, 0, i))
    otile = pl.BlockSpec((LANES, bq), lambda hp, i: (hp, i))
    return pl.pallas_call(
        body, name=name, grid=(N_PAIRS, nq),
        in_specs=[pl.BlockSpec((bq, 2 * LANES), lambda hp, i: (i, hp)),
                  pl.BlockSpec((t, 2 * LANES), lambda hp, i: (0, hp)),
                  pl.BlockSpec((LANES, t), lambda hp, i: (hp, 0)),
                  pl.BlockSpec((None, SUBLANES, t), lambda hp, i: (hp, 0, 0))],
        out_specs=[otile, otile, rows],
        out_shape=[jax.ShapeDtypeStruct((D_MODEL, t), F32), jax.ShapeDtypeStruct((D_MODEL, t), BF16),
                   jax.ShapeDtypeStruct((N_PAIRS, SUBLANES, t), F32)],
        scratch_shapes=[pltpu.VMEM((2, HEAD_DIM, bq), F32)],
        compiler_params=_cparams("parallel", "arbitrary"),
    )(qh, kh, vt, c_pairs)


def _attn_delta_t(dot, ot, name):
    t = dot.shape[1]
    bt = _pick(t, 512)

    def body(do_ref, o_ref, d_ref, dob_ref):
        dob = do_ref[...].astype(BF16)
        prod = dob.astype(F32) * o_ref[...]
        d_ref[...] = jnp.zeros_like(d_ref)
        d_ref[0:1, :] = jnp.sum(prod[0:HEAD_DIM], axis=0, keepdims=True)
        d_ref[1:2, :] = jnp.sum(prod[HEAD_DIM:], axis=0, keepdims=True)
        dob_ref[...] = dob

    tile = pl.BlockSpec((LANES, bt), lambda hp, i: (hp, i))
    return pl.pallas_call(
        body, name=name, grid=(N_PAIRS, t // bt), in_specs=[tile, tile],
        out_specs=[pl.BlockSpec((None, SUBLANES, bt), lambda hp, i: (hp, 0, i)), tile],
        out_shape=[jax.ShapeDtypeStruct((N_PAIRS, SUBLANES, t), F32), jax.ShapeDtypeStruct((D_MODEL, t), BF16)],
        compiler_params=_cparams("parallel", "parallel"),
    )(dot, ot)


def _attn_bwd_t(qh, kh, kt, qkv, dob, dotb, c_pairs, lse_pairs, d_pairs, name):
    t = qh.shape[0]
    bk = _pick(t, ATTN_BLOCK)
    nk = t // bk
    scale = 1.0 / math.sqrt(HEAD_DIM)

    def body(k_ref, kt_ref, v_ref, q_ref, do_ref, dot_ref, c_ref, lse_ref, d_ref,
             dk_ref, dv_ref, dc_ref, dq_hbm, dka_ref, dva_ref, dqa_ref, sem):
        hp, j = pl.program_id(0), pl.program_id(1)
        k0 = pl.multiple_of(j * bk, bk)
        ks = (k_ref[:, 0:LANES], k_ref[:, LANES:2 * LANES])
        vb = v_ref[...]
        ck = [_block_scalar(c_ref, a, k0) for a in (0, 1)]
        dka_ref[...] = jnp.zeros_like(dka_ref)
        dva_ref[...] = jnp.zeros_like(dva_ref)

        @pl.when(j == 0)
        def _():
            dqa_ref[...] = jnp.zeros_like(dqa_ref)

        keep = _rows_iota((bk, bk)) <= lax.broadcasted_iota(jnp.int32, (bk, bk), 1)
        top = _rows_iota((LANES, bk)) < HEAD_DIM

        def step(i, masked):
            q0 = pl.multiple_of(i * bk, bk)
            qb = q_ref[pl.ds(q0, bk), :]
            dob_ = do_ref[pl.ds(q0, bk), :]
            dot2 = dot_ref[:, pl.ds(q0, bk)]
            zero = jnp.zeros_like(dot2)
            for a in (0, 1):
                qa = qb[:, a * LANES:(a + 1) * LANES]
                st = lax.dot_general(ks[a], qa, _DOT_DIMS["nt"], preferred_element_type=F32)
                sigma = _block_scalar(c_ref, a, q0) - ck[a]
                pt = jnp.exp(st - (lse_ref[a:a + 1, pl.ds(q0, bk)] - sigma))
                if masked:
                    pt = jnp.where(keep, pt, 0.0)
                dota = jnp.where(top, dot2, zero) if a == 0 else jnp.where(top, zero, dot2)
                dpt = lax.dot_general(vb, dota, _DOT_DIMS["nn"], preferred_element_type=F32)
                dstb = (pt * (dpt - d_ref[a:a + 1, pl.ds(q0, bk)])).astype(BF16)
                dva_ref[a] += lax.dot_general(pt.astype(BF16), dob_, _DOT_DIMS["nn"], preferred_element_type=F32)
                dka_ref[a] += lax.dot_general(dstb, qa, _DOT_DIMS["nn"], preferred_element_type=F32)
                dqa_ref[a, :, pl.ds(q0, bk)] += lax.dot_general(
                    kt_ref[a * LANES:(a + 1) * LANES, :], dstb, _DOT_DIMS["nn"], preferred_element_type=F32)

        def loop_body(i, carry):
            step(i, False)
            return carry

        step(j, True)
        lax.fori_loop(j + 1, nk, loop_body, 0)
        first = lax.broadcasted_iota(jnp.int32, (bk, LANES), 1) < HEAD_DIM
        dka, dkb = dka_ref[0], dka_ref[1]
        dk_ref[...] = (jnp.where(first, dka, pltpu.roll(dkb, HEAD_DIM, axis=1)) * scale).astype(BF16)
        dv_ref[...] = jnp.where(first, dva_ref[0], dva_ref[1]).astype(BF16)
        dc_ref[...] = jnp.zeros_like(dc_ref)
        dc_ref[0:1, :] = dka.T[AUG_C:AUG_C + 1, :]
        dc_ref[1:2, :] = dkb.T[AUG_C:AUG_C + 1, :]

        @pl.when(j == nk - 1)
        def _():
            for a in (0, 1):
                row0 = pl.multiple_of((2 * hp + a) * LANES, LANES)
                cp = pltpu.make_async_copy(dqa_ref.at[a], dq_hbm.at[pl.ds(row0, LANES), :], sem)
                cp.start()
                cp.wait()

    once = pl.Buffered(1)
    rows = pl.BlockSpec((None, SUBLANES, t), lambda hp, j: (hp, 0, 0))
    kv_out = pl.BlockSpec((bk, LANES), lambda hp, j: (j, hp))
    return pl.pallas_call(
        body, name=name, grid=(N_PAIRS, nk),
        in_specs=[pl.BlockSpec((bk, 2 * LANES), lambda hp, j: (j, hp)),
                  pl.BlockSpec((2 * LANES, bk), lambda hp, j: (hp, j)),
                  pl.BlockSpec((bk, LANES), lambda hp, j: (j, 2 * N_PAIRS + hp)),
                  pl.BlockSpec((t, 2 * LANES), lambda hp, j: (0, hp), pipeline_mode=once),
                  pl.BlockSpec((t, LANES), lambda hp, j: (0, hp), pipeline_mode=once),
                  pl.BlockSpec((LANES, t), lambda hp, j: (hp, 0), pipeline_mode=once),
                  rows, rows, rows],
        out_specs=[kv_out, kv_out, pl.BlockSpec((None, SUBLANES, bk), lambda hp, j: (hp, 0, j)),
                   pl.BlockSpec(memory_space=pltpu.HBM)],
        out_shape=[jax.ShapeDtypeStruct((t, D_MODEL), BF16), jax.ShapeDtypeStruct((t, D_MODEL), BF16),
                   jax.ShapeDtypeStruct((N_PAIRS, SUBLANES, t), F32),
                   jax.ShapeDtypeStruct((N_HEADS * LANES, t), F32)],
        scratch_shapes=[pltpu.VMEM((2, bk, LANES), F32), pltpu.VMEM((2, bk, LANES), F32),
                        pltpu.VMEM((2, LANES, t), F32), pltpu.SemaphoreType.DMA],
        compiler_params=_cparams("arbitrary", "arbitrary"),
    )(kh, kt, qkv, qh, dob, dotb, c_pairs, lse_pairs, d_pairs)


def _scan(a, u, name, reverse=False):
    t, c = a.shape
    bt, cb = _pick(t, ROW_TILE), _pick(c, 1024)
    nt = t // bt
    ngroups = bt // SUBLANES

    def body(a_ref, u_ref, h_ref, carry_ref, as_ref, us_ref):
        @pl.when(pl.program_id(1) == 0)
        def _():
            carry_ref[...] = jnp.zeros_like(carry_ref)

        av, uv = a_ref[...], u_ref[...]
        sub = _rows_iota((bt, cb)) % SUBLANES
        for s in (1, 2, 4):
            if reverse:
                a_sh, u_sh = pltpu.roll(av, bt - s, axis=0), pltpu.roll(uv, bt - s, axis=0)
                valid = sub < SUBLANES - s
            else:
                a_sh, u_sh = pltpu.roll(av, s, axis=0), pltpu.roll(uv, s, axis=0)
                valid = sub >= s
            uv = jnp.where(valid, uv + av * u_sh, uv)
            av = jnp.where(valid, av * a_sh, av)
        as_ref[...] = av
        us_ref[...] = uv
        edge = 0 if reverse else SUBLANES - 1
        pick = _rows_iota((SUBLANES, cb)) == edge

        def group(gi, carry):
            g = (ngroups - 1 - gi) if reverse else gi
            r0 = pl.multiple_of(g * SUBLANES, SUBLANES)
            h8 = us_ref[pl.ds(r0, SUBLANES), :] + as_ref[pl.ds(r0, SUBLANES), :] * carry
            h_ref[pl.ds(r0, SUBLANES), :] = h8
            return jnp.sum(jnp.where(pick, h8, 0.0), axis=0, keepdims=True)

        carry_ref[...] = lax.fori_loop(0, ngroups, group, carry_ref[...])

    if reverse:
        tile = pl.BlockSpec((bt, cb), lambda j, i: (nt - 1 - i, j))
    else:
        tile = pl.BlockSpec((bt, cb), lambda j, i: (i, j))
    return pl.pallas_call(
        body, name=name, grid=(c // cb, nt), in_specs=[tile, tile], out_specs=tile,
        out_shape=jax.ShapeDtypeStruct((t, c), F32),
        scratch_shapes=[pltpu.VMEM((1, cb), F32), pltpu.VMEM((bt, cb), F32), pltpu.VMEM((bt, cb), F32)],
        compiler_params=_cparams("parallel", "arbitrary"),
    )(a, u)


def _rg_conv_fwd(proj, cw, cb_, name):
    t = proj.shape[0]
    bt, cb = _pick(t, ROW_TILE), D_MODEL

    def body(x_ref, halo_ref, w_ref, b_ref, o_ref):
        keep = (pl.program_id(0) > 0).astype(F32)
        o_ref[...] = _causal_conv(x_ref[...], halo_ref[...] * keep, w_ref, CONV_B) + b_ref[...]

    tile = pl.BlockSpec((bt, cb), lambda i, j: (i, j))
    return pl.pallas_call(
        body, name=name, grid=(t // bt, 1),
        in_specs=[tile, _prev_halo_spec(bt, cb), pl.BlockSpec((CONV_B, cb), lambda i, j: (0, 0)),
                  pl.BlockSpec((1, cb), lambda i, j: (0, 0))],
        out_specs=tile, out_shape=jax.ShapeDtypeStruct((t, D_MODEL), F32),
        compiler_params=_cparams("parallel", "parallel"),
    )(proj, proj, cw, cb_.reshape(1, D_MODEL))


def _conv_bwd_w(x, dy, ksize, name):
    t, c = dy.shape
    bt = _pick(t, ROW_TILE)
    r = bt // SUBLANES

    def body(x_ref, halo_ref, dy_ref, dw_ref, db_ref):
        @pl.when(pl.program_id(0) == 0)
        def _():
            dw_ref[...] = jnp.zeros_like(dw_ref)
            db_ref[...] = jnp.zeros_like(db_ref)

        keep = (pl.program_id(0) > 0).astype(F32)
        xv, halo, dyv = x_ref[...], halo_ref[...] * keep, dy_ref[...]
        db_ref[...] += jnp.sum(dyv, axis=0, keepdims=True)
        for j in range(ksize):
            dw_ref[j:j + 1, :] += jnp.sum(dyv * _shift_down(xv, halo, ksize - 1 - j), axis=0, keepdims=True)

    tile = pl.BlockSpec((bt, c), lambda i: (i, 0))
    return pl.pallas_call(
        body, name=name, grid=(t // bt,),
        in_specs=[tile, pl.BlockSpec((SUBLANES, c), lambda i: (jnp.maximum(i * r - 1, 0), 0)), tile],
        out_specs=[pl.BlockSpec((ksize, c), lambda i: (0, 0)), pl.BlockSpec((1, c), lambda i: (0, 0))],
        out_shape=[jax.ShapeDtypeStruct((ksize, c), F32), jax.ShapeDtypeStruct((1, c), F32)],
        compiler_params=_cparams("arbitrary"),
    )(x, x, dy)


def _rg_gate_math(xc, wa_ref, wi_ref, ba_ref, bi_ref, lam_ref):
    xb = xc.astype(BF16)
    ra = lax.dot_general(xb, wa_ref[...], _DOT_DIMS["nn"], preferred_element_type=F32) + ba_ref[...]
    ia = lax.dot_general(xb, wi_ref[...], _DOT_DIMS["nn"], preferred_element_type=F32) + bi_ref[...]
    r, ig = _sigmoid(ra), _sigmoid(ia)
    sp = _softplus(-lam_ref[...])
    log_a = -LRU_C * r * sp
    a = jnp.exp(log_a)
    mult = jnp.sqrt(-_expm1(2.0 * log_a))
    return xb, r, ig, sp, a, mult


def _rg_gate_specs(bt, time_first):
    if time_first:
        tile = pl.BlockSpec((bt, BLOCK_B), lambda i, n: (i, n))
        w = pl.BlockSpec((None, BLOCK_B, BLOCK_B), lambda i, n: (n, 0, 0))
        v = pl.BlockSpec((None, 1, BLOCK_B), lambda i, n: (n, 0, 0))
    else:
        tile = pl.BlockSpec((bt, BLOCK_B), lambda n, i: (i, n))
        w = pl.BlockSpec((None, BLOCK_B, BLOCK_B), lambda n, i: (n, 0, 0))
        v = pl.BlockSpec((None, 1, BLOCK_B), lambda n, i: (n, 0, 0))
    return tile, w, v


def _rg_gate_fwd(xc, wa, ba, wi, bi, lam, name):
    t = xc.shape[0]
    bt = _pick(t, 512)

    def body(x_ref, wa_ref, wi_ref, ba_ref, bi_ref, lam_ref, a_ref, u_ref):
        xcv = x_ref[...]
        _, _, ig, _, a, mult = _rg_gate_math(xcv, wa_ref, wi_ref, ba_ref, bi_ref, lam_ref)
        a_ref[...] = a
        u_ref[...] = mult * (ig * xcv)

    tile, w, v = _rg_gate_specs(bt, True)
    return pl.pallas_call(
        body, name=name, grid=(t // bt, N_BLOCKS_B), in_specs=[tile, w, w, v, v, v], out_specs=[tile, tile],
        out_shape=[jax.ShapeDtypeStruct((t, D_MODEL), F32), jax.ShapeDtypeStruct((t, D_MODEL), F32)],
        compiler_params=_cparams("parallel", "parallel"),
    )(xc, wa, wi, ba, bi, lam)


def _rg_gate_bwd(xc, g, h, wa, ba, wi, bi, lam, name):
    t = xc.shape[0]
    bt = _pick(t, 512)
    rr = bt // SUBLANES

    def body(x_ref, g_ref, h_ref, hh_ref, wa_ref, wi_ref, ba_ref, bi_ref, lam_ref,
             dx_ref, dwa_ref, dwi_ref, dba_ref, dbi_ref, dlam_ref):
        @pl.when(pl.program_id(1) == 0)
        def _():
            for ref in (dwa_ref, dwi_ref, dba_ref, dbi_ref, dlam_ref):
                ref[...] = jnp.zeros_like(ref)

        keep = (pl.program_id(1) > 0).astype(F32)
        xcv, gv = x_ref[...], g_ref[...]
        xb, r, ig, sp, a, mult = _rg_gate_math(xcv, wa_ref, wi_ref, ba_ref, bi_ref, lam_ref)
        h_prev = _shift_down(h_ref[...], hh_ref[...] * keep, 1)
        da = gv * h_prev
        dmult = gv * ig * xcv
        dig = gv * mult * xcv
        dxc = gv * mult * ig
        dlog_a = da * a - dmult * (a * a / mult)
        dr = dlog_a * (-LRU_C * sp)
        dsp = jnp.sum(dlog_a * (-LRU_C * r), axis=0, keepdims=True)
        dlam_ref[...] += dsp * (-_sigmoid(-lam_ref[...]))
        dra = dr * r * (1.0 - r)
        dia = dig * ig * (1.0 - ig)
        dba_ref[...] += jnp.sum(dra, axis=0, keepdims=True)
        dbi_ref[...] += jnp.sum(dia, axis=0, keepdims=True)
        drab, diab = dra.astype(BF16), dia.astype(BF16)
        dwa_ref[...] += lax.dot_general(xb, drab, _DOT_DIMS["tn"], preferred_element_type=F32)
        dwi_ref[...] += lax.dot_general(xb, diab, _DOT_DIMS["tn"], preferred_element_type=F32)
        dxc = dxc + lax.dot_general(drab, wa_ref[...], _DOT_DIMS["nt"], preferred_element_type=F32)
        dxc = dxc + lax.dot_general(diab, wi_ref[...], _DOT_DIMS["nt"], preferred_element_type=F32)
        dx_ref[...] = dxc

    tile, w, v = _rg_gate_specs(bt, False)
    halo = pl.BlockSpec((SUBLANES, BLOCK_B), lambda n, i: (jnp.maximum(i * rr - 1, 0), n))
    wshape = jax.ShapeDtypeStruct((N_BLOCKS_B, BLOCK_B, BLOCK_B), F32)
    vshape = jax.ShapeDtypeStruct((N_BLOCKS_B, 1, BLOCK_B), F32)
    return pl.pallas_call(
        body, name=name, grid=(N_BLOCKS_B, t // bt),
        in_specs=[tile, tile, tile, halo, w, w, v, v, v],
        out_specs=[tile, w, w, v, v, v],
        out_shape=[jax.ShapeDtypeStruct((t, D_MODEL), F32), wshape, wshape, vshape, vshape, vshape],
        compiler_params=_cparams("parallel", "arbitrary"),
    )(xc, g, h, h, wa, wi, ba, bi, lam)


def _rg_out_fwd(h, proj, name):
    t = h.shape[0]
    bt = _pick(t, ROW_TILE)

    def body(h_ref, g_ref, y_ref):
        y_ref[...] = (h_ref[...] * _gelu(g_ref[...])).astype(BF16)

    tile = pl.BlockSpec((bt, D_MODEL), lambda i: (i, 0))
    return pl.pallas_call(
        body, name=name, grid=(t // bt,), in_specs=[tile, pl.BlockSpec((bt, D_MODEL), lambda i: (i, 1))],
        out_specs=tile, out_shape=jax.ShapeDtypeStruct((t, D_MODEL), BF16),
        compiler_params=_cparams("parallel"),
    )(h, proj)


def _rg_out_bwd(dy, h, proj, name):
    t = h.shape[0]
    bt = _pick(t, ROW_TILE)

    def body(dy_ref, h_ref, g_ref, dh_ref, dg_ref):
        gl, glg = _gelu_and_grad(g_ref[...])
        dyv = dy_ref[...]
        dh_ref[...] = dyv * gl
        dg_ref[...] = (dyv * h_ref[...] * glg).astype(BF16)

    tile = pl.BlockSpec((bt, D_MODEL), lambda i: (i, 0))
    return pl.pallas_call(
        body, name=name, grid=(t // bt,),
        in_specs=[tile, tile, pl.BlockSpec((bt, D_MODEL), lambda i: (i, 1))], out_specs=[tile, tile],
        out_shape=[jax.ShapeDtypeStruct((t, D_MODEL), F32), jax.ShapeDtypeStruct((t, D_MODEL), BF16)],
        compiler_params=_cparams("parallel"),
    )(dy, h, proj)


def _shift_up_one(a, name):
    t, c = a.shape
    bt = _pick(t, ROW_TILE)

    def body(a_ref, halo_ref, o_ref):
        o_ref[...] = _shift_up(a_ref[...], halo_ref[...], 1)

    tile = pl.BlockSpec((bt, c), lambda i, j: (i, j))
    return pl.pallas_call(
        body, name=name, grid=(t // bt, 1), in_specs=[tile, _next_halo_spec(bt, c, t)], out_specs=tile,
        out_shape=jax.ShapeDtypeStruct((t, c), F32), compiler_params=_cparams("parallel", "parallel"),
    )(a, a)


ADAM_ROWS = 64


def _adamw(recv, w, m, v, name):
    _, r, c = recv.shape
    br = ADAM_ROWS
    assert r % br == 0

    def body(r_ref, w_ref, m_ref, v_ref, g_ref, d_ref, nm_ref, nv_ref):
        g = r_ref[0].astype(F32)
        for s in range(1, N_DEV):
            g = g + r_ref[s].astype(F32)
        m_new = ADAM_B1 * m_ref[...] + (1.0 - ADAM_B1) * g
        v_new = ADAM_B2 * v_ref[...] + (1.0 - ADAM_B2) * (g * g)
        m_hat = m_new / (1.0 - ADAM_B1 ** ADAM_STEP)
        v_hat = v_new / (1.0 - ADAM_B2 ** ADAM_STEP)
        g_ref[...] = g
        d_ref[...] = -ADAM_LR * (m_hat / (jnp.sqrt(v_hat) + ADAM_EPS) + ADAM_WD * w_ref[...])
        nm_ref[...] = m_new
        nv_ref[...] = v_new

    tile = pl.BlockSpec((br, c), lambda i: (i, 0))
    shape = jax.ShapeDtypeStruct((r, c), F32)
    return pl.pallas_call(
        body, name=name, grid=(r // br,),
        in_specs=[pl.BlockSpec((N_DEV, br, c), lambda i: (0, i, 0)), tile, tile, tile],
        out_specs=[tile] * 4, out_shape=[shape] * 4, compiler_params=_cparams("parallel"),
    )(recv, w, m, v)


def _exchange(src, scatter, name):
    slab = src.shape[1:] if scatter else src.shape

    def body(src_ref, out_ref, send_sems, recv_sems, local_sem):
        pos = [lax.axis_index(ax) for ax in MESH_AXES]
        me = 4 * pos[0] + 2 * pos[1] + pos[2]

        def peer_of(k):
            p = [(1 - pos[b]) if (k >> (2 - b)) & 1 else pos[b] for b in range(3)]
            return tuple(p), 4 * p[0] + 2 * p[1] + p[2]

        def copy(k):
            peer, peer_idx = peer_of(k)
            return pltpu.make_async_remote_copy(
                src_ref=src_ref.at[peer_idx] if scatter else src_ref, dst_ref=out_ref.at[me],
                send_sem=send_sems.at[k - 1], recv_sem=recv_sems.at[k - 1],
                device_id=peer, device_id_type=pl.DeviceIdType.MESH)

        def arrival(k):
            peer, peer_idx = peer_of(k)
            return pltpu.make_async_remote_copy(
                src_ref=src_ref.at[me] if scatter else src_ref, dst_ref=out_ref.at[peer_idx],
                send_sem=send_sems.at[k - 1], recv_sem=recv_sems.at[k - 1],
                device_id=peer, device_id_type=pl.DeviceIdType.MESH)

        mine = pltpu.make_async_copy(src_ref.at[me] if scatter else src_ref, out_ref.at[me], local_sem)
        mine.start()
        sends = [copy(k) for k in range(1, N_DEV)]
        for cp in sends:
            cp.start()
        for k in range(1, N_DEV):
            arrival(k).wait_recv()
        for cp in sends:
            cp.wait_send()
        mine.wait()

    hbm = pl.BlockSpec(memory_space=pltpu.HBM)
    return pl.pallas_call(
        body, name=name, in_specs=[hbm], out_specs=hbm,
        out_shape=jax.ShapeDtypeStruct((N_DEV,) + tuple(slab), src.dtype),
        scratch_shapes=[pltpu.SemaphoreType.DMA((N_DEV - 1,)), pltpu.SemaphoreType.DMA((N_DEV - 1,)),
                        pltpu.SemaphoreType.DMA],
        compiler_params=pltpu.CompilerParams(has_side_effects=True),
    )(src)


WEIGHTS = ['a_w_in', 'a_b_f', 'a_w_out', 'b_w_in', 'b_conv_w', 'b_conv_b', 'b_w_a', 'b_b_a', 'b_w_i', 'b_b_i',
           'b_lam', 'b_w_out', 'f_w_up', 'f_conv_w', 'f_conv_b', 'f_w_down', 'ln1_g', 'ln1_b', 'ln2_g', 'ln2_b',
           'ple_w', 'ple_gate_w', 'ple_gate_b']
SHARD_AXIS = {'a_w_in': 2, 'a_b_f': None, 'a_w_out': 1, 'b_w_in': 2, 'b_conv_w': 2, 'b_conv_b': 1, 'b_w_a': None,
              'b_b_a': None, 'b_w_i': None, 'b_b_i': None, 'b_lam': 1, 'b_w_out': 1, 'f_w_up': 2, 'f_conv_w': 2,
              'f_conv_b': None, 'f_w_down': 1, 'ln1_g': None, 'ln1_b': None, 'ln2_g': None, 'ln2_b': None,
              'ple_w': 2, 'ple_gate_w': 1, 'ple_gate_b': None}
MATMUL_WEIGHTS = ['a_w_in', 'a_w_out', 'b_w_in', 'b_w_out', 'f_w_up', 'f_w_down', 'ple_w', 'ple_gate_w']
SMALL_SHARDED = ['b_conv_w', 'b_conv_b', 'b_lam', 'f_conv_w']
PACK_COLS = 1024


def _to_shards(full, axis):
    return jnp.stack(jnp.split(full, N_DEV, axis=axis))


def _from_shards(pieces, axis):
    return jnp.concatenate([pieces[d] for d in range(N_DEV)], axis=axis)


PIECE_ROWS = 16


def _piece_rows(size):
    rows = -(-size // PACK_COLS)
    return -(-rows // PIECE_ROWS) * PIECE_ROWS


def _pack_pieces(pieces, total_mult=PIECE_ROWS):
    lead = pieces[0].shape[:-1]
    blocks = []
    for pc in pieces:
        n = pc.shape[-1]
        rows = _piece_rows(n)
        pad = [(0, 0)] * len(lead) + [(0, rows * PACK_COLS - n)]
        blocks.append(jnp.pad(pc, pad).reshape(lead + (rows, PACK_COLS)))
    total = sum(b.shape[-2] for b in blocks)
    extra = -total % total_mult
    if extra:
        blocks.append(jnp.zeros(lead + (extra, PACK_COLS), pieces[0].dtype))
    return jnp.concatenate(blocks, axis=len(lead))


def _unpack_pieces(packed, shapes):
    lead = packed.shape[:-2]
    out, row = [], 0
    for shp in shapes:
        size = math.prod(shp)
        rows = _piece_rows(size)
        block = lax.slice_in_dim(packed, row, row + rows, axis=len(lead))
        flat = block.reshape(lead + (rows * PACK_COLS,))
        out.append(lax.slice_in_dim(flat, 0, size, axis=len(lead)).reshape(lead + tuple(shp)))
        row += rows
    return out


def _gather_weights(local, names, dtype, name):
    packed = _pack_pieces([local[n].astype(dtype).reshape(-1) for n in names])
    gathered = _exchange(packed, False, name)
    pieces = _unpack_pieces(gathered, [local[n].shape for n in names])
    return {n: _from_shards(pc, SHARD_AXIS[n]) for n, pc in zip(names, pieces)}


def _mixer_a_fwd(tag, xb, w):
    qkv = _mm(xb, w["wqkv"], "nn", BF16, f"{tag}_qkv")
    fg = _mm(xb, w["wf"], "nn", F32, f"{tag}_fgproj")
    fg_rows = fg[:, :N_HEADS].T
    c_rows = _fgate_fwd(fg_rows, w["b_f"], f"{tag}_fgate")
    c_pairs = _rows_to_pairs(c_rows)
    qh, kh = _attn_prep(qkv, _rows_to_cols(c_rows), f"{tag}_attn_prep")
    ot, otb, lse_pairs = _attn_fwd_t(qh, kh, qkv[:, 2 * D_MODEL:].T, c_pairs, f"{tag}_attn")
    m = _mm(otb, w["wout"], "tn", F32, f"{tag}_oproj")
    return m, dict(qkv=qkv, qh=qh, kh=kh, fg_rows=fg_rows, c_pairs=c_pairs, ot=ot, otb=otb, lse_pairs=lse_pairs)


def _mixer_a_bwd(tag, dz, dzb, xb, w, s):
    t = xb.shape[0]
    dot = _mm(w["wout"], dzb, "nt", F32, f"{tag}_b_do")
    g_wout = _mm(s["otb"], dzb, "nn", F32, f"{tag}_b_dwout")
    d_pairs, dotb = _attn_delta_t(dot, s["ot"], f"{tag}_b_delta")
    dk, dv, dck_pairs, dq_aug = _attn_bwd_t(s["qh"], s["kh"], s["kh"].T, s["qkv"], dotb.T, dotb, s["c_pairs"],
                                            s["lse_pairs"], d_pairs, f"{tag}_b_attn")
    dq_aug = dq_aug.reshape(N_HEADS, LANES, t)
    dfg_rows, db_f = _fgate_bwd(dck_pairs[:, :2, :].reshape(N_HEADS, t), dq_aug[:, AUG_ONE, :], s["fg_rows"],
                                w["b_f"], f"{tag}_b_fgate")
    dq = dq_aug[:, :HEAD_DIM, :].reshape(D_MODEL, t).astype(BF16).T
    dqkv = jnp.concatenate([dq, dk, dv], axis=1)
    dfg = jnp.pad(dfg_rows.T, ((0, 0), (0, LANES - N_HEADS))).astype(BF16)
    dx = _mm(dqkv, w["wqkv"], "nt", F32, f"{tag}_b_dx_qkv", add=dz, add_scale=ALPHA)
    dx = _mm(dfg, w["wf"], "nt", F32, f"{tag}_b_dx_fg", add=dx)
    g_wqkv = _mm(xb, dqkv, "tn", F32, f"{tag}_b_dwqkv")
    g_wf = _mm(xb, dfg, "tn", F32, f"{tag}_b_dwf")[:, :N_HEADS]
    grads = dict(a_w_in=jnp.concatenate([g_wqkv, g_wf], axis=1), a_b_f=db_f.reshape(N_HEADS), a_w_out=g_wout)
    return dx, grads


def _mixer_b_fwd(tag, xb, w):
    proj = _mm(xb, w["win"], "nn", F32, f"{tag}_proj")
    xc = _rg_conv_fwd(proj, w["conv_w"], w["conv_b"], f"{tag}_conv")
    a, u = _rg_gate_fwd(xc, w["wa"], w["ba"], w["wi"], w["bi"], w["lam"], f"{tag}_gate")
    h = _scan(a, u, f"{tag}_scan")
    y = _rg_out_fwd(h, proj, f"{tag}_out")
    m = _mm(y, w["wout"], "nn", F32, f"{tag}_oproj")
    return m, dict(proj=proj, xc=xc, a=a, h=h, y=y)


def _mixer_b_bwd(tag, dz, dzb, xb, w, s):
    dy = _mm(dzb, w["wout"], "nt", F32, f"{tag}_b_dy")
    g_wout = _mm(s["y"], dzb, "tn", F32, f"{tag}_b_dwout")
    dh, dgate = _rg_out_bwd(dy, s["h"], s["proj"], f"{tag}_b_out")
    g = _scan(_shift_up_one(s["a"], f"{tag}_b_shift"), dh, f"{tag}_b_scan", reverse=True)
    dxc, g_wa, g_wi, g_ba, g_bi, g_lam = _rg_gate_bwd(
        s["xc"], g, s["h"], w["wa"], w["ba"], w["wi"], w["bi"], w["lam"], f"{tag}_b_gate")
    dxp = _conv_bwd_x(dxc, w["conv_w"], f"{tag}_b_convx", BF16)
    g_cw, g_cb = _conv_bwd_w(s["proj"], dxc, CONV_B, f"{tag}_b_convw")
    dproj = jnp.concatenate([dxp, dgate], axis=1)
    dx = _mm(dproj, w["win"], "nt", F32, f"{tag}_b_dx", add=dz, add_scale=ALPHA)
    g_win = _mm(xb, dproj, "tn", F32, f"{tag}_b_dwin")
    grads = dict(b_w_in=g_win, b_conv_w=g_cw, b_conv_b=g_cb.reshape(D_MODEL), b_w_a=g_wa,
                 b_b_a=g_ba.reshape(N_BLOCKS_B, BLOCK_B), b_w_i=g_wi, b_b_i=g_bi.reshape(N_BLOCKS_B, BLOCK_B),
                 b_lam=g_lam.reshape(D_MODEL), b_w_out=g_wout)
    return dx, grads


def _layer_fwd(i, x, xb, pb, w):
    tag = f"L{i}"
    mix = _mixer_a_fwd if i % 2 == 0 else _mixer_b_fwd
    m, sm = mix(tag, xb, w)
    x1, x1b, z1 = _ln_fwd(x, m, w["ln1_g"], w["ln1_b"], f"{tag}_ln1")
    h = _mm(x1b, w["wup"], "nn", F32, f"{tag}_ffn_up")
    y = _ffn_mid_fwd(h, w["fconv_w"], w["fconv_b"], f"{tag}_ffn_mid")
    ff = _mm(y, w["wdown"], "nn", F32, f"{tag}_ffn_down")
    x2, x2b, z2 = _ln_fwd(x1, ff, w["ln2_g"], w["ln2_b"], f"{tag}_ln2")
    gl = _mm(x2b, w["wg"], "nn", F32, f"{tag}_ple_gate")
    e = _mm(pb, w["wp"], "nn", F32, f"{tag}_ple_emb")
    x3, x3b = _ple_fwd(x2, gl, e, w["bg"], f"{tag}_ple")
    saved = dict(mixer=sm, xb=xb, x1b=x1b, z1=z1, h=h, y=y, x2b=x2b, z2=z2, gl=gl, e=e, pb=pb)
    return x3, x3b, saved


def _layer_bwd(i, dx3, w, s):
    tag = f"L{i}"
    dgl, de, g_bg = _ple_bwd(dx3, s["gl"], s["e"], w["bg"], f"{tag}_b_ple")
    g_wg = _mm(s["x2b"], dgl, "tn", F32, f"{tag}_b_dwg")
    g_wp = _mm(s["pb"], de, "tn", F32, f"{tag}_b_dwp")
    dx2 = _mm(dgl, w["wg"], "nt", F32, f"{tag}_b_dx2", add=dx3)
    dz2, dz2b, g_ln2g, g_ln2b = _ln_bwd(dx2, s["z2"], w["ln2_g"], f"{tag}_b_ln2")
    dy = _mm(dz2b, w["wdown"], "nt", F32, f"{tag}_b_dy")
    g_wdown = _mm(s["y"], dz2b, "tn", F32, f"{tag}_b_dwdown")
    dval, dgate, g_fcw, g_fcb = _ffn_mid_bwd_a(s["h"], dy, w["fconv_w"], w["fconv_b"], f"{tag}_b_ffn_mid")
    dhv = _conv_bwd_x(dval, w["fconv_w"][:, :D_FF], f"{tag}_b_convx_v", BF16)
    dhg = _conv_bwd_x(dgate, w["fconv_w"][:, D_FF:], f"{tag}_b_convx_g", BF16)
    dx1 = _mm(dhv, w["wup"][:, :D_FF], "nt", F32, f"{tag}_b_dx1_v", add=dz2, add_scale=ALPHA)
    dx1 = _mm(dhg, w["wup"][:, D_FF:], "nt", F32, f"{tag}_b_dx1_g", add=dx1)
    g_wup = jnp.concatenate([_mm(s["x1b"], dhv, "tn", F32, f"{tag}_b_dwup_v"),
                             _mm(s["x1b"], dhg, "tn", F32, f"{tag}_b_dwup_g")], axis=1)
    dz1, dz1b, g_ln1g, g_ln1b = _ln_bwd(dx1, s["z1"], w["ln1_g"], f"{tag}_b_ln1")
    mix_bwd = _mixer_a_bwd if i % 2 == 0 else _mixer_b_bwd
    dx, g_mix = mix_bwd(tag, dz1, dz1b, s["xb"], w, s["mixer"])
    grads = dict(f_w_up=g_wup, f_conv_w=g_fcw, f_conv_b=g_fcb.reshape(2 * D_FF), f_w_down=g_wdown,
                 ln1_g=g_ln1g.reshape(D_MODEL), ln1_b=g_ln1b.reshape(D_MODEL), ln2_g=g_ln2g.reshape(D_MODEL),
                 ln2_b=g_ln2b.reshape(D_MODEL), ple_w=g_wp, ple_gate_w=g_wg, ple_gate_b=g_bg.reshape(D_MODEL))
    return dx, g_mix, grads


def _layer_weights(i, full, rep):
    j = i // 2
    w = dict(ln1_g=rep["ln1_g"][i], ln1_b=rep["ln1_b"][i], ln2_g=rep["ln2_g"][i], ln2_b=rep["ln2_b"][i],
             wup=full["f_w_up"][i], fconv_w=full["f_conv_w"][i], fconv_b=rep["f_conv_b"][i],
             wdown=full["f_w_down"][i], wp=full["ple_w"][i], wg=full["ple_gate_w"][i], bg=rep["ple_gate_b"][i])
    if i % 2 == 0:
        w_in = full["a_w_in"][j]
        w.update(wqkv=w_in[:, :3 * D_MODEL],
                 wf=jnp.pad(w_in[:, 3 * D_MODEL:], ((0, 0), (0, LANES - N_HEADS))),
                 b_f=rep["a_b_f"][j], wout=full["a_w_out"][j])
    else:
        w.update(win=full["b_w_in"][j], conv_w=full["b_conv_w"][j], conv_b=full["b_conv_b"][j],
                 wa=rep["b_w_a"][j].astype(BF16), wi=rep["b_w_i"][j].astype(BF16),
                 ba=rep["b_b_a"][j].reshape(N_BLOCKS_B, 1, BLOCK_B), bi=rep["b_b_i"][j].reshape(N_BLOCKS_B, 1, BLOCK_B),
                 lam=full["b_lam"][j].reshape(N_BLOCKS_B, 1, BLOCK_B), wout=full["b_w_out"][j])
    return w


def _fwd_bwd(x, p, target, full, rep):
    weights = [_layer_weights(i, full, rep) for i in range(DEPTH)]
    xb = x.astype(BF16)
    pb = p.astype(BF16)
    saved = []
    for i in range(DEPTH):
        x, xb, s = _layer_fwd(i, x, xb, pb[i], weights[i])
        saved.append(s)
    dx, loss_local = _loss_head(x, target, "loss_head")

    per_layer = {n: [None] * (DEPTH if n.startswith(("f_", "ln", "ple")) else DEPTH // 2) for n in WEIGHTS}
    for i in reversed(range(DEPTH)):
        dx, g_mix, g_layer = _layer_bwd(i, dx, weights[i], saved[i])
        for n, g in g_layer.items():
            per_layer[n][i] = g
        for n, g in g_mix.items():
            per_layer[n][i // 2] = g
    return loss_local, dx, {n: jnp.stack(per_layer[n]) for n in WEIGHTS}


def _train_step(x, p, target, local, moments_m, moments_v):
    full = _gather_weights(local, MATMUL_WEIGHTS, BF16, "gather_matmul_weights")
    full.update(_gather_weights(local, SMALL_SHARDED, F32, "gather_small_weights"))
    rep = {n: local[n] for n in WEIGHTS if SHARD_AXIS[n] is None}
    loss_local, dx, grads_full = _fwd_bwd(x, p, target, full, rep)

    unpacked = [{} for _ in range(4)]
    for group, dtype, tag in ((MATMUL_WEIGHTS, BF16, "big"), ([n for n in WEIGHTS if n not in MATMUL_WEIGHTS], F32, "small")):
        pieces = []
        for n in group:
            g = grads_full[n].astype(dtype)
            if SHARD_AXIS[n] is None:
                pieces.append(jnp.broadcast_to(g.reshape(1, -1), (N_DEV, g.size)))
            else:
                pieces.append(_to_shards(g, SHARD_AXIS[n]).reshape(N_DEV, -1))
        recv = _exchange(_pack_pieces(pieces, ADAM_ROWS), True, f"reduce_scatter_grads_{tag}")

        def pack_local(d):
            return _pack_pieces([d[n].astype(F32).reshape(-1) for n in group], ADAM_ROWS)

        outs = _adamw(recv, pack_local(local), pack_local(moments_m), pack_local(moments_v), f"adamw_{tag}")
        shapes = [local[n].shape for n in group]
        for dst, packed in zip(unpacked, outs):
            dst.update(zip(group, _unpack_pieces(packed, shapes)))
    return loss_local, dx, unpacked


def kernel(x, p, a_w_in, a_b_f, a_w_out, b_w_in, b_conv_w, b_conv_b, b_w_a, b_b_a, b_w_i, b_b_i, b_lam, b_w_out, f_w_up, f_conv_w, f_conv_b, f_w_down, ln1_g, ln1_b, ln2_g, ln2_b, ple_w, ple_gate_w, ple_gate_b, loss_target, m_a_w_in, m_a_b_f, m_a_w_out, m_b_w_in, m_b_conv_w, m_b_conv_b, m_b_w_a, m_b_b_a, m_b_w_i, m_b_b_i, m_b_lam, m_b_w_out, m_f_w_up, m_f_conv_w, m_f_conv_b, m_f_w_down, m_ln1_g, m_ln1_b, m_ln2_g, m_ln2_b, m_ple_w, m_ple_gate_w, m_ple_gate_b, v_a_w_in, v_a_b_f, v_a_w_out, v_b_w_in, v_b_conv_w, v_b_conv_b, v_b_w_a, v_b_b_a, v_b_w_i, v_b_b_i, v_b_lam, v_b_w_out, v_f_w_up, v_f_conv_w, v_f_conv_b, v_f_w_down, v_ln1_g, v_ln1_b, v_ln2_g, v_ln2_b, v_ple_w, v_ple_gate_w, v_ple_gate_b):
    given = dict(locals())
    local = {n: given[n] for n in WEIGHTS}
    mom_m = {n: given["m_" + n] for n in WEIGHTS}
    mom_v = {n: given["v_" + n] for n in WEIGHTS}
    t = x.shape[1]
    loss_local, dx, (grad, delta, new_m, new_v) = _train_step(
        x.reshape(t, D_MODEL), p.reshape(DEPTH, t, D_PLE), loss_target.reshape(t, D_MODEL), local, mom_m, mom_v)
    loss = lax.psum(loss_local, MESH_AXES)
    return (loss, dx.reshape(1, t, D_MODEL), *[grad[n] for n in WEIGHTS], *[delta[n] for n in WEIGHTS],
            *[new_m[n] for n in WEIGHTS], *[new_v[n] for n in WEIGHTS])
```

```python
import math

import jax
import jax.numpy as jnp
from jax import lax
from jax.experimental import pallas as pl
from jax.experimental.pallas import tpu as pltpu

F32 = jnp.float32
BF16 = jnp.bfloat16

D_MODEL = 1024
DEPTH = 4
N_HEADS = 16
HEAD_DIM = 64
N_PAIRS = N_HEADS // 2
N_BLOCKS_B = 8
BLOCK_B = 128
CONV_B = 4
LRU_C = 8.0
D_FF = 2816
CONV_F = 3
D_PLE = 256
LN_EPS = 1e-5
ALPHA = (2.0 * DEPTH) ** 0.25
ADAM_LR, ADAM_B1, ADAM_B2, ADAM_EPS, ADAM_WD, ADAM_STEP = 0.001, 0.9, 0.999, 1e-08, 0.01, 10
N_DEV = 8
MESH_AXES = ("x", "y", "c")

LANES = 128
SUBLANES = 8
VMEM_LIMIT_BYTES = 56 * 1024 * 1024
ATTN_BLOCK = 512
ROW_TILE = 256
NEG_BIG = -1e30


def _cparams(*sem):
    return pltpu.CompilerParams(dimension_semantics=sem if sem else None, vmem_limit_bytes=VMEM_LIMIT_BYTES)


def _pick(n, pref):
    if n <= pref:
        return n
    best = None
    for t in range(LANES, pref + 1, LANES):
        if n % t == 0:
            best = t
    assert best is not None, (n, pref)
    return best


def _sigmoid(x):
    return 1.0 / (1.0 + jnp.exp(-x))


def _log1p(x):
    u = 1.0 + x
    d = u - 1.0
    return jnp.where(d == 0.0, x, jnp.log(u) * (x / jnp.where(d == 0.0, 1.0, d)))


def _expm1(x):
    u = jnp.exp(x)
    lu = jnp.log(u)
    return jnp.where(u == 1.0, x, (u - 1.0) * (x / jnp.where(u == 1.0, 1.0, lu)))


def _softplus(x):
    return jnp.maximum(x, 0.0) + _log1p(jnp.exp(-jnp.abs(x)))


_GELU_C = math.sqrt(2.0 / math.pi)


def _gelu(x):
    return 0.5 * x * (1.0 + jnp.tanh(_GELU_C * (x + 0.044715 * x * x * x)))


def _gelu_and_grad(x):
    t = jnp.tanh(_GELU_C * (x + 0.044715 * x * x * x))
    du = _GELU_C * (1.0 + 3.0 * 0.044715 * x * x)
    return 0.5 * x * (1.0 + t), 0.5 * (1.0 + t) + 0.5 * x * (1.0 - t * t) * du


_DOT_DIMS = {"nn": (((1,), (0,)), ((), ())), "nt": (((1,), (1,)), ((), ())), "tn": (((0,), (0,)), ((), ()))}


def _mm(a, b, mode, out_dtype, name, add=None, add_scale=1.0, tm=512, tn=1408, tk=1408):
    if mode == "nn":
        (m, k), (k2, n) = a.shape, b.shape
    elif mode == "nt":
        (m, k), (n, k2) = a.shape, b.shape
    else:
        (k, m), (k2, n) = a.shape, b.shape
    assert k == k2 and a.dtype == BF16 and b.dtype == BF16, (a.shape, b.shape, a.dtype, b.dtype)
    tm, tn, tk = _pick(m, tm), _pick(n, tn), _pick(k, tk)
    nk = k // tk
    dims = _DOT_DIMS[mode]

    def body(*refs):
        if add is None:
            a_ref, b_ref, o_ref, acc_ref = refs
        else:
            a_ref, b_ref, add_ref, o_ref, acc_ref = refs
        kk = pl.program_id(2)

        @pl.when(kk == 0)
        def _():
            acc_ref[...] = jnp.zeros_like(acc_ref)

        acc_ref[...] += lax.dot_general(a_ref[...], b_ref[...], dims, preferred_element_type=F32)

        @pl.when(kk == nk - 1)
        def _():
            r = acc_ref[...]
            if add is not None:
                r = r + add_scale * add_ref[...]
            o_ref[...] = r.astype(out_dtype)

    a_spec = (pl.BlockSpec((tk, tm), lambda j, i, kk: (kk, i)) if mode == "tn"
              else pl.BlockSpec((tm, tk), lambda j, i, kk: (i, kk)))
    b_spec = (pl.BlockSpec((tn, tk), lambda j, i, kk: (j, kk)) if mode == "nt"
              else pl.BlockSpec((tk, tn), lambda j, i, kk: (kk, j)))
    o_spec = pl.BlockSpec((tm, tn), lambda j, i, kk: (i, j))
    in_specs, args = [a_spec, b_spec], [a, b]
    if add is not None:
        assert add.shape == (m, n) and add.dtype == F32
        in_specs.append(o_spec)
        args.append(add)
    return pl.pallas_call(
        body, name=name, grid=(n // tn, m // tm, nk),
        in_specs=in_specs, out_specs=o_spec,
        out_shape=jax.ShapeDtypeStruct((m, n), out_dtype),
        scratch_shapes=[pltpu.VMEM((tm, tn), F32)],
        compiler_params=_cparams("parallel", "parallel", "arbitrary"),
    )(*args)


def _ln_fwd(x, m, g, b, name):
    t, d = x.shape
    bt = _pick(t, ROW_TILE)

    def body(x_ref, m_ref, g_ref, b_ref, y_ref, yb_ref, z_ref):
        z = ALPHA * x_ref[...] + m_ref[...]
        mu = jnp.mean(z, axis=-1, keepdims=True)
        zc = z - mu
        var = jnp.mean(zc * zc, axis=-1, keepdims=True)
        y = zc * lax.rsqrt(var + LN_EPS) * g_ref[...] + b_ref[...]
        y_ref[...] = y
        yb_ref[...] = y.astype(BF16)
        z_ref[...] = z

    row = pl.BlockSpec((bt, d), lambda i: (i, 0))
    vec = pl.BlockSpec((1, d), lambda i: (0, 0))
    return pl.pallas_call(
        body, name=name, grid=(t // bt,), in_specs=[row, row, vec, vec], out_specs=[row, row, row],
        out_shape=[jax.ShapeDtypeStruct((t, d), F32), jax.ShapeDtypeStruct((t, d), BF16),
                   jax.ShapeDtypeStruct((t, d), F32)],
        compiler_params=_cparams("parallel"),
    )(x, m, g.reshape(1, d), b.reshape(1, d))


def _ln_bwd(dy, z, g, name):
    t, d = dy.shape
    bt = _pick(t, ROW_TILE)

    def body(dy_ref, z_ref, g_ref, dz_ref, dzb_ref, dg_ref, db_ref):
        @pl.when(pl.program_id(0) == 0)
        def _():
            dg_ref[...] = jnp.zeros_like(dg_ref)
            db_ref[...] = jnp.zeros_like(db_ref)

        z = z_ref[...]
        dyv = dy_ref[...]
        mu = jnp.mean(z, axis=-1, keepdims=True)
        zc = z - mu
        var = jnp.mean(zc * zc, axis=-1, keepdims=True)
        rstd = lax.rsqrt(var + LN_EPS)
        xhat = zc * rstd
        dxh = dyv * g_ref[...]
        m1 = jnp.mean(dxh, axis=-1, keepdims=True)
        m2 = jnp.mean(dxh * xhat, axis=-1, keepdims=True)
        dz = rstd * (dxh - m1 - xhat * m2)
        dz_ref[...] = dz
        dzb_ref[...] = dz.astype(BF16)
        dg_ref[...] += jnp.sum(dyv * xhat, axis=0, keepdims=True)
        db_ref[...] += jnp.sum(dyv, axis=0, keepdims=True)

    row = pl.BlockSpec((bt, d), lambda i: (i, 0))
    vec = pl.BlockSpec((1, d), lambda i: (0, 0))
    return pl.pallas_call(
        body, name=name, grid=(t // bt,), in_specs=[row, row, vec], out_specs=[row, row, vec, vec],
        out_shape=[jax.ShapeDtypeStruct((t, d), F32), jax.ShapeDtypeStruct((t, d), BF16),
                   jax.ShapeDtypeStruct((1, d), F32), jax.ShapeDtypeStruct((1, d), F32)],
        compiler_params=_cparams("arbitrary"),
    )(dy, z, g.reshape(1, d))


def _rows_iota(shape):
    return lax.broadcasted_iota(jnp.int32, shape, 0)


def _shift_down(x, halo, s):
    if s == 0:
        return x
    rolled = pltpu.roll(x, s, axis=0)
    first = jnp.where(_rows_iota((SUBLANES, x.shape[1])) < s, pltpu.roll(halo, s, axis=0), rolled[:SUBLANES])
    return jnp.concatenate([first, rolled[SUBLANES:]], axis=0)


def _shift_up(x, halo, s):
    if s == 0:
        return x
    n = x.shape[0]
    rolled = pltpu.roll(x, n - s, axis=0)
    last = jnp.where(_rows_iota((SUBLANES, x.shape[1])) < SUBLANES - s, rolled[n - SUBLANES:],
                     pltpu.roll(halo, SUBLANES - s, axis=0))
    return jnp.concatenate([rolled[:n - SUBLANES], last], axis=0)


def _prev_halo_spec(bt, cb, col_off=0):
    r = bt // SUBLANES
    return pl.BlockSpec((SUBLANES, cb), lambda i, j: (jnp.maximum(i * r - 1, 0), j + col_off))


def _next_halo_spec(bt, cb, t, col_off=0):
    r = bt // SUBLANES
    last = t // SUBLANES - 1
    return pl.BlockSpec((SUBLANES, cb), lambda i, j: (jnp.minimum((i + 1) * r, last), j + col_off))


def _causal_conv(x, halo, w_ref, ksize):
    acc = None
    for j in range(ksize):
        term = w_ref[j:j + 1, :] * _shift_down(x, halo, ksize - 1 - j)
        acc = term if acc is None else acc + term
    return acc


def _anticausal_conv(y, halo, w_ref, ksize):
    acc = None
    for j in range(ksize):
        term = w_ref[j:j + 1, :] * _shift_up(y, halo, ksize - 1 - j)
        acc = term if acc is None else acc + term
    return acc


def _ffn_mid_fwd(h, cw, cb_, name):
    t = h.shape[0]
    bt, cb = _pick(t, ROW_TILE), _pick(D_FF, 1408)
    nc = D_FF // cb

    def body(hv_ref, hvh_ref, hg_ref, hgh_ref, wv_ref, wg_ref, bv_ref, bg_ref, y_ref):
        keep = (pl.program_id(0) > 0).astype(F32)
        val = _causal_conv(hv_ref[...], hvh_ref[...] * keep, wv_ref, CONV_F) + bv_ref[...]
        gate = _causal_conv(hg_ref[...], hgh_ref[...] * keep, wg_ref, CONV_F) + bg_ref[...]
        y_ref[...] = (_gelu(gate) * val).astype(BF16)

    tile_v = pl.BlockSpec((bt, cb), lambda i, j: (i, j))
    tile_g = pl.BlockSpec((bt, cb), lambda i, j: (i, j + nc))
    wv = pl.BlockSpec((CONV_F, cb), lambda i, j: (0, j))
    wg = pl.BlockSpec((CONV_F, cb), lambda i, j: (0, j + nc))
    bv = pl.BlockSpec((1, cb), lambda i, j: (0, j))
    bg = pl.BlockSpec((1, cb), lambda i, j: (0, j + nc))
    cb2 = cb_.reshape(1, 2 * D_FF)
    return pl.pallas_call(
        body, name=name, grid=(t // bt, nc),
        in_specs=[tile_v, _prev_halo_spec(bt, cb), tile_g, _prev_halo_spec(bt, cb, nc), wv, wg, bv, bg],
        out_specs=tile_v, out_shape=jax.ShapeDtypeStruct((t, D_FF), BF16),
        compiler_params=_cparams("parallel", "parallel"),
    )(h, h, h, h, cw, cw, cb2, cb2)


def _ffn_mid_bwd_a(h, dy, cw, cb_, name):
    t = h.shape[0]
    bt, cb = _pick(t, ROW_TILE), _pick(D_FF, 1408)
    nc = D_FF // cb

    def body(hv_ref, hvh_ref, hg_ref, hgh_ref, dy_ref, wv_ref, wg_ref, bv_ref, bg_ref,
             dv_ref, dg_ref, dwv_ref, dwg_ref, dbv_ref, dbg_ref):
        @pl.when(pl.program_id(1) == 0)
        def _():
            for r in (dwv_ref, dwg_ref, dbv_ref, dbg_ref):
                r[...] = jnp.zeros_like(r)

        keep = (pl.program_id(1) > 0).astype(F32)
        hv, hvh = hv_ref[...], hvh_ref[...] * keep
        hg, hgh = hg_ref[...], hgh_ref[...] * keep
        val = _causal_conv(hv, hvh, wv_ref, CONV_F) + bv_ref[...]
        gate = _causal_conv(hg, hgh, wg_ref, CONV_F) + bg_ref[...]
        gl, glg = _gelu_and_grad(gate)
        dyv = dy_ref[...]
        dval = dyv * gl
        dgate = dyv * val * glg
        dv_ref[...] = dval
        dg_ref[...] = dgate
        dbv_ref[...] += jnp.sum(dval, axis=0, keepdims=True)
        dbg_ref[...] += jnp.sum(dgate, axis=0, keepdims=True)
        for j in range(CONV_F):
            s = CONV_F - 1 - j
            dwv_ref[j:j + 1, :] += jnp.sum(dval * _shift_down(hv, hvh, s), axis=0, keepdims=True)
            dwg_ref[j:j + 1, :] += jnp.sum(dgate * _shift_down(hg, hgh, s), axis=0, keepdims=True)

    tile_v = pl.BlockSpec((bt, cb), lambda j, i: (i, j))
    tile_g = pl.BlockSpec((bt, cb), lambda j, i: (i, j + nc))
    r = bt // SUBLANES
    halo_v = pl.BlockSpec((SUBLANES, cb), lambda j, i: (jnp.maximum(i * r - 1, 0), j))
    halo_g = pl.BlockSpec((SUBLANES, cb), lambda j, i: (jnp.maximum(i * r - 1, 0), j + nc))
    wv = pl.BlockSpec((CONV_F, cb), lambda j, i: (0, j))
    wg = pl.BlockSpec((CONV_F, cb), lambda j, i: (0, j + nc))
    bv = pl.BlockSpec((1, cb), lambda j, i: (0, j))
    bg = pl.BlockSpec((1, cb), lambda j, i: (0, j + nc))
    cb2 = cb_.reshape(1, 2 * D_FF)
    dv, dg, dwv, dwg, dbv, dbg = pl.pallas_call(
        body, name=name, grid=(nc, t // bt),
        in_specs=[tile_v, halo_v, tile_g, halo_g, tile_v, wv, wg, bv, bg],
        out_specs=[tile_v, tile_v, wv, wv, bv, bv],
        out_shape=[jax.ShapeDtypeStruct((t, D_FF), F32), jax.ShapeDtypeStruct((t, D_FF), F32),
                   jax.ShapeDtypeStruct((CONV_F, D_FF), F32), jax.ShapeDtypeStruct((CONV_F, D_FF), F32),
                   jax.ShapeDtypeStruct((1, D_FF), F32), jax.ShapeDtypeStruct((1, D_FF), F32)],
        compiler_params=_cparams("parallel", "arbitrary"),
    )(h, h, h, h, dy, cw, cw, cb2, cb2)
    return dv, dg, jnp.concatenate([dwv, dwg], axis=1), jnp.concatenate([dbv, dbg], axis=1)


def _conv_bwd_x(dy, cw, name, out_dtype):
    t, c = dy.shape
    ksize = cw.shape[0]
    bt, cb = _pick(t, ROW_TILE), _pick(c, 1408)

    def body(dy_ref, halo_ref, w_ref, o_ref):
        keep = (pl.program_id(0) < t // bt - 1).astype(F32)
        o_ref[...] = _anticausal_conv(dy_ref[...], halo_ref[...] * keep, w_ref, ksize).astype(out_dtype)

    tile = pl.BlockSpec((bt, cb), lambda i, j: (i, j))
    return pl.pallas_call(
        body, name=name, grid=(t // bt, c // cb),
        in_specs=[tile, _next_halo_spec(bt, cb, t), pl.BlockSpec((ksize, cb), lambda i, j: (0, j))],
        out_specs=tile, out_shape=jax.ShapeDtypeStruct((t, c), out_dtype),
        compiler_params=_cparams("parallel", "parallel"),
    )(dy, dy, cw)


def _ple_fwd(x2, g, e, bg, name):
    t, d = x2.shape
    bt = _pick(t, ROW_TILE)

    def body(x_ref, g_ref, e_ref, b_ref, y_ref, yb_ref):
        y = x_ref[...] + _sigmoid(g_ref[...] + b_ref[...]) * e_ref[...]
        y_ref[...] = y
        yb_ref[...] = y.astype(BF16)

    row = pl.BlockSpec((bt, d), lambda i: (i, 0))
    vec = pl.BlockSpec((1, d), lambda i: (0, 0))
    return pl.pallas_call(
        body, name=name, grid=(t // bt,), in_specs=[row, row, row, vec], out_specs=[row, row],
        out_shape=[jax.ShapeDtypeStruct((t, d), F32), jax.ShapeDtypeStruct((t, d), BF16)],
        compiler_params=_cparams("parallel"),
    )(x2, g, e, bg.reshape(1, d))


def _ple_bwd(dx3, g, e, bg, name):
    t, d = dx3.shape
    bt = _pick(t, ROW_TILE)

    def body(dx_ref, g_ref, e_ref, b_ref, dg_ref, de_ref, db_ref):
        @pl.when(pl.program_id(0) == 0)
        def _():
            db_ref[...] = jnp.zeros_like(db_ref)

        dx = dx_ref[...]
        gate = _sigmoid(g_ref[...] + b_ref[...])
        dg = dx * e_ref[...] * gate * (1.0 - gate)
        dg_ref[...] = dg.astype(BF16)
        de_ref[...] = (dx * gate).astype(BF16)
        db_ref[...] += jnp.sum(dg, axis=0, keepdims=True)

    row = pl.BlockSpec((bt, d), lambda i: (i, 0))
    vec = pl.BlockSpec((1, d), lambda i: (0, 0))
    return pl.pallas_call(
        body, name=name, grid=(t // bt,), in_specs=[row, row, row, vec], out_specs=[row, row, vec],
        out_shape=[jax.ShapeDtypeStruct((t, d), BF16), jax.ShapeDtypeStruct((t, d), BF16),
                   jax.ShapeDtypeStruct((1, d), F32)],
        compiler_params=_cparams("arbitrary"),
    )(dx3, g, e, bg.reshape(1, d))


def _loss_head(y, target, name):
    t, d = y.shape
    bt = _pick(t, ROW_TILE)

    def body(y_ref, t_ref, dy_ref, l_ref, acc_ref):
        @pl.when(pl.program_id(0) == 0)
        def _():
            acc_ref[...] = jnp.zeros_like(acc_ref)

        err = y_ref[...] - t_ref[...]
        dy_ref[...] = err * (1.0 / d)
        acc_ref[...] += jnp.sum(err * err, axis=0, keepdims=True)

        @pl.when(pl.program_id(0) == t // bt - 1)
        def _():
            l_ref[...] = jnp.full(l_ref.shape, (0.5 / d) * jnp.sum(acc_ref[...]), F32)

    row = pl.BlockSpec((bt, d), lambda i: (i, 0))
    dy, l = pl.pallas_call(
        body, name=name, grid=(t // bt,), in_specs=[row, row],
        out_specs=[row, pl.BlockSpec((SUBLANES, LANES), lambda i: (0, 0))],
        out_shape=[jax.ShapeDtypeStruct((t, d), F32), jax.ShapeDtypeStruct((SUBLANES, LANES), F32)],
        scratch_shapes=[pltpu.VMEM((1, d), F32)],
        compiler_params=_cparams("arbitrary"),
    )(y, target)
    return dy, l[0, 0]


def _split3(x):
    hi = x.astype(BF16)
    r1 = x - hi.astype(F32)
    mid = r1.astype(BF16)
    lo = (r1 - mid.astype(F32)).astype(BF16)
    return hi, mid, lo


def _tri_dot(x, tri):
    hi, mid, lo = _split3(x)
    dims = (((1,), (0,)), ((), ()))
    return (lax.dot_general(hi, tri, dims, preferred_element_type=F32)
            + lax.dot_general(mid, tri, dims, preferred_element_type=F32)
            + lax.dot_general(lo, tri, dims, preferred_element_type=F32))


def _fgate_fwd(fg_rows, b_f, name):
    hh, t = fg_rows.shape
    bt = _pick(t, 512)

    def body(fg_ref, b_ref, c_ref, carry_ref):
        @pl.when(pl.program_id(0) == 0)
        def _():
            carry_ref[...] = jnp.zeros_like(carry_ref)

        xx = fg_ref[...] + b_ref[...]
        logf = jnp.minimum(xx, 0.0) - _log1p(jnp.exp(-jnp.abs(xx)))
        r = lax.broadcasted_iota(jnp.int32, (bt, bt), 0)
        c = lax.broadcasted_iota(jnp.int32, (bt, bt), 1)
        tri = (r <= c).astype(BF16)
        cs = _tri_dot(logf, tri) + carry_ref[...]
        c_ref[...] = cs
        carry_ref[...] = cs[:, bt - 1:bt]

    return pl.pallas_call(
        body, name=name, grid=(t // bt,),
        in_specs=[pl.BlockSpec((hh, bt), lambda i: (0, i)), pl.BlockSpec((hh, 1), lambda i: (0, 0))],
        out_specs=pl.BlockSpec((hh, bt), lambda i: (0, i)),
        out_shape=jax.ShapeDtypeStruct((hh, t), F32),
        scratch_shapes=[pltpu.VMEM((hh, 1), F32)],
        compiler_params=_cparams("arbitrary"),
    )(fg_rows, b_f.reshape(hh, 1))


def _fgate_bwd(dck_rows, dcq_rows, fg_rows, b_f, name):
    hh, t = fg_rows.shape
    bt = _pick(t, 512)
    nb = t // bt

    def body(dc_ref, dcq_ref, fg_ref, b_ref, dfg_ref, db_ref, carry_ref):
        @pl.when(pl.program_id(0) == 0)
        def _():
            carry_ref[...] = jnp.zeros_like(carry_ref)
            db_ref[...] = jnp.zeros_like(db_ref)

        r = lax.broadcasted_iota(jnp.int32, (bt, bt), 0)
        c = lax.broadcasted_iota(jnp.int32, (bt, bt), 1)
        tri = (r >= c).astype(BF16)
        dlogf = _tri_dot(dc_ref[...] + dcq_ref[...], tri) + carry_ref[...]
        carry_ref[...] = dlogf[:, 0:1]
        xx = fg_ref[...] + b_ref[...]
        dfg = dlogf * _sigmoid(-xx)
        dfg_ref[...] = dfg
        db_ref[...] += jnp.sum(dfg, axis=1, keepdims=True)

    blk = pl.BlockSpec((hh, bt), lambda i: (0, nb - 1 - i))
    vec = pl.BlockSpec((hh, 1), lambda i: (0, 0))
    return pl.pallas_call(
        body, name=name, grid=(nb,), in_specs=[blk, blk, blk, vec], out_specs=[blk, vec],
        out_shape=[jax.ShapeDtypeStruct((hh, t), F32), jax.ShapeDtypeStruct((hh, 1), F32)],
        scratch_shapes=[pltpu.VMEM((hh, 1), F32)],
        compiler_params=_cparams("arbitrary"),
    )(dck_rows, dcq_rows, fg_rows, b_f.reshape(hh, 1))


def _rows_to_cols(r):
    hh, t = r.shape
    return jnp.repeat(r.reshape(hh // 2, 2, t).transpose(0, 2, 1), HEAD_DIM, axis=-1)


def _rows_to_pairs(r):
    hh, t = r.shape
    return jnp.pad(r.reshape(hh // 2, 2, t), ((0, 0), (0, SUBLANES - 2), (0, 0)))


AUG_C = HEAD_DIM
AUG_ONE = HEAD_DIM + 3


def _attn_prep(qkv, c_cols, name):
    t = qkv.shape[0]
    bt = _pick(t, ATTN_BLOCK)
    scale = 1.0 / math.sqrt(HEAD_DIM)

    def body(q_ref, k_ref, c_ref, qh_ref, kh_ref):
        lane = lax.broadcasted_iota(jnp.int32, (bt, LANES), 1)
        q2 = q_ref[...].astype(F32)
        k2 = k_ref[...].astype(F32) * scale
        c2 = c_ref[...]
        parts = [p.astype(F32) for p in _split3(c2 - c2[0:1, :])]
        swapped = [pltpu.roll(p, HEAD_DIM, axis=1) for p in parts]
        for a in (0, 1):
            qa = q2 if a == 0 else pltpu.roll(q2, HEAD_DIM, axis=1)
            ka = k2 if a == 0 else pltpu.roll(k2, HEAD_DIM, axis=1)
            hi, mid, lo = swapped if a == 0 else parts
            kaug = jnp.where(lane < HEAD_DIM, ka,
                             jnp.where(lane == AUG_C, hi,
                                       jnp.where(lane == AUG_C + 1, mid,
                                                 jnp.where(lane == AUG_C + 2, lo,
                                                           jnp.where(lane == AUG_ONE, 1.0, 0.0)))))
            qaug = jnp.where(lane < HEAD_DIM, qa, jnp.where(lane < AUG_ONE, -1.0, 0.0))
            qh_ref[:, a * LANES:(a + 1) * LANES] = qaug.astype(BF16)
            kh_ref[:, a * LANES:(a + 1) * LANES] = kaug.astype(BF16)

    out = pl.BlockSpec((bt, 2 * LANES), lambda i, hp: (i, hp))
    shape = jax.ShapeDtypeStruct((t, N_HEADS * LANES), BF16)
    return pl.pallas_call(
        body, name=name, grid=(t // bt, N_PAIRS),
        in_specs=[pl.BlockSpec((bt, LANES), lambda i, hp: (i, hp)),
                  pl.BlockSpec((bt, LANES), lambda i, hp: (i, N_PAIRS + hp)),
                  pl.BlockSpec((None, bt, LANES), lambda i, hp: (hp, i, 0))],
        out_specs=[out, out], out_shape=[shape, shape],
        compiler_params=_cparams("parallel", "parallel"),
    )(qkv, qkv, c_cols)


def _block_scalar(c_ref, a, start):
    return c_ref[a:a + 1, pl.ds(start, LANES)][:, 0:1]


def _attn_fwd_t(qh, kh, vt, c_pairs, name):
    t = qh.shape[0]
    bq = _pick(t, ATTN_BLOCK)
    nq = t // bq

    def body(q_ref, k_ref, vt_ref, c_ref, ot_ref, otb_ref, lse_ref, acc_ref):
        i = pl.program_id(1)
        q0 = pl.multiple_of(i * bq, bq)
        qs = (q_ref[:, 0:LANES], q_ref[:, LANES:2 * LANES])
        cq = [_block_scalar(c_ref, a, q0) for a in (0, 1)]
        acc_ref[...] = jnp.zeros_like(acc_ref)
        keep = _rows_iota((bq, bq)) <= lax.broadcasted_iota(jnp.int32, (bq, bq), 1)

        def step(j, carry, masked):
            k0 = pl.multiple_of(j * bq, bq)
            kb = k_ref[pl.ds(k0, bq), :]
            new = []
            for a in (0, 1):
                m_old, l_old = carry[2 * a], carry[2 * a + 1]
                st = lax.dot_general(kb[:, a * LANES:(a + 1) * LANES], qs[a], _DOT_DIMS["nt"],
                                     preferred_element_type=F32)
                if masked:
                    st = jnp.where(keep, st, NEG_BIG)
                sigma = cq[a] - _block_scalar(c_ref, a, k0)
                m_new = jnp.maximum(m_old, jnp.max(st, axis=0, keepdims=True) + sigma)
                pt = jnp.exp(st - (m_new - sigma))
                alpha = jnp.exp(m_old - m_new)
                l_new = alpha * l_old + jnp.sum(pt, axis=0, keepdims=True)
                vta = vt_ref[a * HEAD_DIM:(a + 1) * HEAD_DIM, pl.ds(k0, bq)]
                acc_ref[a] = alpha * acc_ref[a] + lax.dot_general(
                    vta, pt.astype(BF16), _DOT_DIMS["nn"], preferred_element_type=F32)
                new += [m_new, l_new]
            return tuple(new)

        neg = jnp.full((1, bq), NEG_BIG, F32)
        zero = jnp.zeros((1, bq), F32)
        carry = lax.fori_loop(0, i, lambda j, c: step(j, c, False), (neg, zero, neg, zero))
        m_a, l_a, m_b, l_b = step(i, carry, True)
        ot = jnp.concatenate([acc_ref[0] / l_a, acc_ref[1] / l_b], axis=0)
        ot_ref[...] = ot
        otb_ref[...] = ot.astype(BF16)
        lse_ref[...] = jnp.zeros_like(lse_ref)
        lse_ref[0:1, :] = m_a + jnp.log(l_a)
        lse_ref[1:2, :] = m_b + jnp.log(l_b)

    rows = pl.BlockSpec((None, SUBLANES, bq), lambda hp, i: (hp, 0, i))
    otile = pl.BlockSpec((LANES, bq), lambda hp, i: (hp, i))
    return pl.pallas_call(
        body, name=name, grid=(N_PAIRS, nq),
        in_specs=[pl.BlockSpec((bq, 2 * LANES), lambda hp, i: (i, hp)),
                  pl.BlockSpec((t, 2 * LANES), lambda hp, i: (0, hp)),
                  pl.BlockSpec((LANES, t), lambda hp, i: (hp, 0)),
                  pl.BlockSpec((None, SUBLANES, t), lambda hp, i: (hp, 0, 0))],
        out_specs=[otile, otile, rows],
        out_shape=[jax.ShapeDtypeStruct((D_MODEL, t), F32), jax.ShapeDtypeStruct((D_MODEL, t), BF16),
                   jax.ShapeDtypeStruct((N_PAIRS, SUBLANES, t), F32)],
        scratch_shapes=[pltpu.VMEM((2, HEAD_DIM, bq), F32)],
        compiler_params=_cparams("parallel", "arbitrary"),
    )(qh, kh, vt, c_pairs)


def _attn_delta_t(dot, ot, name):
    t = dot.shape[1]
    bt = _pick(t, 512)

    def body(do_ref, o_ref, d_ref, dob_ref):
        dob = do_ref[...].astype(BF16)
        prod = dob.astype(F32) * o_ref[...]
        d_ref[...] = jnp.zeros_like(d_ref)
        d_ref[0:1, :] = jnp.sum(prod[0:HEAD_DIM], axis=0, keepdims=True)
        d_ref[1:2, :] = jnp.sum(prod[HEAD_DIM:], axis=0, keepdims=True)
        dob_ref[...] = dob

    tile = pl.BlockSpec((LANES, bt), lambda hp, i: (hp, i))
    return pl.pallas_call(
        body, name=name, grid=(N_PAIRS, t // bt), in_specs=[tile, tile],
        out_specs=[pl.BlockSpec((None, SUBLANES, bt), lambda hp, i: (hp, 0, i)), tile],
        out_shape=[jax.ShapeDtypeStruct((N_PAIRS, SUBLANES, t), F32), jax.ShapeDtypeStruct((D_MODEL, t), BF16)],
        compiler_params=_cparams("parallel", "parallel"),
    )(dot, ot)


def _attn_bwd_t(kh, kt, qt, qkv, dotb, c_pairs, lse_pairs, d_pairs, name):
    t = kh.shape[0]
    bk = _pick(t, ATTN_BLOCK)
    nk = t // bk
    scale = 1.0 / math.sqrt(HEAD_DIM)

    def body(k_ref, kt_ref, v_ref, qt_ref, dot_ref, c_ref, lse_ref, d_ref,
             dk_ref, dv_ref, dc_ref, dq_hbm, dka_ref, dva_ref, dqa_ref, sem):
        hp, j = pl.program_id(0), pl.program_id(1)
        k0 = pl.multiple_of(j * bk, bk)
        ks = (k_ref[:, 0:LANES], k_ref[:, LANES:2 * LANES])
        vb = v_ref[...]
        ck = [_block_scalar(c_ref, a, k0) for a in (0, 1)]
        dka_ref[...] = jnp.zeros_like(dka_ref)
        dva_ref[...] = jnp.zeros_like(dva_ref)

        @pl.when(j == 0)
        def _():
            dqa_ref[...] = jnp.zeros_like(dqa_ref)

        keep = _rows_iota((bk, bk)) <= lax.broadcasted_iota(jnp.int32, (bk, bk), 1)
        top = _rows_iota((LANES, bk)) < HEAD_DIM

        def step(i, masked):
            q0 = pl.multiple_of(i * bk, bk)
            dot2 = dot_ref[:, pl.ds(q0, bk)]
            zero = jnp.zeros_like(dot2)
            for a in (0, 1):
                qta = qt_ref[a * LANES:(a + 1) * LANES, pl.ds(q0, bk)]
                st = lax.dot_general(ks[a], qta, _DOT_DIMS["nn"], preferred_element_type=F32)
                sigma = _block_scalar(c_ref, a, q0) - ck[a]
                pt = jnp.exp(st - (lse_ref[a:a + 1, pl.ds(q0, bk)] - sigma))
                if masked:
                    pt = jnp.where(keep, pt, 0.0)
                dota = jnp.where(top, dot2, zero) if a == 0 else jnp.where(top, zero, dot2)
                dpt = lax.dot_general(vb, dota, _DOT_DIMS["nn"], preferred_element_type=F32)
                dstb = (pt * (dpt - d_ref[a:a + 1, pl.ds(q0, bk)])).astype(BF16)
                dva_ref[a] += lax.dot_general(dot2[a * HEAD_DIM:(a + 1) * HEAD_DIM, :], pt.astype(BF16),
                                              _DOT_DIMS["nt"], preferred_element_type=F32)
                dka_ref[a] += lax.dot_general(qta, dstb, _DOT_DIMS["nt"], preferred_element_type=F32)
                dqa_ref[a, :, pl.ds(q0, bk)] += lax.dot_general(
                    kt_ref[a * LANES:(a + 1) * LANES, :], dstb, _DOT_DIMS["nn"], preferred_element_type=F32)

        def loop_body(i, carry):
            step(i, False)
            return carry

        step(j, True)
        lax.fori_loop(j + 1, nk, loop_body, 0)
        dk_ref[...] = (jnp.concatenate([dka_ref[0, 0:HEAD_DIM, :], dka_ref[1, 0:HEAD_DIM, :]], axis=0)
                       * scale).astype(BF16)
        dv_ref[...] = jnp.concatenate([dva_ref[0], dva_ref[1]], axis=0).astype(BF16)
        dc_ref[...] = jnp.zeros_like(dc_ref)
        dc_ref[0:1, :] = dka_ref[0, AUG_C:AUG_C + 1, :]
        dc_ref[1:2, :] = dka_ref[1, AUG_C:AUG_C + 1, :]

        @pl.when(j == nk - 1)
        def _():
            for a in (0, 1):
                row0 = pl.multiple_of((2 * hp + a) * LANES, LANES)
                cp = pltpu.make_async_copy(dqa_ref.at[a], dq_hbm.at[pl.ds(row0, LANES), :], sem)
                cp.start()
                cp.wait()

    once = pl.Buffered(1)
    rows = pl.BlockSpec((None, SUBLANES, t), lambda hp, j: (hp, 0, 0))
    kv_out = pl.BlockSpec((LANES, bk), lambda hp, j: (hp, j))
    return pl.pallas_call(
        body, name=name, grid=(N_PAIRS, nk),
        in_specs=[pl.BlockSpec((bk, 2 * LANES), lambda hp, j: (j, hp)),
                  pl.BlockSpec((2 * LANES, bk), lambda hp, j: (hp, j)),
                  pl.BlockSpec((bk, LANES), lambda hp, j: (j, 2 * N_PAIRS + hp)),
                  pl.BlockSpec((2 * LANES, t), lambda hp, j: (hp, 0), pipeline_mode=once),
                  pl.BlockSpec((LANES, t), lambda hp, j: (hp, 0), pipeline_mode=once),
                  rows, rows, rows],
        out_specs=[kv_out, kv_out, pl.BlockSpec((None, SUBLANES, bk), lambda hp, j: (hp, 0, j)),
                   pl.BlockSpec(memory_space=pltpu.HBM)],
        out_shape=[jax.ShapeDtypeStruct((D_MODEL, t), BF16), jax.ShapeDtypeStruct((D_MODEL, t), BF16),
                   jax.ShapeDtypeStruct((N_PAIRS, SUBLANES, t), F32),
                   jax.ShapeDtypeStruct((N_HEADS * LANES, t), F32)],
        scratch_shapes=[pltpu.VMEM((2, LANES, bk), F32), pltpu.VMEM((2, HEAD_DIM, bk), F32),
                        pltpu.VMEM((2, LANES, t), F32), pltpu.SemaphoreType.DMA],
        compiler_params=_cparams("arbitrary", "arbitrary"),
    )(kh, kt, qkv, qt, dotb, c_pairs, lse_pairs, d_pairs)


def _scan(a, u, name, reverse=False):
    t, c = a.shape
    bt, cb = _pick(t, ROW_TILE), _pick(c, 1024)
    nt = t // bt
    ngroups = bt // SUBLANES

    def body(a_ref, u_ref, h_ref, carry_ref, as_ref, us_ref):
        @pl.when(pl.program_id(1) == 0)
        def _():
            carry_ref[...] = jnp.zeros_like(carry_ref)

        av, uv = a_ref[...], u_ref[...]
        sub = _rows_iota((bt, cb)) % SUBLANES
        for s in (1, 2, 4):
            if reverse:
                a_sh, u_sh = pltpu.roll(av, bt - s, axis=0), pltpu.roll(uv, bt - s, axis=0)
                valid = sub < SUBLANES - s
            else:
                a_sh, u_sh = pltpu.roll(av, s, axis=0), pltpu.roll(uv, s, axis=0)
                valid = sub >= s
            uv = jnp.where(valid, uv + av * u_sh, uv)
            av = jnp.where(valid, av * a_sh, av)
        as_ref[...] = av
        us_ref[...] = uv
        edge = 0 if reverse else SUBLANES - 1
        pick = _rows_iota((SUBLANES, cb)) == edge

        def group(gi, carry):
            g = (ngroups - 1 - gi) if reverse else gi
            r0 = pl.multiple_of(g * SUBLANES, SUBLANES)
            h8 = us_ref[pl.ds(r0, SUBLANES), :] + as_ref[pl.ds(r0, SUBLANES), :] * carry
            h_ref[pl.ds(r0, SUBLANES), :] = h8
            return jnp.sum(jnp.where(pick, h8, 0.0), axis=0, keepdims=True)

        carry_ref[...] = lax.fori_loop(0, ngroups, group, carry_ref[...])

    if reverse:
        tile = pl.BlockSpec((bt, cb), lambda j, i: (nt - 1 - i, j))
    else:
        tile = pl.BlockSpec((bt, cb), lambda j, i: (i, j))
    return pl.pallas_call(
        body, name=name, grid=(c // cb, nt), in_specs=[tile, tile], out_specs=tile,
        out_shape=jax.ShapeDtypeStruct((t, c), F32),
        scratch_shapes=[pltpu.VMEM((1, cb), F32), pltpu.VMEM((bt, cb), F32), pltpu.VMEM((bt, cb), F32)],
        compiler_params=_cparams("parallel", "arbitrary"),
    )(a, u)


def _rg_conv_fwd(proj, cw, cb_, name):
    t = proj.shape[0]
    bt, cb = _pick(t, ROW_TILE), D_MODEL

    def body(x_ref, halo_ref, w_ref, b_ref, o_ref):
        keep = (pl.program_id(0) > 0).astype(F32)
        o_ref[...] = _causal_conv(x_ref[...], halo_ref[...] * keep, w_ref, CONV_B) + b_ref[...]

    tile = pl.BlockSpec((bt, cb), lambda i, j: (i, j))
    return pl.pallas_call(
        body, name=name, grid=(t // bt, 1),
        in_specs=[tile, _prev_halo_spec(bt, cb), pl.BlockSpec((CONV_B, cb), lambda i, j: (0, 0)),
                  pl.BlockSpec((1, cb), lambda i, j: (0, 0))],
        out_specs=tile, out_shape=jax.ShapeDtypeStruct((t, D_MODEL), F32),
        compiler_params=_cparams("parallel", "parallel"),
    )(proj, proj, cw, cb_.reshape(1, D_MODEL))


def _conv_bwd_w(x, dy, ksize, name):
    t, c = dy.shape
    bt = _pick(t, ROW_TILE)
    r = bt // SUBLANES

    def body(x_ref, halo_ref, dy_ref, dw_ref, db_ref):
        @pl.when(pl.program_id(0) == 0)
        def _():
            dw_ref[...] = jnp.zeros_like(dw_ref)
            db_ref[...] = jnp.zeros_like(db_ref)

        keep = (pl.program_id(0) > 0).astype(F32)
        xv, halo, dyv = x_ref[...], halo_ref[...] * keep, dy_ref[...]
        db_ref[...] += jnp.sum(dyv, axis=0, keepdims=True)
        for j in range(ksize):
            dw_ref[j:j + 1, :] += jnp.sum(dyv * _shift_down(xv, halo, ksize - 1 - j), axis=0, keepdims=True)

    tile = pl.BlockSpec((bt, c), lambda i: (i, 0))
    return pl.pallas_call(
        body, name=name, grid=(t // bt,),
        in_specs=[tile, pl.BlockSpec((SUBLANES, c), lambda i: (jnp.maximum(i * r - 1, 0), 0)), tile],
        out_specs=[pl.BlockSpec((ksize, c), lambda i: (0, 0)), pl.BlockSpec((1, c), lambda i: (0, 0))],
        out_shape=[jax.ShapeDtypeStruct((ksize, c), F32), jax.ShapeDtypeStruct((1, c), F32)],
        compiler_params=_cparams("arbitrary"),
    )(x, x, dy)


def _rg_gate_math(xc, wa_ref, wi_ref, ba_ref, bi_ref, lam_ref):
    xb = xc.astype(BF16)
    ra = lax.dot_general(xb, wa_ref[...], _DOT_DIMS["nn"], preferred_element_type=F32) + ba_ref[...]
    ia = lax.dot_general(xb, wi_ref[...], _DOT_DIMS["nn"], preferred_element_type=F32) + bi_ref[...]
    r, ig = _sigmoid(ra), _sigmoid(ia)
    sp = _softplus(-lam_ref[...])
    log_a = -LRU_C * r * sp
    a = jnp.exp(log_a)
    mult = jnp.sqrt(-_expm1(2.0 * log_a))
    return xb, r, ig, sp, a, mult


def _rg_gate_specs(bt, time_first):
    if time_first:
        tile = pl.BlockSpec((bt, BLOCK_B), lambda i, n: (i, n))
        w = pl.BlockSpec((None, BLOCK_B, BLOCK_B), lambda i, n: (n, 0, 0))
        v = pl.BlockSpec((None, 1, BLOCK_B), lambda i, n: (n, 0, 0))
    else:
        tile = pl.BlockSpec((bt, BLOCK_B), lambda n, i: (i, n))
        w = pl.BlockSpec((None, BLOCK_B, BLOCK_B), lambda n, i: (n, 0, 0))
        v = pl.BlockSpec((None, 1, BLOCK_B), lambda n, i: (n, 0, 0))
    return tile, w, v


def _rg_gate_fwd(xc, wa, ba, wi, bi, lam, name):
    t = xc.shape[0]
    bt = _pick(t, 512)

    def body(x_ref, wa_ref, wi_ref, ba_ref, bi_ref, lam_ref, a_ref, u_ref):
        xcv = x_ref[...]
        _, _, ig, _, a, mult = _rg_gate_math(xcv, wa_ref, wi_ref, ba_ref, bi_ref, lam_ref)
        a_ref[...] = a
        u_ref[...] = mult * (ig * xcv)

    tile, w, v = _rg_gate_specs(bt, True)
    return pl.pallas_call(
        body, name=name, grid=(t // bt, N_BLOCKS_B), in_specs=[tile, w, w, v, v, v], out_specs=[tile, tile],
        out_shape=[jax.ShapeDtypeStruct((t, D_MODEL), F32), jax.ShapeDtypeStruct((t, D_MODEL), F32)],
        compiler_params=_cparams("parallel", "parallel"),
    )(xc, wa, wi, ba, bi, lam)


def _rg_gate_bwd(xc, g, h, wa, ba, wi, bi, lam, name):
    t = xc.shape[0]
    bt = _pick(t, 512)
    rr = bt // SUBLANES

    def body(x_ref, g_ref, h_ref, hh_ref, wa_ref, wi_ref, ba_ref, bi_ref, lam_ref,
             dx_ref, dwa_ref, dwi_ref, dba_ref, dbi_ref, dlam_ref):
        @pl.when(pl.program_id(1) == 0)
        def _():
            for ref in (dwa_ref, dwi_ref, dba_ref, dbi_ref, dlam_ref):
                ref[...] = jnp.zeros_like(ref)

        keep = (pl.program_id(1) > 0).astype(F32)
        xcv, gv = x_ref[...], g_ref[...]
        xb, r, ig, sp, a, mult = _rg_gate_math(xcv, wa_ref, wi_ref, ba_ref, bi_ref, lam_ref)
        h_prev = _shift_down(h_ref[...], hh_ref[...] * keep, 1)
        da = gv * h_prev
        dmult = gv * ig * xcv
        dig = gv * mult * xcv
        dxc = gv * mult * ig
        dlog_a = da * a - dmult * (a * a / mult)
        dr = dlog_a * (-LRU_C * sp)
        dsp = jnp.sum(dlog_a * (-LRU_C * r), axis=0, keepdims=True)
        dlam_ref[...] += dsp * (-_sigmoid(-lam_ref[...]))
        dra = dr * r * (1.0 - r)
        dia = dig * ig * (1.0 - ig)
        dba_ref[...] += jnp.sum(dra, axis=0, keepdims=True)
        dbi_ref[...] += jnp.sum(dia, axis=0, keepdims=True)
        drab, diab = dra.astype(BF16), dia.astype(BF16)
        dwa_ref[...] += lax.dot_general(xb, drab, _DOT_DIMS["tn"], preferred_element_type=F32)
        dwi_ref[...] += lax.dot_general(xb, diab, _DOT_DIMS["tn"], preferred_element_type=F32)
        dxc = dxc + lax.dot_general(drab, wa_ref[...], _DOT_DIMS["nt"], preferred_element_type=F32)
        dxc = dxc + lax.dot_general(diab, wi_ref[...], _DOT_DIMS["nt"], preferred_element_type=F32)
        dx_ref[...] = dxc

    tile, w, v = _rg_gate_specs(bt, False)
    halo = pl.BlockSpec((SUBLANES, BLOCK_B), lambda n, i: (jnp.maximum(i * rr - 1, 0), n))
    wshape = jax.ShapeDtypeStruct((N_BLOCKS_B, BLOCK_B, BLOCK_B), F32)
    vshape = jax.ShapeDtypeStruct((N_BLOCKS_B, 1, BLOCK_B), F32)
    return pl.pallas_call(
        body, name=name, grid=(N_BLOCKS_B, t // bt),
        in_specs=[tile, tile, tile, halo, w, w, v, v, v],
        out_specs=[tile, w, w, v, v, v],
        out_shape=[jax.ShapeDtypeStruct((t, D_MODEL), F32), wshape, wshape, vshape, vshape, vshape],
        compiler_params=_cparams("parallel", "arbitrary"),
    )(xc, g, h, h, wa, wi, ba, bi, lam)


def _rg_out_fwd(h, proj, name):
    t = h.shape[0]
    bt = _pick(t, ROW_TILE)

    def body(h_ref, g_ref, y_ref):
        y_ref[...] = (h_ref[...] * _gelu(g_ref[...])).astype(BF16)

    tile = pl.BlockSpec((bt, D_MODEL), lambda i: (i, 0))
    return pl.pallas_call(
        body, name=name, grid=(t // bt,), in_specs=[tile, pl.BlockSpec((bt, D_MODEL), lambda i: (i, 1))],
        out_specs=tile, out_shape=jax.ShapeDtypeStruct((t, D_MODEL), BF16),
        compiler_params=_cparams("parallel"),
    )(h, proj)


def _rg_out_bwd(dy, h, proj, name):
    t = h.shape[0]
    bt = _pick(t, ROW_TILE)

    def body(dy_ref, h_ref, g_ref, dh_ref, dg_ref):
        gl, glg = _gelu_and_grad(g_ref[...])
        dyv = dy_ref[...]
        dh_ref[...] = dyv * gl
        dg_ref[...] = (dyv * h_ref[...] * glg).astype(BF16)

    tile = pl.BlockSpec((bt, D_MODEL), lambda i: (i, 0))
    return pl.pallas_call(
        body, name=name, grid=(t // bt,),
        in_specs=[tile, tile, pl.BlockSpec((bt, D_MODEL), lambda i: (i, 1))], out_specs=[tile, tile],
        out_shape=[jax.ShapeDtypeStruct((t, D_MODEL), F32), jax.ShapeDtypeStruct((t, D_MODEL), BF16)],
        compiler_params=_cparams("parallel"),
    )(dy, h, proj)


def _shift_up_one(a, name):
    t, c = a.shape
    bt = _pick(t, ROW_TILE)

    def body(a_ref, halo_ref, o_ref):
        o_ref[...] = _shift_up(a_ref[...], halo_ref[...], 1)

    tile = pl.BlockSpec((bt, c), lambda i, j: (i, j))
    return pl.pallas_call(
        body, name=name, grid=(t // bt, 1), in_specs=[tile, _next_halo_spec(bt, c, t)], out_specs=tile,
        out_shape=jax.ShapeDtypeStruct((t, c), F32), compiler_params=_cparams("parallel", "parallel"),
    )(a, a)


ADAM_ROWS = 64


def _adamw(recv, w, m, v, name):
    _, r, c = recv.shape
    br = ADAM_ROWS
    assert r % br == 0

    def body(r_ref, w_ref, m_ref, v_ref, g_ref, d_ref, nm_ref, nv_ref):
        g = r_ref[0].astype(F32)
        for s in range(1, N_DEV):
            g = g + r_ref[s].astype(F32)
        m_new = ADAM_B1 * m_ref[...] + (1.0 - ADAM_B1) * g
        v_new = ADAM_B2 * v_ref[...] + (1.0 - ADAM_B2) * (g * g)
        m_hat = m_new / (1.0 - ADAM_B1 ** ADAM_STEP)
        v_hat = v_new / (1.0 - ADAM_B2 ** ADAM_STEP)
        g_ref[...] = g
        d_ref[...] = -ADAM_LR * (m_hat / (jnp.sqrt(v_hat) + ADAM_EPS) + ADAM_WD * w_ref[...])
        nm_ref[...] = m_new
        nv_ref[...] = v_new

    tile = pl.BlockSpec((br, c), lambda i: (i, 0))
    shape = jax.ShapeDtypeStruct((r, c), F32)
    return pl.pallas_call(
        body, name=name, grid=(r // br,),
        in_specs=[pl.BlockSpec((N_DEV, br, c), lambda i: (0, i, 0)), tile, tile, tile],
        out_specs=[tile] * 4, out_shape=[shape] * 4, compiler_params=_cparams("parallel"),
    )(recv, w, m, v)


def _exchange(src, scatter, name):
    slab = src.shape[1:] if scatter else src.shape

    def body(src_ref, out_ref, send_sems, recv_sems, local_sem):
        pos = [lax.axis_index(ax) for ax in MESH_AXES]
        me = 4 * pos[0] + 2 * pos[1] + pos[2]

        def peer_of(k):
            p = [(1 - pos[b]) if (k >> (2 - b)) & 1 else pos[b] for b in range(3)]
            return tuple(p), 4 * p[0] + 2 * p[1] + p[2]

        def copy(k):
            peer, peer_idx = peer_of(k)
            return pltpu.make_async_remote_copy(
                src_ref=src_ref.at[peer_idx] if scatter else src_ref, dst_ref=out_ref.at[me],
                send_sem=send_sems.at[k - 1], recv_sem=recv_sems.at[k - 1],
                device_id=peer, device_id_type=pl.DeviceIdType.MESH)

        def arrival(k):
            peer, peer_idx = peer_of(k)
            return pltpu.make_async_remote_copy(
                src_ref=src_ref.at[me] if scatter else src_ref, dst_ref=out_ref.at[peer_idx],
                send_sem=send_sems.at[k - 1], recv_sem=recv_sems.at[k - 1],
                device_id=peer, device_id_type=pl.DeviceIdType.MESH)

        mine = pltpu.make_async_copy(src_ref.at[me] if scatter else src_ref, out_ref.at[me], local_sem)
        mine.start()
        sends = [copy(k) for k in range(1, N_DEV)]
        for cp in sends:
            cp.start()
        for k in range(1, N_DEV):
            arrival(k).wait_recv()
        for cp in sends:
            cp.wait_send()
        mine.wait()

    hbm = pl.BlockSpec(memory_space=pltpu.HBM)
    return pl.pallas_call(
        body, name=name, in_specs=[hbm], out_specs=hbm,
        out_shape=jax.ShapeDtypeStruct((N_DEV,) + tuple(slab), src.dtype),
        scratch_shapes=[pltpu.SemaphoreType.DMA((N_DEV - 1,)), pltpu.SemaphoreType.DMA((N_DEV - 1,)),
                        pltpu.SemaphoreType.DMA],
        compiler_params=pltpu.CompilerParams(has_side_effects=True),
    )(src)


WEIGHTS = ['a_w_in', 'a_b_f', 'a_w_out', 'b_w_in', 'b_conv_w', 'b_conv_b', 'b_w_a', 'b_b_a', 'b_w_i', 'b_b_i',
           'b_lam', 'b_w_out', 'f_w_up', 'f_conv_w', 'f_conv_b', 'f_w_down', 'ln1_g', 'ln1_b', 'ln2_g', 'ln2_b',
           'ple_w', 'ple_gate_w', 'ple_gate_b']
SHARD_AXIS = {'a_w_in': 2, 'a_b_f': None, 'a_w_out': 1, 'b_w_in': 2, 'b_conv_w': 2, 'b_conv_b': 1, 'b_w_a': None,
              'b_b_a': None, 'b_w_i': None, 'b_b_i': None, 'b_lam': 1, 'b_w_out': 1, 'f_w_up': 2, 'f_conv_w': 2,
              'f_conv_b': None, 'f_w_down': 1, 'ln1_g': None, 'ln1_b': None, 'ln2_g': None, 'ln2_b': None,
              'ple_w': 2, 'ple_gate_w': 1, 'ple_gate_b': None}
MATMUL_WEIGHTS = ['a_w_in', 'a_w_out', 'b_w_in', 'b_w_out', 'f_w_up', 'f_w_down', 'ple_w', 'ple_gate_w']
SMALL_SHARDED = ['b_conv_w', 'b_conv_b', 'b_lam', 'f_conv_w']
PACK_COLS = 1024


def _to_shards(full, axis):
    return jnp.stack(jnp.split(full, N_DEV, axis=axis))


def _from_shards(pieces, axis):
    return jnp.concatenate([pieces[d] for d in range(N_DEV)], axis=axis)


PIECE_ROWS = 16


def _piece_rows(size):
    rows = -(-size // PACK_COLS)
    return -(-rows // PIECE_ROWS) * PIECE_ROWS


def _pack_pieces(pieces, total_mult=PIECE_ROWS):
    lead = pieces[0].shape[:-1]
    blocks = []
    for pc in pieces:
        n = pc.shape[-1]
        rows = _piece_rows(n)
        pad = [(0, 0)] * len(lead) + [(0, rows * PACK_COLS - n)]
        blocks.append(jnp.pad(pc, pad).reshape(lead + (rows, PACK_COLS)))
    total = sum(b.shape[-2] for b in blocks)
    extra = -total % total_mult
    if extra:
        blocks.append(jnp.zeros(lead + (extra, PACK_COLS), pieces[0].dtype))
    return jnp.concatenate(blocks, axis=len(lead))


def _unpack_pieces(packed, shapes):
    lead = packed.shape[:-2]
    out, row = [], 0
    for shp in shapes:
        size = math.prod(shp)
        rows = _piece_rows(size)
        block = lax.slice_in_dim(packed, row, row + rows, axis=len(lead))
        flat = block.reshape(lead + (rows * PACK_COLS,))
        out.append(lax.slice_in_dim(flat, 0, size, axis=len(lead)).reshape(lead + tuple(shp)))
        row += rows
    return out


def _gather_weights(local, names, dtype, name):
    packed = _pack_pieces([local[n].astype(dtype).reshape(-1) for n in names])
    gathered = _exchange(packed, False, name)
    pieces = _unpack_pieces(gathered, [local[n].shape for n in names])
    return {n: _from_shards(pc, SHARD_AXIS[n]) for n, pc in zip(names, pieces)}


def _mixer_a_fwd(tag, xb, w):
    qkv = _mm(xb, w["wqkv"], "nn", BF16, f"{tag}_qkv")
    fg = _mm(xb, w["wf"], "nn", F32, f"{tag}_fgproj")
    fg_rows = fg[:, :N_HEADS].T
    c_rows = _fgate_fwd(fg_rows, w["b_f"], f"{tag}_fgate")
    c_pairs = _rows_to_pairs(c_rows)
    qh, kh = _attn_prep(qkv, _rows_to_cols(c_rows), f"{tag}_attn_prep")
    ot, otb, lse_pairs = _attn_fwd_t(qh, kh, qkv[:, 2 * D_MODEL:].T, c_pairs, f"{tag}_attn")
    m = _mm(otb, w["wout"], "tn", F32, f"{tag}_oproj")
    return m, dict(qkv=qkv, qh=qh, kh=kh, fg_rows=fg_rows, c_pairs=c_pairs, ot=ot, otb=otb, lse_pairs=lse_pairs)


def _mixer_a_bwd(tag, dz, dzb, xb, w, s):
    t = xb.shape[0]
    dot = _mm(w["wout"], dzb, "nt", F32, f"{tag}_b_do")
    g_wout = _mm(s["otb"], dzb, "nn", F32, f"{tag}_b_dwout")
    d_pairs, dotb = _attn_delta_t(dot, s["ot"], f"{tag}_b_delta")
    dkt, dvt, dck_pairs, dq_aug = _attn_bwd_t(s["kh"], s["kh"].T, s["qh"].T, s["qkv"], dotb, s["c_pairs"],
                                              s["lse_pairs"], d_pairs, f"{tag}_b_attn")
    dq_aug = dq_aug.reshape(N_HEADS, LANES, t)
    dfg_rows, db_f = _fgate_bwd(dck_pairs[:, :2, :].reshape(N_HEADS, t), dq_aug[:, AUG_ONE, :], s["fg_rows"],
                                w["b_f"], f"{tag}_b_fgate")
    dqt = dq_aug[:, :HEAD_DIM, :].reshape(D_MODEL, t).astype(BF16)
    dqkv = jnp.concatenate([dqt, dkt, dvt], axis=0).T
    dfg = jnp.pad(dfg_rows.T, ((0, 0), (0, LANES - N_HEADS))).astype(BF16)
    dx = _mm(dqkv, w["wqkv"], "nt", F32, f"{tag}_b_dx_qkv", add=dz, add_scale=ALPHA)
    dx = _mm(dfg, w["wf"], "nt", F32, f"{tag}_b_dx_fg", add=dx)
    g_wqkv = _mm(xb, dqkv, "tn", F32, f"{tag}_b_dwqkv")
    g_wf = _mm(xb, dfg, "tn", F32, f"{tag}_b_dwf")[:, :N_HEADS]
    grads = dict(a_w_in=jnp.concatenate([g_wqkv, g_wf], axis=1), a_b_f=db_f.reshape(N_HEADS), a_w_out=g_wout)
    return dx, grads


def _mixer_b_fwd(tag, xb, w):
    proj = _mm(xb, w["win"], "nn", F32, f"{tag}_proj")
    xc = _rg_conv_fwd(proj, w["conv_w"], w["conv_b"], f"{tag}_conv")
    a, u = _rg_gate_fwd(xc, w["wa"], w["ba"], w["wi"], w["bi"], w["lam"], f"{tag}_gate")
    h = _scan(a, u, f"{tag}_scan")
    y = _rg_out_fwd(h, proj, f"{tag}_out")
    m = _mm(y, w["wout"], "nn", F32, f"{tag}_oproj")
    return m, dict(proj=proj, xc=xc, a=a, h=h, y=y)


def _mixer_b_bwd(tag, dz, dzb, xb, w, s):
    dy = _mm(dzb, w["wout"], "nt", F32, f"{tag}_b_dy")
    g_wout = _mm(s["y"], dzb, "tn", F32, f"{tag}_b_dwout")
    dh, dgate = _rg_out_bwd(dy, s["h"], s["proj"], f"{tag}_b_out")
    g = _scan(_shift_up_one(s["a"], f"{tag}_b_shift"), dh, f"{tag}_b_scan", reverse=True)
    dxc, g_wa, g_wi, g_ba, g_bi, g_lam = _rg_gate_bwd(
        s["xc"], g, s["h"], w["wa"], w["ba"], w["wi"], w["bi"], w["lam"], f"{tag}_b_gate")
    dxp = _conv_bwd_x(dxc, w["conv_w"], f"{tag}_b_convx", BF16)
    g_cw, g_cb = _conv_bwd_w(s["proj"], dxc, CONV_B, f"{tag}_b_convw")
    dproj = jnp.concatenate([dxp, dgate], axis=1)
    dx = _mm(dproj, w["win"], "nt", F32, f"{tag}_b_dx", add=dz, add_scale=ALPHA)
    g_win = _mm(xb, dproj, "tn", F32, f"{tag}_b_dwin")
    grads = dict(b_w_in=g_win, b_conv_w=g_cw, b_conv_b=g_cb.reshape(D_MODEL), b_w_a=g_wa,
                 b_b_a=g_ba.reshape(N_BLOCKS_B, BLOCK_B), b_w_i=g_wi, b_b_i=g_bi.reshape(N_BLOCKS_B, BLOCK_B),
                 b_lam=g_lam.reshape(D_MODEL), b_w_out=g_wout)
    return dx, grads


def _layer_fwd(i, x, xb, pb, w):
    tag = f"L{i}"
    mix = _mixer_a_fwd if i % 2 == 0 else _mixer_b_fwd
    m, sm = mix(tag, xb, w)
    x1, x1b, z1 = _ln_fwd(x, m, w["ln1_g"], w["ln1_b"], f"{tag}_ln1")
    h = _mm(x1b, w["wup"], "nn", F32, f"{tag}_ffn_up")
    y = _ffn_mid_fwd(h, w["fconv_w"], w["fconv_b"], f"{tag}_ffn_mid")
    ff = _mm(y, w["wdown"], "nn", F32, f"{tag}_ffn_down")
    x2, x2b, z2 = _ln_fwd(x1, ff, w["ln2_g"], w["ln2_b"], f"{tag}_ln2")
    gl = _mm(x2b, w["wg"], "nn", F32, f"{tag}_ple_gate")
    e = _mm(pb, w["wp"], "nn", F32, f"{tag}_ple_emb")
    x3, x3b = _ple_fwd(x2, gl, e, w["bg"], f"{tag}_ple")
    saved = dict(mixer=sm, xb=xb, x1b=x1b, z1=z1, h=h, y=y, x2b=x2b, z2=z2, gl=gl, e=e, pb=pb)
    return x3, x3b, saved


def _layer_bwd(i, dx3, w, s):
    tag = f"L{i}"
    dgl, de, g_bg = _ple_bwd(dx3, s["gl"], s["e"], w["bg"], f"{tag}_b_ple")
    g_wg = _mm(s["x2b"], dgl, "tn", F32, f"{tag}_b_dwg")
    g_wp = _mm(s["pb"], de, "tn", F32, f"{tag}_b_dwp")
    dx2 = _mm(dgl, w["wg"], "nt", F32, f"{tag}_b_dx2", add=dx3)
    dz2, dz2b, g_ln2g, g_ln2b = _ln_bwd(dx2, s["z2"], w["ln2_g"], f"{tag}_b_ln2")
    dy = _mm(dz2b, w["wdown"], "nt", F32, f"{tag}_b_dy")
    g_wdown = _mm(s["y"], dz2b, "tn", F32, f"{tag}_b_dwdown")
    dval, dgate, g_fcw, g_fcb = _ffn_mid_bwd_a(s["h"], dy, w["fconv_w"], w["fconv_b"], f"{tag}_b_ffn_mid")
    dhv = _conv_bwd_x(dval, w["fconv_w"][:, :D_FF], f"{tag}_b_convx_v", BF16)
    dhg = _conv_bwd_x(dgate, w["fconv_w"][:, D_FF:], f"{tag}_b_convx_g", BF16)
    dx1 = _mm(dhv, w["wup"][:, :D_FF], "nt", F32, f"{tag}_b_dx1_v", add=dz2, add_scale=ALPHA)
    dx1 = _mm(dhg, w["wup"][:, D_FF:], "nt", F32, f"{tag}_b_dx1_g", add=dx1)
    g_wup = jnp.concatenate([_mm(s["x1b"], dhv, "tn", F32, f"{tag}_b_dwup_v"),
                             _mm(s["x1b"], dhg, "tn", F32, f"{tag}_b_dwup_g")], axis=1)
    dz1, dz1b, g_ln1g, g_ln1b = _ln_bwd(dx1, s["z1"], w["ln1_g"], f"{tag}_b_ln1")
    mix_bwd = _mixer_a_bwd if i % 2 == 0 else _mixer_b_bwd
    dx, g_mix = mix_bwd(tag, dz1, dz1b, s["xb"], w, s["mixer"])
    grads = dict(f_w_up=g_wup, f_conv_w=g_fcw, f_conv_b=g_fcb.reshape(2 * D_FF), f_w_down=g_wdown,
                 ln1_g=g_ln1g.reshape(D_MODEL), ln1_b=g_ln1b.reshape(D_MODEL), ln2_g=g_ln2g.reshape(D_MODEL),
                 ln2_b=g_ln2b.reshape(D_MODEL), ple_w=g_wp, ple_gate_w=g_wg, ple_gate_b=g_bg.reshape(D_MODEL))
    return dx, g_mix, grads


def _layer_weights(i, full, rep):
    j = i // 2
    w = dict(ln1_g=rep["ln1_g"][i], ln1_b=rep["ln1_b"][i], ln2_g=rep["ln2_g"][i], ln2_b=rep["ln2_b"][i],
             wup=full["f_w_up"][i], fconv_w=full["f_conv_w"][i], fconv_b=rep["f_conv_b"][i],
             wdown=full["f_w_down"][i], wp=full["ple_w"][i], wg=full["ple_gate_w"][i], bg=rep["ple_gate_b"][i])
    if i % 2 == 0:
        w_in = full["a_w_in"][j]
        w.update(wqkv=w_in[:, :3 * D_MODEL],
                 wf=jnp.pad(w_in[:, 3 * D_MODEL:], ((0, 0), (0, LANES - N_HEADS))),
                 b_f=rep["a_b_f"][j], wout=full["a_w_out"][j])
    else:
        w.update(win=full["b_w_in"][j], conv_w=full["b_conv_w"][j], conv_b=full["b_conv_b"][j],
                 wa=rep["b_w_a"][j].astype(BF16), wi=rep["b_w_i"][j].astype(BF16),
                 ba=rep["b_b_a"][j].reshape(N_BLOCKS_B, 1, BLOCK_B), bi=rep["b_b_i"][j].reshape(N_BLOCKS_B, 1, BLOCK_B),
                 lam=full["b_lam"][j].reshape(N_BLOCKS_B, 1, BLOCK_B), wout=full["b_w_out"][j])
    return w


def _fwd_bwd(x, p, target, full, rep):
    weights = [_layer_weights(i, full, rep) for i in range(DEPTH)]
    xb = x.astype(BF16)
    pb = p.astype(BF16)
    saved = []
    for i in range(DEPTH):
        x, xb, s = _layer_fwd(i, x, xb, pb[i], weights[i])
        saved.append(s)
    dx, loss_local = _loss_head(x, target, "loss_head")

    per_layer = {n: [None] * (DEPTH if n.startswith(("f_", "ln", "ple")) else DEPTH // 2) for n in WEIGHTS}
    for i in reversed(range(DEPTH)):
        dx, g_mix, g_layer = _layer_bwd(i, dx, weights[i], saved[i])
        for n, g in g_layer.items():
            per_layer[n][i] = g
        for n, g in g_mix.items():
            per_layer[n][i // 2] = g
    return loss_local, dx, {n: jnp.stack(per_layer[n]) for n in WEIGHTS}


def _train_step(x, p, target, local, moments_m, moments_v):
    full = _gather_weights(local, MATMUL_WEIGHTS, BF16, "gather_matmul_weights")
    full.update(_gather_weights(local, SMALL_SHARDED, F32, "gather_small_weights"))
    rep = {n: local[n] for n in WEIGHTS if SHARD_AXIS[n] is None}
    loss_local, dx, grads_full = _fwd_bwd(x, p, target, full, rep)

    unpacked = [{} for _ in range(4)]
    for group, dtype, tag in ((MATMUL_WEIGHTS, BF16, "big"), ([n for n in WEIGHTS if n not in MATMUL_WEIGHTS], F32, "small")):
        pieces = []
        for n in group:
            g = grads_full[n].astype(dtype)
            if SHARD_AXIS[n] is None:
                pieces.append(jnp.broadcast_to(g.reshape(1, -1), (N_DEV, g.size)))
            else:
                pieces.append(_to_shards(g, SHARD_AXIS[n]).reshape(N_DEV, -1))
        recv = _exchange(_pack_pieces(pieces, ADAM_ROWS), True, f"reduce_scatter_grads_{tag}")

        def pack_local(d):
            return _pack_pieces([d[n].astype(F32).reshape(-1) for n in group], ADAM_ROWS)

        outs = _adamw(recv, pack_local(local), pack_local(moments_m), pack_local(moments_v), f"adamw_{tag}")
        shapes = [local[n].shape for n in group]
        for dst, packed in zip(unpacked, outs):
            dst.update(zip(group, _unpack_pieces(packed, shapes)))
    return loss_local, dx, unpacked


def kernel(x, p, a_w_in, a_b_f, a_w_out, b_w_in, b_conv_w, b_conv_b, b_w_a, b_b_a, b_w_i, b_b_i, b_lam, b_w_out, f_w_up, f_conv_w, f_conv_b, f_w_down, ln1_g, ln1_b, ln2_g, ln2_b, ple_w, ple_gate_w, ple_gate_b, loss_target, m_a_w_in, m_a_b_f, m_a_w_out, m_b_w_in, m_b_conv_w, m_b_conv_b, m_b_w_a, m_b_b_a, m_b_w_i, m_b_b_i, m_b_lam, m_b_w_out, m_f_w_up, m_f_conv_w, m_f_conv_b, m_f_w_down, m_ln1_g, m_ln1_b, m_ln2_g, m_ln2_b, m_ple_w, m_ple_gate_w, m_ple_gate_b, v_a_w_in, v_a_b_f, v_a_w_out, v_b_w_in, v_b_conv_w, v_b_conv_b, v_b_w_a, v_b_b_a, v_b_w_i, v_b_b_i, v_b_lam, v_b_w_out, v_f_w_up, v_f_conv_w, v_f_conv_b, v_f_w_down, v_ln1_g, v_ln1_b, v_ln2_g, v_ln2_b, v_ple_w, v_ple_gate_w, v_ple_gate_b):
    given = dict(locals())
    local = {n: given[n] for n in WEIGHTS}
    mom_m = {n: given["m_" + n] for n in WEIGHTS}
    mom_v = {n: given["v_" + n] for n in WEIGHTS}
    t = x.shape[1]
    loss_local, dx, (grad, delta, new_m, new_v) = _train_step(
        x.reshape(t, D_MODEL), p.reshape(DEPTH, t, D_PLE), loss_target.reshape(t, D_MODEL), local, mom_m, mom_v)
    loss = lax.psum(loss_local, MESH_AXES)
    return (loss, dx.reshape(1, t, D_MODEL), *[grad[n] for n in WEIGHTS], *[delta[n] for n in WEIGHTS],
            *[new_m[n] for n in WEIGHTS], *[new_v[n] for n in WEIGHTS])
```

```python
import math

import jax
import jax.numpy as jnp
from jax import lax
from jax.experimental import pallas as pl
from jax.experimental.pallas import tpu as pltpu

F32 = jnp.float32
BF16 = jnp.bfloat16

D_MODEL = 1024
DEPTH = 4
N_HEADS = 16
HEAD_DIM = 64
N_PAIRS = N_HEADS // 2
N_BLOCKS_B = 8
BLOCK_B = 128
CONV_B = 4
LRU_C = 8.0
D_FF = 2816
CONV_F = 3
D_PLE = 256
LN_EPS = 1e-5
ALPHA = (2.0 * DEPTH) ** 0.25
ADAM_LR, ADAM_B1, ADAM_B2, ADAM_EPS, ADAM_WD, ADAM_STEP = 0.001, 0.9, 0.999, 1e-08, 0.01, 10
N_DEV = 8
MESH_AXES = ("x", "y", "c")

LANES = 128
SUBLANES = 8
VMEM_LIMIT_BYTES = 56 * 1024 * 1024
ATTN_FWD_BLOCK = 1024
ATTN_BWD_BLOCK = 512
ROW_TILE = 256
NEG_BIG = -1e30


def _cparams(*sem):
    return pltpu.CompilerParams(dimension_semantics=sem if sem else None, vmem_limit_bytes=VMEM_LIMIT_BYTES)


def _pick(n, pref):
    if n <= pref:
        return n
    best = None
    for t in range(LANES, pref + 1, LANES):
        if n % t == 0:
            best = t
    assert best is not None, (n, pref)
    return best


def _sigmoid(x):
    return 1.0 / (1.0 + jnp.exp(-x))


def _log1p(x):
    u = 1.0 + x
    d = u - 1.0
    return jnp.where(d == 0.0, x, jnp.log(u) * (x / jnp.where(d == 0.0, 1.0, d)))


def _expm1(x):
    u = jnp.exp(x)
    lu = jnp.log(u)
    return jnp.where(u == 1.0, x, (u - 1.0) * (x / jnp.where(u == 1.0, 1.0, lu)))


def _softplus(x):
    return jnp.maximum(x, 0.0) + _log1p(jnp.exp(-jnp.abs(x)))


_GELU_C = math.sqrt(2.0 / math.pi)


def _gelu(x):
    return 0.5 * x * (1.0 + jnp.tanh(_GELU_C * (x + 0.044715 * x * x * x)))


def _gelu_and_grad(x):
    t = jnp.tanh(_GELU_C * (x + 0.044715 * x * x * x))
    du = _GELU_C * (1.0 + 3.0 * 0.044715 * x * x)
    return 0.5 * x * (1.0 + t), 0.5 * (1.0 + t) + 0.5 * x * (1.0 - t * t) * du


_DOT_DIMS = {"nn": (((1,), (0,)), ((), ())), "nt": (((1,), (1,)), ((), ())), "tn": (((0,), (0,)), ((), ()))}


def _mm(a, b, mode, out_dtype, name, add=None, add_scale=1.0, tm=512, tn=1408, tk=1408):
    if mode == "nn":
        (m, k), (k2, n) = a.shape, b.shape
    elif mode == "nt":
        (m, k), (n, k2) = a.shape, b.shape
    else:
        (k, m), (k2, n) = a.shape, b.shape
    assert k == k2 and a.dtype == BF16 and b.dtype == BF16, (a.shape, b.shape, a.dtype, b.dtype)
    tm, tn, tk = _pick(m, tm), _pick(n, tn), _pick(k, tk)
    nk = k // tk
    dims = _DOT_DIMS[mode]

    def body(*refs):
        if add is None:
            a_ref, b_ref, o_ref, acc_ref = refs
        else:
            a_ref, b_ref, add_ref, o_ref, acc_ref = refs
        kk = pl.program_id(2)

        @pl.when(kk == 0)
        def _():
            acc_ref[...] = jnp.zeros_like(acc_ref)

        acc_ref[...] += lax.dot_general(a_ref[...], b_ref[...], dims, preferred_element_type=F32)

        @pl.when(kk == nk - 1)
        def _():
            r = acc_ref[...]
            if add is not None:
                r = r + add_scale * add_ref[...]
            o_ref[...] = r.astype(out_dtype)

    a_spec = (pl.BlockSpec((tk, tm), lambda j, i, kk: (kk, i)) if mode == "tn"
              else pl.BlockSpec((tm, tk), lambda j, i, kk: (i, kk)))
    b_spec = (pl.BlockSpec((tn, tk), lambda j, i, kk: (j, kk)) if mode == "nt"
              else pl.BlockSpec((tk, tn), lambda j, i, kk: (kk, j)))
    o_spec = pl.BlockSpec((tm, tn), lambda j, i, kk: (i, j))
    in_specs, args = [a_spec, b_spec], [a, b]
    if add is not None:
        assert add.shape == (m, n) and add.dtype == F32
        in_specs.append(o_spec)
        args.append(add)
    return pl.pallas_call(
        body, name=name, grid=(n // tn, m // tm, nk),
        in_specs=in_specs, out_specs=o_spec,
        out_shape=jax.ShapeDtypeStruct((m, n), out_dtype),
        scratch_shapes=[pltpu.VMEM((tm, tn), F32)],
        compiler_params=_cparams("parallel", "parallel", "arbitrary"),
    )(*args)


def _ln_fwd(x, m, g, b, name):
    t, d = x.shape
    bt = _pick(t, ROW_TILE)

    def body(x_ref, m_ref, g_ref, b_ref, y_ref, yb_ref, z_ref):
        z = ALPHA * x_ref[...] + m_ref[...]
        mu = jnp.mean(z, axis=-1, keepdims=True)
        zc = z - mu
        var = jnp.mean(zc * zc, axis=-1, keepdims=True)
        y = zc * lax.rsqrt(var + LN_EPS) * g_ref[...] + b_ref[...]
        y_ref[...] = y
        yb_ref[...] = y.astype(BF16)
        z_ref[...] = z

    row = pl.BlockSpec((bt, d), lambda i: (i, 0))
    vec = pl.BlockSpec((1, d), lambda i: (0, 0))
    return pl.pallas_call(
        body, name=name, grid=(t // bt,), in_specs=[row, row, vec, vec], out_specs=[row, row, row],
        out_shape=[jax.ShapeDtypeStruct((t, d), F32), jax.ShapeDtypeStruct((t, d), BF16),
                   jax.ShapeDtypeStruct((t, d), F32)],
        compiler_params=_cparams("parallel"),
    )(x, m, g.reshape(1, d), b.reshape(1, d))


def _ln_bwd(dy, z, g, name):
    t, d = dy.shape
    bt = _pick(t, ROW_TILE)

    def body(dy_ref, z_ref, g_ref, dz_ref, dzb_ref, dg_ref, db_ref):
        @pl.when(pl.program_id(0) == 0)
        def _():
            dg_ref[...] = jnp.zeros_like(dg_ref)
            db_ref[...] = jnp.zeros_like(db_ref)

        z = z_ref[...]
        dyv = dy_ref[...]
        mu = jnp.mean(z, axis=-1, keepdims=True)
        zc = z - mu
        var = jnp.mean(zc * zc, axis=-1, keepdims=True)
        rstd = lax.rsqrt(var + LN_EPS)
        xhat = zc * rstd
        dxh = dyv * g_ref[...]
        m1 = jnp.mean(dxh, axis=-1, keepdims=True)
        m2 = jnp.mean(dxh * xhat, axis=-1, keepdims=True)
        dz = rstd * (dxh - m1 - xhat * m2)
        dz_ref[...] = dz
        dzb_ref[...] = dz.astype(BF16)
        dg_ref[...] += jnp.sum(dyv * xhat, axis=0, keepdims=True)
        db_ref[...] += jnp.sum(dyv, axis=0, keepdims=True)

    row = pl.BlockSpec((bt, d), lambda i: (i, 0))
    vec = pl.BlockSpec((1, d), lambda i: (0, 0))
    return pl.pallas_call(
        body, name=name, grid=(t // bt,), in_specs=[row, row, vec], out_specs=[row, row, vec, vec],
        out_shape=[jax.ShapeDtypeStruct((t, d), F32), jax.ShapeDtypeStruct((t, d), BF16),
                   jax.ShapeDtypeStruct((1, d), F32), jax.ShapeDtypeStruct((1, d), F32)],
        compiler_params=_cparams("arbitrary"),
    )(dy, z, g.reshape(1, d))


def _rows_iota(shape):
    return lax.broadcasted_iota(jnp.int32, shape, 0)


def _shift_down(x, halo, s):
    if s == 0:
        return x
    rolled = pltpu.roll(x, s, axis=0)
    first = jnp.where(_rows_iota((SUBLANES, x.shape[1])) < s, pltpu.roll(halo, s, axis=0), rolled[:SUBLANES])
    return jnp.concatenate([first, rolled[SUBLANES:]], axis=0)


def _shift_up(x, halo, s):
    if s == 0:
        return x
    n = x.shape[0]
    rolled = pltpu.roll(x, n - s, axis=0)
    last = jnp.where(_rows_iota((SUBLANES, x.shape[1])) < SUBLANES - s, rolled[n - SUBLANES:],
                     pltpu.roll(halo, SUBLANES - s, axis=0))
    return jnp.concatenate([rolled[:n - SUBLANES], last], axis=0)


def _prev_halo_spec(bt, cb, col_off=0):
    r = bt // SUBLANES
    return pl.BlockSpec((SUBLANES, cb), lambda i, j: (jnp.maximum(i * r - 1, 0), j + col_off))


def _next_halo_spec(bt, cb, t, col_off=0):
    r = bt // SUBLANES
    last = t // SUBLANES - 1
    return pl.BlockSpec((SUBLANES, cb), lambda i, j: (jnp.minimum((i + 1) * r, last), j + col_off))


def _causal_conv(x, halo, w_ref, ksize):
    acc = None
    for j in range(ksize):
        term = w_ref[j:j + 1, :] * _shift_down(x, halo, ksize - 1 - j)
        acc = term if acc is None else acc + term
    return acc


def _anticausal_conv(y, halo, w_ref, ksize):
    acc = None
    for j in range(ksize):
        term = w_ref[j:j + 1, :] * _shift_up(y, halo, ksize - 1 - j)
        acc = term if acc is None else acc + term
    return acc


def _ffn_mid_fwd(h, cw, cb_, name):
    t = h.shape[0]
    bt, cb = _pick(t, ROW_TILE), _pick(D_FF, 1408)
    nc = D_FF // cb

    def body(hv_ref, hvh_ref, hg_ref, hgh_ref, wv_ref, wg_ref, bv_ref, bg_ref, y_ref):
        keep = (pl.program_id(0) > 0).astype(F32)
        val = _causal_conv(hv_ref[...], hvh_ref[...] * keep, wv_ref, CONV_F) + bv_ref[...]
        gate = _causal_conv(hg_ref[...], hgh_ref[...] * keep, wg_ref, CONV_F) + bg_ref[...]
        y_ref[...] = (_gelu(gate) * val).astype(BF16)

    tile_v = pl.BlockSpec((bt, cb), lambda i, j: (i, j))
    tile_g = pl.BlockSpec((bt, cb), lambda i, j: (i, j + nc))
    wv = pl.BlockSpec((CONV_F, cb), lambda i, j: (0, j))
    wg = pl.BlockSpec((CONV_F, cb), lambda i, j: (0, j + nc))
    bv = pl.BlockSpec((1, cb), lambda i, j: (0, j))
    bg = pl.BlockSpec((1, cb), lambda i, j: (0, j + nc))
    cb2 = cb_.reshape(1, 2 * D_FF)
    return pl.pallas_call(
        body, name=name, grid=(t // bt, nc),
        in_specs=[tile_v, _prev_halo_spec(bt, cb), tile_g, _prev_halo_spec(bt, cb, nc), wv, wg, bv, bg],
        out_specs=tile_v, out_shape=jax.ShapeDtypeStruct((t, D_FF), BF16),
        compiler_params=_cparams("parallel", "parallel"),
    )(h, h, h, h, cw, cw, cb2, cb2)


def _ffn_mid_bwd_a(h, dy, cw, cb_, name):
    t = h.shape[0]
    bt, cb = _pick(t, ROW_TILE), _pick(D_FF, 1408)
    nc = D_FF // cb

    def body(hv_ref, hvh_ref, hg_ref, hgh_ref, dy_ref, wv_ref, wg_ref, bv_ref, bg_ref,
             dv_ref, dg_ref, dwv_ref, dwg_ref, dbv_ref, dbg_ref):
        @pl.when(pl.program_id(1) == 0)
        def _():
            for r in (dwv_ref, dwg_ref, dbv_ref, dbg_ref):
                r[...] = jnp.zeros_like(r)

        keep = (pl.program_id(1) > 0).astype(F32)
        hv, hvh = hv_ref[...], hvh_ref[...] * keep
        hg, hgh = hg_ref[...], hgh_ref[...] * keep
        val = _causal_conv(hv, hvh, wv_ref, CONV_F) + bv_ref[...]
        gate = _causal_conv(hg, hgh, wg_ref, CONV_F) + bg_ref[...]
        gl, glg = _gelu_and_grad(gate)
        dyv = dy_ref[...]
        dval = dyv * gl
        dgate = dyv * val * glg
        dv_ref[...] = dval
        dg_ref[...] = dgate
        dbv_ref[...] += jnp.sum(dval, axis=0, keepdims=True)
        dbg_ref[...] += jnp.sum(dgate, axis=0, keepdims=True)
        for j in range(CONV_F):
            s = CONV_F - 1 - j
            dwv_ref[j:j + 1, :] += jnp.sum(dval * _shift_down(hv, hvh, s), axis=0, keepdims=True)
            dwg_ref[j:j + 1, :] += jnp.sum(dgate * _shift_down(hg, hgh, s), axis=0, keepdims=True)

    tile_v = pl.BlockSpec((bt, cb), lambda j, i: (i, j))
    tile_g = pl.BlockSpec((bt, cb), lambda j, i: (i, j + nc))
    r = bt // SUBLANES
    halo_v = pl.BlockSpec((SUBLANES, cb), lambda j, i: (jnp.maximum(i * r - 1, 0), j))
    halo_g = pl.BlockSpec((SUBLANES, cb), lambda j, i: (jnp.maximum(i * r - 1, 0), j + nc))
    wv = pl.BlockSpec((CONV_F, cb), lambda j, i: (0, j))
    wg = pl.BlockSpec((CONV_F, cb), lambda j, i: (0, j + nc))
    bv = pl.BlockSpec((1, cb), lambda j, i: (0, j))
    bg = pl.BlockSpec((1, cb), lambda j, i: (0, j + nc))
    cb2 = cb_.reshape(1, 2 * D_FF)
    dv, dg, dwv, dwg, dbv, dbg = pl.pallas_call(
        body, name=name, grid=(nc, t // bt),
        in_specs=[tile_v, halo_v, tile_g, halo_g, tile_v, wv, wg, bv, bg],
        out_specs=[tile_v, tile_v, wv, wv, bv, bv],
        out_shape=[jax.ShapeDtypeStruct((t, D_FF), F32), jax.ShapeDtypeStruct((t, D_FF), F32),
                   jax.ShapeDtypeStruct((CONV_F, D_FF), F32), jax.ShapeDtypeStruct((CONV_F, D_FF), F32),
                   jax.ShapeDtypeStruct((1, D_FF), F32), jax.ShapeDtypeStruct((1, D_FF), F32)],
        compiler_params=_cparams("parallel", "arbitrary"),
    )(h, h, h, h, dy, cw, cw, cb2, cb2)
    return dv, dg, jnp.concatenate([dwv, dwg], axis=1), jnp.concatenate([dbv, dbg], axis=1)


def _conv_bwd_x(dy, cw, name, out_dtype):
    t, c = dy.shape
    ksize = cw.shape[0]
    bt, cb = _pick(t, ROW_TILE), _pick(c, 1408)

    def body(dy_ref, halo_ref, w_ref, o_ref):
        keep = (pl.program_id(0) < t // bt - 1).astype(F32)
        o_ref[...] = _anticausal_conv(dy_ref[...], halo_ref[...] * keep, w_ref, ksize).astype(out_dtype)

    tile = pl.BlockSpec((bt, cb), lambda i, j: (i, j))
    return pl.pallas_call(
        body, name=name, grid=(t // bt, c // cb),
        in_specs=[tile, _next_halo_spec(bt, cb, t), pl.BlockSpec((ksize, cb), lambda i, j: (0, j))],
        out_specs=tile, out_shape=jax.ShapeDtypeStruct((t, c), out_dtype),
        compiler_params=_cparams("parallel", "parallel"),
    )(dy, dy, cw)


def _ple_fwd(x2, g, e, bg, name):
    t, d = x2.shape
    bt = _pick(t, ROW_TILE)

    def body(x_ref, g_ref, e_ref, b_ref, y_ref, yb_ref):
        y = x_ref[...] + _sigmoid(g_ref[...] + b_ref[...]) * e_ref[...]
        y_ref[...] = y
        yb_ref[...] = y.astype(BF16)

    row = pl.BlockSpec((bt, d), lambda i: (i, 0))
    vec = pl.BlockSpec((1, d), lambda i: (0, 0))
    return pl.pallas_call(
        body, name=name, grid=(t // bt,), in_specs=[row, row, row, vec], out_specs=[row, row],
        out_shape=[jax.ShapeDtypeStruct((t, d), F32), jax.ShapeDtypeStruct((t, d), BF16)],
        compiler_params=_cparams("parallel"),
    )(x2, g, e, bg.reshape(1, d))


def _ple_bwd(dx3, g, e, bg, name):
    t, d = dx3.shape
    bt = _pick(t, ROW_TILE)

    def body(dx_ref, g_ref, e_ref, b_ref, dg_ref, de_ref, db_ref):
        @pl.when(pl.program_id(0) == 0)
        def _():
            db_ref[...] = jnp.zeros_like(db_ref)

        dx = dx_ref[...]
        gate = _sigmoid(g_ref[...] + b_ref[...])
        dg = dx * e_ref[...] * gate * (1.0 - gate)
        dg_ref[...] = dg.astype(BF16)
        de_ref[...] = (dx * gate).astype(BF16)
        db_ref[...] += jnp.sum(dg, axis=0, keepdims=True)

    row = pl.BlockSpec((bt, d), lambda i: (i, 0))
    vec = pl.BlockSpec((1, d), lambda i: (0, 0))
    return pl.pallas_call(
        body, name=name, grid=(t // bt,), in_specs=[row, row, row, vec], out_specs=[row, row, vec],
        out_shape=[jax.ShapeDtypeStruct((t, d), BF16), jax.ShapeDtypeStruct((t, d), BF16),
                   jax.ShapeDtypeStruct((1, d), F32)],
        compiler_params=_cparams("arbitrary"),
    )(dx3, g, e, bg.reshape(1, d))


def _loss_head(y, target, name):
    t, d = y.shape
    bt = _pick(t, ROW_TILE)

    def body(y_ref, t_ref, dy_ref, l_ref, acc_ref):
        @pl.when(pl.program_id(0) == 0)
        def _():
            acc_ref[...] = jnp.zeros_like(acc_ref)

        err = y_ref[...] - t_ref[...]
        dy_ref[...] = err * (1.0 / d)
        acc_ref[...] += jnp.sum(err * err, axis=0, keepdims=True)

        @pl.when(pl.program_id(0) == t // bt - 1)
        def _():
            l_ref[...] = jnp.full(l_ref.shape, (0.5 / d) * jnp.sum(acc_ref[...]), F32)

    row = pl.BlockSpec((bt, d), lambda i: (i, 0))
    dy, l = pl.pallas_call(
        body, name=name, grid=(t // bt,), in_specs=[row, row],
        out_specs=[row, pl.BlockSpec((SUBLANES, LANES), lambda i: (0, 0))],
        out_shape=[jax.ShapeDtypeStruct((t, d), F32), jax.ShapeDtypeStruct((SUBLANES, LANES), F32)],
        scratch_shapes=[pltpu.VMEM((1, d), F32)],
        compiler_params=_cparams("arbitrary"),
    )(y, target)
    return dy, l[0, 0]


def _split3(x):
    hi = x.astype(BF16)
    r1 = x - hi.astype(F32)
    mid = r1.astype(BF16)
    lo = (r1 - mid.astype(F32)).astype(BF16)
    return hi, mid, lo


def _tri_dot(x, tri):
    hi, mid, lo = _split3(x)
    dims = (((1,), (0,)), ((), ()))
    return (lax.dot_general(hi, tri, dims, preferred_element_type=F32)
            + lax.dot_general(mid, tri, dims, preferred_element_type=F32)
            + lax.dot_general(lo, tri, dims, preferred_element_type=F32))


def _fgate_fwd(fg_rows, b_f, name):
    hh, t = fg_rows.shape
    bt = _pick(t, 512)

    def body(fg_ref, b_ref, c_ref, carry_ref):
        @pl.when(pl.program_id(0) == 0)
        def _():
            carry_ref[...] = jnp.zeros_like(carry_ref)

        xx = fg_ref[...] + b_ref[...]
        logf = jnp.minimum(xx, 0.0) - _log1p(jnp.exp(-jnp.abs(xx)))
        r = lax.broadcasted_iota(jnp.int32, (bt, bt), 0)
        c = lax.broadcasted_iota(jnp.int32, (bt, bt), 1)
        tri = (r <= c).astype(BF16)
        cs = _tri_dot(logf, tri) + carry_ref[...]
        c_ref[...] = cs
        carry_ref[...] = cs[:, bt - 1:bt]

    return pl.pallas_call(
        body, name=name, grid=(t // bt,),
        in_specs=[pl.BlockSpec((hh, bt), lambda i: (0, i)), pl.BlockSpec((hh, 1), lambda i: (0, 0))],
        out_specs=pl.BlockSpec((hh, bt), lambda i: (0, i)),
        out_shape=jax.ShapeDtypeStruct((hh, t), F32),
        scratch_shapes=[pltpu.VMEM((hh, 1), F32)],
        compiler_params=_cparams("arbitrary"),
    )(fg_rows, b_f.reshape(hh, 1))


def _fgate_bwd(dck_rows, dcq_rows, fg_rows, b_f, name):
    hh, t = fg_rows.shape
    bt = _pick(t, 512)
    nb = t // bt

    def body(dc_ref, dcq_ref, fg_ref, b_ref, dfg_ref, db_ref, carry_ref):
        @pl.when(pl.program_id(0) == 0)
        def _():
            carry_ref[...] = jnp.zeros_like(carry_ref)
            db_ref[...] = jnp.zeros_like(db_ref)

        r = lax.broadcasted_iota(jnp.int32, (bt, bt), 0)
        c = lax.broadcasted_iota(jnp.int32, (bt, bt), 1)
        tri = (r >= c).astype(BF16)
        dlogf = _tri_dot(dc_ref[...] + dcq_ref[...], tri) + carry_ref[...]
        carry_ref[...] = dlogf[:, 0:1]
        xx = fg_ref[...] + b_ref[...]
        dfg = dlogf * _sigmoid(-xx)
        dfg_ref[...] = dfg
        db_ref[...] += jnp.sum(dfg, axis=1, keepdims=True)

    blk = pl.BlockSpec((hh, bt), lambda i: (0, nb - 1 - i))
    vec = pl.BlockSpec((hh, 1), lambda i: (0, 0))
    return pl.pallas_call(
        body, name=name, grid=(nb,), in_specs=[blk, blk, blk, vec], out_specs=[blk, vec],
        out_shape=[jax.ShapeDtypeStruct((hh, t), F32), jax.ShapeDtypeStruct((hh, 1), F32)],
        scratch_shapes=[pltpu.VMEM((hh, 1), F32)],
        compiler_params=_cparams("arbitrary"),
    )(dck_rows, dcq_rows, fg_rows, b_f.reshape(hh, 1))


def _rows_to_cols(r):
    hh, t = r.shape
    return jnp.repeat(r.reshape(hh // 2, 2, t).transpose(0, 2, 1), HEAD_DIM, axis=-1)


def _rows_to_pairs(r):
    hh, t = r.shape
    return jnp.pad(r.reshape(hh // 2, 2, t), ((0, 0), (0, SUBLANES - 2), (0, 0)))


AUG_C = HEAD_DIM
AUG_ONE = HEAD_DIM + 3


def _attn_prep(qkv, c_cols, name):
    t = qkv.shape[0]
    bt = _pick(t, ATTN_FWD_BLOCK)
    scale = 1.0 / math.sqrt(HEAD_DIM)

    def body(q_ref, k_ref, c_ref, qh_ref, kh_ref):
        lane = lax.broadcasted_iota(jnp.int32, (bt, LANES), 1)
        q2 = q_ref[...].astype(F32)
        k2 = k_ref[...].astype(F32) * scale
        c2 = c_ref[...]
        parts = [p.astype(F32) for p in _split3(c2 - c2[0:1, :])]
        swapped = [pltpu.roll(p, HEAD_DIM, axis=1) for p in parts]
        for a in (0, 1):
            qa = q2 if a == 0 else pltpu.roll(q2, HEAD_DIM, axis=1)
            ka = k2 if a == 0 else pltpu.roll(k2, HEAD_DIM, axis=1)
            hi, mid, lo = swapped if a == 0 else parts
            kaug = jnp.where(lane < HEAD_DIM, ka,
                             jnp.where(lane == AUG_C, hi,
                                       jnp.where(lane == AUG_C + 1, mid,
                                                 jnp.where(lane == AUG_C + 2, lo,
                                                           jnp.where(lane == AUG_ONE, 1.0, 0.0)))))
            qaug = jnp.where(lane < HEAD_DIM, qa, jnp.where(lane < AUG_ONE, -1.0, 0.0))
            qh_ref[:, a * LANES:(a + 1) * LANES] = qaug.astype(BF16)
            kh_ref[:, a * LANES:(a + 1) * LANES] = kaug.astype(BF16)

    out = pl.BlockSpec((bt, 2 * LANES), lambda i, hp: (i, hp))
    shape = jax.ShapeDtypeStruct((t, N_HEADS * LANES), BF16)
    return pl.pallas_call(
        body, name=name, grid=(t // bt, N_PAIRS),
        in_specs=[pl.BlockSpec((bt, LANES), lambda i, hp: (i, hp)),
                  pl.BlockSpec((bt, LANES), lambda i, hp: (i, N_PAIRS + hp)),
                  pl.BlockSpec((None, bt, LANES), lambda i, hp: (hp, i, 0))],
        out_specs=[out, out], out_shape=[shape, shape],
        compiler_params=_cparams("parallel", "parallel"),
    )(qkv, qkv, c_cols)


def _block_scalar(c_ref, a, start):
    return c_ref[a:a + 1, pl.ds(start, LANES)][:, 0:1]


def _attn_fwd_t(qh, kh, vt, c_pairs, name):
    t = qh.shape[0]
    bq = _pick(t, ATTN_FWD_BLOCK)
    nq = t // bq

    def body(q_ref, k_ref, vt_ref, c_ref, ot_ref, otb_ref, lse_ref, acc_ref):
        i = pl.program_id(1)
        q0 = pl.multiple_of(i * bq, bq)
        qs = (q_ref[:, 0:LANES], q_ref[:, LANES:2 * LANES])
        cq = [_block_scalar(c_ref, a, q0) for a in (0, 1)]
        acc_ref[...] = jnp.zeros_like(acc_ref)
        keep = _rows_iota((bq, bq)) <= lax.broadcasted_iota(jnp.int32, (bq, bq), 1)

        def step(j, carry, masked):
            k0 = pl.multiple_of(j * bq, bq)
            kb = k_ref[pl.ds(k0, bq), :]
            new = []
            for a in (0, 1):
                m_old, l_old = carry[2 * a], carry[2 * a + 1]
                st = lax.dot_general(kb[:, a * LANES:(a + 1) * LANES], qs[a], _DOT_DIMS["nt"],
                                     preferred_element_type=F32)
                if masked:
                    st = jnp.where(keep, st, NEG_BIG)
                sigma = cq[a] - _block_scalar(c_ref, a, k0)
                m_new = jnp.maximum(m_old, jnp.max(st, axis=0, keepdims=True) + sigma)
                pt = jnp.exp(st - (m_new - sigma))
                alpha = jnp.exp(m_old - m_new)
                l_new = alpha * l_old + jnp.sum(pt, axis=0, keepdims=True)
                vta = vt_ref[a * HEAD_DIM:(a + 1) * HEAD_DIM, pl.ds(k0, bq)]
                acc_ref[a] = alpha * acc_ref[a] + lax.dot_general(
                    vta, pt.astype(BF16), _DOT_DIMS["nn"], preferred_element_type=F32)
                new += [m_new, l_new]
            return tuple(new)

        neg = jnp.full((1, bq), NEG_BIG, F32)
        zero = jnp.zeros((1, bq), F32)
        carry = lax.fori_loop(0, i, lambda j, c: step(j, c, False), (neg, zero, neg, zero))
        m_a, l_a, m_b, l_b = step(i, carry, True)
        ot = jnp.concatenate([acc_ref[0] / l_a, acc_ref[1] / l_b], axis=0)
        ot_ref[...] = ot
        otb_ref[...] = ot.astype(BF16)
        lse_ref[...] = jnp.zeros_like(lse_ref)
        lse_ref[0:1, :] = m_a + jnp.log(l_a)
        lse_ref[1:2, :] = m_b + jnp.log(l_b)

    rows = pl.BlockSpec((None, SUBLANES, bq), lambda hp, i: (hp, 0, i))
    otile = pl.BlockSpec((LANES, bq), lambda hp, i: (hp, i))
    return pl.pallas_call(
        body, name=name, grid=(N_PAIRS, nq),
        in_specs=[pl.BlockSpec((bq, 2 * LANES), lambda hp, i: (i, hp)),
                  pl.BlockSpec((t, 2 * LANES), lambda hp, i: (0, hp)),
                  pl.BlockSpec((LANES, t), lambda hp, i: (hp, 0)),
                  pl.BlockSpec((None, SUBLANES, t), lambda hp, i: (hp, 0, 0))],
        out_specs=[otile, otile, rows],
        out_shape=[jax.ShapeDtypeStruct((D_MODEL, t), F32), jax.ShapeDtypeStruct((D_MODEL, t), BF16),
                   jax.ShapeDtypeStruct((N_PAIRS, SUBLANES, t), F32)],
        scratch_shapes=[pltpu.VMEM((2, HEAD_DIM, bq), F32)],
        compiler_params=_cparams("parallel", "arbitrary"),
    )(qh, kh, vt, c_pairs)


def _attn_delta_t(dot, ot, name):
    t = dot.shape[1]
    bt = _pick(t, 512)

    def body(do_ref, o_ref, d_ref, dob_ref):
        dob = do_ref[...].astype(BF16)
        prod = dob.astype(F32) * o_ref[...]
        d_ref[...] = jnp.zeros_like(d_ref)
        d_ref[0:1, :] = jnp.sum(prod[0:HEAD_DIM], axis=0, keepdims=True)
        d_ref[1:2, :] = jnp.sum(prod[HEAD_DIM:], axis=0, keepdims=True)
        dob_ref[...] = dob

    tile = pl.BlockSpec((LANES, bt), lambda hp, i: (hp, i))
    return pl.pallas_call(
        body, name=name, grid=(N_PAIRS, t // bt), in_specs=[tile, tile],
        out_specs=[pl.BlockSpec((None, SUBLANES, bt), lambda hp, i: (hp, 0, i)), tile],
        out_shape=[jax.ShapeDtypeStruct((N_PAIRS, SUBLANES, t), F32), jax.ShapeDtypeStruct((D_MODEL, t), BF16)],
        compiler_params=_cparams("parallel", "parallel"),
    )(dot, ot)


def _attn_bwd_t(kh, kt, qt, qkv, dotb, c_pairs, lse_pairs, d_pairs, name):
    t = kh.shape[0]
    bk = _pick(t, ATTN_BWD_BLOCK)
    nk = t // bk
    ref_tile = _pick(t, ATTN_FWD_BLOCK)
    assert ref_tile % bk == 0
    scale = 1.0 / math.sqrt(HEAD_DIM)

    def ref_start(start):
        return pl.multiple_of((start // ref_tile) * ref_tile, ref_tile)

    def body(k_ref, kt_ref, v_ref, qt_ref, dot_ref, c_ref, lse_ref, d_ref,
             dk_ref, dv_ref, dc_ref, dq_hbm, dka_ref, dva_ref, dqa_ref, sem):
        hp, j = pl.program_id(0), pl.program_id(1)
        k0 = pl.multiple_of(j * bk, bk)
        ks = (k_ref[:, 0:LANES], k_ref[:, LANES:2 * LANES])
        vb = v_ref[...]
        ck = [_block_scalar(c_ref, a, ref_start(k0)) for a in (0, 1)]
        dka_ref[...] = jnp.zeros_like(dka_ref)
        dva_ref[...] = jnp.zeros_like(dva_ref)

        @pl.when(j == 0)
        def _():
            dqa_ref[...] = jnp.zeros_like(dqa_ref)

        keep = _rows_iota((bk, bk)) <= lax.broadcasted_iota(jnp.int32, (bk, bk), 1)
        top = _rows_iota((LANES, bk)) < HEAD_DIM

        def step(i, masked):
            q0 = pl.multiple_of(i * bk, bk)
            dot2 = dot_ref[:, pl.ds(q0, bk)]
            zero = jnp.zeros_like(dot2)
            for a in (0, 1):
                qta = qt_ref[a * LANES:(a + 1) * LANES, pl.ds(q0, bk)]
                st = lax.dot_general(ks[a], qta, _DOT_DIMS["nn"], preferred_element_type=F32)
                sigma = _block_scalar(c_ref, a, ref_start(q0)) - ck[a]
                pt = jnp.exp(st - (lse_ref[a:a + 1, pl.ds(q0, bk)] - sigma))
                if masked:
                    pt = jnp.where(keep, pt, 0.0)
                dota = jnp.where(top, dot2, zero) if a == 0 else jnp.where(top, zero, dot2)
                dpt = lax.dot_general(vb, dota, _DOT_DIMS["nn"], preferred_element_type=F32)
                dstb = (pt * (dpt - d_ref[a:a + 1, pl.ds(q0, bk)])).astype(BF16)
                dva_ref[a] += lax.dot_general(dot2[a * HEAD_DIM:(a + 1) * HEAD_DIM, :], pt.astype(BF16),
                                              _DOT_DIMS["nt"], preferred_element_type=F32)
                dka_ref[a] += lax.dot_general(qta, dstb, _DOT_DIMS["nt"], preferred_element_type=F32)
                dqa_ref[a, :, pl.ds(q0, bk)] += lax.dot_general(
                    kt_ref[a * LANES:(a + 1) * LANES, :], dstb, _DOT_DIMS["nn"], preferred_element_type=F32)

        def loop_body(i, carry):
            step(i, False)
            return carry

        step(j, True)
        lax.fori_loop(j + 1, nk, loop_body, 0)
        dk_ref[...] = (jnp.concatenate([dka_ref[0, 0:HEAD_DIM, :], dka_ref[1, 0:HEAD_DIM, :]], axis=0)
                       * scale).astype(BF16)
        dv_ref[...] = jnp.concatenate([dva_ref[0], dva_ref[1]], axis=0).astype(BF16)
        dc_ref[...] = jnp.zeros_like(dc_ref)
        dc_ref[0:1, :] = dka_ref[0, AUG_C:AUG_C + 1, :]
        dc_ref[1:2, :] = dka_ref[1, AUG_C:AUG_C + 1, :]

        @pl.when(j == nk - 1)
        def _():
            for a in (0, 1):
                row0 = pl.multiple_of((2 * hp + a) * LANES, LANES)
                cp = pltpu.make_async_copy(dqa_ref.at[a], dq_hbm.at[pl.ds(row0, LANES), :], sem)
                cp.start()
                cp.wait()

    once = pl.Buffered(1)
    rows = pl.BlockSpec((None, SUBLANES, t), lambda hp, j: (hp, 0, 0))
    kv_out = pl.BlockSpec((LANES, bk), lambda hp, j: (hp, j))
    return pl.pallas_call(
        body, name=name, grid=(N_PAIRS, nk),
        in_specs=[pl.BlockSpec((bk, 2 * LANES), lambda hp, j: (j, hp)),
                  pl.BlockSpec((2 * LANES, bk), lambda hp, j: (hp, j)),
                  pl.BlockSpec((bk, LANES), lambda hp, j: (j, 2 * N_PAIRS + hp)),
                  pl.BlockSpec((2 * LANES, t), lambda hp, j: (hp, 0), pipeline_mode=once),
                  pl.BlockSpec((LANES, t), lambda hp, j: (hp, 0), pipeline_mode=once),
                  rows, rows, rows],
        out_specs=[kv_out, kv_out, pl.BlockSpec((None, SUBLANES, bk), lambda hp, j: (hp, 0, j)),
                   pl.BlockSpec(memory_space=pltpu.HBM)],
        out_shape=[jax.ShapeDtypeStruct((D_MODEL, t), BF16), jax.ShapeDtypeStruct((D_MODEL, t), BF16),
                   jax.ShapeDtypeStruct((N_PAIRS, SUBLANES, t), F32),
                   jax.ShapeDtypeStruct((N_HEADS * LANES, t), F32)],
        scratch_shapes=[pltpu.VMEM((2, LANES, bk), F32), pltpu.VMEM((2, HEAD_DIM, bk), F32),
                        pltpu.VMEM((2, LANES, t), F32), pltpu.SemaphoreType.DMA],
        compiler_params=_cparams("arbitrary", "arbitrary"),
    )(kh, kt, qkv, qt, dotb, c_pairs, lse_pairs, d_pairs)


def _scan(a, u, name, reverse=False):
    t, c = a.shape
    bt, cb = _pick(t, ROW_TILE), _pick(c, 1024)
    nt = t // bt
    ngroups = bt // SUBLANES

    def body(a_ref, u_ref, h_ref, carry_ref, as_ref, us_ref):
        @pl.when(pl.program_id(1) == 0)
        def _():
            carry_ref[...] = jnp.zeros_like(carry_ref)

        av, uv = a_ref[...], u_ref[...]
        sub = _rows_iota((bt, cb)) % SUBLANES
        for s in (1, 2, 4):
            if reverse:
                a_sh, u_sh = pltpu.roll(av, bt - s, axis=0), pltpu.roll(uv, bt - s, axis=0)
                valid = sub < SUBLANES - s
            else:
                a_sh, u_sh = pltpu.roll(av, s, axis=0), pltpu.roll(uv, s, axis=0)
                valid = sub >= s
            uv = jnp.where(valid, uv + av * u_sh, uv)
            av = jnp.where(valid, av * a_sh, av)
        as_ref[...] = av
        us_ref[...] = uv
        edge = 0 if reverse else SUBLANES - 1
        pick = _rows_iota((SUBLANES, cb)) == edge

        def group(gi, carry):
            g = (ngroups - 1 - gi) if reverse else gi
            r0 = pl.multiple_of(g * SUBLANES, SUBLANES)
            h8 = us_ref[pl.ds(r0, SUBLANES), :] + as_ref[pl.ds(r0, SUBLANES), :] * carry
            h_ref[pl.ds(r0, SUBLANES), :] = h8
            return jnp.sum(jnp.where(pick, h8, 0.0), axis=0, keepdims=True)

        carry_ref[...] = lax.fori_loop(0, ngroups, group, carry_ref[...])

    if reverse:
        tile = pl.BlockSpec((bt, cb), lambda j, i: (nt - 1 - i, j))
    else:
        tile = pl.BlockSpec((bt, cb), lambda j, i: (i, j))
    return pl.pallas_call(
        body, name=name, grid=(c // cb, nt), in_specs=[tile, tile], out_specs=tile,
        out_shape=jax.ShapeDtypeStruct((t, c), F32),
        scratch_shapes=[pltpu.VMEM((1, cb), F32), pltpu.VMEM((bt, cb), F32), pltpu.VMEM((bt, cb), F32)],
        compiler_params=_cparams("parallel", "arbitrary"),
    )(a, u)


def _rg_conv_fwd(proj, cw, cb_, name):
    t = proj.shape[0]
    bt, cb = _pick(t, ROW_TILE), D_MODEL

    def body(x_ref, halo_ref, w_ref, b_ref, o_ref):
        keep = (pl.program_id(0) > 0).astype(F32)
        o_ref[...] = _causal_conv(x_ref[...], halo_ref[...] * keep, w_ref, CONV_B) + b_ref[...]

    tile = pl.BlockSpec((bt, cb), lambda i, j: (i, j))
    return pl.pallas_call(
        body, name=name, grid=(t // bt, 1),
        in_specs=[tile, _prev_halo_spec(bt, cb), pl.BlockSpec((CONV_B, cb), lambda i, j: (0, 0)),
                  pl.BlockSpec((1, cb), lambda i, j: (0, 0))],
        out_specs=tile, out_shape=jax.ShapeDtypeStruct((t, D_MODEL), F32),
        compiler_params=_cparams("parallel", "parallel"),
    )(proj, proj, cw, cb_.reshape(1, D_MODEL))


def _conv_bwd_w(x, dy, ksize, name):
    t, c = dy.shape
    bt = _pick(t, ROW_TILE)
    r = bt // SUBLANES

    def body(x_ref, halo_ref, dy_ref, dw_ref, db_ref):
        @pl.when(pl.program_id(0) == 0)
        def _():
            dw_ref[...] = jnp.zeros_like(dw_ref)
            db_ref[...] = jnp.zeros_like(db_ref)

        keep = (pl.program_id(0) > 0).astype(F32)
        xv, halo, dyv = x_ref[...], halo_ref[...] * keep, dy_ref[...]
        db_ref[...] += jnp.sum(dyv, axis=0, keepdims=True)
        for j in range(ksize):
            dw_ref[j:j + 1, :] += jnp.sum(dyv * _shift_down(xv, halo, ksize - 1 - j), axis=0, keepdims=True)

    tile = pl.BlockSpec((bt, c), lambda i: (i, 0))
    return pl.pallas_call(
        body, name=name, grid=(t // bt,),
        in_specs=[tile, pl.BlockSpec((SUBLANES, c), lambda i: (jnp.maximum(i * r - 1, 0), 0)), tile],
        out_specs=[pl.BlockSpec((ksize, c), lambda i: (0, 0)), pl.BlockSpec((1, c), lambda i: (0, 0))],
        out_shape=[jax.ShapeDtypeStruct((ksize, c), F32), jax.ShapeDtypeStruct((1, c), F32)],
        compiler_params=_cparams("arbitrary"),
    )(x, x, dy)


def _rg_gate_math(xc, wa_ref, wi_ref, ba_ref, bi_ref, lam_ref):
    xb = xc.astype(BF16)
    ra = lax.dot_general(xb, wa_ref[...], _DOT_DIMS["nn"], preferred_element_type=F32) + ba_ref[...]
    ia = lax.dot_general(xb, wi_ref[...], _DOT_DIMS["nn"], preferred_element_type=F32) + bi_ref[...]
    r, ig = _sigmoid(ra), _sigmoid(ia)
    sp = _softplus(-lam_ref[...])
    log_a = -LRU_C * r * sp
    a = jnp.exp(log_a)
    mult = jnp.sqrt(-_expm1(2.0 * log_a))
    return xb, r, ig, sp, a, mult


def _rg_gate_specs(bt, time_first):
    if time_first:
        tile = pl.BlockSpec((bt, BLOCK_B), lambda i, n: (i, n))
        w = pl.BlockSpec((None, BLOCK_B, BLOCK_B), lambda i, n: (n, 0, 0))
        v = pl.BlockSpec((None, 1, BLOCK_B), lambda i, n: (n, 0, 0))
    else:
        tile = pl.BlockSpec((bt, BLOCK_B), lambda n, i: (i, n))
        w = pl.BlockSpec((None, BLOCK_B, BLOCK_B), lambda n, i: (n, 0, 0))
        v = pl.BlockSpec((None, 1, BLOCK_B), lambda n, i: (n, 0, 0))
    return tile, w, v


def _rg_gate_fwd(xc, wa, ba, wi, bi, lam, name):
    t = xc.shape[0]
    bt = _pick(t, 512)

    def body(x_ref, wa_ref, wi_ref, ba_ref, bi_ref, lam_ref, a_ref, u_ref):
        xcv = x_ref[...]
        _, _, ig, _, a, mult = _rg_gate_math(xcv, wa_ref, wi_ref, ba_ref, bi_ref, lam_ref)
        a_ref[...] = a
        u_ref[...] = mult * (ig * xcv)

    tile, w, v = _rg_gate_specs(bt, True)
    return pl.pallas_call(
        body, name=name, grid=(t // bt, N_BLOCKS_B), in_specs=[tile, w, w, v, v, v], out_specs=[tile, tile],
        out_shape=[jax.ShapeDtypeStruct((t, D_MODEL), F32), jax.ShapeDtypeStruct((t, D_MODEL), F32)],
        compiler_params=_cparams("parallel", "parallel"),
    )(xc, wa, wi, ba, bi, lam)


def _rg_gate_bwd(xc, g, h, wa, ba, wi, bi, lam, name):
    t = xc.shape[0]
    bt = _pick(t, 512)
    rr = bt // SUBLANES

    def body(x_ref, g_ref, h_ref, hh_ref, wa_ref, wi_ref, ba_ref, bi_ref, lam_ref,
             dx_ref, dwa_ref, dwi_ref, dba_ref, dbi_ref, dlam_ref):
        @pl.when(pl.program_id(1) == 0)
        def _():
            for ref in (dwa_ref, dwi_ref, dba_ref, dbi_ref, dlam_ref):
                ref[...] = jnp.zeros_like(ref)

        keep = (pl.program_id(1) > 0).astype(F32)
        xcv, gv = x_ref[...], g_ref[...]
        xb, r, ig, sp, a, mult = _rg_gate_math(xcv, wa_ref, wi_ref, ba_ref, bi_ref, lam_ref)
        h_prev = _shift_down(h_ref[...], hh_ref[...] * keep, 1)
        da = gv * h_prev
        dmult = gv * ig * xcv
        dig = gv * mult * xcv
        dxc = gv * mult * ig
        dlog_a = da * a - dmult * (a * a / mult)
        dr = dlog_a * (-LRU_C * sp)
        dsp = jnp.sum(dlog_a * (-LRU_C * r), axis=0, keepdims=True)
        dlam_ref[...] += dsp * (-_sigmoid(-lam_ref[...]))
        dra = dr * r * (1.0 - r)
        dia = dig * ig * (1.0 - ig)
        dba_ref[...] += jnp.sum(dra, axis=0, keepdims=True)
        dbi_ref[...] += jnp.sum(dia, axis=0, keepdims=True)
        drab, diab = dra.astype(BF16), dia.astype(BF16)
        dwa_ref[...] += lax.dot_general(xb, drab, _DOT_DIMS["tn"], preferred_element_type=F32)
        dwi_ref[...] += lax.dot_general(xb, diab, _DOT_DIMS["tn"], preferred_element_type=F32)
        dxc = dxc + lax.dot_general(drab, wa_ref[...], _DOT_DIMS["nt"], preferred_element_type=F32)
        dxc = dxc + lax.dot_general(diab, wi_ref[...], _DOT_DIMS["nt"], preferred_element_type=F32)
        dx_ref[...] = dxc

    tile, w, v = _rg_gate_specs(bt, False)
    halo = pl.BlockSpec((SUBLANES, BLOCK_B), lambda n, i: (jnp.maximum(i * rr - 1, 0), n))
    wshape = jax.ShapeDtypeStruct((N_BLOCKS_B, BLOCK_B, BLOCK_B), F32)
    vshape = jax.ShapeDtypeStruct((N_BLOCKS_B, 1, BLOCK_B), F32)
    return pl.pallas_call(
        body, name=name, grid=(N_BLOCKS_B, t // bt),
        in_specs=[tile, tile, tile, halo, w, w, v, v, v],
        out_specs=[tile, w, w, v, v, v],
        out_shape=[jax.ShapeDtypeStruct((t, D_MODEL), F32), wshape, wshape, vshape, vshape, vshape],
        compiler_params=_cparams("parallel", "arbitrary"),
    )(xc, g, h, h, wa, wi, ba, bi, lam)


def _rg_out_fwd(h, proj, name):
    t = h.shape[0]
    bt = _pick(t, ROW_TILE)

    def body(h_ref, g_ref, y_ref):
        y_ref[...] = (h_ref[...] * _gelu(g_ref[...])).astype(BF16)

    tile = pl.BlockSpec((bt, D_MODEL), lambda i: (i, 0))
    return pl.pallas_call(
        body, name=name, grid=(t // bt,), in_specs=[tile, pl.BlockSpec((bt, D_MODEL), lambda i: (i, 1))],
        out_specs=tile, out_shape=jax.ShapeDtypeStruct((t, D_MODEL), BF16),
        compiler_params=_cparams("parallel"),
    )(h, proj)


def _rg_out_bwd(dy, h, proj, name):
    t = h.shape[0]
    bt = _pick(t, ROW_TILE)

    def body(dy_ref, h_ref, g_ref, dh_ref, dg_ref):
        gl, glg = _gelu_and_grad(g_ref[...])
        dyv = dy_ref[...]
        dh_ref[...] = dyv * gl
        dg_ref[...] = (dyv * h_ref[...] * glg).astype(BF16)

    tile = pl.BlockSpec((bt, D_MODEL), lambda i: (i, 0))
    return pl.pallas_call(
        body, name=name, grid=(t // bt,),
        in_specs=[tile, tile, pl.BlockSpec((bt, D_MODEL), lambda i: (i, 1))], out_specs=[tile, tile],
        out_shape=[jax.ShapeDtypeStruct((t, D_MODEL), F32), jax.ShapeDtypeStruct((t, D_MODEL), BF16)],
        compiler_params=_cparams("parallel"),
    )(dy, h, proj)


def _shift_up_one(a, name):
    t, c = a.shape
    bt = _pick(t, ROW_TILE)

    def body(a_ref, halo_ref, o_ref):
        o_ref[...] = _shift_up(a_ref[...], halo_ref[...], 1)

    tile = pl.BlockSpec((bt, c), lambda i, j: (i, j))
    return pl.pallas_call(
        body, name=name, grid=(t // bt, 1), in_specs=[tile, _next_halo_spec(bt, c, t)], out_specs=tile,
        out_shape=jax.ShapeDtypeStruct((t, c), F32), compiler_params=_cparams("parallel", "parallel"),
    )(a, a)


ADAM_ROWS = 64


def _adamw(recv, w, m, v, name):
    _, r, c = recv.shape
    br = ADAM_ROWS
    assert r % br == 0

    def body(r_ref, w_ref, m_ref, v_ref, g_ref, d_ref, nm_ref, nv_ref):
        g = r_ref[0].astype(F32)
        for s in range(1, N_DEV):
            g = g + r_ref[s].astype(F32)
        m_new = ADAM_B1 * m_ref[...] + (1.0 - ADAM_B1) * g
        v_new = ADAM_B2 * v_ref[...] + (1.0 - ADAM_B2) * (g * g)
        m_hat = m_new / (1.0 - ADAM_B1 ** ADAM_STEP)
        v_hat = v_new / (1.0 - ADAM_B2 ** ADAM_STEP)
        g_ref[...] = g
        d_ref[...] = -ADAM_LR * (m_hat / (jnp.sqrt(v_hat) + ADAM_EPS) + ADAM_WD * w_ref[...])
        nm_ref[...] = m_new
        nv_ref[...] = v_new

    tile = pl.BlockSpec((br, c), lambda i: (i, 0))
    shape = jax.ShapeDtypeStruct((r, c), F32)
    return pl.pallas_call(
        body, name=name, grid=(r // br,),
        in_specs=[pl.BlockSpec((N_DEV, br, c), lambda i: (0, i, 0)), tile, tile, tile],
        out_specs=[tile] * 4, out_shape=[shape] * 4, compiler_params=_cparams("parallel"),
    )(recv, w, m, v)


def _all_to_all(src, name):
    def body(src_ref, out_ref, send_sems, recv_sems, local_sem):
        pos = [lax.axis_index(ax) for ax in MESH_AXES]
        me = 4 * pos[0] + 2 * pos[1] + pos[2]

        def peer_of(k):
            p = [(1 - pos[b]) if (k >> (2 - b)) & 1 else pos[b] for b in range(3)]
            return tuple(p), 4 * p[0] + 2 * p[1] + p[2]

        def copy(k):
            peer, peer_idx = peer_of(k)
            return pltpu.make_async_remote_copy(
                src_ref=src_ref.at[peer_idx], dst_ref=out_ref.at[me],
                send_sem=send_sems.at[k - 1], recv_sem=recv_sems.at[k - 1],
                device_id=peer, device_id_type=pl.DeviceIdType.MESH)

        def arrival(k):
            peer, peer_idx = peer_of(k)
            return pltpu.make_async_remote_copy(
                src_ref=src_ref.at[me], dst_ref=out_ref.at[peer_idx],
                send_sem=send_sems.at[k - 1], recv_sem=recv_sems.at[k - 1],
                device_id=peer, device_id_type=pl.DeviceIdType.MESH)

        mine = pltpu.make_async_copy(src_ref.at[me], out_ref.at[me], local_sem)
        mine.start()
        sends = [copy(k) for k in range(1, N_DEV)]
        for cp in sends:
            cp.start()
        for k in range(1, N_DEV):
            arrival(k).wait_recv()
        for cp in sends:
            cp.wait_send()
        mine.wait()

    hbm = pl.BlockSpec(memory_space=pltpu.HBM)
    return pl.pallas_call(
        body, name=name, in_specs=[hbm], out_specs=hbm,
        out_shape=jax.ShapeDtypeStruct(src.shape, src.dtype),
        scratch_shapes=[pltpu.SemaphoreType.DMA((N_DEV - 1,)), pltpu.SemaphoreType.DMA((N_DEV - 1,)),
                        pltpu.SemaphoreType.DMA],
        compiler_params=pltpu.CompilerParams(has_side_effects=True),
    )(src)


def _all_gather(src, name):
    def body(src_ref, out_ref, send_sems, recv_sems, local_sem):
        x, y, c = (lax.axis_index(ax) for ax in MESH_AXES)
        me, sibling = (x, y, c), (x, y, 1 - c)
        chips = [(1 - x, y), (x, 1 - y), (1 - x, 1 - y)]

        def slot(px, py, pc):
            return out_ref.at[4 * px + 2 * py + pc]

        def copy(k, block, to, from_src=False):
            return pltpu.make_async_remote_copy(
                src_ref=src_ref if from_src else slot(*block), dst_ref=slot(*block),
                send_sem=send_sems.at[k], recv_sem=recv_sems.at[k],
                device_id=to, device_id_type=pl.DeviceIdType.MESH)

        mine = pltpu.make_async_copy(src_ref, slot(*me), local_sem)
        mine.start()
        first = [copy(0, me, sibling, True)] + [copy(1 + j, me, (*chip, c), True) for j, chip in enumerate(chips)]
        for cp in first:
            cp.start()
        passed = [copy(4 + j, (*chip, c), sibling) for j, chip in enumerate(chips)]
        for j, chip in enumerate(chips):
            copy(1 + j, (*chip, c), me).wait_recv()
            passed[j].start()
        copy(0, sibling, me).wait_recv()
        for j, chip in enumerate(chips):
            copy(4 + j, (*chip, 1 - c), me).wait_recv()
        for cp in first + passed:
            cp.wait_send()
        mine.wait()

    hbm = pl.BlockSpec(memory_space=pltpu.HBM)
    return pl.pallas_call(
        body, name=name, in_specs=[hbm], out_specs=hbm,
        out_shape=jax.ShapeDtypeStruct((N_DEV,) + tuple(src.shape), src.dtype),
        scratch_shapes=[pltpu.SemaphoreType.DMA((N_DEV - 1,)), pltpu.SemaphoreType.DMA((N_DEV - 1,)),
                        pltpu.SemaphoreType.DMA],
        compiler_params=pltpu.CompilerParams(has_side_effects=True),
    )(src)


WEIGHTS = ['a_w_in', 'a_b_f', 'a_w_out', 'b_w_in', 'b_conv_w', 'b_conv_b', 'b_w_a', 'b_b_a', 'b_w_i', 'b_b_i',
           'b_lam', 'b_w_out', 'f_w_up', 'f_conv_w', 'f_conv_b', 'f_w_down', 'ln1_g', 'ln1_b', 'ln2_g', 'ln2_b',
           'ple_w', 'ple_gate_w', 'ple_gate_b']
SHARD_AXIS = {'a_w_in': 2, 'a_b_f': None, 'a_w_out': 1, 'b_w_in': 2, 'b_conv_w': 2, 'b_conv_b': 1, 'b_w_a': None,
              'b_b_a': None, 'b_w_i': None, 'b_b_i': None, 'b_lam': 1, 'b_w_out': 1, 'f_w_up': 2, 'f_conv_w': 2,
              'f_conv_b': None, 'f_w_down': 1, 'ln1_g': None, 'ln1_b': None, 'ln2_g': None, 'ln2_b': None,
              'ple_w': 2, 'ple_gate_w': 1, 'ple_gate_b': None}
MATMUL_WEIGHTS = ['a_w_in', 'a_w_out', 'b_w_in', 'b_w_out', 'f_w_up', 'f_w_down', 'ple_w', 'ple_gate_w']
SMALL_SHARDED = ['b_conv_w', 'b_conv_b', 'b_lam', 'f_conv_w']
PACK_COLS = 1024


def _to_shards(full, axis):
    return jnp.stack(jnp.split(full, N_DEV, axis=axis))


def _from_shards(pieces, axis):
    return jnp.concatenate([pieces[d] for d in range(N_DEV)], axis=axis)


PIECE_ROWS = 16


def _piece_rows(size):
    rows = -(-size // PACK_COLS)
    return -(-rows // PIECE_ROWS) * PIECE_ROWS


def _pack_pieces(pieces, total_mult=PIECE_ROWS):
    lead = pieces[0].shape[:-1]
    blocks = []
    for pc in pieces:
        n = pc.shape[-1]
        rows = _piece_rows(n)
        pad = [(0, 0)] * len(lead) + [(0, rows * PACK_COLS - n)]
        blocks.append(jnp.pad(pc, pad).reshape(lead + (rows, PACK_COLS)))
    total = sum(b.shape[-2] for b in blocks)
    extra = -total % total_mult
    if extra:
        blocks.append(jnp.zeros(lead + (extra, PACK_COLS), pieces[0].dtype))
    return jnp.concatenate(blocks, axis=len(lead))


def _unpack_pieces(packed, shapes):
    lead = packed.shape[:-2]
    out, row = [], 0
    for shp in shapes:
        size = math.prod(shp)
        rows = _piece_rows(size)
        block = lax.slice_in_dim(packed, row, row + rows, axis=len(lead))
        flat = block.reshape(lead + (rows * PACK_COLS,))
        out.append(lax.slice_in_dim(flat, 0, size, axis=len(lead)).reshape(lead + tuple(shp)))
        row += rows
    return out


def _gather_weights(local, names, dtype, name):
    packed = _pack_pieces([local[n].astype(dtype).reshape(-1) for n in names])
    gathered = _all_gather(packed, name)
    pieces = _unpack_pieces(gathered, [local[n].shape for n in names])
    return {n: _from_shards(pc, SHARD_AXIS[n]) for n, pc in zip(names, pieces)}


def _mixer_a_fwd(tag, xb, w):
    qkv = _mm(xb, w["wqkv"], "nn", BF16, f"{tag}_qkv")
    fg = _mm(xb, w["wf"], "nn", F32, f"{tag}_fgproj")
    fg_rows = fg[:, :N_HEADS].T
    c_rows = _fgate_fwd(fg_rows, w["b_f"], f"{tag}_fgate")
    c_pairs = _rows_to_pairs(c_rows)
    qh, kh = _attn_prep(qkv, _rows_to_cols(c_rows), f"{tag}_attn_prep")
    ot, otb, lse_pairs = _attn_fwd_t(qh, kh, qkv[:, 2 * D_MODEL:].T, c_pairs, f"{tag}_attn")
    m = _mm(otb, w["wout"], "tn", F32, f"{tag}_oproj")
    return m, dict(qkv=qkv, qh=qh, kh=kh, fg_rows=fg_rows, c_pairs=c_pairs, ot=ot, otb=otb, lse_pairs=lse_pairs)


def _mixer_a_bwd(tag, dz, dzb, xb, w, s):
    t = xb.shape[0]
    dot = _mm(w["wout"], dzb, "nt", F32, f"{tag}_b_do")
    g_wout = _mm(s["otb"], dzb, "nn", F32, f"{tag}_b_dwout")
    d_pairs, dotb = _attn_delta_t(dot, s["ot"], f"{tag}_b_delta")
    dkt, dvt, dck_pairs, dq_aug = _attn_bwd_t(s["kh"], s["kh"].T, s["qh"].T, s["qkv"], dotb, s["c_pairs"],
                                              s["lse_pairs"], d_pairs, f"{tag}_b_attn")
    dq_aug = dq_aug.reshape(N_HEADS, LANES, t)
    dfg_rows, db_f = _fgate_bwd(dck_pairs[:, :2, :].reshape(N_HEADS, t), dq_aug[:, AUG_ONE, :], s["fg_rows"],
                                w["b_f"], f"{tag}_b_fgate")
    dqt = dq_aug[:, :HEAD_DIM, :].reshape(D_MODEL, t).astype(BF16)
    dqkv = jnp.concatenate([dqt, dkt, dvt], axis=0).T
    dfg = jnp.pad(dfg_rows.T, ((0, 0), (0, LANES - N_HEADS))).astype(BF16)
    dx = _mm(dqkv, w["wqkv"], "nt", F32, f"{tag}_b_dx_qkv", add=dz, add_scale=ALPHA)
    dx = _mm(dfg, w["wf"], "nt", F32, f"{tag}_b_dx_fg", add=dx)
    g_wqkv = _mm(xb, dqkv, "tn", F32, f"{tag}_b_dwqkv")
    g_wf = _mm(xb, dfg, "tn", F32, f"{tag}_b_dwf")[:, :N_HEADS]
    grads = dict(a_w_in=jnp.concatenate([g_wqkv, g_wf], axis=1), a_b_f=db_f.reshape(N_HEADS), a_w_out=g_wout)
    return dx, grads


def _mixer_b_fwd(tag, xb, w):
    proj = _mm(xb, w["win"], "nn", F32, f"{tag}_proj")
    xc = _rg_conv_fwd(proj, w["conv_w"], w["conv_b"], f"{tag}_conv")
    a, u = _rg_gate_fwd(xc, w["wa"], w["ba"], w["wi"], w["bi"], w["lam"], f"{tag}_gate")
    h = _scan(a, u, f"{tag}_scan")
    y = _rg_out_fwd(h, proj, f"{tag}_out")
    m = _mm(y, w["wout"], "nn", F32, f"{tag}_oproj")
    return m, dict(proj=proj, xc=xc, a=a, h=h, y=y)


def _mixer_b_bwd(tag, dz, dzb, xb, w, s):
    dy = _mm(dzb, w["wout"], "nt", F32, f"{tag}_b_dy")
    g_wout = _mm(s["y"], dzb, "tn", F32, f"{tag}_b_dwout")
    dh, dgate = _rg_out_bwd(dy, s["h"], s["proj"], f"{tag}_b_out")
    g = _scan(_shift_up_one(s["a"], f"{tag}_b_shift"), dh, f"{tag}_b_scan", reverse=True)
    dxc, g_wa, g_wi, g_ba, g_bi, g_lam = _rg_gate_bwd(
        s["xc"], g, s["h"], w["wa"], w["ba"], w["wi"], w["bi"], w["lam"], f"{tag}_b_gate")
    dxp = _conv_bwd_x(dxc, w["conv_w"], f"{tag}_b_convx", BF16)
    g_cw, g_cb = _conv_bwd_w(s["proj"], dxc, CONV_B, f"{tag}_b_convw")
    dproj = jnp.concatenate([dxp, dgate], axis=1)
    dx = _mm(dproj, w["win"], "nt", F32, f"{tag}_b_dx", add=dz, add_scale=ALPHA)
    g_win = _mm(xb, dproj, "tn", F32, f"{tag}_b_dwin")
    grads = dict(b_w_in=g_win, b_conv_w=g_cw, b_conv_b=g_cb.reshape(D_MODEL), b_w_a=g_wa,
                 b_b_a=g_ba.reshape(N_BLOCKS_B, BLOCK_B), b_w_i=g_wi, b_b_i=g_bi.reshape(N_BLOCKS_B, BLOCK_B),
                 b_lam=g_lam.reshape(D_MODEL), b_w_out=g_wout)
    return dx, grads


def _layer_fwd(i, x, xb, pb, w):
    tag = f"L{i}"
    mix = _mixer_a_fwd if i % 2 == 0 else _mixer_b_fwd
    m, sm = mix(tag, xb, w)
    x1, x1b, z1 = _ln_fwd(x, m, w["ln1_g"], w["ln1_b"], f"{tag}_ln1")
    h = _mm(x1b, w["wup"], "nn", F32, f"{tag}_ffn_up")
    y = _ffn_mid_fwd(h, w["fconv_w"], w["fconv_b"], f"{tag}_ffn_mid")
    ff = _mm(y, w["wdown"], "nn", F32, f"{tag}_ffn_down")
    x2, x2b, z2 = _ln_fwd(x1, ff, w["ln2_g"], w["ln2_b"], f"{tag}_ln2")
    gl = _mm(x2b, w["wg"], "nn", F32, f"{tag}_ple_gate")
    e = _mm(pb, w["wp"], "nn", F32, f"{tag}_ple_emb")
    x3, x3b = _ple_fwd(x2, gl, e, w["bg"], f"{tag}_ple")
    saved = dict(mixer=sm, xb=xb, x1b=x1b, z1=z1, h=h, y=y, x2b=x2b, z2=z2, gl=gl, e=e, pb=pb)
    return x3, x3b, saved


def _layer_bwd(i, dx3, w, s):
    tag = f"L{i}"
    dgl, de, g_bg = _ple_bwd(dx3, s["gl"], s["e"], w["bg"], f"{tag}_b_ple")
    g_wg = _mm(s["x2b"], dgl, "tn", F32, f"{tag}_b_dwg")
    g_wp = _mm(s["pb"], de, "tn", F32, f"{tag}_b_dwp")
    dx2 = _mm(dgl, w["wg"], "nt", F32, f"{tag}_b_dx2", add=dx3)
    dz2, dz2b, g_ln2g, g_ln2b = _ln_bwd(dx2, s["z2"], w["ln2_g"], f"{tag}_b_ln2")
    dy = _mm(dz2b, w["wdown"], "nt", F32, f"{tag}_b_dy")
    g_wdown = _mm(s["y"], dz2b, "tn", F32, f"{tag}_b_dwdown", tm=1408)
    dval, dgate, g_fcw, g_fcb = _ffn_mid_bwd_a(s["h"], dy, w["fconv_w"], w["fconv_b"], f"{tag}_b_ffn_mid")
    dhv = _conv_bwd_x(dval, w["fconv_w"][:, :D_FF], f"{tag}_b_convx_v", BF16)
    dhg = _conv_bwd_x(dgate, w["fconv_w"][:, D_FF:], f"{tag}_b_convx_g", BF16)
    dx1 = _mm(dhv, w["wup"][:, :D_FF], "nt", F32, f"{tag}_b_dx1_v", add=dz2, add_scale=ALPHA)
    dx1 = _mm(dhg, w["wup"][:, D_FF:], "nt", F32, f"{tag}_b_dx1_g", add=dx1)
    g_wup = jnp.concatenate([_mm(s["x1b"], dhv, "tn", F32, f"{tag}_b_dwup_v", tm=1024),
                             _mm(s["x1b"], dhg, "tn", F32, f"{tag}_b_dwup_g", tm=1024)], axis=1)
    dz1, dz1b, g_ln1g, g_ln1b = _ln_bwd(dx1, s["z1"], w["ln1_g"], f"{tag}_b_ln1")
    mix_bwd = _mixer_a_bwd if i % 2 == 0 else _mixer_b_bwd
    dx, g_mix = mix_bwd(tag, dz1, dz1b, s["xb"], w, s["mixer"])
    grads = dict(f_w_up=g_wup, f_conv_w=g_fcw, f_conv_b=g_fcb.reshape(2 * D_FF), f_w_down=g_wdown,
                 ln1_g=g_ln1g.reshape(D_MODEL), ln1_b=g_ln1b.reshape(D_MODEL), ln2_g=g_ln2g.reshape(D_MODEL),
                 ln2_b=g_ln2b.reshape(D_MODEL), ple_w=g_wp, ple_gate_w=g_wg, ple_gate_b=g_bg.reshape(D_MODEL))
    return dx, g_mix, grads


def _layer_weights(i, full, rep):
    j = i // 2
    w = dict(ln1_g=rep["ln1_g"][i], ln1_b=rep["ln1_b"][i], ln2_g=rep["ln2_g"][i], ln2_b=rep["ln2_b"][i],
             wup=full["f_w_up"][i], fconv_w=full["f_conv_w"][i], fconv_b=rep["f_conv_b"][i],
             wdown=full["f_w_down"][i], wp=full["ple_w"][i], wg=full["ple_gate_w"][i], bg=rep["ple_gate_b"][i])
    if i % 2 == 0:
        w_in = full["a_w_in"][j]
        w.update(wqkv=w_in[:, :3 * D_MODEL],
                 wf=jnp.pad(w_in[:, 3 * D_MODEL:], ((0, 0), (0, LANES - N_HEADS))),
                 b_f=rep["a_b_f"][j], wout=full["a_w_out"][j])
    else:
        w.update(win=full["b_w_in"][j], conv_w=full["b_conv_w"][j], conv_b=full["b_conv_b"][j],
                 wa=rep["b_w_a"][j].astype(BF16), wi=rep["b_w_i"][j].astype(BF16),
                 ba=rep["b_b_a"][j].reshape(N_BLOCKS_B, 1, BLOCK_B), bi=rep["b_b_i"][j].reshape(N_BLOCKS_B, 1, BLOCK_B),
                 lam=full["b_lam"][j].reshape(N_BLOCKS_B, 1, BLOCK_B), wout=full["b_w_out"][j])
    return w


def _fwd_bwd(x, p, target, full, rep):
    weights = [_layer_weights(i, full, rep) for i in range(DEPTH)]
    xb = x.astype(BF16)
    pb = p.astype(BF16)
    saved = []
    for i in range(DEPTH):
        x, xb, s = _layer_fwd(i, x, xb, pb[i], weights[i])
        saved.append(s)
    dx, loss_local = _loss_head(x, target, "loss_head")

    per_layer = {n: [None] * (DEPTH if n.startswith(("f_", "ln", "ple")) else DEPTH // 2) for n in WEIGHTS}
    for i in reversed(range(DEPTH)):
        dx, g_mix, g_layer = _layer_bwd(i, dx, weights[i], saved[i])
        for n, g in g_layer.items():
            per_layer[n][i] = g
        for n, g in g_mix.items():
            per_layer[n][i // 2] = g
    return loss_local, dx, {n: jnp.stack(per_layer[n]) for n in WEIGHTS}


def _train_step(x, p, target, local, moments_m, moments_v):
    full = _gather_weights(local, MATMUL_WEIGHTS, BF16, "gather_matmul_weights")
    full.update(_gather_weights(local, SMALL_SHARDED, F32, "gather_small_weights"))
    rep = {n: local[n] for n in WEIGHTS if SHARD_AXIS[n] is None}
    loss_local, dx, grads_full = _fwd_bwd(x, p, target, full, rep)

    unpacked = [{} for _ in range(4)]
    for group, dtype, tag in ((MATMUL_WEIGHTS, BF16, "big"), ([n for n in WEIGHTS if n not in MATMUL_WEIGHTS], F32, "small")):
        pieces = []
        for n in group:
            g = grads_full[n].astype(dtype)
            if SHARD_AXIS[n] is None:
                pieces.append(jnp.broadcast_to(g.reshape(1, -1), (N_DEV, g.size)))
            else:
                pieces.append(_to_shards(g, SHARD_AXIS[n]).reshape(N_DEV, -1))
        recv = _all_to_all(_pack_pieces(pieces, ADAM_ROWS), f"reduce_scatter_grads_{tag}")

        def pack_local(d):
            return _pack_pieces([d[n].astype(F32).reshape(-1) for n in group], ADAM_ROWS)

        outs = _adamw(recv, pack_local(local), pack_local(moments_m), pack_local(moments_v), f"adamw_{tag}")
        shapes = [local[n].shape for n in group]
        for dst, packed in zip(unpacked, outs):
            dst.update(zip(group, _unpack_pieces(packed, shapes)))
    return loss_local, dx, unpacked


def kernel(x, p, a_w_in, a_b_f, a_w_out, b_w_in, b_conv_w, b_conv_b, b_w_a, b_b_a, b_w_i, b_b_i, b_lam, b_w_out, f_w_up, f_conv_w, f_conv_b, f_w_down, ln1_g, ln1_b, ln2_g, ln2_b, ple_w, ple_gate_w, ple_gate_b, loss_target, m_a_w_in, m_a_b_f, m_a_w_out, m_b_w_in, m_b_conv_w, m_b_conv_b, m_b_w_a, m_b_b_a, m_b_w_i, m_b_b_i, m_b_lam, m_b_w_out, m_f_w_up, m_f_conv_w, m_f_conv_b, m_f_w_down, m_ln1_g, m_ln1_b, m_ln2_g, m_ln2_b, m_ple_w, m_ple_gate_w, m_ple_gate_b, v_a_w_in, v_a_b_f, v_a_w_out, v_b_w_in, v_b_conv_w, v_b_conv_b, v_b_w_a, v_b_b_a, v_b_w_i, v_b_b_i, v_b_lam, v_b_w_out, v_f_w_up, v_f_conv_w, v_f_conv_b, v_f_w_down, v_ln1_g, v_ln1_b, v_ln2_g, v_ln2_b, v_ple_w, v_ple_gate_w, v_ple_gate_b):
    given = dict(locals())
    local = {n: given[n] for n in WEIGHTS}
    mom_m = {n: given["m_" + n] for n in WEIGHTS}
    mom_v = {n: given["v_" + n] for n in WEIGHTS}
    t = x.shape[1]
    loss_local, dx, (grad, delta, new_m, new_v) = _train_step(
        x.reshape(t, D_MODEL), p.reshape(DEPTH, t, D_PLE), loss_target.reshape(t, D_MODEL), local, mom_m, mom_v)
    loss = lax.psum(loss_local, MESH_AXES)
    return (loss, dx.reshape(1, t, D_MODEL), *[grad[n] for n in WEIGHTS], *[delta[n] for n in WEIGHTS],
            *[new_m[n] for n in WEIGHTS], *[new_v[n] for n in WEIGHTS])
```

```python
import math

import jax
import jax.numpy as jnp
from jax import lax
from jax.experimental import pallas as pl
from jax.experimental.pallas import tpu as pltpu

F32 = jnp.float32
BF16 = jnp.bfloat16

D_MODEL = 1024
DEPTH = 4
N_HEADS = 16
HEAD_DIM = 64
N_PAIRS = N_HEADS // 2
N_BLOCKS_B = 8
BLOCK_B = 128
CONV_B = 4
LRU_C = 8.0
D_FF = 2816
CONV_F = 3
D_PLE = 256
LN_EPS = 1e-5
ALPHA = (2.0 * DEPTH) ** 0.25
ADAM_LR, ADAM_B1, ADAM_B2, ADAM_EPS, ADAM_WD, ADAM_STEP = 0.001, 0.9, 0.999, 1e-08, 0.01, 10
N_DEV = 8
MESH_AXES = ("x", "y", "c")

LANES = 128
SUBLANES = 8
VMEM_LIMIT_BYTES = 56 * 1024 * 1024
ATTN_FWD_BLOCK = 1024
ATTN_BWD_BLOCK = 512
ROW_TILE = 256
NEG_BIG = -1e30


def _cparams(*sem):
    return pltpu.CompilerParams(dimension_semantics=sem if sem else None, vmem_limit_bytes=VMEM_LIMIT_BYTES)


def _pick(n, pref):
    if n <= pref:
        return n
    best = None
    for t in range(LANES, pref + 1, LANES):
        if n % t == 0:
            best = t
    assert best is not None, (n, pref)
    return best


def _sigmoid(x):
    return 1.0 / (1.0 + jnp.exp(-x))


def _log1p(x):
    u = 1.0 + x
    d = u - 1.0
    return jnp.where(d == 0.0, x, jnp.log(u) * (x / jnp.where(d == 0.0, 1.0, d)))


def _expm1(x):
    u = jnp.exp(x)
    lu = jnp.log(u)
    return jnp.where(u == 1.0, x, (u - 1.0) * (x / jnp.where(u == 1.0, 1.0, lu)))


def _softplus(x):
    return jnp.maximum(x, 0.0) + _log1p(jnp.exp(-jnp.abs(x)))


_GELU_C = math.sqrt(2.0 / math.pi)


def _gelu(x):
    return 0.5 * x * (1.0 + jnp.tanh(_GELU_C * (x + 0.044715 * x * x * x)))


def _gelu_and_grad(x):
    t = jnp.tanh(_GELU_C * (x + 0.044715 * x * x * x))
    du = _GELU_C * (1.0 + 3.0 * 0.044715 * x * x)
    return 0.5 * x * (1.0 + t), 0.5 * (1.0 + t) + 0.5 * x * (1.0 - t * t) * du


_DOT_DIMS = {"nn": (((1,), (0,)), ((), ())), "nt": (((1,), (1,)), ((), ())), "tn": (((0,), (0,)), ((), ()))}


def _mm(a, b, mode, out_dtype, name, add=None, add_scale=1.0, tm=512, tn=1408, tk=1408):
    if mode == "nn":
        (m, k), (k2, n) = a.shape, b.shape
    elif mode == "nt":
        (m, k), (n, k2) = a.shape, b.shape
    else:
        (k, m), (k2, n) = a.shape, b.shape
    assert k == k2 and a.dtype == BF16 and b.dtype == BF16, (a.shape, b.shape, a.dtype, b.dtype)
    tm, tn, tk = _pick(m, tm), _pick(n, tn), _pick(k, tk)
    nk = k // tk
    dims = _DOT_DIMS[mode]

    def body(*refs):
        if add is None:
            a_ref, b_ref, o_ref, acc_ref = refs
        else:
            a_ref, b_ref, add_ref, o_ref, acc_ref = refs
        kk = pl.program_id(2)

        @pl.when(kk == 0)
        def _():
            acc_ref[...] = jnp.zeros_like(acc_ref)

        acc_ref[...] += lax.dot_general(a_ref[...], b_ref[...], dims, preferred_element_type=F32)

        @pl.when(kk == nk - 1)
        def _():
            r = acc_ref[...]
            if add is not None:
                r = r + add_scale * add_ref[...]
            o_ref[...] = r.astype(out_dtype)

    a_spec = (pl.BlockSpec((tk, tm), lambda j, i, kk: (kk, i)) if mode == "tn"
              else pl.BlockSpec((tm, tk), lambda j, i, kk: (i, kk)))
    b_spec = (pl.BlockSpec((tn, tk), lambda j, i, kk: (j, kk)) if mode == "nt"
              else pl.BlockSpec((tk, tn), lambda j, i, kk: (kk, j)))
    o_spec = pl.BlockSpec((tm, tn), lambda j, i, kk: (i, j))
    in_specs, args = [a_spec, b_spec], [a, b]
    if add is not None:
        assert add.shape == (m, n) and add.dtype == F32
        in_specs.append(o_spec)
        args.append(add)
    return pl.pallas_call(
        body, name=name, grid=(n // tn, m // tm, nk),
        in_specs=in_specs, out_specs=o_spec,
        out_shape=jax.ShapeDtypeStruct((m, n), out_dtype),
        scratch_shapes=[pltpu.VMEM((tm, tn), F32)],
        compiler_params=_cparams("parallel", "parallel", "arbitrary"),
    )(*args)


def _ln_fwd(x, m, g, b, name):
    t, d = x.shape
    bt = _pick(t, ROW_TILE)

    def body(x_ref, m_ref, g_ref, b_ref, y_ref, yb_ref, z_ref):
        z = ALPHA * x_ref[...] + m_ref[...]
        mu = jnp.mean(z, axis=-1, keepdims=True)
        zc = z - mu
        var = jnp.mean(zc * zc, axis=-1, keepdims=True)
        y = zc * lax.rsqrt(var + LN_EPS) * g_ref[...] + b_ref[...]
        y_ref[...] = y
        yb_ref[...] = y.astype(BF16)
        z_ref[...] = z

    row = pl.BlockSpec((bt, d), lambda i: (i, 0))
    vec = pl.BlockSpec((1, d), lambda i: (0, 0))
    return pl.pallas_call(
        body, name=name, grid=(t // bt,), in_specs=[row, row, vec, vec], out_specs=[row, row, row],
        out_shape=[jax.ShapeDtypeStruct((t, d), F32), jax.ShapeDtypeStruct((t, d), BF16),
                   jax.ShapeDtypeStruct((t, d), F32)],
        compiler_params=_cparams("parallel"),
    )(x, m, g.reshape(1, d), b.reshape(1, d))


def _ln_bwd(dy, z, g, name):
    t, d = dy.shape
    bt = _pick(t, ROW_TILE)

    def body(dy_ref, z_ref, g_ref, dz_ref, dzb_ref, dg_ref, db_ref):
        @pl.when(pl.program_id(0) == 0)
        def _():
            dg_ref[...] = jnp.zeros_like(dg_ref)
            db_ref[...] = jnp.zeros_like(db_ref)

        z = z_ref[...]
        dyv = dy_ref[...]
        mu = jnp.mean(z, axis=-1, keepdims=True)
        zc = z - mu
        var = jnp.mean(zc * zc, axis=-1, keepdims=True)
        rstd = lax.rsqrt(var + LN_EPS)
        xhat = zc * rstd
        dxh = dyv * g_ref[...]
        m1 = jnp.mean(dxh, axis=-1, keepdims=True)
        m2 = jnp.mean(dxh * xhat, axis=-1, keepdims=True)
        dz = rstd * (dxh - m1 - xhat * m2)
        dz_ref[...] = dz
        dzb_ref[...] = dz.astype(BF16)
        dg_ref[...] += jnp.sum(dyv * xhat, axis=0, keepdims=True)
        db_ref[...] += jnp.sum(dyv, axis=0, keepdims=True)

    row = pl.BlockSpec((bt, d), lambda i: (i, 0))
    vec = pl.BlockSpec((1, d), lambda i: (0, 0))
    return pl.pallas_call(
        body, name=name, grid=(t // bt,), in_specs=[row, row, vec], out_specs=[row, row, vec, vec],
        out_shape=[jax.ShapeDtypeStruct((t, d), F32), jax.ShapeDtypeStruct((t, d), BF16),
                   jax.ShapeDtypeStruct((1, d), F32), jax.ShapeDtypeStruct((1, d), F32)],
        compiler_params=_cparams("arbitrary"),
    )(dy, z, g.reshape(1, d))


def _rows_iota(shape):
    return lax.broadcasted_iota(jnp.int32, shape, 0)


def _shift_down(x, halo, s):
    if s == 0:
        return x
    rolled = pltpu.roll(x, s, axis=0)
    first = jnp.where(_rows_iota((SUBLANES, x.shape[1])) < s, pltpu.roll(halo, s, axis=0), rolled[:SUBLANES])
    return jnp.concatenate([first, rolled[SUBLANES:]], axis=0)


def _shift_up(x, halo, s):
    if s == 0:
        return x
    n = x.shape[0]
    rolled = pltpu.roll(x, n - s, axis=0)
    last = jnp.where(_rows_iota((SUBLANES, x.shape[1])) < SUBLANES - s, rolled[n - SUBLANES:],
                     pltpu.roll(halo, SUBLANES - s, axis=0))
    return jnp.concatenate([rolled[:n - SUBLANES], last], axis=0)


def _prev_halo_spec(bt, cb, col_off=0):
    r = bt // SUBLANES
    return pl.BlockSpec((SUBLANES, cb), lambda i, j: (jnp.maximum(i * r - 1, 0), j + col_off))


def _next_halo_spec(bt, cb, t, col_off=0):
    r = bt // SUBLANES
    last = t // SUBLANES - 1
    return pl.BlockSpec((SUBLANES, cb), lambda i, j: (jnp.minimum((i + 1) * r, last), j + col_off))


def _causal_conv(x, halo, w_ref, ksize):
    acc = None
    for j in range(ksize):
        term = w_ref[j:j + 1, :] * _shift_down(x, halo, ksize - 1 - j)
        acc = term if acc is None else acc + term
    return acc


def _anticausal_conv(y, halo, w_ref, ksize):
    acc = None
    for j in range(ksize):
        term = w_ref[j:j + 1, :] * _shift_up(y, halo, ksize - 1 - j)
        acc = term if acc is None else acc + term
    return acc


def _ffn_mid_fwd(h, cw, cb_, name):
    t = h.shape[0]
    bt, cb = _pick(t, ROW_TILE), _pick(D_FF, 1408)
    nc = D_FF // cb

    def body(hv_ref, hvh_ref, hg_ref, hgh_ref, wv_ref, wg_ref, bv_ref, bg_ref, y_ref):
        keep = (pl.program_id(0) > 0).astype(F32)
        val = _causal_conv(hv_ref[...], hvh_ref[...] * keep, wv_ref, CONV_F) + bv_ref[...]
        gate = _causal_conv(hg_ref[...], hgh_ref[...] * keep, wg_ref, CONV_F) + bg_ref[...]
        y_ref[...] = (_gelu(gate) * val).astype(BF16)

    tile_v = pl.BlockSpec((bt, cb), lambda i, j: (i, j))
    tile_g = pl.BlockSpec((bt, cb), lambda i, j: (i, j + nc))
    wv = pl.BlockSpec((CONV_F, cb), lambda i, j: (0, j))
    wg = pl.BlockSpec((CONV_F, cb), lambda i, j: (0, j + nc))
    bv = pl.BlockSpec((1, cb), lambda i, j: (0, j))
    bg = pl.BlockSpec((1, cb), lambda i, j: (0, j + nc))
    cb2 = cb_.reshape(1, 2 * D_FF)
    return pl.pallas_call(
        body, name=name, grid=(t // bt, nc),
        in_specs=[tile_v, _prev_halo_spec(bt, cb), tile_g, _prev_halo_spec(bt, cb, nc), wv, wg, bv, bg],
        out_specs=tile_v, out_shape=jax.ShapeDtypeStruct((t, D_FF), BF16),
        compiler_params=_cparams("parallel", "parallel"),
    )(h, h, h, h, cw, cw, cb2, cb2)


def _ffn_mid_bwd(h, dy, cw, cb_, name):
    t = h.shape[0]
    bt, cb = _pick(t, ROW_TILE), _pick(D_FF, 1408)
    nc = D_FF // cb
    nt = t // bt
    ext = bt + SUBLANES

    def body(hv_ref, hvp_ref, hvn_ref, hg_ref, hgp_ref, hgn_ref, dy_ref, dyn_ref, wv_ref, wg_ref, bv_ref, bg_ref,
             dhv_ref, dhg_ref, dwv_ref, dwg_ref, dbv_ref, dbg_ref):
        i = pl.program_id(1)

        @pl.when(i == 0)
        def _():
            for r in (dwv_ref, dwg_ref, dbv_ref, dbg_ref):
                r[...] = jnp.zeros_like(r)

        has_prev = (i > 0).astype(F32)
        has_next = (i < nt - 1).astype(F32)
        hv = jnp.concatenate([hv_ref[...], hvn_ref[...]], axis=0)
        hg = jnp.concatenate([hg_ref[...], hgn_ref[...]], axis=0)
        dyv = jnp.concatenate([dy_ref[...], dyn_ref[...] * has_next], axis=0)
        hvp, hgp = hvp_ref[...] * has_prev, hgp_ref[...] * has_prev
        val = _causal_conv(hv, hvp, wv_ref, CONV_F) + bv_ref[...]
        gate = _causal_conv(hg, hgp, wg_ref, CONV_F) + bg_ref[...]
        gl, glg = _gelu_and_grad(gate)
        dval = dyv * gl
        dgate = dyv * val * glg
        own_v, own_g = dval[:bt], dgate[:bt]
        dbv_ref[...] += jnp.sum(own_v, axis=0, keepdims=True)
        dbg_ref[...] += jnp.sum(own_g, axis=0, keepdims=True)
        dhv = dhg = None
        for j in range(CONV_F):
            s = CONV_F - 1 - j
            dwv_ref[j:j + 1, :] += jnp.sum(own_v * _shift_down(hv, hvp, s)[:bt], axis=0, keepdims=True)
            dwg_ref[j:j + 1, :] += jnp.sum(own_g * _shift_down(hg, hgp, s)[:bt], axis=0, keepdims=True)
            up_v = dval if s == 0 else pltpu.roll(dval, ext - s, axis=0)
            up_g = dgate if s == 0 else pltpu.roll(dgate, ext - s, axis=0)
            tv, tg = wv_ref[j:j + 1, :] * up_v[:bt], wg_ref[j:j + 1, :] * up_g[:bt]
            dhv = tv if dhv is None else dhv + tv
            dhg = tg if dhg is None else dhg + tg
        dhv_ref[...] = dhv.astype(BF16)
        dhg_ref[...] = dhg.astype(BF16)

    r = bt // SUBLANES
    last = t // SUBLANES - 1

    def specs(off):
        return (pl.BlockSpec((bt, cb), lambda j, i: (i, j + off)),
                pl.BlockSpec((SUBLANES, cb), lambda j, i: (jnp.maximum(i * r - 1, 0), j + off)),
                pl.BlockSpec((SUBLANES, cb), lambda j, i: (jnp.minimum((i + 1) * r, last), j + off)))

    tile_v, prev_v, next_v = specs(0)
    tile_g, prev_g, next_g = specs(nc)
    wv = pl.BlockSpec((CONV_F, cb), lambda j, i: (0, j))
    wg = pl.BlockSpec((CONV_F, cb), lambda j, i: (0, j + nc))
    bv = pl.BlockSpec((1, cb), lambda j, i: (0, j))
    bg = pl.BlockSpec((1, cb), lambda j, i: (0, j + nc))
    cb2 = cb_.reshape(1, 2 * D_FF)
    dhv, dhg, dwv, dwg, dbv, dbg = pl.pallas_call(
        body, name=name, grid=(nc, nt),
        in_specs=[tile_v, prev_v, next_v, tile_g, prev_g, next_g, tile_v, next_v, wv, wg, bv, bg],
        out_specs=[tile_v, tile_v, wv, wv, bv, bv],
        out_shape=[jax.ShapeDtypeStruct((t, D_FF), BF16), jax.ShapeDtypeStruct((t, D_FF), BF16),
                   jax.ShapeDtypeStruct((CONV_F, D_FF), F32), jax.ShapeDtypeStruct((CONV_F, D_FF), F32),
                   jax.ShapeDtypeStruct((1, D_FF), F32), jax.ShapeDtypeStruct((1, D_FF), F32)],
        compiler_params=_cparams("parallel", "arbitrary"),
    )(h, h, h, h, h, h, dy, dy, cw, cw, cb2, cb2)
    return dhv, dhg, jnp.concatenate([dwv, dwg], axis=1), jnp.concatenate([dbv, dbg], axis=1)


def _conv_bwd_x(dy, cw, name, out_dtype):
    t, c = dy.shape
    ksize = cw.shape[0]
    bt, cb = _pick(t, ROW_TILE), _pick(c, 1408)

    def body(dy_ref, halo_ref, w_ref, o_ref):
        keep = (pl.program_id(0) < t // bt - 1).astype(F32)
        o_ref[...] = _anticausal_conv(dy_ref[...], halo_ref[...] * keep, w_ref, ksize).astype(out_dtype)

    tile = pl.BlockSpec((bt, cb), lambda i, j: (i, j))
    return pl.pallas_call(
        body, name=name, grid=(t // bt, c // cb),
        in_specs=[tile, _next_halo_spec(bt, cb, t), pl.BlockSpec((ksize, cb), lambda i, j: (0, j))],
        out_specs=tile, out_shape=jax.ShapeDtypeStruct((t, c), out_dtype),
        compiler_params=_cparams("parallel", "parallel"),
    )(dy, dy, cw)


def _ple_fwd(x2, g, e, bg, name):
    t, d = x2.shape
    bt = _pick(t, ROW_TILE)

    def body(x_ref, g_ref, e_ref, b_ref, y_ref, yb_ref):
        y = x_ref[...] + _sigmoid(g_ref[...] + b_ref[...]) * e_ref[...]
        y_ref[...] = y
        yb_ref[...] = y.astype(BF16)

    row = pl.BlockSpec((bt, d), lambda i: (i, 0))
    vec = pl.BlockSpec((1, d), lambda i: (0, 0))
    return pl.pallas_call(
        body, name=name, grid=(t // bt,), in_specs=[row, row, row, vec], out_specs=[row, row],
        out_shape=[jax.ShapeDtypeStruct((t, d), F32), jax.ShapeDtypeStruct((t, d), BF16)],
        compiler_params=_cparams("parallel"),
    )(x2, g, e, bg.reshape(1, d))


def _ple_bwd(dx3, g, e, bg, name):
    t, d = dx3.shape
    bt = _pick(t, ROW_TILE)

    def body(dx_ref, g_ref, e_ref, b_ref, dg_ref, de_ref, db_ref):
        @pl.when(pl.program_id(0) == 0)
        def _():
            db_ref[...] = jnp.zeros_like(db_ref)

        dx = dx_ref[...]
        gate = _sigmoid(g_ref[...] + b_ref[...])
        dg = dx * e_ref[...] * gate * (1.0 - gate)
        dg_ref[...] = dg.astype(BF16)
        de_ref[...] = (dx * gate).astype(BF16)
        db_ref[...] += jnp.sum(dg, axis=0, keepdims=True)

    row = pl.BlockSpec((bt, d), lambda i: (i, 0))
    vec = pl.BlockSpec((1, d), lambda i: (0, 0))
    return pl.pallas_call(
        body, name=name, grid=(t // bt,), in_specs=[row, row, row, vec], out_specs=[row, row, vec],
        out_shape=[jax.ShapeDtypeStruct((t, d), BF16), jax.ShapeDtypeStruct((t, d), BF16),
                   jax.ShapeDtypeStruct((1, d), F32)],
        compiler_params=_cparams("arbitrary"),
    )(dx3, g, e, bg.reshape(1, d))


def _loss_head(y, target, name):
    t, d = y.shape
    bt = _pick(t, ROW_TILE)

    def body(y_ref, t_ref, dy_ref, l_ref, acc_ref):
        @pl.when(pl.program_id(0) == 0)
        def _():
            acc_ref[...] = jnp.zeros_like(acc_ref)

        err = y_ref[...] - t_ref[...]
        dy_ref[...] = err * (1.0 / d)
        acc_ref[...] += jnp.sum(err * err, axis=0, keepdims=True)

        @pl.when(pl.program_id(0) == t // bt - 1)
        def _():
            l_ref[...] = jnp.full(l_ref.shape, (0.5 / d) * jnp.sum(acc_ref[...]), F32)

    row = pl.BlockSpec((bt, d), lambda i: (i, 0))
    dy, l = pl.pallas_call(
        body, name=name, grid=(t // bt,), in_specs=[row, row],
        out_specs=[row, pl.BlockSpec((SUBLANES, LANES), lambda i: (0, 0))],
        out_shape=[jax.ShapeDtypeStruct((t, d), F32), jax.ShapeDtypeStruct((SUBLANES, LANES), F32)],
        scratch_shapes=[pltpu.VMEM((1, d), F32)],
        compiler_params=_cparams("arbitrary"),
    )(y, target)
    return dy, l[0, 0]


def _split3(x):
    hi = x.astype(BF16)
    r1 = x - hi.astype(F32)
    mid = r1.astype(BF16)
    lo = (r1 - mid.astype(F32)).astype(BF16)
    return hi, mid, lo


def _tri_dot(x, tri):
    hi, mid, lo = _split3(x)
    dims = (((1,), (0,)), ((), ()))
    return (lax.dot_general(hi, tri, dims, preferred_element_type=F32)
            + lax.dot_general(mid, tri, dims, preferred_element_type=F32)
            + lax.dot_general(lo, tri, dims, preferred_element_type=F32))


def _fgate_fwd(fg_rows, b_f, name):
    hh, t = fg_rows.shape
    bt = _pick(t, 512)

    def body(fg_ref, b_ref, c_ref, carry_ref):
        @pl.when(pl.program_id(0) == 0)
        def _():
            carry_ref[...] = jnp.zeros_like(carry_ref)

        xx = fg_ref[...] + b_ref[...]
        logf = jnp.minimum(xx, 0.0) - _log1p(jnp.exp(-jnp.abs(xx)))
        r = lax.broadcasted_iota(jnp.int32, (bt, bt), 0)
        c = lax.broadcasted_iota(jnp.int32, (bt, bt), 1)
        tri = (r <= c).astype(BF16)
        cs = _tri_dot(logf, tri) + carry_ref[...]
        c_ref[...] = cs
        carry_ref[...] = cs[:, bt - 1:bt]

    return pl.pallas_call(
        body, name=name, grid=(t // bt,),
        in_specs=[pl.BlockSpec((hh, bt), lambda i: (0, i)), pl.BlockSpec((hh, 1), lambda i: (0, 0))],
        out_specs=pl.BlockSpec((hh, bt), lambda i: (0, i)),
        out_shape=jax.ShapeDtypeStruct((hh, t), F32),
        scratch_shapes=[pltpu.VMEM((hh, 1), F32)],
        compiler_params=_cparams("arbitrary"),
    )(fg_rows, b_f.reshape(hh, 1))


def _fgate_bwd(dck_rows, dcq_rows, fg_rows, b_f, name):
    hh, t = fg_rows.shape
    bt = _pick(t, 512)
    nb = t // bt

    def body(dc_ref, dcq_ref, fg_ref, b_ref, dfg_ref, db_ref, carry_ref):
        @pl.when(pl.program_id(0) == 0)
        def _():
            carry_ref[...] = jnp.zeros_like(carry_ref)
            db_ref[...] = jnp.zeros_like(db_ref)

        r = lax.broadcasted_iota(jnp.int32, (bt, bt), 0)
        c = lax.broadcasted_iota(jnp.int32, (bt, bt), 1)
        tri = (r >= c).astype(BF16)
        dlogf = _tri_dot(dc_ref[...] + dcq_ref[...], tri) + carry_ref[...]
        carry_ref[...] = dlogf[:, 0:1]
        xx = fg_ref[...] + b_ref[...]
        dfg = dlogf * _sigmoid(-xx)
        dfg_ref[...] = dfg
        db_ref[...] += jnp.sum(dfg, axis=1, keepdims=True)

    blk = pl.BlockSpec((hh, bt), lambda i: (0, nb - 1 - i))
    vec = pl.BlockSpec((hh, 1), lambda i: (0, 0))
    return pl.pallas_call(
        body, name=name, grid=(nb,), in_specs=[blk, blk, blk, vec], out_specs=[blk, vec],
        out_shape=[jax.ShapeDtypeStruct((hh, t), F32), jax.ShapeDtypeStruct((hh, 1), F32)],
        scratch_shapes=[pltpu.VMEM((hh, 1), F32)],
        compiler_params=_cparams("arbitrary"),
    )(dck_rows, dcq_rows, fg_rows, b_f.reshape(hh, 1))


def _rows_to_cols(r):
    hh, t = r.shape
    return jnp.repeat(r.reshape(hh // 2, 2, t).transpose(0, 2, 1), HEAD_DIM, axis=-1)


def _rows_to_pairs(r):
    hh, t = r.shape
    return jnp.pad(r.reshape(hh // 2, 2, t), ((0, 0), (0, SUBLANES - 2), (0, 0)))


AUG_C = HEAD_DIM
AUG_ONE = HEAD_DIM + 3


def _attn_prep(qkv, c_cols, name):
    t = qkv.shape[0]
    bt = _pick(t, ATTN_FWD_BLOCK)
    scale = 1.0 / math.sqrt(HEAD_DIM)

    def body(q_ref, k_ref, c_ref, qh_ref, kh_ref):
        lane = lax.broadcasted_iota(jnp.int32, (bt, LANES), 1)
        q2 = q_ref[...].astype(F32)
        k2 = k_ref[...].astype(F32) * scale
        c2 = c_ref[...]
        parts = [p.astype(F32) for p in _split3(c2 - c2[0:1, :])]
        swapped = [pltpu.roll(p, HEAD_DIM, axis=1) for p in parts]
        for a in (0, 1):
            qa = q2 if a == 0 else pltpu.roll(q2, HEAD_DIM, axis=1)
            ka = k2 if a == 0 else pltpu.roll(k2, HEAD_DIM, axis=1)
            hi, mid, lo = swapped if a == 0 else parts
            kaug = jnp.where(lane < HEAD_DIM, ka,
                             jnp.where(lane == AUG_C, hi,
                                       jnp.where(lane == AUG_C + 1, mid,
                                                 jnp.where(lane == AUG_C + 2, lo,
                                                           jnp.where(lane == AUG_ONE, 1.0, 0.0)))))
            qaug = jnp.where(lane < HEAD_DIM, qa, jnp.where(lane < AUG_ONE, -1.0, 0.0))
            qh_ref[:, a * LANES:(a + 1) * LANES] = qaug.astype(BF16)
            kh_ref[:, a * LANES:(a + 1) * LANES] = kaug.astype(BF16)

    out = pl.BlockSpec((bt, 2 * LANES), lambda i, hp: (i, hp))
    shape = jax.ShapeDtypeStruct((t, N_HEADS * LANES), BF16)
    return pl.pallas_call(
        body, name=name, grid=(t // bt, N_PAIRS),
        in_specs=[pl.BlockSpec((bt, LANES), lambda i, hp: (i, hp)),
                  pl.BlockSpec((bt, LANES), lambda i, hp: (i, N_PAIRS + hp)),
                  pl.BlockSpec((None, bt, LANES), lambda i, hp: (hp, i, 0))],
        out_specs=[out, out], out_shape=[shape, shape],
        compiler_params=_cparams("parallel", "parallel"),
    )(qkv, qkv, c_cols)


def _block_scalar(c_ref, a, start):
    return c_ref[a:a + 1, pl.ds(start, LANES)][:, 0:1]


def _attn_fwd_t(qh, kh, vt, c_pairs, name):
    t = qh.shape[0]
    bq = _pick(t, ATTN_FWD_BLOCK)
    nq = t // bq

    def body(q_ref, k_ref, vt_ref, c_ref, ot_ref, otb_ref, lse_ref, acc_ref):
        i = pl.program_id(1)
        q0 = pl.multiple_of(i * bq, bq)
        qs = (q_ref[:, 0:LANES], q_ref[:, LANES:2 * LANES])
        cq = [_block_scalar(c_ref, a, q0) for a in (0, 1)]
        acc_ref[...] = jnp.zeros_like(acc_ref)
        keep = _rows_iota((bq, bq)) <= lax.broadcasted_iota(jnp.int32, (bq, bq), 1)

        def step(j, carry, masked):
            k0 = pl.multiple_of(j * bq, bq)
            kb = k_ref[pl.ds(k0, bq), :]
            new = []
            for a in (0, 1):
                m_old, l_old = carry[2 * a], carry[2 * a + 1]
                st = lax.dot_general(kb[:, a * LANES:(a + 1) * LANES], qs[a], _DOT_DIMS["nt"],
                                     preferred_element_type=F32)
                if masked:
                    st = jnp.where(keep, st, NEG_BIG)
                sigma = cq[a] - _block_scalar(c_ref, a, k0)
                m_new = jnp.maximum(m_old, jnp.max(st, axis=0, keepdims=True) + sigma)
                pt = jnp.exp(st - (m_new - sigma))
                alpha = jnp.exp(m_old - m_new)
                l_new = alpha * l_old + jnp.sum(pt, axis=0, keepdims=True)
                vta = vt_ref[a * HEAD_DIM:(a + 1) * HEAD_DIM, pl.ds(k0, bq)]
                acc_ref[a] = alpha * acc_ref[a] + lax.dot_general(
                    vta, pt.astype(BF16), _DOT_DIMS["nn"], preferred_element_type=F32)
                new += [m_new, l_new]
            return tuple(new)

        neg = jnp.full((1, bq), NEG_BIG, F32)
        zero = jnp.zeros((1, bq), F32)
        carry = lax.fori_loop(0, i, lambda j, c: step(j, c, False), (neg, zero, neg, zero))
        m_a, l_a, m_b, l_b = step(i, carry, True)
        ot = jnp.concatenate([acc_ref[0] / l_a, acc_ref[1] / l_b], axis=0)
        ot_ref[...] = ot
        otb_ref[...] = ot.astype(BF16)
        lse_ref[...] = jnp.zeros_like(lse_ref)
        lse_ref[0:1, :] = m_a + jnp.log(l_a)
        lse_ref[1:2, :] = m_b + jnp.log(l_b)

    rows = pl.BlockSpec((None, SUBLANES, bq), lambda hp, i: (hp, 0, i))
    otile = pl.BlockSpec((LANES, bq), lambda hp, i: (hp, i))
    return pl.pallas_call(
        body, name=name, grid=(N_PAIRS, nq),
        in_specs=[pl.BlockSpec((bq, 2 * LANES), lambda hp, i: (i, hp)),
                  pl.BlockSpec((t, 2 * LANES), lambda hp, i: (0, hp)),
                  pl.BlockSpec((LANES, t), lambda hp, i: (hp, 0)),
                  pl.BlockSpec((None, SUBLANES, t), lambda hp, i: (hp, 0, 0))],
        out_specs=[otile, otile, rows],
        out_shape=[jax.ShapeDtypeStruct((D_MODEL, t), F32), jax.ShapeDtypeStruct((D_MODEL, t), BF16),
                   jax.ShapeDtypeStruct((N_PAIRS, SUBLANES, t), F32)],
        scratch_shapes=[pltpu.VMEM((2, HEAD_DIM, bq), F32)],
        compiler_params=_cparams("parallel", "arbitrary"),
    )(qh, kh, vt, c_pairs)


def _attn_delta_t(dot, ot, name):
    t = dot.shape[1]
    bt = _pick(t, 512)

    def body(do_ref, o_ref, d_ref, dob_ref):
        dob = do_ref[...].astype(BF16)
        prod = dob.astype(F32) * o_ref[...]
        d_ref[...] = jnp.zeros_like(d_ref)
        d_ref[0:1, :] = jnp.sum(prod[0:HEAD_DIM], axis=0, keepdims=True)
        d_ref[1:2, :] = jnp.sum(prod[HEAD_DIM:], axis=0, keepdims=True)
        dob_ref[...] = dob

    tile = pl.BlockSpec((LANES, bt), lambda hp, i: (hp, i))
    return pl.pallas_call(
        body, name=name, grid=(N_PAIRS, t // bt), in_specs=[tile, tile],
        out_specs=[pl.BlockSpec((None, SUBLANES, bt), lambda hp, i: (hp, 0, i)), tile],
        out_shape=[jax.ShapeDtypeStruct((N_PAIRS, SUBLANES, t), F32), jax.ShapeDtypeStruct((D_MODEL, t), BF16)],
        compiler_params=_cparams("parallel", "parallel"),
    )(dot, ot)


def _attn_bwd_t(kh, kt, qt, qkv, dotb, c_pairs, lse_pairs, d_pairs, name):
    t = kh.shape[0]
    bk = _pick(t, ATTN_BWD_BLOCK)
    nk = t // bk
    ref_tile = _pick(t, ATTN_FWD_BLOCK)
    assert ref_tile % bk == 0
    scale = 1.0 / math.sqrt(HEAD_DIM)

    def ref_start(start):
        return pl.multiple_of((start // ref_tile) * ref_tile, ref_tile)

    def body(k_ref, kt_ref, v_ref, qt_ref, dot_ref, c_ref, lse_ref, d_ref,
             dk_ref, dv_ref, dc_ref, dq_hbm, dka_ref, dva_ref, dqa_ref, sem):
        hp, j = pl.program_id(0), pl.program_id(1)
        k0 = pl.multiple_of(j * bk, bk)
        ks = (k_ref[:, 0:LANES], k_ref[:, LANES:2 * LANES])
        vb = v_ref[...]
        ck = [_block_scalar(c_ref, a, ref_start(k0)) for a in (0, 1)]
        dka_ref[...] = jnp.zeros_like(dka_ref)
        dva_ref[...] = jnp.zeros_like(dva_ref)

        @pl.when(j == 0)
        def _():
            dqa_ref[...] = jnp.zeros_like(dqa_ref)

        keep = _rows_iota((bk, bk)) <= lax.broadcasted_iota(jnp.int32, (bk, bk), 1)
        top = _rows_iota((LANES, bk)) < HEAD_DIM

        def step(i, masked):
            q0 = pl.multiple_of(i * bk, bk)
            dot2 = dot_ref[:, pl.ds(q0, bk)]
            zero = jnp.zeros_like(dot2)
            for a in (0, 1):
                qta = qt_ref[a * LANES:(a + 1) * LANES, pl.ds(q0, bk)]
                st = lax.dot_general(ks[a], qta, _DOT_DIMS["nn"], preferred_element_type=F32)
                sigma = _block_scalar(c_ref, a, ref_start(q0)) - ck[a]
                pt = jnp.exp(st - (lse_ref[a:a + 1, pl.ds(q0, bk)] - sigma))
                if masked:
                    pt = jnp.where(keep, pt, 0.0)
                dota = jnp.where(top, dot2, zero) if a == 0 else jnp.where(top, zero, dot2)
                dpt = lax.dot_general(vb, dota, _DOT_DIMS["nn"], preferred_element_type=F32)
                dstb = (pt * (dpt - d_ref[a:a + 1, pl.ds(q0, bk)])).astype(BF16)
                dva_ref[a] += lax.dot_general(dot2[a * HEAD_DIM:(a + 1) * HEAD_DIM, :], pt.astype(BF16),
                                              _DOT_DIMS["nt"], preferred_element_type=F32)
                dka_ref[a] += lax.dot_general(qta, dstb, _DOT_DIMS["nt"], preferred_element_type=F32)
                dqa_ref[a, :, pl.ds(q0, bk)] += lax.dot_general(
                    kt_ref[a * LANES:(a + 1) * LANES, :], dstb, _DOT_DIMS["nn"], preferred_element_type=F32)

        def loop_body(i, carry):
            step(i, False)
            return carry

        step(j, True)
        lax.fori_loop(j + 1, nk, loop_body, 0)
        dk_ref[...] = (jnp.concatenate([dka_ref[0, 0:HEAD_DIM, :], dka_ref[1, 0:HEAD_DIM, :]], axis=0)
                       * scale).astype(BF16)
        dv_ref[...] = jnp.concatenate([dva_ref[0], dva_ref[1]], axis=0).astype(BF16)
        dc_ref[...] = jnp.zeros_like(dc_ref)
        dc_ref[0:1, :] = dka_ref[0, AUG_C:AUG_C + 1, :]
        dc_ref[1:2, :] = dka_ref[1, AUG_C:AUG_C + 1, :]

        @pl.when(j == nk - 1)
        def _():
            for a in (0, 1):
                row0 = pl.multiple_of((2 * hp + a) * LANES, LANES)
                cp = pltpu.make_async_copy(dqa_ref.at[a], dq_hbm.at[pl.ds(row0, LANES), :], sem)
                cp.start()
                cp.wait()

    once = pl.Buffered(1)
    rows = pl.BlockSpec((None, SUBLANES, t), lambda hp, j: (hp, 0, 0))
    kv_out = pl.BlockSpec((LANES, bk), lambda hp, j: (hp, j))
    return pl.pallas_call(
        body, name=name, grid=(N_PAIRS, nk),
        in_specs=[pl.BlockSpec((bk, 2 * LANES), lambda hp, j: (j, hp)),
                  pl.BlockSpec((2 * LANES, bk), lambda hp, j: (hp, j)),
                  pl.BlockSpec((bk, LANES), lambda hp, j: (j, 2 * N_PAIRS + hp)),
                  pl.BlockSpec((2 * LANES, t), lambda hp, j: (hp, 0), pipeline_mode=once),
                  pl.BlockSpec((LANES, t), lambda hp, j: (hp, 0), pipeline_mode=once),
                  rows, rows, rows],
        out_specs=[kv_out, kv_out, pl.BlockSpec((None, SUBLANES, bk), lambda hp, j: (hp, 0, j)),
                   pl.BlockSpec(memory_space=pltpu.HBM)],
        out_shape=[jax.ShapeDtypeStruct((D_MODEL, t), BF16), jax.ShapeDtypeStruct((D_MODEL, t), BF16),
                   jax.ShapeDtypeStruct((N_PAIRS, SUBLANES, t), F32),
                   jax.ShapeDtypeStruct((N_HEADS * LANES, t), F32)],
        scratch_shapes=[pltpu.VMEM((2, LANES, bk), F32), pltpu.VMEM((2, HEAD_DIM, bk), F32),
                        pltpu.VMEM((2, LANES, t), F32), pltpu.SemaphoreType.DMA],
        compiler_params=_cparams("arbitrary", "arbitrary"),
    )(kh, kt, qkv, qt, dotb, c_pairs, lse_pairs, d_pairs)


def _scan(a, u, name, reverse=False):
    t, c = a.shape
    bt, cb = _pick(t, ROW_TILE), _pick(c, 1024)
    nt = t // bt
    ngroups = bt // SUBLANES

    def body(a_ref, u_ref, h_ref, carry_ref, as_ref, us_ref):
        @pl.when(pl.program_id(1) == 0)
        def _():
            carry_ref[...] = jnp.zeros_like(carry_ref)

        av, uv = a_ref[...], u_ref[...]
        sub = _rows_iota((bt, cb)) % SUBLANES
        for s in (1, 2, 4):
            if reverse:
                a_sh, u_sh = pltpu.roll(av, bt - s, axis=0), pltpu.roll(uv, bt - s, axis=0)
                valid = sub < SUBLANES - s
            else:
                a_sh, u_sh = pltpu.roll(av, s, axis=0), pltpu.roll(uv, s, axis=0)
                valid = sub >= s
            uv = jnp.where(valid, uv + av * u_sh, uv)
            av = jnp.where(valid, av * a_sh, av)
        as_ref[...] = av
        us_ref[...] = uv
        edge = 0 if reverse else SUBLANES - 1
        pick = _rows_iota((SUBLANES, cb)) == edge

        def group(gi, carry):
            g = (ngroups - 1 - gi) if reverse else gi
            r0 = pl.multiple_of(g * SUBLANES, SUBLANES)
            h8 = us_ref[pl.ds(r0, SUBLANES), :] + as_ref[pl.ds(r0, SUBLANES), :] * carry
            h_ref[pl.ds(r0, SUBLANES), :] = h8
            return jnp.sum(jnp.where(pick, h8, 0.0), axis=0, keepdims=True)

        carry_ref[...] = lax.fori_loop(0, ngroups, group, carry_ref[...])

    if reverse:
        tile = pl.BlockSpec((bt, cb), lambda j, i: (nt - 1 - i, j))
    else:
        tile = pl.BlockSpec((bt, cb), lambda j, i: (i, j))
    return pl.pallas_call(
        body, name=name, grid=(c // cb, nt), in_specs=[tile, tile], out_specs=tile,
        out_shape=jax.ShapeDtypeStruct((t, c), F32),
        scratch_shapes=[pltpu.VMEM((1, cb), F32), pltpu.VMEM((bt, cb), F32), pltpu.VMEM((bt, cb), F32)],
        compiler_params=_cparams("parallel", "arbitrary"),
    )(a, u)


def _rg_conv_fwd(proj, cw, cb_, name):
    t = proj.shape[0]
    bt, cb = _pick(t, ROW_TILE), D_MODEL

    def body(x_ref, halo_ref, w_ref, b_ref, o_ref):
        keep = (pl.program_id(0) > 0).astype(F32)
        o_ref[...] = _causal_conv(x_ref[...], halo_ref[...] * keep, w_ref, CONV_B) + b_ref[...]

    tile = pl.BlockSpec((bt, cb), lambda i, j: (i, j))
    return pl.pallas_call(
        body, name=name, grid=(t // bt, 1),
        in_specs=[tile, _prev_halo_spec(bt, cb), pl.BlockSpec((CONV_B, cb), lambda i, j: (0, 0)),
                  pl.BlockSpec((1, cb), lambda i, j: (0, 0))],
        out_specs=tile, out_shape=jax.ShapeDtypeStruct((t, D_MODEL), F32),
        compiler_params=_cparams("parallel", "parallel"),
    )(proj, proj, cw, cb_.reshape(1, D_MODEL))


def _conv_bwd_w(x, dy, ksize, name):
    t, c = dy.shape
    bt = _pick(t, ROW_TILE)
    r = bt // SUBLANES

    def body(x_ref, halo_ref, dy_ref, dw_ref, db_ref):
        @pl.when(pl.program_id(0) == 0)
        def _():
            dw_ref[...] = jnp.zeros_like(dw_ref)
            db_ref[...] = jnp.zeros_like(db_ref)

        keep = (pl.program_id(0) > 0).astype(F32)
        xv, halo, dyv = x_ref[...], halo_ref[...] * keep, dy_ref[...]
        db_ref[...] += jnp.sum(dyv, axis=0, keepdims=True)
        for j in range(ksize):
            dw_ref[j:j + 1, :] += jnp.sum(dyv * _shift_down(xv, halo, ksize - 1 - j), axis=0, keepdims=True)

    tile = pl.BlockSpec((bt, c), lambda i: (i, 0))
    return pl.pallas_call(
        body, name=name, grid=(t // bt,),
        in_specs=[tile, pl.BlockSpec((SUBLANES, c), lambda i: (jnp.maximum(i * r - 1, 0), 0)), tile],
        out_specs=[pl.BlockSpec((ksize, c), lambda i: (0, 0)), pl.BlockSpec((1, c), lambda i: (0, 0))],
        out_shape=[jax.ShapeDtypeStruct((ksize, c), F32), jax.ShapeDtypeStruct((1, c), F32)],
        compiler_params=_cparams("arbitrary"),
    )(x, x, dy)


def _rg_gate_math(xc, wa_ref, wi_ref, ba_ref, bi_ref, lam_ref):
    xb = xc.astype(BF16)
    ra = lax.dot_general(xb, wa_ref[...], _DOT_DIMS["nn"], preferred_element_type=F32) + ba_ref[...]
    ia = lax.dot_general(xb, wi_ref[...], _DOT_DIMS["nn"], preferred_element_type=F32) + bi_ref[...]
    r, ig = _sigmoid(ra), _sigmoid(ia)
    sp = _softplus(-lam_ref[...])
    log_a = -LRU_C * r * sp
    a = jnp.exp(log_a)
    mult = jnp.sqrt(-_expm1(2.0 * log_a))
    return xb, r, ig, sp, a, mult


def _rg_gate_specs(bt, time_first):
    if time_first:
        tile = pl.BlockSpec((bt, BLOCK_B), lambda i, n: (i, n))
        w = pl.BlockSpec((None, BLOCK_B, BLOCK_B), lambda i, n: (n, 0, 0))
        v = pl.BlockSpec((None, 1, BLOCK_B), lambda i, n: (n, 0, 0))
    else:
        tile = pl.BlockSpec((bt, BLOCK_B), lambda n, i: (i, n))
        w = pl.BlockSpec((None, BLOCK_B, BLOCK_B), lambda n, i: (n, 0, 0))
        v = pl.BlockSpec((None, 1, BLOCK_B), lambda n, i: (n, 0, 0))
    return tile, w, v


def _rg_gate_fwd(xc, wa, ba, wi, bi, lam, name):
    t = xc.shape[0]
    bt = _pick(t, 512)

    def body(x_ref, wa_ref, wi_ref, ba_ref, bi_ref, lam_ref, a_ref, u_ref):
        xcv = x_ref[...]
        _, _, ig, _, a, mult = _rg_gate_math(xcv, wa_ref, wi_ref, ba_ref, bi_ref, lam_ref)
        a_ref[...] = a
        u_ref[...] = mult * (ig * xcv)

    tile, w, v = _rg_gate_specs(bt, True)
    return pl.pallas_call(
        body, name=name, grid=(t // bt, N_BLOCKS_B), in_specs=[tile, w, w, v, v, v], out_specs=[tile, tile],
        out_shape=[jax.ShapeDtypeStruct((t, D_MODEL), F32), jax.ShapeDtypeStruct((t, D_MODEL), F32)],
        compiler_params=_cparams("parallel", "parallel"),
    )(xc, wa, wi, ba, bi, lam)


def _rg_gate_bwd(xc, g, h, wa, ba, wi, bi, lam, name):
    t = xc.shape[0]
    bt = _pick(t, 512)
    rr = bt // SUBLANES

    def body(x_ref, g_ref, h_ref, hh_ref, wa_ref, wi_ref, ba_ref, bi_ref, lam_ref,
             dx_ref, dwa_ref, dwi_ref, dba_ref, dbi_ref, dlam_ref):
        @pl.when(pl.program_id(1) == 0)
        def _():
            for ref in (dwa_ref, dwi_ref, dba_ref, dbi_ref, dlam_ref):
                ref[...] = jnp.zeros_like(ref)

        keep = (pl.program_id(1) > 0).astype(F32)
        xcv, gv = x_ref[...], g_ref[...]
        xb, r, ig, sp, a, mult = _rg_gate_math(xcv, wa_ref, wi_ref, ba_ref, bi_ref, lam_ref)
        h_prev = _shift_down(h_ref[...], hh_ref[...] * keep, 1)
        da = gv * h_prev
        dmult = gv * ig * xcv
        dig = gv * mult * xcv
        dxc = gv * mult * ig
        dlog_a = da * a - dmult * (a * a / mult)
        dr = dlog_a * (-LRU_C * sp)
        dsp = jnp.sum(dlog_a * (-LRU_C * r), axis=0, keepdims=True)
        dlam_ref[...] += dsp * (-_sigmoid(-lam_ref[...]))
        dra = dr * r * (1.0 - r)
        dia = dig * ig * (1.0 - ig)
        dba_ref[...] += jnp.sum(dra, axis=0, keepdims=True)
        dbi_ref[...] += jnp.sum(dia, axis=0, keepdims=True)
        drab, diab = dra.astype(BF16), dia.astype(BF16)
        dwa_ref[...] += lax.dot_general(xb, drab, _DOT_DIMS["tn"], preferred_element_type=F32)
        dwi_ref[...] += lax.dot_general(xb, diab, _DOT_DIMS["tn"], preferred_element_type=F32)
        dxc = dxc + lax.dot_general(drab, wa_ref[...], _DOT_DIMS["nt"], preferred_element_type=F32)
        dxc = dxc + lax.dot_general(diab, wi_ref[...], _DOT_DIMS["nt"], preferred_element_type=F32)
        dx_ref[...] = dxc

    tile, w, v = _rg_gate_specs(bt, False)
    halo = pl.BlockSpec((SUBLANES, BLOCK_B), lambda n, i: (jnp.maximum(i * rr - 1, 0), n))
    wshape = jax.ShapeDtypeStruct((N_BLOCKS_B, BLOCK_B, BLOCK_B), F32)
    vshape = jax.ShapeDtypeStruct((N_BLOCKS_B, 1, BLOCK_B), F32)
    return pl.pallas_call(
        body, name=name, grid=(N_BLOCKS_B, t // bt),
        in_specs=[tile, tile, tile, halo, w, w, v, v, v],
        out_specs=[tile, w, w, v, v, v],
        out_shape=[jax.ShapeDtypeStruct((t, D_MODEL), F32), wshape, wshape, vshape, vshape, vshape],
        compiler_params=_cparams("parallel", "arbitrary"),
    )(xc, g, h, h, wa, wi, ba, bi, lam)


def _rg_out_fwd(h, proj, name):
    t = h.shape[0]
    bt = _pick(t, ROW_TILE)

    def body(h_ref, g_ref, y_ref):
        y_ref[...] = (h_ref[...] * _gelu(g_ref[...])).astype(BF16)

    tile = pl.BlockSpec((bt, D_MODEL), lambda i: (i, 0))
    return pl.pallas_call(
        body, name=name, grid=(t // bt,), in_specs=[tile, pl.BlockSpec((bt, D_MODEL), lambda i: (i, 1))],
        out_specs=tile, out_shape=jax.ShapeDtypeStruct((t, D_MODEL), BF16),
        compiler_params=_cparams("parallel"),
    )(h, proj)


def _rg_out_bwd(dy, h, proj, name):
    t = h.shape[0]
    bt = _pick(t, ROW_TILE)

    def body(dy_ref, h_ref, g_ref, dh_ref, dg_ref):
        gl, glg = _gelu_and_grad(g_ref[...])
        dyv = dy_ref[...]
        dh_ref[...] = dyv * gl
        dg_ref[...] = (dyv * h_ref[...] * glg).astype(BF16)

    tile = pl.BlockSpec((bt, D_MODEL), lambda i: (i, 0))
    return pl.pallas_call(
        body, name=name, grid=(t // bt,),
        in_specs=[tile, tile, pl.BlockSpec((bt, D_MODEL), lambda i: (i, 1))], out_specs=[tile, tile],
        out_shape=[jax.ShapeDtypeStruct((t, D_MODEL), F32), jax.ShapeDtypeStruct((t, D_MODEL), BF16)],
        compiler_params=_cparams("parallel"),
    )(dy, h, proj)


def _shift_up_one(a, name):
    t, c = a.shape
    bt = _pick(t, ROW_TILE)

    def body(a_ref, halo_ref, o_ref):
        o_ref[...] = _shift_up(a_ref[...], halo_ref[...], 1)

    tile = pl.BlockSpec((bt, c), lambda i, j: (i, j))
    return pl.pallas_call(
        body, name=name, grid=(t // bt, 1), in_specs=[tile, _next_halo_spec(bt, c, t)], out_specs=tile,
        out_shape=jax.ShapeDtypeStruct((t, c), F32), compiler_params=_cparams("parallel", "parallel"),
    )(a, a)


ADAM_ROWS = 64


def _adamw(recv, w, m, v, name):
    _, r, c = recv.shape
    br = ADAM_ROWS
    assert r % br == 0

    def body(r_ref, w_ref, m_ref, v_ref, g_ref, d_ref, nm_ref, nv_ref):
        g = r_ref[0].astype(F32)
        for s in range(1, N_DEV):
            g = g + r_ref[s].astype(F32)
        m_new = ADAM_B1 * m_ref[...] + (1.0 - ADAM_B1) * g
        v_new = ADAM_B2 * v_ref[...] + (1.0 - ADAM_B2) * (g * g)
        m_hat = m_new / (1.0 - ADAM_B1 ** ADAM_STEP)
        v_hat = v_new / (1.0 - ADAM_B2 ** ADAM_STEP)
        g_ref[...] = g
        d_ref[...] = -ADAM_LR * (m_hat / (jnp.sqrt(v_hat) + ADAM_EPS) + ADAM_WD * w_ref[...])
        nm_ref[...] = m_new
        nv_ref[...] = v_new

    tile = pl.BlockSpec((br, c), lambda i: (i, 0))
    shape = jax.ShapeDtypeStruct((r, c), F32)
    return pl.pallas_call(
        body, name=name, grid=(r // br,),
        in_specs=[pl.BlockSpec((N_DEV, br, c), lambda i: (0, i, 0)), tile, tile, tile],
        out_specs=[tile] * 4, out_shape=[shape] * 4, compiler_params=_cparams("parallel"),
    )(recv, w, m, v)


def _all_to_all(src, name):
    def body(src_ref, out_ref, send_sems, recv_sems, local_sem):
        pos = [lax.axis_index(ax) for ax in MESH_AXES]
        me = 4 * pos[0] + 2 * pos[1] + pos[2]

        def peer_of(k):
            p = [(1 - pos[b]) if (k >> (2 - b)) & 1 else pos[b] for b in range(3)]
            return tuple(p), 4 * p[0] + 2 * p[1] + p[2]

        def copy(k):
            peer, peer_idx = peer_of(k)
            return pltpu.make_async_remote_copy(
                src_ref=src_ref.at[peer_idx], dst_ref=out_ref.at[me],
                send_sem=send_sems.at[k - 1], recv_sem=recv_sems.at[k - 1],
                device_id=peer, device_id_type=pl.DeviceIdType.MESH)

        def arrival(k):
            peer, peer_idx = peer_of(k)
            return pltpu.make_async_remote_copy(
                src_ref=src_ref.at[me], dst_ref=out_ref.at[peer_idx],
                send_sem=send_sems.at[k - 1], recv_sem=recv_sems.at[k - 1],
                device_id=peer, device_id_type=pl.DeviceIdType.MESH)

        mine = pltpu.make_async_copy(src_ref.at[me], out_ref.at[me], local_sem)
        mine.start()
        sends = [copy(k) for k in range(1, N_DEV)]
        for cp in sends:
            cp.start()
        for k in range(1, N_DEV):
            arrival(k).wait_recv()
        for cp in sends:
            cp.wait_send()
        mine.wait()

    hbm = pl.BlockSpec(memory_space=pltpu.HBM)
    return pl.pallas_call(
        body, name=name, in_specs=[hbm], out_specs=hbm,
        out_shape=jax.ShapeDtypeStruct(src.shape, src.dtype),
        scratch_shapes=[pltpu.SemaphoreType.DMA((N_DEV - 1,)), pltpu.SemaphoreType.DMA((N_DEV - 1,)),
                        pltpu.SemaphoreType.DMA],
        compiler_params=pltpu.CompilerParams(has_side_effects=True),
    )(src)


def _all_gather(src, name):
    def body(src_ref, out_ref, send_sems, recv_sems, local_sem):
        x, y, c = (lax.axis_index(ax) for ax in MESH_AXES)
        me, sibling = (x, y, c), (x, y, 1 - c)
        chips = [(1 - x, y), (x, 1 - y), (1 - x, 1 - y)]

        def slot(px, py, pc):
            return out_ref.at[4 * px + 2 * py + pc]

        def copy(k, block, to, from_src=False):
            return pltpu.make_async_remote_copy(
                src_ref=src_ref if from_src else slot(*block), dst_ref=slot(*block),
                send_sem=send_sems.at[k], recv_sem=recv_sems.at[k],
                device_id=to, device_id_type=pl.DeviceIdType.MESH)

        mine = pltpu.make_async_copy(src_ref, slot(*me), local_sem)
        mine.start()
        first = [copy(0, me, sibling, True)] + [copy(1 + j, me, (*chip, c), True) for j, chip in enumerate(chips)]
        for cp in first:
            cp.start()
        passed = [copy(4 + j, (*chip, c), sibling) for j, chip in enumerate(chips)]
        for j, chip in enumerate(chips):
            copy(1 + j, (*chip, c), me).wait_recv()
            passed[j].start()
        copy(0, sibling, me).wait_recv()
        for j, chip in enumerate(chips):
            copy(4 + j, (*chip, 1 - c), me).wait_recv()
        for cp in first + passed:
            cp.wait_send()
        mine.wait()

    hbm = pl.BlockSpec(memory_space=pltpu.HBM)
    return pl.pallas_call(
        body, name=name, in_specs=[hbm], out_specs=hbm,
        out_shape=jax.ShapeDtypeStruct((N_DEV,) + tuple(src.shape), src.dtype),
        scratch_shapes=[pltpu.SemaphoreType.DMA((N_DEV - 1,)), pltpu.SemaphoreType.DMA((N_DEV - 1,)),
                        pltpu.SemaphoreType.DMA],
        compiler_params=pltpu.CompilerParams(has_side_effects=True),
    )(src)


WEIGHTS = ['a_w_in', 'a_b_f', 'a_w_out', 'b_w_in', 'b_conv_w', 'b_conv_b', 'b_w_a', 'b_b_a', 'b_w_i', 'b_b_i',
           'b_lam', 'b_w_out', 'f_w_up', 'f_conv_w', 'f_conv_b', 'f_w_down', 'ln1_g', 'ln1_b', 'ln2_g', 'ln2_b',
           'ple_w', 'ple_gate_w', 'ple_gate_b']
SHARD_AXIS = {'a_w_in': 2, 'a_b_f': None, 'a_w_out': 1, 'b_w_in': 2, 'b_conv_w': 2, 'b_conv_b': 1, 'b_w_a': None,
              'b_b_a': None, 'b_w_i': None, 'b_b_i': None, 'b_lam': 1, 'b_w_out': 1, 'f_w_up': 2, 'f_conv_w': 2,
              'f_conv_b': None, 'f_w_down': 1, 'ln1_g': None, 'ln1_b': None, 'ln2_g': None, 'ln2_b': None,
              'ple_w': 2, 'ple_gate_w': 1, 'ple_gate_b': None}
MATMUL_WEIGHTS = ['a_w_in', 'a_w_out', 'b_w_in', 'b_w_out', 'f_w_up', 'f_w_down', 'ple_w', 'ple_gate_w']
SMALL_SHARDED = ['b_conv_w', 'b_conv_b', 'b_lam', 'f_conv_w']
PACK_COLS = 1024


def _to_shards(full, axis):
    return jnp.stack(jnp.split(full, N_DEV, axis=axis))


def _from_shards(pieces, axis):
    return jnp.concatenate([pieces[d] for d in range(N_DEV)], axis=axis)


PIECE_ROWS = 16


def _piece_rows(size):
    rows = -(-size // PACK_COLS)
    return -(-rows // PIECE_ROWS) * PIECE_ROWS


def _pack_pieces(pieces, total_mult=PIECE_ROWS):
    lead = pieces[0].shape[:-1]
    blocks = []
    for pc in pieces:
        n = pc.shape[-1]
        rows = _piece_rows(n)
        pad = [(0, 0)] * len(lead) + [(0, rows * PACK_COLS - n)]
        blocks.append(jnp.pad(pc, pad).reshape(lead + (rows, PACK_COLS)))
    total = sum(b.shape[-2] for b in blocks)
    extra = -total % total_mult
    if extra:
        blocks.append(jnp.zeros(lead + (extra, PACK_COLS), pieces[0].dtype))
    return jnp.concatenate(blocks, axis=len(lead))


def _unpack_pieces(packed, shapes):
    lead = packed.shape[:-2]
    out, row = [], 0
    for shp in shapes:
        size = math.prod(shp)
        rows = _piece_rows(size)
        block = lax.slice_in_dim(packed, row, row + rows, axis=len(lead))
        flat = block.reshape(lead + (rows * PACK_COLS,))
        out.append(lax.slice_in_dim(flat, 0, size, axis=len(lead)).reshape(lead + tuple(shp)))
        row += rows
    return out


def _gather_weights(local, names, dtype, name):
    packed = _pack_pieces([local[n].astype(dtype).reshape(-1) for n in names])
    gathered = _all_gather(packed, name)
    pieces = _unpack_pieces(gathered, [local[n].shape for n in names])
    return {n: _from_shards(pc, SHARD_AXIS[n]) for n, pc in zip(names, pieces)}


def _mixer_a_fwd(tag, xb, w):
    qkv = _mm(xb, w["wqkv"], "nn", BF16, f"{tag}_qkv")
    fg = _mm(xb, w["wf"], "nn", F32, f"{tag}_fgproj")
    fg_rows = fg[:, :N_HEADS].T
    c_rows = _fgate_fwd(fg_rows, w["b_f"], f"{tag}_fgate")
    c_pairs = _rows_to_pairs(c_rows)
    qh, kh = _attn_prep(qkv, _rows_to_cols(c_rows), f"{tag}_attn_prep")
    ot, otb, lse_pairs = _attn_fwd_t(qh, kh, qkv[:, 2 * D_MODEL:].T, c_pairs, f"{tag}_attn")
    m = _mm(otb, w["wout"], "tn", F32, f"{tag}_oproj")
    return m, dict(qkv=qkv, qh=qh, kh=kh, fg_rows=fg_rows, c_pairs=c_pairs, ot=ot, otb=otb, lse_pairs=lse_pairs)


def _mixer_a_bwd(tag, dz, dzb, xb, w, s):
    t = xb.shape[0]
    dot = _mm(w["wout"], dzb, "nt", F32, f"{tag}_b_do")
    g_wout = _mm(s["otb"], dzb, "nn", F32, f"{tag}_b_dwout")
    d_pairs, dotb = _attn_delta_t(dot, s["ot"], f"{tag}_b_delta")
    dkt, dvt, dck_pairs, dq_aug = _attn_bwd_t(s["kh"], s["kh"].T, s["qh"].T, s["qkv"], dotb, s["c_pairs"],
                                              s["lse_pairs"], d_pairs, f"{tag}_b_attn")
    dq_aug = dq_aug.reshape(N_HEADS, LANES, t)
    dfg_rows, db_f = _fgate_bwd(dck_pairs[:, :2, :].reshape(N_HEADS, t), dq_aug[:, AUG_ONE, :], s["fg_rows"],
                                w["b_f"], f"{tag}_b_fgate")
    dqt = dq_aug[:, :HEAD_DIM, :].reshape(D_MODEL, t).astype(BF16)
    dqkv = jnp.concatenate([dqt, dkt, dvt], axis=0).T
    dfg = jnp.pad(dfg_rows.T, ((0, 0), (0, LANES - N_HEADS))).astype(BF16)
    dx = _mm(dqkv, w["wqkv"], "nt", F32, f"{tag}_b_dx_qkv", add=dz, add_scale=ALPHA, tk=3 * D_MODEL)
    dx = _mm(dfg, w["wf"], "nt", F32, f"{tag}_b_dx_fg", add=dx)
    g_wqkv = _mm(xb, dqkv, "tn", F32, f"{tag}_b_dwqkv")
    g_wf = _mm(xb, dfg, "tn", F32, f"{tag}_b_dwf")[:, :N_HEADS]
    grads = dict(a_w_in=jnp.concatenate([g_wqkv, g_wf], axis=1), a_b_f=db_f.reshape(N_HEADS), a_w_out=g_wout)
    return dx, grads


def _mixer_b_fwd(tag, xb, w):
    proj = _mm(xb, w["win"], "nn", F32, f"{tag}_proj")
    xc = _rg_conv_fwd(proj, w["conv_w"], w["conv_b"], f"{tag}_conv")
    a, u = _rg_gate_fwd(xc, w["wa"], w["ba"], w["wi"], w["bi"], w["lam"], f"{tag}_gate")
    h = _scan(a, u, f"{tag}_scan")
    y = _rg_out_fwd(h, proj, f"{tag}_out")
    m = _mm(y, w["wout"], "nn", F32, f"{tag}_oproj")
    return m, dict(proj=proj, xc=xc, a=a, h=h, y=y)


def _mixer_b_bwd(tag, dz, dzb, xb, w, s):
    dy = _mm(dzb, w["wout"], "nt", F32, f"{tag}_b_dy")
    g_wout = _mm(s["y"], dzb, "tn", F32, f"{tag}_b_dwout")
    dh, dgate = _rg_out_bwd(dy, s["h"], s["proj"], f"{tag}_b_out")
    g = _scan(_shift_up_one(s["a"], f"{tag}_b_shift"), dh, f"{tag}_b_scan", reverse=True)
    dxc, g_wa, g_wi, g_ba, g_bi, g_lam = _rg_gate_bwd(
        s["xc"], g, s["h"], w["wa"], w["ba"], w["wi"], w["bi"], w["lam"], f"{tag}_b_gate")
    dxp = _conv_bwd_x(dxc, w["conv_w"], f"{tag}_b_convx", BF16)
    g_cw, g_cb = _conv_bwd_w(s["proj"], dxc, CONV_B, f"{tag}_b_convw")
    dproj = jnp.concatenate([dxp, dgate], axis=1)
    dx = _mm(dproj, w["win"], "nt", F32, f"{tag}_b_dx", add=dz, add_scale=ALPHA, tk=2 * D_MODEL)
    g_win = _mm(xb, dproj, "tn", F32, f"{tag}_b_dwin")
    grads = dict(b_w_in=g_win, b_conv_w=g_cw, b_conv_b=g_cb.reshape(D_MODEL), b_w_a=g_wa,
                 b_b_a=g_ba.reshape(N_BLOCKS_B, BLOCK_B), b_w_i=g_wi, b_b_i=g_bi.reshape(N_BLOCKS_B, BLOCK_B),
                 b_lam=g_lam.reshape(D_MODEL), b_w_out=g_wout)
    return dx, grads


def _layer_fwd(i, x, xb, pb, w):
    tag = f"L{i}"
    mix = _mixer_a_fwd if i % 2 == 0 else _mixer_b_fwd
    m, sm = mix(tag, xb, w)
    x1, x1b, z1 = _ln_fwd(x, m, w["ln1_g"], w["ln1_b"], f"{tag}_ln1")
    h = _mm(x1b, w["wup"], "nn", F32, f"{tag}_ffn_up")
    y = _ffn_mid_fwd(h, w["fconv_w"], w["fconv_b"], f"{tag}_ffn_mid")
    ff = _mm(y, w["wdown"], "nn", F32, f"{tag}_ffn_down", tk=D_FF)
    x2, x2b, z2 = _ln_fwd(x1, ff, w["ln2_g"], w["ln2_b"], f"{tag}_ln2")
    gl = _mm(x2b, w["wg"], "nn", F32, f"{tag}_ple_gate")
    e = _mm(pb, w["wp"], "nn", F32, f"{tag}_ple_emb")
    x3, x3b = _ple_fwd(x2, gl, e, w["bg"], f"{tag}_ple")
    saved = dict(mixer=sm, xb=xb, x1b=x1b, z1=z1, h=h, y=y, x2b=x2b, z2=z2, gl=gl, e=e, pb=pb)
    return x3, x3b, saved


def _layer_bwd(i, dx3, w, s):
    tag = f"L{i}"
    dgl, de, g_bg = _ple_bwd(dx3, s["gl"], s["e"], w["bg"], f"{tag}_b_ple")
    g_wg = _mm(s["x2b"], dgl, "tn", F32, f"{tag}_b_dwg")
    g_wp = _mm(s["pb"], de, "tn", F32, f"{tag}_b_dwp")
    dx2 = _mm(dgl, w["wg"], "nt", F32, f"{tag}_b_dx2", add=dx3)
    dz2, dz2b, g_ln2g, g_ln2b = _ln_bwd(dx2, s["z2"], w["ln2_g"], f"{tag}_b_ln2")
    dy = _mm(dz2b, w["wdown"], "nt", F32, f"{tag}_b_dy")
    g_wdown = _mm(s["y"], dz2b, "tn", F32, f"{tag}_b_dwdown", tm=1408)
    dhv, dhg, g_fcw, g_fcb = _ffn_mid_bwd(s["h"], dy, w["fconv_w"], w["fconv_b"], f"{tag}_b_ffn_mid")
    dx1 = _mm(dhv, w["wup"][:, :D_FF], "nt", F32, f"{tag}_b_dx1_v", add=dz2, add_scale=ALPHA, tk=D_FF)
    dx1 = _mm(dhg, w["wup"][:, D_FF:], "nt", F32, f"{tag}_b_dx1_g", add=dx1, tk=D_FF)
    g_wup = jnp.concatenate([_mm(dhv, s["x1b"], "tn", F32, f"{tag}_b_dwup_v", tm=1408),
                             _mm(dhg, s["x1b"], "tn", F32, f"{tag}_b_dwup_g", tm=1408)], axis=0).T
    dz1, dz1b, g_ln1g, g_ln1b = _ln_bwd(dx1, s["z1"], w["ln1_g"], f"{tag}_b_ln1")
    mix_bwd = _mixer_a_bwd if i % 2 == 0 else _mixer_b_bwd
    dx, g_mix = mix_bwd(tag, dz1, dz1b, s["xb"], w, s["mixer"])
    grads = dict(f_w_up=g_wup, f_conv_w=g_fcw, f_conv_b=g_fcb.reshape(2 * D_FF), f_w_down=g_wdown,
                 ln1_g=g_ln1g.reshape(D_MODEL), ln1_b=g_ln1b.reshape(D_MODEL), ln2_g=g_ln2g.reshape(D_MODEL),
                 ln2_b=g_ln2b.reshape(D_MODEL), ple_w=g_wp, ple_gate_w=g_wg, ple_gate_b=g_bg.reshape(D_MODEL))
    return dx, g_mix, grads


def _layer_weights(i, full, rep):
    j = i // 2
    w = dict(ln1_g=rep["ln1_g"][i], ln1_b=rep["ln1_b"][i], ln2_g=rep["ln2_g"][i], ln2_b=rep["ln2_b"][i],
             wup=full["f_w_up"][i], fconv_w=full["f_conv_w"][i], fconv_b=rep["f_conv_b"][i],
             wdown=full["f_w_down"][i], wp=full["ple_w"][i], wg=full["ple_gate_w"][i], bg=rep["ple_gate_b"][i])
    if i % 2 == 0:
        w_in = full["a_w_in"][j]
        w.update(wqkv=w_in[:, :3 * D_MODEL],
                 wf=jnp.pad(w_in[:, 3 * D_MODEL:], ((0, 0), (0, LANES - N_HEADS))),
                 b_f=rep["a_b_f"][j], wout=full["a_w_out"][j])
    else:
        w.update(win=full["b_w_in"][j], conv_w=full["b_conv_w"][j], conv_b=full["b_conv_b"][j],
                 wa=rep["b_w_a"][j].astype(BF16), wi=rep["b_w_i"][j].astype(BF16),
                 ba=rep["b_b_a"][j].reshape(N_BLOCKS_B, 1, BLOCK_B), bi=rep["b_b_i"][j].reshape(N_BLOCKS_B, 1, BLOCK_B),
                 lam=full["b_lam"][j].reshape(N_BLOCKS_B, 1, BLOCK_B), wout=full["b_w_out"][j])
    return w


def _fwd_bwd(x, p, target, full, rep):
    weights = [_layer_weights(i, full, rep) for i in range(DEPTH)]
    xb = x.astype(BF16)
    pb = p.astype(BF16)
    saved = []
    for i in range(DEPTH):
        x, xb, s = _layer_fwd(i, x, xb, pb[i], weights[i])
        saved.append(s)
    dx, loss_local = _loss_head(x, target, "loss_head")

    per_layer = {n: [None] * (DEPTH if n.startswith(("f_", "ln", "ple")) else DEPTH // 2) for n in WEIGHTS}
    for i in reversed(range(DEPTH)):
        dx, g_mix, g_layer = _layer_bwd(i, dx, weights[i], saved[i])
        for n, g in g_layer.items():
            per_layer[n][i] = g
        for n, g in g_mix.items():
            per_layer[n][i // 2] = g
    return loss_local, dx, {n: jnp.stack(per_layer[n]) for n in WEIGHTS}


def _train_step(x, p, target, local, moments_m, moments_v):
    full = _gather_weights(local, MATMUL_WEIGHTS, BF16, "gather_matmul_weights")
    full.update(_gather_weights(local, SMALL_SHARDED, F32, "gather_small_weights"))
    rep = {n: local[n] for n in WEIGHTS if SHARD_AXIS[n] is None}
    loss_local, dx, grads_full = _fwd_bwd(x, p, target, full, rep)

    unpacked = [{} for _ in range(4)]
    for group, dtype, tag in ((MATMUL_WEIGHTS, BF16, "big"), ([n for n in WEIGHTS if n not in MATMUL_WEIGHTS], F32, "small")):
        pieces = []
        for n in group:
            g = grads_full[n].astype(dtype)
            if SHARD_AXIS[n] is None:
                pieces.append(jnp.broadcast_to(g.reshape(1, -1), (N_DEV, g.size)))
            else:
                pieces.append(_to_shards(g, SHARD_AXIS[n]).reshape(N_DEV, -1))
        recv = _all_to_all(_pack_pieces(pieces, ADAM_ROWS), f"reduce_scatter_grads_{tag}")

        def pack_local(d):
            return _pack_pieces([d[n].astype(F32).reshape(-1) for n in group], ADAM_ROWS)

        outs = _adamw(recv, pack_local(local), pack_local(moments_m), pack_local(moments_v), f"adamw_{tag}")
        shapes = [local[n].shape for n in group]
        for dst, packed in zip(unpacked, outs):
            dst.update(zip(group, _unpack_pieces(packed, shapes)))
    return loss_local, dx, unpacked


def kernel(x, p, a_w_in, a_b_f, a_w_out, b_w_in, b_conv_w, b_conv_b, b_w_a, b_b_a, b_w_i, b_b_i, b_lam, b_w_out, f_w_up, f_conv_w, f_conv_b, f_w_down, ln1_g, ln1_b, ln2_g, ln2_b, ple_w, ple_gate_w, ple_gate_b, loss_target, m_a_w_in, m_a_b_f, m_a_w_out, m_b_w_in, m_b_conv_w, m_b_conv_b, m_b_w_a, m_b_b_a, m_b_w_i, m_b_b_i, m_b_lam, m_b_w_out, m_f_w_up, m_f_conv_w, m_f_conv_b, m_f_w_down, m_ln1_g, m_ln1_b, m_ln2_g, m_ln2_b, m_ple_w, m_ple_gate_w, m_ple_gate_b, v_a_w_in, v_a_b_f, v_a_w_out, v_b_w_in, v_b_conv_w, v_b_conv_b, v_b_w_a, v_b_b_a, v_b_w_i, v_b_b_i, v_b_lam, v_b_w_out, v_f_w_up, v_f_conv_w, v_f_conv_b, v_f_w_down, v_ln1_g, v_ln1_b, v_ln2_g, v_ln2_b, v_ple_w, v_ple_gate_w, v_ple_gate_b):
    given = dict(locals())
    local = {n: given[n] for n in WEIGHTS}
    mom_m = {n: given["m_" + n] for n in WEIGHTS}
    mom_v = {n: given["v_" + n] for n in WEIGHTS}
    t = x.shape[1]
    loss_local, dx, (grad, delta, new_m, new_v) = _train_step(
        x.reshape(t, D_MODEL), p.reshape(DEPTH, t, D_PLE), loss_target.reshape(t, D_MODEL), local, mom_m, mom_v)
    loss = lax.psum(loss_local, MESH_AXES)
    return (loss, dx.reshape(1, t, D_MODEL), *[grad[n] for n in WEIGHTS], *[delta[n] for n in WEIGHTS],
            *[new_m[n] for n in WEIGHTS], *[new_v[n] for n in WEIGHTS])
```

```python
import math

import jax
import jax.numpy as jnp
from jax import lax
from jax.experimental import pallas as pl
from jax.experimental.pallas import tpu as pltpu

F32 = jnp.float32
BF16 = jnp.bfloat16

D_MODEL = 1024
DEPTH = 4
N_HEADS = 16
HEAD_DIM = 64
N_PAIRS = N_HEADS // 2
N_BLOCKS_B = 8
BLOCK_B = 128
CONV_B = 4
LRU_C = 8.0
D_FF = 2816
CONV_F = 3
D_PLE = 256
LN_EPS = 1e-5
ALPHA = (2.0 * DEPTH) ** 0.25
ADAM_LR, ADAM_B1, ADAM_B2, ADAM_EPS, ADAM_WD, ADAM_STEP = 0.001, 0.9, 0.999, 1e-08, 0.01, 10
N_DEV = 8
MESH_AXES = ("x", "y", "c")

LANES = 128
SUBLANES = 8
VMEM_LIMIT_BYTES = 56 * 1024 * 1024
ATTN_FWD_BLOCK = 1024
ATTN_BWD_BLOCK = 512
ROW_TILE = 256
NEG_BIG = -1e30


def _cparams(*sem):
    return pltpu.CompilerParams(dimension_semantics=sem if sem else None, vmem_limit_bytes=VMEM_LIMIT_BYTES)


def _pick(n, pref):
    if n <= pref:
        return n
    best = None
    for t in range(LANES, pref + 1, LANES):
        if n % t == 0:
            best = t
    assert best is not None, (n, pref)
    return best


def _sigmoid(x):
    return 1.0 / (1.0 + jnp.exp(-x))


def _log1p(x):
    u = 1.0 + x
    d = u - 1.0
    return jnp.where(d == 0.0, x, jnp.log(u) * (x / jnp.where(d == 0.0, 1.0, d)))


def _expm1(x):
    u = jnp.exp(x)
    lu = jnp.log(u)
    return jnp.where(u == 1.0, x, (u - 1.0) * (x / jnp.where(u == 1.0, 1.0, lu)))


def _softplus(x):
    return jnp.maximum(x, 0.0) + _log1p(jnp.exp(-jnp.abs(x)))


_GELU_C = math.sqrt(2.0 / math.pi)


def _gelu(x):
    return 0.5 * x * (1.0 + jnp.tanh(_GELU_C * (x + 0.044715 * x * x * x)))


def _gelu_and_grad(x):
    t = jnp.tanh(_GELU_C * (x + 0.044715 * x * x * x))
    du = _GELU_C * (1.0 + 3.0 * 0.044715 * x * x)
    return 0.5 * x * (1.0 + t), 0.5 * (1.0 + t) + 0.5 * x * (1.0 - t * t) * du


_DOT_DIMS = {"nn": (((1,), (0,)), ((), ())), "nt": (((1,), (1,)), ((), ())), "tn": (((0,), (0,)), ((), ()))}


def _mm(a, b, mode, out_dtype, name, add=None, add_scale=1.0, tm=512, tn=1408, tk=1408):
    if mode == "nn":
        (m, k), (k2, n) = a.shape, b.shape
    elif mode == "nt":
        (m, k), (n, k2) = a.shape, b.shape
    else:
        (k, m), (k2, n) = a.shape, b.shape
    assert k == k2 and a.dtype == BF16 and b.dtype == BF16, (a.shape, b.shape, a.dtype, b.dtype)
    tm, tn, tk = _pick(m, tm), _pick(n, tn), _pick(k, tk)
    nk = k // tk
    dims = _DOT_DIMS[mode]

    def body(*refs):
        if add is None:
            a_ref, b_ref, o_ref, acc_ref = refs
        else:
            a_ref, b_ref, add_ref, o_ref, acc_ref = refs
        kk = pl.program_id(2)

        @pl.when(kk == 0)
        def _():
            acc_ref[...] = jnp.zeros_like(acc_ref)

        acc_ref[...] += lax.dot_general(a_ref[...], b_ref[...], dims, preferred_element_type=F32)

        @pl.when(kk == nk - 1)
        def _():
            r = acc_ref[...]
            if add is not None:
                r = r + add_scale * add_ref[...]
            o_ref[...] = r.astype(out_dtype)

    a_spec = (pl.BlockSpec((tk, tm), lambda j, i, kk: (kk, i)) if mode == "tn"
              else pl.BlockSpec((tm, tk), lambda j, i, kk: (i, kk)))
    b_spec = (pl.BlockSpec((tn, tk), lambda j, i, kk: (j, kk)) if mode == "nt"
              else pl.BlockSpec((tk, tn), lambda j, i, kk: (kk, j)))
    o_spec = pl.BlockSpec((tm, tn), lambda j, i, kk: (i, j))
    in_specs, args = [a_spec, b_spec], [a, b]
    if add is not None:
        assert add.shape == (m, n) and add.dtype == F32
        in_specs.append(o_spec)
        args.append(add)
    return pl.pallas_call(
        body, name=name, grid=(n // tn, m // tm, nk),
        in_specs=in_specs, out_specs=o_spec,
        out_shape=jax.ShapeDtypeStruct((m, n), out_dtype),
        scratch_shapes=[pltpu.VMEM((tm, tn), F32)],
        compiler_params=_cparams("parallel", "parallel", "arbitrary"),
    )(*args)


def _ln_fwd(x, m, g, b, name):
    t, d = x.shape
    bt = _pick(t, ROW_TILE)

    def body(x_ref, m_ref, g_ref, b_ref, y_ref, yb_ref, z_ref):
        z = ALPHA * x_ref[...] + m_ref[...]
        mu = jnp.mean(z, axis=-1, keepdims=True)
        zc = z - mu
        var = jnp.mean(zc * zc, axis=-1, keepdims=True)
        y = zc * lax.rsqrt(var + LN_EPS) * g_ref[...] + b_ref[...]
        y_ref[...] = y
        yb_ref[...] = y.astype(BF16)
        z_ref[...] = z

    row = pl.BlockSpec((bt, d), lambda i: (i, 0))
    vec = pl.BlockSpec((1, d), lambda i: (0, 0))
    return pl.pallas_call(
        body, name=name, grid=(t // bt,), in_specs=[row, row, vec, vec], out_specs=[row, row, row],
        out_shape=[jax.ShapeDtypeStruct((t, d), F32), jax.ShapeDtypeStruct((t, d), BF16),
                   jax.ShapeDtypeStruct((t, d), F32)],
        compiler_params=_cparams("parallel"),
    )(x, m, g.reshape(1, d), b.reshape(1, d))


def _ln_bwd(dy, z, g, name):
    t, d = dy.shape
    bt = _pick(t, ROW_TILE)

    def body(dy_ref, z_ref, g_ref, dz_ref, dzb_ref, dg_ref, db_ref):
        @pl.when(pl.program_id(0) == 0)
        def _():
            dg_ref[...] = jnp.zeros_like(dg_ref)
            db_ref[...] = jnp.zeros_like(db_ref)

        z = z_ref[...]
        dyv = dy_ref[...]
        mu = jnp.mean(z, axis=-1, keepdims=True)
        zc = z - mu
        var = jnp.mean(zc * zc, axis=-1, keepdims=True)
        rstd = lax.rsqrt(var + LN_EPS)
        xhat = zc * rstd
        dxh = dyv * g_ref[...]
        m1 = jnp.mean(dxh, axis=-1, keepdims=True)
        m2 = jnp.mean(dxh * xhat, axis=-1, keepdims=True)
        dz = rstd * (dxh - m1 - xhat * m2)
        dz_ref[...] = dz
        dzb_ref[...] = dz.astype(BF16)
        dg_ref[...] += jnp.sum(dyv * xhat, axis=0, keepdims=True)
        db_ref[...] += jnp.sum(dyv, axis=0, keepdims=True)

    row = pl.BlockSpec((bt, d), lambda i: (i, 0))
    vec = pl.BlockSpec((1, d), lambda i: (0, 0))
    return pl.pallas_call(
        body, name=name, grid=(t // bt,), in_specs=[row, row, vec], out_specs=[row, row, vec, vec],
        out_shape=[jax.ShapeDtypeStruct((t, d), F32), jax.ShapeDtypeStruct((t, d), BF16),
                   jax.ShapeDtypeStruct((1, d), F32), jax.ShapeDtypeStruct((1, d), F32)],
        compiler_params=_cparams("arbitrary"),
    )(dy, z, g.reshape(1, d))


def _rows_iota(shape):
    return lax.broadcasted_iota(jnp.int32, shape, 0)


def _shift_down(x, halo, s):
    if s == 0:
        return x
    rolled = pltpu.roll(x, s, axis=0)
    first = jnp.where(_rows_iota((SUBLANES, x.shape[1])) < s, pltpu.roll(halo, s, axis=0), rolled[:SUBLANES])
    return jnp.concatenate([first, rolled[SUBLANES:]], axis=0)


def _shift_up(x, halo, s):
    if s == 0:
        return x
    n = x.shape[0]
    rolled = pltpu.roll(x, n - s, axis=0)
    last = jnp.where(_rows_iota((SUBLANES, x.shape[1])) < SUBLANES - s, rolled[n - SUBLANES:],
                     pltpu.roll(halo, SUBLANES - s, axis=0))
    return jnp.concatenate([rolled[:n - SUBLANES], last], axis=0)


def _prev_halo_spec(bt, cb, col_off=0):
    r = bt // SUBLANES
    return pl.BlockSpec((SUBLANES, cb), lambda i, j: (jnp.maximum(i * r - 1, 0), j + col_off))


def _next_halo_spec(bt, cb, t, col_off=0):
    r = bt // SUBLANES
    last = t // SUBLANES - 1
    return pl.BlockSpec((SUBLANES, cb), lambda i, j: (jnp.minimum((i + 1) * r, last), j + col_off))


def _causal_conv(x, halo, w_ref, ksize):
    acc = None
    for j in range(ksize):
        term = w_ref[j:j + 1, :] * _shift_down(x, halo, ksize - 1 - j)
        acc = term if acc is None else acc + term
    return acc


def _anticausal_conv(y, halo, w_ref, ksize):
    acc = None
    for j in range(ksize):
        term = w_ref[j:j + 1, :] * _shift_up(y, halo, ksize - 1 - j)
        acc = term if acc is None else acc + term
    return acc


def _ffn_mid_fwd(h, cw, cb_, name):
    t = h.shape[0]
    bt, cb = _pick(t, ROW_TILE), _pick(D_FF, 1408)
    nc = D_FF // cb

    def body(hv_ref, hvh_ref, hg_ref, hgh_ref, wv_ref, wg_ref, bv_ref, bg_ref, y_ref):
        keep = (pl.program_id(0) > 0).astype(F32)
        val = _causal_conv(hv_ref[...], hvh_ref[...] * keep, wv_ref, CONV_F) + bv_ref[...]
        gate = _causal_conv(hg_ref[...], hgh_ref[...] * keep, wg_ref, CONV_F) + bg_ref[...]
        y_ref[...] = (_gelu(gate) * val).astype(BF16)

    tile_v = pl.BlockSpec((bt, cb), lambda i, j: (i, j))
    tile_g = pl.BlockSpec((bt, cb), lambda i, j: (i, j + nc))
    wv = pl.BlockSpec((CONV_F, cb), lambda i, j: (0, j))
    wg = pl.BlockSpec((CONV_F, cb), lambda i, j: (0, j + nc))
    bv = pl.BlockSpec((1, cb), lambda i, j: (0, j))
    bg = pl.BlockSpec((1, cb), lambda i, j: (0, j + nc))
    cb2 = cb_.reshape(1, 2 * D_FF)
    return pl.pallas_call(
        body, name=name, grid=(t // bt, nc),
        in_specs=[tile_v, _prev_halo_spec(bt, cb), tile_g, _prev_halo_spec(bt, cb, nc), wv, wg, bv, bg],
        out_specs=tile_v, out_shape=jax.ShapeDtypeStruct((t, D_FF), BF16),
        compiler_params=_cparams("parallel", "parallel"),
    )(h, h, h, h, cw, cw, cb2, cb2)


def _ffn_mid_bwd_a(h, dy, cw, cb_, name):
    t = h.shape[0]
    bt, cb = _pick(t, ROW_TILE), _pick(D_FF, 1408)
    nc = D_FF // cb

    def body(hv_ref, hvh_ref, hg_ref, hgh_ref, dy_ref, wv_ref, wg_ref, bv_ref, bg_ref,
             dv_ref, dg_ref, dwv_ref, dwg_ref, dbv_ref, dbg_ref):
        @pl.when(pl.program_id(1) == 0)
        def _():
            for r in (dwv_ref, dwg_ref, dbv_ref, dbg_ref):
                r[...] = jnp.zeros_like(r)

        keep = (pl.program_id(1) > 0).astype(F32)
        hv, hvh = hv_ref[...], hvh_ref[...] * keep
        hg, hgh = hg_ref[...], hgh_ref[...] * keep
        val = _causal_conv(hv, hvh, wv_ref, CONV_F) + bv_ref[...]
        gate = _causal_conv(hg, hgh, wg_ref, CONV_F) + bg_ref[...]
        gl, glg = _gelu_and_grad(gate)
        dyv = dy_ref[...]
        dval = dyv * gl
        dgate = dyv * val * glg
        dv_ref[...] = dval
        dg_ref[...] = dgate
        dbv_ref[...] += jnp.sum(dval, axis=0, keepdims=True)
        dbg_ref[...] += jnp.sum(dgate, axis=0, keepdims=True)
        for j in range(CONV_F):
            s = CONV_F - 1 - j
            dwv_ref[j:j + 1, :] += jnp.sum(dval * _shift_down(hv, hvh, s), axis=0, keepdims=True)
            dwg_ref[j:j + 1, :] += jnp.sum(dgate * _shift_down(hg, hgh, s), axis=0, keepdims=True)

    tile_v = pl.BlockSpec((bt, cb), lambda j, i: (i, j))
    tile_g = pl.BlockSpec((bt, cb), lambda j, i: (i, j + nc))
    r = bt // SUBLANES
    halo_v = pl.BlockSpec((SUBLANES, cb), lambda j, i: (jnp.maximum(i * r - 1, 0), j))
    halo_g = pl.BlockSpec((SUBLANES, cb), lambda j, i: (jnp.maximum(i * r - 1, 0), j + nc))
    wv = pl.BlockSpec((CONV_F, cb), lambda j, i: (0, j))
    wg = pl.BlockSpec((CONV_F, cb), lambda j, i: (0, j + nc))
    bv = pl.BlockSpec((1, cb), lambda j, i: (0, j))
    bg = pl.BlockSpec((1, cb), lambda j, i: (0, j + nc))
    cb2 = cb_.reshape(1, 2 * D_FF)
    dv, dg, dwv, dwg, dbv, dbg = pl.pallas_call(
        body, name=name, grid=(nc, t // bt),
        in_specs=[tile_v, halo_v, tile_g, halo_g, tile_v, wv, wg, bv, bg],
        out_specs=[tile_v, tile_v, wv, wv, bv, bv],
        out_shape=[jax.ShapeDtypeStruct((t, D_FF), F32), jax.ShapeDtypeStruct((t, D_FF), F32),
                   jax.ShapeDtypeStruct((CONV_F, D_FF), F32), jax.ShapeDtypeStruct((CONV_F, D_FF), F32),
                   jax.ShapeDtypeStruct((1, D_FF), F32), jax.ShapeDtypeStruct((1, D_FF), F32)],
        compiler_params=_cparams("parallel", "arbitrary"),
    )(h, h, h, h, dy, cw, cw, cb2, cb2)
    return dv, dg, jnp.concatenate([dwv, dwg], axis=1), jnp.concatenate([dbv, dbg], axis=1)


def _conv_bwd_x(dy, cw, name, out_dtype):
    t, c = dy.shape
    ksize = cw.shape[0]
    bt, cb = _pick(t, ROW_TILE), _pick(c, 1408)

    def body(dy_ref, halo_ref, w_ref, o_ref):
        keep = (pl.program_id(0) < t // bt - 1).astype(F32)
        o_ref[...] = _anticausal_conv(dy_ref[...], halo_ref[...] * keep, w_ref, ksize).astype(out_dtype)

    tile = pl.BlockSpec((bt, cb), lambda i, j: (i, j))
    return pl.pallas_call(
        body, name=name, grid=(t // bt, c // cb),
        in_specs=[tile, _next_halo_spec(bt, cb, t), pl.BlockSpec((ksize, cb), lambda i, j: (0, j))],
        out_specs=tile, out_shape=jax.ShapeDtypeStruct((t, c), out_dtype),
        compiler_params=_cparams("parallel", "parallel"),
    )(dy, dy, cw)


def _ple_fwd(x2, g, e, bg, name):
    t, d = x2.shape
    bt = _pick(t, ROW_TILE)

    def body(x_ref, g_ref, e_ref, b_ref, y_ref, yb_ref):
        y = x_ref[...] + _sigmoid(g_ref[...] + b_ref[...]) * e_ref[...]
        y_ref[...] = y
        yb_ref[...] = y.astype(BF16)

    row = pl.BlockSpec((bt, d), lambda i: (i, 0))
    vec = pl.BlockSpec((1, d), lambda i: (0, 0))
    return pl.pallas_call(
        body, name=name, grid=(t // bt,), in_specs=[row, row, row, vec], out_specs=[row, row],
        out_shape=[jax.ShapeDtypeStruct((t, d), F32), jax.ShapeDtypeStruct((t, d), BF16)],
        compiler_params=_cparams("parallel"),
    )(x2, g, e, bg.reshape(1, d))


def _ple_bwd(dx3, g, e, bg, name):
    t, d = dx3.shape
    bt = _pick(t, ROW_TILE)

    def body(dx_ref, g_ref, e_ref, b_ref, dg_ref, de_ref, db_ref):
        @pl.when(pl.program_id(0) == 0)
        def _():
            db_ref[...] = jnp.zeros_like(db_ref)

        dx = dx_ref[...]
        gate = _sigmoid(g_ref[...] + b_ref[...])
        dg = dx * e_ref[...] * gate * (1.0 - gate)
        dg_ref[...] = dg.astype(BF16)
        de_ref[...] = (dx * gate).astype(BF16)
        db_ref[...] += jnp.sum(dg, axis=0, keepdims=True)

    row = pl.BlockSpec((bt, d), lambda i: (i, 0))
    vec = pl.BlockSpec((1, d), lambda i: (0, 0))
    return pl.pallas_call(
        body, name=name, grid=(t // bt,), in_specs=[row, row, row, vec], out_specs=[row, row, vec],
        out_shape=[jax.ShapeDtypeStruct((t, d), BF16), jax.ShapeDtypeStruct((t, d), BF16),
                   jax.ShapeDtypeStruct((1, d), F32)],
        compiler_params=_cparams("arbitrary"),
    )(dx3, g, e, bg.reshape(1, d))


def _loss_head(y, target, name):
    t, d = y.shape
    bt = _pick(t, ROW_TILE)

    def body(y_ref, t_ref, dy_ref, l_ref, acc_ref):
        @pl.when(pl.program_id(0) == 0)
        def _():
            acc_ref[...] = jnp.zeros_like(acc_ref)

        err = y_ref[...] - t_ref[...]
        dy_ref[...] = err * (1.0 / d)
        acc_ref[...] += jnp.sum(err * err, axis=0, keepdims=True)

        @pl.when(pl.program_id(0) == t // bt - 1)
        def _():
            l_ref[...] = jnp.full(l_ref.shape, (0.5 / d) * jnp.sum(acc_ref[...]), F32)

    row = pl.BlockSpec((bt, d), lambda i: (i, 0))
    dy, l = pl.pallas_call(
        body, name=name, grid=(t // bt,), in_specs=[row, row],
        out_specs=[row, pl.BlockSpec((SUBLANES, LANES), lambda i: (0, 0))],
        out_shape=[jax.ShapeDtypeStruct((t, d), F32), jax.ShapeDtypeStruct((SUBLANES, LANES), F32)],
        scratch_shapes=[pltpu.VMEM((1, d), F32)],
        compiler_params=_cparams("arbitrary"),
    )(y, target)
    return dy, l[0, 0]


def _split3(x):
    hi = x.astype(BF16)
    r1 = x - hi.astype(F32)
    mid = r1.astype(BF16)
    lo = (r1 - mid.astype(F32)).astype(BF16)
    return hi, mid, lo


def _tri_dot(x, tri):
    hi, mid, lo = _split3(x)
    dims = (((1,), (0,)), ((), ()))
    return (lax.dot_general(hi, tri, dims, preferred_element_type=F32)
            + lax.dot_general(mid, tri, dims, preferred_element_type=F32)
            + lax.dot_general(lo, tri, dims, preferred_element_type=F32))


def _fgate_fwd(fg_rows, b_f, name):
    hh, t = fg_rows.shape
    bt = _pick(t, 512)

    def body(fg_ref, b_ref, c_ref, carry_ref):
        @pl.when(pl.program_id(0) == 0)
        def _():
            carry_ref[...] = jnp.zeros_like(carry_ref)

        xx = fg_ref[...] + b_ref[...]
        logf = jnp.minimum(xx, 0.0) - _log1p(jnp.exp(-jnp.abs(xx)))
        r = lax.broadcasted_iota(jnp.int32, (bt, bt), 0)
        c = lax.broadcasted_iota(jnp.int32, (bt, bt), 1)
        tri = (r <= c).astype(BF16)
        cs = _tri_dot(logf, tri) + carry_ref[...]
        c_ref[...] = cs
        carry_ref[...] = cs[:, bt - 1:bt]

    return pl.pallas_call(
        body, name=name, grid=(t // bt,),
        in_specs=[pl.BlockSpec((hh, bt), lambda i: (0, i)), pl.BlockSpec((hh, 1), lambda i: (0, 0))],
        out_specs=pl.BlockSpec((hh, bt), lambda i: (0, i)),
        out_shape=jax.ShapeDtypeStruct((hh, t), F32),
        scratch_shapes=[pltpu.VMEM((hh, 1), F32)],
        compiler_params=_cparams("arbitrary"),
    )(fg_rows, b_f.reshape(hh, 1))


def _fgate_bwd(dck_rows, dcq_rows, fg_rows, b_f, name):
    hh, t = fg_rows.shape
    bt = _pick(t, 512)
    nb = t // bt

    def body(dc_ref, dcq_ref, fg_ref, b_ref, dfg_ref, db_ref, carry_ref):
        @pl.when(pl.program_id(0) == 0)
        def _():
            carry_ref[...] = jnp.zeros_like(carry_ref)
            db_ref[...] = jnp.zeros_like(db_ref)

        r = lax.broadcasted_iota(jnp.int32, (bt, bt), 0)
        c = lax.broadcasted_iota(jnp.int32, (bt, bt), 1)
        tri = (r >= c).astype(BF16)
        dlogf = _tri_dot(dc_ref[...] + dcq_ref[...], tri) + carry_ref[...]
        carry_ref[...] = dlogf[:, 0:1]
        xx = fg_ref[...] + b_ref[...]
        dfg = dlogf * _sigmoid(-xx)
        dfg_ref[...] = dfg
        db_ref[...] += jnp.sum(dfg, axis=1, keepdims=True)

    blk = pl.BlockSpec((hh, bt), lambda i: (0, nb - 1 - i))
    vec = pl.BlockSpec((hh, 1), lambda i: (0, 0))
    return pl.pallas_call(
        body, name=name, grid=(nb,), in_specs=[blk, blk, blk, vec], out_specs=[blk, vec],
        out_shape=[jax.ShapeDtypeStruct((hh, t), F32), jax.ShapeDtypeStruct((hh, 1), F32)],
        scratch_shapes=[pltpu.VMEM((hh, 1), F32)],
        compiler_params=_cparams("arbitrary"),
    )(dck_rows, dcq_rows, fg_rows, b_f.reshape(hh, 1))


def _rows_to_cols(r):
    hh, t = r.shape
    return jnp.repeat(r.reshape(hh // 2, 2, t).transpose(0, 2, 1), HEAD_DIM, axis=-1)


def _rows_to_pairs(r):
    hh, t = r.shape
    return jnp.pad(r.reshape(hh // 2, 2, t), ((0, 0), (0, SUBLANES - 2), (0, 0)))


AUG_C = HEAD_DIM
AUG_ONE = HEAD_DIM + 3


def _attn_prep(qkv, c_cols, name):
    t = qkv.shape[0]
    bt = _pick(t, ATTN_FWD_BLOCK)
    scale = 1.0 / math.sqrt(HEAD_DIM)

    def body(q_ref, k_ref, c_ref, qh_ref, kh_ref):
        lane = lax.broadcasted_iota(jnp.int32, (bt, LANES), 1)
        q2 = q_ref[...].astype(F32)
        k2 = k_ref[...].astype(F32) * scale
        c2 = c_ref[...]
        parts = [p.astype(F32) for p in _split3(c2 - c2[0:1, :])]
        swapped = [pltpu.roll(p, HEAD_DIM, axis=1) for p in parts]
        for a in (0, 1):
            qa = q2 if a == 0 else pltpu.roll(q2, HEAD_DIM, axis=1)
            ka = k2 if a == 0 else pltpu.roll(k2, HEAD_DIM, axis=1)
            hi, mid, lo = swapped if a == 0 else parts
            kaug = jnp.where(lane < HEAD_DIM, ka,
                             jnp.where(lane == AUG_C, hi,
                                       jnp.where(lane == AUG_C + 1, mid,
                                                 jnp.where(lane == AUG_C + 2, lo,
                                                           jnp.where(lane == AUG_ONE, 1.0, 0.0)))))
            qaug = jnp.where(lane < HEAD_DIM, qa, jnp.where(lane < AUG_ONE, -1.0, 0.0))
            qh_ref[:, a * LANES:(a + 1) * LANES] = qaug.astype(BF16)
            kh_ref[:, a * LANES:(a + 1) * LANES] = kaug.astype(BF16)

    out = pl.BlockSpec((bt, 2 * LANES), lambda i, hp: (i, hp))
    shape = jax.ShapeDtypeStruct((t, N_HEADS * LANES), BF16)
    return pl.pallas_call(
        body, name=name, grid=(t // bt, N_PAIRS),
        in_specs=[pl.BlockSpec((bt, LANES), lambda i, hp: (i, hp)),
                  pl.BlockSpec((bt, LANES), lambda i, hp: (i, N_PAIRS + hp)),
                  pl.BlockSpec((None, bt, LANES), lambda i, hp: (hp, i, 0))],
        out_specs=[out, out], out_shape=[shape, shape],
        compiler_params=_cparams("parallel", "parallel"),
    )(qkv, qkv, c_cols)


def _block_scalar(c_ref, a, start):
    return c_ref[a:a + 1, pl.ds(start, LANES)][:, 0:1]


def _attn_fwd_t(qh, kh, vt, c_pairs, name):
    t = qh.shape[0]
    bq = _pick(t, ATTN_FWD_BLOCK)
    nq = t // bq

    def body(q_ref, k_ref, vt_ref, c_ref, ot_ref, otb_ref, lse_ref, acc_ref):
        i = pl.program_id(1)
        q0 = pl.multiple_of(i * bq, bq)
        qs = (q_ref[:, 0:LANES], q_ref[:, LANES:2 * LANES])
        cq = [_block_scalar(c_ref, a, q0) for a in (0, 1)]
        acc_ref[...] = jnp.zeros_like(acc_ref)
        keep = _rows_iota((bq, bq)) <= lax.broadcasted_iota(jnp.int32, (bq, bq), 1)

        def step(j, carry, masked):
            k0 = pl.multiple_of(j * bq, bq)
            kb = k_ref[pl.ds(k0, bq), :]
            new = []
            for a in (0, 1):
                m_old, l_old = carry[2 * a], carry[2 * a + 1]
                st = lax.dot_general(kb[:, a * LANES:(a + 1) * LANES], qs[a], _DOT_DIMS["nt"],
                                     preferred_element_type=F32)
                if masked:
                    st = jnp.where(keep, st, NEG_BIG)
                sigma = cq[a] - _block_scalar(c_ref, a, k0)
                m_new = jnp.maximum(m_old, jnp.max(st, axis=0, keepdims=True) + sigma)
                pt = jnp.exp(st - (m_new - sigma))
                alpha = jnp.exp(m_old - m_new)
                l_new = alpha * l_old + jnp.sum(pt, axis=0, keepdims=True)
                vta = vt_ref[a * HEAD_DIM:(a + 1) * HEAD_DIM, pl.ds(k0, bq)]
                acc_ref[a] = alpha * acc_ref[a] + lax.dot_general(
                    vta, pt.astype(BF16), _DOT_DIMS["nn"], preferred_element_type=F32)
                new += [m_new, l_new]
            return tuple(new)

        neg = jnp.full((1, bq), NEG_BIG, F32)
        zero = jnp.zeros((1, bq), F32)
        carry = lax.fori_loop(0, i, lambda j, c: step(j, c, False), (neg, zero, neg, zero))
        m_a, l_a, m_b, l_b = step(i, carry, True)
        ot = jnp.concatenate([acc_ref[0] / l_a, acc_ref[1] / l_b], axis=0)
        ot_ref[...] = ot
        otb_ref[...] = ot.astype(BF16)
        lse_ref[...] = jnp.zeros_like(lse_ref)
        lse_ref[0:1, :] = m_a + jnp.log(l_a)
        lse_ref[1:2, :] = m_b + jnp.log(l_b)

    rows = pl.BlockSpec((None, SUBLANES, bq), lambda hp, i: (hp, 0, i))
    otile = pl.BlockSpec((LANES, bq), lambda hp, i: (hp, i))
    return pl.pallas_call(
        body, name=name, grid=(N_PAIRS, nq),
        in_specs=[pl.BlockSpec((bq, 2 * LANES), lambda hp, i: (i, hp)),
                  pl.BlockSpec((t, 2 * LANES), lambda hp, i: (0, hp)),
                  pl.BlockSpec((LANES, t), lambda hp, i: (hp, 0)),
                  pl.BlockSpec((None, SUBLANES, t), lambda hp, i: (hp, 0, 0))],
        out_specs=[otile, otile, rows],
        out_shape=[jax.ShapeDtypeStruct((D_MODEL, t), F32), jax.ShapeDtypeStruct((D_MODEL, t), BF16),
                   jax.ShapeDtypeStruct((N_PAIRS, SUBLANES, t), F32)],
        scratch_shapes=[pltpu.VMEM((2, HEAD_DIM, bq), F32)],
        compiler_params=_cparams("parallel", "arbitrary"),
    )(qh, kh, vt, c_pairs)


def _attn_delta_t(dot, ot, name):
    t = dot.shape[1]
    bt = _pick(t, 512)

    def body(do_ref, o_ref, d_ref, dob_ref):
        dob = do_ref[...].astype(BF16)
        prod = dob.astype(F32) * o_ref[...]
        d_ref[...] = jnp.zeros_like(d_ref)
        d_ref[0:1, :] = jnp.sum(prod[0:HEAD_DIM], axis=0, keepdims=True)
        d_ref[1:2, :] = jnp.sum(prod[HEAD_DIM:], axis=0, keepdims=True)
        dob_ref[...] = dob

    tile = pl.BlockSpec((LANES, bt), lambda hp, i: (hp, i))
    return pl.pallas_call(
        body, name=name, grid=(N_PAIRS, t // bt), in_specs=[tile, tile],
        out_specs=[pl.BlockSpec((None, SUBLANES, bt), lambda hp, i: (hp, 0, i)), tile],
        out_shape=[jax.ShapeDtypeStruct((N_PAIRS, SUBLANES, t), F32), jax.ShapeDtypeStruct((D_MODEL, t), BF16)],
        compiler_params=_cparams("parallel", "parallel"),
    )(dot, ot)


def _attn_bwd_t(kh, kt, qt, qkv, dotb, c_pairs, lse_pairs, d_pairs, name):
    t = kh.shape[0]
    bk = _pick(t, ATTN_BWD_BLOCK)
    nk = t // bk
    ref_tile = _pick(t, ATTN_FWD_BLOCK)
    assert ref_tile % bk == 0
    scale = 1.0 / math.sqrt(HEAD_DIM)

    def ref_start(start):
        return pl.multiple_of((start // ref_tile) * ref_tile, ref_tile)

    def body(k_ref, kt_ref, v_ref, qt_ref, dot_ref, c_ref, lse_ref, d_ref,
             dk_ref, dv_ref, dc_ref, dq_hbm, dka_ref, dva_ref, dqa_ref, sem):
        hp, j = pl.program_id(0), pl.program_id(1)
        k0 = pl.multiple_of(j * bk, bk)
        ks = (k_ref[:, 0:LANES], k_ref[:, LANES:2 * LANES])
        vb = v_ref[...]
        ck = [_block_scalar(c_ref, a, ref_start(k0)) for a in (0, 1)]
        dka_ref[...] = jnp.zeros_like(dka_ref)
        dva_ref[...] = jnp.zeros_like(dva_ref)

        @pl.when(j == 0)
        def _():
            dqa_ref[...] = jnp.zeros_like(dqa_ref)

        keep = _rows_iota((bk, bk)) <= lax.broadcasted_iota(jnp.int32, (bk, bk), 1)
        top = _rows_iota((LANES, bk)) < HEAD_DIM

        def step(i, masked):
            q0 = pl.multiple_of(i * bk, bk)
            dot2 = dot_ref[:, pl.ds(q0, bk)]
            zero = jnp.zeros_like(dot2)
            for a in (0, 1):
                qta = qt_ref[a * LANES:(a + 1) * LANES, pl.ds(q0, bk)]
                st = lax.dot_general(ks[a], qta, _DOT_DIMS["nn"], preferred_element_type=F32)
                sigma = _block_scalar(c_ref, a, ref_start(q0)) - ck[a]
                pt = jnp.exp(st - (lse_ref[a:a + 1, pl.ds(q0, bk)] - sigma))
                if masked:
                    pt = jnp.where(keep, pt, 0.0)
                dota = jnp.where(top, dot2, zero) if a == 0 else jnp.where(top, zero, dot2)
                dpt = lax.dot_general(vb, dota, _DOT_DIMS["nn"], preferred_element_type=F32)
                dstb = (pt * (dpt - d_ref[a:a + 1, pl.ds(q0, bk)])).astype(BF16)
                dva_ref[a] += lax.dot_general(dot2[a * HEAD_DIM:(a + 1) * HEAD_DIM, :], pt.astype(BF16),
                                              _DOT_DIMS["nt"], preferred_element_type=F32)
                dka_ref[a] += lax.dot_general(qta, dstb, _DOT_DIMS["nt"], preferred_element_type=F32)
                dqa_ref[a, :, pl.ds(q0, bk)] += lax.dot_general(
                    kt_ref[a * LANES:(a + 1) * LANES, :], dstb, _DOT_DIMS["nn"], preferred_element_type=F32)

        def loop_body(i, carry):
            step(i, False)
            return carry

        step(j, True)
        lax.fori_loop(j + 1, nk, loop_body, 0)
        dk_ref[...] = (jnp.concatenate([dka_ref[0, 0:HEAD_DIM, :], dka_ref[1, 0:HEAD_DIM, :]], axis=0)
                       * scale).astype(BF16)
        dv_ref[...] = jnp.concatenate([dva_ref[0], dva_ref[1]], axis=0).astype(BF16)
        dc_ref[...] = jnp.zeros_like(dc_ref)
        dc_ref[0:1, :] = dka_ref[0, AUG_C:AUG_C + 1, :]
        dc_ref[1:2, :] = dka_ref[1, AUG_C:AUG_C + 1, :]

        @pl.when(j == nk - 1)
        def _():
            for a in (0, 1):
                row0 = pl.multiple_of((2 * hp + a) * LANES, LANES)
                cp = pltpu.make_async_copy(dqa_ref.at[a], dq_hbm.at[pl.ds(row0, LANES), :], sem)
                cp.start()
                cp.wait()

    once = pl.Buffered(1)
    rows = pl.BlockSpec((None, SUBLANES, t), lambda hp, j: (hp, 0, 0))
    kv_out = pl.BlockSpec((LANES, bk), lambda hp, j: (hp, j))
    return pl.pallas_call(
        body, name=name, grid=(N_PAIRS, nk),
        in_specs=[pl.BlockSpec((bk, 2 * LANES), lambda hp, j: (j, hp)),
                  pl.BlockSpec((2 * LANES, bk), lambda hp, j: (hp, j)),
                  pl.BlockSpec((bk, LANES), lambda hp, j: (j, 2 * N_PAIRS + hp)),
                  pl.BlockSpec((2 * LANES, t), lambda hp, j: (hp, 0), pipeline_mode=once),
                  pl.BlockSpec((LANES, t), lambda hp, j: (hp, 0), pipeline_mode=once),
                  rows, rows, rows],
        out_specs=[kv_out, kv_out, pl.BlockSpec((None, SUBLANES, bk), lambda hp, j: (hp, 0, j)),
                   pl.BlockSpec(memory_space=pltpu.HBM)],
        out_shape=[jax.ShapeDtypeStruct((D_MODEL, t), BF16), jax.ShapeDtypeStruct((D_MODEL, t), BF16),
                   jax.ShapeDtypeStruct((N_PAIRS, SUBLANES, t), F32),
                   jax.ShapeDtypeStruct((N_HEADS * LANES, t), F32)],
        scratch_shapes=[pltpu.VMEM((2, LANES, bk), F32), pltpu.VMEM((2, HEAD_DIM, bk), F32),
                        pltpu.VMEM((2, LANES, t), F32), pltpu.SemaphoreType.DMA],
        compiler_params=_cparams("arbitrary", "arbitrary"),
    )(kh, kt, qkv, qt, dotb, c_pairs, lse_pairs, d_pairs)


def _scan(a, u, name, reverse=False):
    t, c = a.shape
    bt, cb = _pick(t, ROW_TILE), _pick(c, 1024)
    nt = t // bt
    ngroups = bt // SUBLANES

    def body(a_ref, u_ref, h_ref, carry_ref, as_ref, us_ref):
        @pl.when(pl.program_id(1) == 0)
        def _():
            carry_ref[...] = jnp.zeros_like(carry_ref)

        av, uv = a_ref[...], u_ref[...]
        sub = _rows_iota((bt, cb)) % SUBLANES
        for s in (1, 2, 4):
            if reverse:
                a_sh, u_sh = pltpu.roll(av, bt - s, axis=0), pltpu.roll(uv, bt - s, axis=0)
                valid = sub < SUBLANES - s
            else:
                a_sh, u_sh = pltpu.roll(av, s, axis=0), pltpu.roll(uv, s, axis=0)
                valid = sub >= s
            uv = jnp.where(valid, uv + av * u_sh, uv)
            av = jnp.where(valid, av * a_sh, av)
        as_ref[...] = av
        us_ref[...] = uv
        edge = 0 if reverse else SUBLANES - 1
        pick = _rows_iota((SUBLANES, cb)) == edge

        def group(gi, carry):
            g = (ngroups - 1 - gi) if reverse else gi
            r0 = pl.multiple_of(g * SUBLANES, SUBLANES)
            h8 = us_ref[pl.ds(r0, SUBLANES), :] + as_ref[pl.ds(r0, SUBLANES), :] * carry
            h_ref[pl.ds(r0, SUBLANES), :] = h8
            return jnp.sum(jnp.where(pick, h8, 0.0), axis=0, keepdims=True)

        carry_ref[...] = lax.fori_loop(0, ngroups, group, carry_ref[...])

    if reverse:
        tile = pl.BlockSpec((bt, cb), lambda j, i: (nt - 1 - i, j))
    else:
        tile = pl.BlockSpec((bt, cb), lambda j, i: (i, j))
    return pl.pallas_call(
        body, name=name, grid=(c // cb, nt), in_specs=[tile, tile], out_specs=tile,
        out_shape=jax.ShapeDtypeStruct((t, c), F32),
        scratch_shapes=[pltpu.VMEM((1, cb), F32), pltpu.VMEM((bt, cb), F32), pltpu.VMEM((bt, cb), F32)],
        compiler_params=_cparams("parallel", "arbitrary"),
    )(a, u)


def _rg_conv_fwd(proj, cw, cb_, name):
    t = proj.shape[0]
    bt, cb = _pick(t, ROW_TILE), D_MODEL

    def body(x_ref, halo_ref, w_ref, b_ref, o_ref):
        keep = (pl.program_id(0) > 0).astype(F32)
        o_ref[...] = _causal_conv(x_ref[...], halo_ref[...] * keep, w_ref, CONV_B) + b_ref[...]

    tile = pl.BlockSpec((bt, cb), lambda i, j: (i, j))
    return pl.pallas_call(
        body, name=name, grid=(t // bt, 1),
        in_specs=[tile, _prev_halo_spec(bt, cb), pl.BlockSpec((CONV_B, cb), lambda i, j: (0, 0)),
                  pl.BlockSpec((1, cb), lambda i, j: (0, 0))],
        out_specs=tile, out_shape=jax.ShapeDtypeStruct((t, D_MODEL), F32),
        compiler_params=_cparams("parallel", "parallel"),
    )(proj, proj, cw, cb_.reshape(1, D_MODEL))


def _conv_bwd_w(x, dy, ksize, name):
    t, c = dy.shape
    bt = _pick(t, ROW_TILE)
    r = bt // SUBLANES

    def body(x_ref, halo_ref, dy_ref, dw_ref, db_ref):
        @pl.when(pl.program_id(0) == 0)
        def _():
            dw_ref[...] = jnp.zeros_like(dw_ref)
            db_ref[...] = jnp.zeros_like(db_ref)

        keep = (pl.program_id(0) > 0).astype(F32)
        xv, halo, dyv = x_ref[...], halo_ref[...] * keep, dy_ref[...]
        db_ref[...] += jnp.sum(dyv, axis=0, keepdims=True)
        for j in range(ksize):
            dw_ref[j:j + 1, :] += jnp.sum(dyv * _shift_down(xv, halo, ksize - 1 - j), axis=0, keepdims=True)

    tile = pl.BlockSpec((bt, c), lambda i: (i, 0))
    return pl.pallas_call(
        body, name=name, grid=(t // bt,),
        in_specs=[tile, pl.BlockSpec((SUBLANES, c), lambda i: (jnp.maximum(i * r - 1, 0), 0)), tile],
        out_specs=[pl.BlockSpec((ksize, c), lambda i: (0, 0)), pl.BlockSpec((1, c), lambda i: (0, 0))],
        out_shape=[jax.ShapeDtypeStruct((ksize, c), F32), jax.ShapeDtypeStruct((1, c), F32)],
        compiler_params=_cparams("arbitrary"),
    )(x, x, dy)


def _rg_gate_math(xc, wa_ref, wi_ref, ba_ref, bi_ref, lam_ref):
    xb = xc.astype(BF16)
    ra = lax.dot_general(xb, wa_ref[...], _DOT_DIMS["nn"], preferred_element_type=F32) + ba_ref[...]
    ia = lax.dot_general(xb, wi_ref[...], _DOT_DIMS["nn"], preferred_element_type=F32) + bi_ref[...]
    r, ig = _sigmoid(ra), _sigmoid(ia)
    sp = _softplus(-lam_ref[...])
    log_a = -LRU_C * r * sp
    a = jnp.exp(log_a)
    mult = jnp.sqrt(-_expm1(2.0 * log_a))
    return xb, r, ig, sp, a, mult


def _rg_gate_specs(bt, time_first):
    if time_first:
        tile = pl.BlockSpec((bt, BLOCK_B), lambda i, n: (i, n))
        w = pl.BlockSpec((None, BLOCK_B, BLOCK_B), lambda i, n: (n, 0, 0))
        v = pl.BlockSpec((None, 1, BLOCK_B), lambda i, n: (n, 0, 0))
    else:
        tile = pl.BlockSpec((bt, BLOCK_B), lambda n, i: (i, n))
        w = pl.BlockSpec((None, BLOCK_B, BLOCK_B), lambda n, i: (n, 0, 0))
        v = pl.BlockSpec((None, 1, BLOCK_B), lambda n, i: (n, 0, 0))
    return tile, w, v


def _rg_gate_fwd(xc, wa, ba, wi, bi, lam, name):
    t = xc.shape[0]
    bt = _pick(t, 512)

    def body(x_ref, wa_ref, wi_ref, ba_ref, bi_ref, lam_ref, a_ref, u_ref):
        xcv = x_ref[...]
        _, _, ig, _, a, mult = _rg_gate_math(xcv, wa_ref, wi_ref, ba_ref, bi_ref, lam_ref)
        a_ref[...] = a
        u_ref[...] = mult * (ig * xcv)

    tile, w, v = _rg_gate_specs(bt, True)
    return pl.pallas_call(
        body, name=name, grid=(t // bt, N_BLOCKS_B), in_specs=[tile, w, w, v, v, v], out_specs=[tile, tile],
        out_shape=[jax.ShapeDtypeStruct((t, D_MODEL), F32), jax.ShapeDtypeStruct((t, D_MODEL), F32)],
        compiler_params=_cparams("parallel", "parallel"),
    )(xc, wa, wi, ba, bi, lam)


def _rg_gate_bwd(xc, g, h, wa, ba, wi, bi, lam, name):
    t = xc.shape[0]
    bt = _pick(t, 512)
    rr = bt // SUBLANES

    def body(x_ref, g_ref, h_ref, hh_ref, wa_ref, wi_ref, ba_ref, bi_ref, lam_ref,
             dx_ref, dwa_ref, dwi_ref, dba_ref, dbi_ref, dlam_ref):
        @pl.when(pl.program_id(1) == 0)
        def _():
            for ref in (dwa_ref, dwi_ref, dba_ref, dbi_ref, dlam_ref):
                ref[...] = jnp.zeros_like(ref)

        keep = (pl.program_id(1) > 0).astype(F32)
        xcv, gv = x_ref[...], g_ref[...]
        xb, r, ig, sp, a, mult = _rg_gate_math(xcv, wa_ref, wi_ref, ba_ref, bi_ref, lam_ref)
        h_prev = _shift_down(h_ref[...], hh_ref[...] * keep, 1)
        da = gv * h_prev
        dmult = gv * ig * xcv
        dig = gv * mult * xcv
        dxc = gv * mult * ig
        dlog_a = da * a - dmult * (a * a / mult)
        dr = dlog_a * (-LRU_C * sp)
        dsp = jnp.sum(dlog_a * (-LRU_C * r), axis=0, keepdims=True)
        dlam_ref[...] += dsp * (-_sigmoid(-lam_ref[...]))
        dra = dr * r * (1.0 - r)
        dia = dig * ig * (1.0 - ig)
        dba_ref[...] += jnp.sum(dra, axis=0, keepdims=True)
        dbi_ref[...] += jnp.sum(dia, axis=0, keepdims=True)
        drab, diab = dra.astype(BF16), dia.astype(BF16)
        dwa_ref[...] += lax.dot_general(xb, drab, _DOT_DIMS["tn"], preferred_element_type=F32)
        dwi_ref[...] += lax.dot_general(xb, diab, _DOT_DIMS["tn"], preferred_element_type=F32)
        dxc = dxc + lax.dot_general(drab, wa_ref[...], _DOT_DIMS["nt"], preferred_element_type=F32)
        dxc = dxc + lax.dot_general(diab, wi_ref[...], _DOT_DIMS["nt"], preferred_element_type=F32)
        dx_ref[...] = dxc

    tile, w, v = _rg_gate_specs(bt, False)
    halo = pl.BlockSpec((SUBLANES, BLOCK_B), lambda n, i: (jnp.maximum(i * rr - 1, 0), n))
    wshape = jax.ShapeDtypeStruct((N_BLOCKS_B, BLOCK_B, BLOCK_B), F32)
    vshape = jax.ShapeDtypeStruct((N_BLOCKS_B, 1, BLOCK_B), F32)
    return pl.pallas_call(
        body, name=name, grid=(N_BLOCKS_B, t // bt),
        in_specs=[tile, tile, tile, halo, w, w, v, v, v],
        out_specs=[tile, w, w, v, v, v],
        out_shape=[jax.ShapeDtypeStruct((t, D_MODEL), F32), wshape, wshape, vshape, vshape, vshape],
        compiler_params=_cparams("parallel", "arbitrary"),
    )(xc, g, h, h, wa, wi, ba, bi, lam)


def _rg_out_fwd(h, proj, name):
    t = h.shape[0]
    bt = _pick(t, ROW_TILE)

    def body(h_ref, g_ref, y_ref):
        y_ref[...] = (h_ref[...] * _gelu(g_ref[...])).astype(BF16)

    tile = pl.BlockSpec((bt, D_MODEL), lambda i: (i, 0))
    return pl.pallas_call(
        body, name=name, grid=(t // bt,), in_specs=[tile, pl.BlockSpec((bt, D_MODEL), lambda i: (i, 1))],
        out_specs=tile, out_shape=jax.ShapeDtypeStruct((t, D_MODEL), BF16),
        compiler_params=_cparams("parallel"),
    )(h, proj)


def _rg_out_bwd(dy, h, proj, name):
    t = h.shape[0]
    bt = _pick(t, ROW_TILE)

    def body(dy_ref, h_ref, g_ref, dh_ref, dg_ref):
        gl, glg = _gelu_and_grad(g_ref[...])
        dyv = dy_ref[...]
        dh_ref[...] = dyv * gl
        dg_ref[...] = (dyv * h_ref[...] * glg).astype(BF16)

    tile = pl.BlockSpec((bt, D_MODEL), lambda i: (i, 0))
    return pl.pallas_call(
        body, name=name, grid=(t // bt,),
        in_specs=[tile, tile, pl.BlockSpec((bt, D_MODEL), lambda i: (i, 1))], out_specs=[tile, tile],
        out_shape=[jax.ShapeDtypeStruct((t, D_MODEL), F32), jax.ShapeDtypeStruct((t, D_MODEL), BF16)],
        compiler_params=_cparams("parallel"),
    )(dy, h, proj)


def _shift_up_one(a, name):
    t, c = a.shape
    bt = _pick(t, ROW_TILE)

    def body(a_ref, halo_ref, o_ref):
        o_ref[...] = _shift_up(a_ref[...], halo_ref[...], 1)

    tile = pl.BlockSpec((bt, c), lambda i, j: (i, j))
    return pl.pallas_call(
        body, name=name, grid=(t // bt, 1), in_specs=[tile, _next_halo_spec(bt, c, t)], out_specs=tile,
        out_shape=jax.ShapeDtypeStruct((t, c), F32), compiler_params=_cparams("parallel", "parallel"),
    )(a, a)


ADAM_ROWS = 64


def _adamw(recv, w, m, v, name):
    _, r, c = recv.shape
    br = ADAM_ROWS
    assert r % br == 0

    def body(r_ref, w_ref, m_ref, v_ref, g_ref, d_ref, nm_ref, nv_ref):
        g = r_ref[0].astype(F32)
        for s in range(1, N_DEV):
            g = g + r_ref[s].astype(F32)
        m_new = ADAM_B1 * m_ref[...] + (1.0 - ADAM_B1) * g
        v_new = ADAM_B2 * v_ref[...] + (1.0 - ADAM_B2) * (g * g)
        m_hat = m_new / (1.0 - ADAM_B1 ** ADAM_STEP)
        v_hat = v_new / (1.0 - ADAM_B2 ** ADAM_STEP)
        g_ref[...] = g
        d_ref[...] = -ADAM_LR * (m_hat / (jnp.sqrt(v_hat) + ADAM_EPS) + ADAM_WD * w_ref[...])
        nm_ref[...] = m_new
        nv_ref[...] = v_new

    tile = pl.BlockSpec((br, c), lambda i: (i, 0))
    shape = jax.ShapeDtypeStruct((r, c), F32)
    return pl.pallas_call(
        body, name=name, grid=(r // br,),
        in_specs=[pl.BlockSpec((N_DEV, br, c), lambda i: (0, i, 0)), tile, tile, tile],
        out_specs=[tile] * 4, out_shape=[shape] * 4, compiler_params=_cparams("parallel"),
    )(recv, w, m, v)


def _all_to_all(src, name):
    def body(src_ref, out_ref, send_sems, recv_sems, local_sem):
        pos = [lax.axis_index(ax) for ax in MESH_AXES]
        me = 4 * pos[0] + 2 * pos[1] + pos[2]

        def peer_of(k):
            p = [(1 - pos[b]) if (k >> (2 - b)) & 1 else pos[b] for b in range(3)]
            return tuple(p), 4 * p[0] + 2 * p[1] + p[2]

        def copy(k):
            peer, peer_idx = peer_of(k)
            return pltpu.make_async_remote_copy(
                src_ref=src_ref.at[peer_idx], dst_ref=out_ref.at[me],
                send_sem=send_sems.at[k - 1], recv_sem=recv_sems.at[k - 1],
                device_id=peer, device_id_type=pl.DeviceIdType.MESH)

        def arrival(k):
            peer, peer_idx = peer_of(k)
            return pltpu.make_async_remote_copy(
                src_ref=src_ref.at[me], dst_ref=out_ref.at[peer_idx],
                send_sem=send_sems.at[k - 1], recv_sem=recv_sems.at[k - 1],
                device_id=peer, device_id_type=pl.DeviceIdType.MESH)

        mine = pltpu.make_async_copy(src_ref.at[me], out_ref.at[me], local_sem)
        mine.start()
        sends = [copy(k) for k in range(1, N_DEV)]
        for cp in sends:
            cp.start()
        for k in range(1, N_DEV):
            arrival(k).wait_recv()
        for cp in sends:
            cp.wait_send()
        mine.wait()

    hbm = pl.BlockSpec(memory_space=pltpu.HBM)
    return pl.pallas_call(
        body, name=name, in_specs=[hbm], out_specs=hbm,
        out_shape=jax.ShapeDtypeStruct(src.shape, src.dtype),
        scratch_shapes=[pltpu.SemaphoreType.DMA((N_DEV - 1,)), pltpu.SemaphoreType.DMA((N_DEV - 1,)),
                        pltpu.SemaphoreType.DMA],
        compiler_params=pltpu.CompilerParams(has_side_effects=True),
    )(src)


def _all_gather(src, name):
    def body(src_ref, out_ref, send_sems, recv_sems, local_sem):
        x, y, c = (lax.axis_index(ax) for ax in MESH_AXES)
        me, sibling = (x, y, c), (x, y, 1 - c)
        chips = [(1 - x, y), (x, 1 - y), (1 - x, 1 - y)]

        def slot(px, py, pc):
            return out_ref.at[4 * px + 2 * py + pc]

        def copy(k, block, to, from_src=False):
            return pltpu.make_async_remote_copy(
                src_ref=src_ref if from_src else slot(*block), dst_ref=slot(*block),
                send_sem=send_sems.at[k], recv_sem=recv_sems.at[k],
                device_id=to, device_id_type=pl.DeviceIdType.MESH)

        mine = pltpu.make_async_copy(src_ref, slot(*me), local_sem)
        mine.start()
        first = [copy(0, me, sibling, True)] + [copy(1 + j, me, (*chip, c), True) for j, chip in enumerate(chips)]
        for cp in first:
            cp.start()
        passed = [copy(4 + j, (*chip, c), sibling) for j, chip in enumerate(chips)]
        for j, chip in enumerate(chips):
            copy(1 + j, (*chip, c), me).wait_recv()
            passed[j].start()
        copy(0, sibling, me).wait_recv()
        for j, chip in enumerate(chips):
            copy(4 + j, (*chip, 1 - c), me).wait_recv()
        for cp in first + passed:
            cp.wait_send()
        mine.wait()

    hbm = pl.BlockSpec(memory_space=pltpu.HBM)
    return pl.pallas_call(
        body, name=name, in_specs=[hbm], out_specs=hbm,
        out_shape=jax.ShapeDtypeStruct((N_DEV,) + tuple(src.shape), src.dtype),
        scratch_shapes=[pltpu.SemaphoreType.DMA((N_DEV - 1,)), pltpu.SemaphoreType.DMA((N_DEV - 1,)),
                        pltpu.SemaphoreType.DMA],
        compiler_params=pltpu.CompilerParams(has_side_effects=True),
    )(src)


WEIGHTS = ['a_w_in', 'a_b_f', 'a_w_out', 'b_w_in', 'b_conv_w', 'b_conv_b', 'b_w_a', 'b_b_a', 'b_w_i', 'b_b_i',
           'b_lam', 'b_w_out', 'f_w_up', 'f_conv_w', 'f_conv_b', 'f_w_down', 'ln1_g', 'ln1_b', 'ln2_g', 'ln2_b',
           'ple_w', 'ple_gate_w', 'ple_gate_b']
SHARD_AXIS = {'a_w_in': 2, 'a_b_f': None, 'a_w_out': 1, 'b_w_in': 2, 'b_conv_w': 2, 'b_conv_b': 1, 'b_w_a': None,
              'b_b_a': None, 'b_w_i': None, 'b_b_i': None, 'b_lam': 1, 'b_w_out': 1, 'f_w_up': 2, 'f_conv_w': 2,
              'f_conv_b': None, 'f_w_down': 1, 'ln1_g': None, 'ln1_b': None, 'ln2_g': None, 'ln2_b': None,
              'ple_w': 2, 'ple_gate_w': 1, 'ple_gate_b': None}
MATMUL_WEIGHTS = ['a_w_in', 'a_w_out', 'b_w_in', 'b_w_out', 'f_w_up', 'f_w_down', 'ple_w', 'ple_gate_w']
SMALL_SHARDED = ['b_conv_w', 'b_conv_b', 'b_lam', 'f_conv_w']
GRADS_AS_BF16 = MATMUL_WEIGHTS + ['b_w_a', 'b_w_i']
PACK_COLS = 1024


def _to_shards(full, axis):
    return jnp.stack(jnp.split(full, N_DEV, axis=axis))


def _from_shards(pieces, axis):
    return jnp.concatenate([pieces[d] for d in range(N_DEV)], axis=axis)


PIECE_ROWS = 16


def _piece_rows(size):
    rows = -(-size // PACK_COLS)
    return -(-rows // PIECE_ROWS) * PIECE_ROWS


def _pack_pieces(pieces, total_mult=PIECE_ROWS):
    lead = pieces[0].shape[:-1]
    blocks = []
    for pc in pieces:
        n = pc.shape[-1]
        rows = _piece_rows(n)
        pad = [(0, 0)] * len(lead) + [(0, rows * PACK_COLS - n)]
        blocks.append(jnp.pad(pc, pad).reshape(lead + (rows, PACK_COLS)))
    total = sum(b.shape[-2] for b in blocks)
    extra = -total % total_mult
    if extra:
        blocks.append(jnp.zeros(lead + (extra, PACK_COLS), pieces[0].dtype))
    return jnp.concatenate(blocks, axis=len(lead))


def _unpack_pieces(packed, shapes):
    lead = packed.shape[:-2]
    out, row = [], 0
    for shp in shapes:
        size = math.prod(shp)
        rows = _piece_rows(size)
        block = lax.slice_in_dim(packed, row, row + rows, axis=len(lead))
        flat = block.reshape(lead + (rows * PACK_COLS,))
        out.append(lax.slice_in_dim(flat, 0, size, axis=len(lead)).reshape(lead + tuple(shp)))
        row += rows
    return out


def _gather_weights(local, names, dtype, name):
    packed = _pack_pieces([local[n].astype(dtype).reshape(-1) for n in names])
    gathered = _all_gather(packed, name)
    pieces = _unpack_pieces(gathered, [local[n].shape for n in names])
    return {n: _from_shards(pc, SHARD_AXIS[n]) for n, pc in zip(names, pieces)}


def _mixer_a_fwd(tag, xb, w):
    qkv = _mm(xb, w["wqkv"], "nn", BF16, f"{tag}_qkv")
    fg = _mm(xb, w["wf"], "nn", F32, f"{tag}_fgproj")
    fg_rows = fg[:, :N_HEADS].T
    c_rows = _fgate_fwd(fg_rows, w["b_f"], f"{tag}_fgate")
    c_pairs = _rows_to_pairs(c_rows)
    qh, kh = _attn_prep(qkv, _rows_to_cols(c_rows), f"{tag}_attn_prep")
    ot, otb, lse_pairs = _attn_fwd_t(qh, kh, qkv[:, 2 * D_MODEL:].T, c_pairs, f"{tag}_attn")
    m = _mm(otb, w["wout"], "tn", F32, f"{tag}_oproj")
    return m, dict(qkv=qkv, qh=qh, kh=kh, fg_rows=fg_rows, c_pairs=c_pairs, ot=ot, otb=otb, lse_pairs=lse_pairs)


def _mixer_a_bwd(tag, dz, dzb, xb, w, s):
    t = xb.shape[0]
    dot = _mm(w["wout"], dzb, "nt", F32, f"{tag}_b_do")
    g_wout = _mm(s["otb"], dzb, "nn", F32, f"{tag}_b_dwout")
    d_pairs, dotb = _attn_delta_t(dot, s["ot"], f"{tag}_b_delta")
    dkt, dvt, dck_pairs, dq_aug = _attn_bwd_t(s["kh"], s["kh"].T, s["qh"].T, s["qkv"], dotb, s["c_pairs"],
                                              s["lse_pairs"], d_pairs, f"{tag}_b_attn")
    dq_aug = dq_aug.reshape(N_HEADS, LANES, t)
    dfg_rows, db_f = _fgate_bwd(dck_pairs[:, :2, :].reshape(N_HEADS, t), dq_aug[:, AUG_ONE, :], s["fg_rows"],
                                w["b_f"], f"{tag}_b_fgate")
    dqt = dq_aug[:, :HEAD_DIM, :].reshape(D_MODEL, t).astype(BF16)
    dqkv = jnp.concatenate([dqt, dkt, dvt], axis=0).T
    dfg = jnp.pad(dfg_rows.T, ((0, 0), (0, LANES - N_HEADS))).astype(BF16)
    dx = _mm(dqkv, w["wqkv"], "nt", F32, f"{tag}_b_dx_qkv", add=dz, add_scale=ALPHA, tk=3 * D_MODEL)
    dx = _mm(dfg, w["wf"], "nt", F32, f"{tag}_b_dx_fg", add=dx)
    g_wqkv = _mm(xb, dqkv, "tn", F32, f"{tag}_b_dwqkv")
    g_wf = _mm(xb, dfg, "tn", F32, f"{tag}_b_dwf")[:, :N_HEADS]
    grads = dict(a_w_in=jnp.concatenate([g_wqkv, g_wf], axis=1), a_b_f=db_f.reshape(N_HEADS), a_w_out=g_wout)
    return dx, grads


def _mixer_b_fwd(tag, xb, w):
    proj = _mm(xb, w["win"], "nn", F32, f"{tag}_proj")
    xc = _rg_conv_fwd(proj, w["conv_w"], w["conv_b"], f"{tag}_conv")
    a, u = _rg_gate_fwd(xc, w["wa"], w["ba"], w["wi"], w["bi"], w["lam"], f"{tag}_gate")
    h = _scan(a, u, f"{tag}_scan")
    y = _rg_out_fwd(h, proj, f"{tag}_out")
    m = _mm(y, w["wout"], "nn", F32, f"{tag}_oproj")
    return m, dict(proj=proj, xc=xc, a=a, h=h, y=y)


def _mixer_b_bwd(tag, dz, dzb, xb, w, s):
    dy = _mm(dzb, w["wout"], "nt", F32, f"{tag}_b_dy")
    g_wout = _mm(s["y"], dzb, "tn", F32, f"{tag}_b_dwout")
    dh, dgate = _rg_out_bwd(dy, s["h"], s["proj"], f"{tag}_b_out")
    g = _scan(_shift_up_one(s["a"], f"{tag}_b_shift"), dh, f"{tag}_b_scan", reverse=True)
    dxc, g_wa, g_wi, g_ba, g_bi, g_lam = _rg_gate_bwd(
        s["xc"], g, s["h"], w["wa"], w["ba"], w["wi"], w["bi"], w["lam"], f"{tag}_b_gate")
    dxp = _conv_bwd_x(dxc, w["conv_w"], f"{tag}_b_convx", BF16)
    g_cw, g_cb = _conv_bwd_w(s["proj"], dxc, CONV_B, f"{tag}_b_convw")
    dproj = jnp.concatenate([dxp, dgate], axis=1)
    dx = _mm(dproj, w["win"], "nt", F32, f"{tag}_b_dx", add=dz, add_scale=ALPHA, tk=2 * D_MODEL)
    g_win = _mm(xb, dproj, "tn", F32, f"{tag}_b_dwin")
    grads = dict(b_w_in=g_win, b_conv_w=g_cw, b_conv_b=g_cb.reshape(D_MODEL), b_w_a=g_wa,
                 b_b_a=g_ba.reshape(N_BLOCKS_B, BLOCK_B), b_w_i=g_wi, b_b_i=g_bi.reshape(N_BLOCKS_B, BLOCK_B),
                 b_lam=g_lam.reshape(D_MODEL), b_w_out=g_wout)
    return dx, grads


def _layer_fwd(i, x, xb, pb, w):
    tag = f"L{i}"
    mix = _mixer_a_fwd if i % 2 == 0 else _mixer_b_fwd
    m, sm = mix(tag, xb, w)
    x1, x1b, z1 = _ln_fwd(x, m, w["ln1_g"], w["ln1_b"], f"{tag}_ln1")
    h = _mm(x1b, w["wup"], "nn", F32, f"{tag}_ffn_up")
    y = _ffn_mid_fwd(h, w["fconv_w"], w["fconv_b"], f"{tag}_ffn_mid")
    ff = _mm(y, w["wdown"], "nn", F32, f"{tag}_ffn_down", tk=D_FF)
    x2, x2b, z2 = _ln_fwd(x1, ff, w["ln2_g"], w["ln2_b"], f"{tag}_ln2")
    gl = _mm(x2b, w["wg"], "nn", F32, f"{tag}_ple_gate")
    e = _mm(pb, w["wp"], "nn", F32, f"{tag}_ple_emb")
    x3, x3b = _ple_fwd(x2, gl, e, w["bg"], f"{tag}_ple")
    saved = dict(mixer=sm, xb=xb, x1b=x1b, z1=z1, h=h, y=y, x2b=x2b, z2=z2, gl=gl, e=e, pb=pb)
    return x3, x3b, saved


def _layer_bwd(i, dx3, w, s):
    tag = f"L{i}"
    dgl, de, g_bg = _ple_bwd(dx3, s["gl"], s["e"], w["bg"], f"{tag}_b_ple")
    g_wg = _mm(s["x2b"], dgl, "tn", F32, f"{tag}_b_dwg")
    g_wp = _mm(s["pb"], de, "tn", F32, f"{tag}_b_dwp")
    dx2 = _mm(dgl, w["wg"], "nt", F32, f"{tag}_b_dx2", add=dx3)
    dz2, dz2b, g_ln2g, g_ln2b = _ln_bwd(dx2, s["z2"], w["ln2_g"], f"{tag}_b_ln2")
    dy = _mm(dz2b, w["wdown"], "nt", F32, f"{tag}_b_dy")
    g_wdown = _mm(s["y"], dz2b, "tn", F32, f"{tag}_b_dwdown", tm=1408)
    dval, dgate, g_fcw, g_fcb = _ffn_mid_bwd_a(s["h"], dy, w["fconv_w"], w["fconv_b"], f"{tag}_b_ffn_mid")
    dhv = _conv_bwd_x(dval, w["fconv_w"][:, :D_FF], f"{tag}_b_convx_v", BF16)
    dhg = _conv_bwd_x(dgate, w["fconv_w"][:, D_FF:], f"{tag}_b_convx_g", BF16)
    dx1 = _mm(dhv, w["wup"][:, :D_FF], "nt", F32, f"{tag}_b_dx1_v", add=dz2, add_scale=ALPHA, tk=D_FF)
    dx1 = _mm(dhg, w["wup"][:, D_FF:], "nt", F32, f"{tag}_b_dx1_g", add=dx1, tk=D_FF)
    g_wup = jnp.concatenate([_mm(dhv, s["x1b"], "tn", F32, f"{tag}_b_dwup_v", tm=1408),
                             _mm(dhg, s["x1b"], "tn", F32, f"{tag}_b_dwup_g", tm=1408)], axis=0).T
    dz1, dz1b, g_ln1g, g_ln1b = _ln_bwd(dx1, s["z1"], w["ln1_g"], f"{tag}_b_ln1")
    mix_bwd = _mixer_a_bwd if i % 2 == 0 else _mixer_b_bwd
    dx, g_mix = mix_bwd(tag, dz1, dz1b, s["xb"], w, s["mixer"])
    grads = dict(f_w_up=g_wup, f_conv_w=g_fcw, f_conv_b=g_fcb.reshape(2 * D_FF), f_w_down=g_wdown,
                 ln1_g=g_ln1g.reshape(D_MODEL), ln1_b=g_ln1b.reshape(D_MODEL), ln2_g=g_ln2g.reshape(D_MODEL),
                 ln2_b=g_ln2b.reshape(D_MODEL), ple_w=g_wp, ple_gate_w=g_wg, ple_gate_b=g_bg.reshape(D_MODEL))
    return dx, g_mix, grads


def _layer_weights(i, full, rep):
    j = i // 2
    w = dict(ln1_g=rep["ln1_g"][i], ln1_b=rep["ln1_b"][i], ln2_g=rep["ln2_g"][i], ln2_b=rep["ln2_b"][i],
             wup=full["f_w_up"][i], fconv_w=full["f_conv_w"][i], fconv_b=rep["f_conv_b"][i],
             wdown=full["f_w_down"][i], wp=full["ple_w"][i], wg=full["ple_gate_w"][i], bg=rep["ple_gate_b"][i])
    if i % 2 == 0:
        w_in = full["a_w_in"][j]
        w.update(wqkv=w_in[:, :3 * D_MODEL],
                 wf=jnp.pad(w_in[:, 3 * D_MODEL:], ((0, 0), (0, LANES - N_HEADS))),
                 b_f=rep["a_b_f"][j], wout=full["a_w_out"][j])
    else:
        w.update(win=full["b_w_in"][j], conv_w=full["b_conv_w"][j], conv_b=full["b_conv_b"][j],
                 wa=rep["b_w_a"][j].astype(BF16), wi=rep["b_w_i"][j].astype(BF16),
                 ba=rep["b_b_a"][j].reshape(N_BLOCKS_B, 1, BLOCK_B), bi=rep["b_b_i"][j].reshape(N_BLOCKS_B, 1, BLOCK_B),
                 lam=full["b_lam"][j].reshape(N_BLOCKS_B, 1, BLOCK_B), wout=full["b_w_out"][j])
    return w


def _fwd_bwd(x, p, target, full, rep):
    weights = [_layer_weights(i, full, rep) for i in range(DEPTH)]
    xb = x.astype(BF16)
    pb = p.astype(BF16)
    saved = []
    for i in range(DEPTH):
        x, xb, s = _layer_fwd(i, x, xb, pb[i], weights[i])
        saved.append(s)
    dx, loss_local = _loss_head(x, target, "loss_head")

    per_layer = {n: [None] * (DEPTH if n.startswith(("f_", "ln", "ple")) else DEPTH // 2) for n in WEIGHTS}
    for i in reversed(range(DEPTH)):
        dx, g_mix, g_layer = _layer_bwd(i, dx, weights[i], saved[i])
        for n, g in g_layer.items():
            per_layer[n][i] = g
        for n, g in g_mix.items():
            per_layer[n][i // 2] = g
    return loss_local, dx, per_layer


def _train_step(x, p, target, local, moments_m, moments_v):
    full = _gather_weights(local, MATMUL_WEIGHTS, BF16, "gather_matmul_weights")
    full.update(_gather_weights(local, SMALL_SHARDED, F32, "gather_small_weights"))
    rep = {n: local[n] for n in WEIGHTS if SHARD_AXIS[n] is None}
    loss_local, dx, grads_layers = _fwd_bwd(x, p, target, full, rep)

    unpacked = [{} for _ in range(4)]
    for group, dtype, tag in ((GRADS_AS_BF16, BF16, "big"), ([n for n in WEIGHTS if n not in GRADS_AS_BF16], F32, "small")):
        pieces = []
        for n in group:
            parts = []
            for g in grads_layers[n]:
                g = g.astype(dtype)
                if SHARD_AXIS[n] is None:
                    parts.append(jnp.broadcast_to(g.reshape(1, -1), (N_DEV, g.size)))
                else:
                    parts.append(_to_shards(g, SHARD_AXIS[n] - 1).reshape(N_DEV, -1))
            pieces.append(jnp.concatenate(parts, axis=1))
        recv = _all_to_all(_pack_pieces(pieces, ADAM_ROWS), f"reduce_scatter_grads_{tag}")

        def pack_local(d):
            return _pack_pieces([d[n].astype(F32).reshape(-1) for n in group], ADAM_ROWS)

        outs = _adamw(recv, pack_local(local), pack_local(moments_m), pack_local(moments_v), f"adamw_{tag}")
        shapes = [local[n].shape for n in group]
        for dst, packed in zip(unpacked, outs):
            dst.update(zip(group, _unpack_pieces(packed, shapes)))
    return loss_local, dx, unpacked


def kernel(x, p, a_w_in, a_b_f, a_w_out, b_w_in, b_conv_w, b_conv_b, b_w_a, b_b_a, b_w_i, b_b_i, b_lam, b_w_out, f_w_up, f_conv_w, f_conv_b, f_w_down, ln1_g, ln1_b, ln2_g, ln2_b, ple_w, ple_gate_w, ple_gate_b, loss_target, m_a_w_in, m_a_b_f, m_a_w_out, m_b_w_in, m_b_conv_w, m_b_conv_b, m_b_w_a, m_b_b_a, m_b_w_i, m_b_b_i, m_b_lam, m_b_w_out, m_f_w_up, m_f_conv_w, m_f_conv_b, m_f_w_down, m_ln1_g, m_ln1_b, m_ln2_g, m_ln2_b, m_ple_w, m_ple_gate_w, m_ple_gate_b, v_a_w_in, v_a_b_f, v_a_w_out, v_b_w_in, v_b_conv_w, v_b_conv_b, v_b_w_a, v_b_b_a, v_b_w_i, v_b_b_i, v_b_lam, v_b_w_out, v_f_w_up, v_f_conv_w, v_f_conv_b, v_f_w_down, v_ln1_g, v_ln1_b, v_ln2_g, v_ln2_b, v_ple_w, v_ple_gate_w, v_ple_gate_b):
    given = dict(locals())
    local = {n: given[n] for n in WEIGHTS}
    mom_m = {n: given["m_" + n] for n in WEIGHTS}
    mom_v = {n: given["v_" + n] for n in WEIGHTS}
    t = x.shape[1]
    loss_local, dx, (grad, delta, new_m, new_v) = _train_step(
        x.reshape(t, D_MODEL), p.reshape(DEPTH, t, D_PLE), loss_target.reshape(t, D_MODEL), local, mom_m, mom_v)
    loss = lax.psum(loss_local, MESH_AXES)
    return (loss, dx.reshape(1, t, D_MODEL), *[grad[n] for n in WEIGHTS], *[delta[n] for n in WEIGHTS],
            *[new_m[n] for n in WEIGHTS], *[new_v[n] for n in WEIGHTS])
```

```python
import math

import jax
import jax.numpy as jnp
from jax import lax
from jax.experimental import pallas as pl
from jax.experimental.pallas import tpu as pltpu

F32 = jnp.float32
BF16 = jnp.bfloat16

D_MODEL = 1024
DEPTH = 4
N_HEADS = 16
HEAD_DIM = 64
N_PAIRS = N_HEADS // 2
N_BLOCKS_B = 8
BLOCK_B = 128
CONV_B = 4
LRU_C = 8.0
D_FF = 2816
CONV_F = 3
D_PLE = 256
LN_EPS = 1e-5
ALPHA = (2.0 * DEPTH) ** 0.25
ADAM_LR, ADAM_B1, ADAM_B2, ADAM_EPS, ADAM_WD, ADAM_STEP = 0.001, 0.9, 0.999, 1e-08, 0.01, 10
N_DEV = 8
MESH_AXES = ("x", "y", "c")

LANES = 128
SUBLANES = 8
VMEM_LIMIT_BYTES = 56 * 1024 * 1024
ATTN_FWD_BLOCK = 1024
ATTN_BWD_BLOCK = 512
ROW_TILE = 256
NEG_BIG = -1e30


def _cparams(*sem):
    return pltpu.CompilerParams(dimension_semantics=sem if sem else None, vmem_limit_bytes=VMEM_LIMIT_BYTES)


def _pick(n, pref):
    if n <= pref:
        return n
    best = None
    for t in range(LANES, pref + 1, LANES):
        if n % t == 0:
            best = t
    assert best is not None, (n, pref)
    return best


def _sigmoid(x):
    return 1.0 / (1.0 + jnp.exp(-x))


def _log1p(x):
    u = 1.0 + x
    d = u - 1.0
    return jnp.where(d == 0.0, x, jnp.log(u) * (x / jnp.where(d == 0.0, 1.0, d)))


def _expm1(x):
    u = jnp.exp(x)
    lu = jnp.log(u)
    return jnp.where(u == 1.0, x, (u - 1.0) * (x / jnp.where(u == 1.0, 1.0, lu)))


def _softplus(x):
    return jnp.maximum(x, 0.0) + _log1p(jnp.exp(-jnp.abs(x)))


_GELU_C = math.sqrt(2.0 / math.pi)


def _gelu(x):
    return 0.5 * x * (1.0 + jnp.tanh(_GELU_C * (x + 0.044715 * x * x * x)))


def _gelu_and_grad(x):
    t = jnp.tanh(_GELU_C * (x + 0.044715 * x * x * x))
    du = _GELU_C * (1.0 + 3.0 * 0.044715 * x * x)
    return 0.5 * x * (1.0 + t), 0.5 * (1.0 + t) + 0.5 * x * (1.0 - t * t) * du


_DOT_DIMS = {"nn": (((1,), (0,)), ((), ())), "nt": (((1,), (1,)), ((), ())), "tn": (((0,), (0,)), ((), ()))}


def _mm(a, b, mode, out_dtype, name, add=None, add_scale=1.0, tm=512, tn=1408, tk=1408):
    if mode == "nn":
        (m, k), (k2, n) = a.shape, b.shape
    elif mode == "nt":
        (m, k), (n, k2) = a.shape, b.shape
    else:
        (k, m), (k2, n) = a.shape, b.shape
    assert k == k2 and a.dtype == BF16 and b.dtype == BF16, (a.shape, b.shape, a.dtype, b.dtype)
    tm, tn, tk = _pick(m, tm), _pick(n, tn), _pick(k, tk)
    nk = k // tk
    dims = _DOT_DIMS[mode]

    def body(*refs):
        if add is None:
            a_ref, b_ref, o_ref, acc_ref = refs
        else:
            a_ref, b_ref, add_ref, o_ref, acc_ref = refs
        kk = pl.program_id(2)

        @pl.when(kk == 0)
        def _():
            acc_ref[...] = jnp.zeros_like(acc_ref)

        acc_ref[...] += lax.dot_general(a_ref[...], b_ref[...], dims, preferred_element_type=F32)

        @pl.when(kk == nk - 1)
        def _():
            r = acc_ref[...]
            if add is not None:
                r = r + add_scale * add_ref[...]
            o_ref[...] = r.astype(out_dtype)

    a_spec = (pl.BlockSpec((tk, tm), lambda j, i, kk: (kk, i)) if mode == "tn"
              else pl.BlockSpec((tm, tk), lambda j, i, kk: (i, kk)))
    b_spec = (pl.BlockSpec((tn, tk), lambda j, i, kk: (j, kk)) if mode == "nt"
              else pl.BlockSpec((tk, tn), lambda j, i, kk: (kk, j)))
    o_spec = pl.BlockSpec((tm, tn), lambda j, i, kk: (i, j))
    in_specs, args = [a_spec, b_spec], [a, b]
    if add is not None:
        assert add.shape == (m, n) and add.dtype == F32
        in_specs.append(o_spec)
        args.append(add)
    return pl.pallas_call(
        body, name=name, grid=(n // tn, m // tm, nk),
        in_specs=in_specs, out_specs=o_spec,
        out_shape=jax.ShapeDtypeStruct((m, n), out_dtype),
        scratch_shapes=[pltpu.VMEM((tm, tn), F32)],
        compiler_params=_cparams("parallel", "parallel", "arbitrary"),
    )(*args)


def _ln_fwd(x, m, g, b, name):
    t, d = x.shape
    bt = _pick(t, ROW_TILE)

    def body(x_ref, m_ref, g_ref, b_ref, y_ref, yb_ref, z_ref):
        z = ALPHA * x_ref[...] + m_ref[...]
        mu = jnp.mean(z, axis=-1, keepdims=True)
        zc = z - mu
        var = jnp.mean(zc * zc, axis=-1, keepdims=True)
        y = zc * lax.rsqrt(var + LN_EPS) * g_ref[...] + b_ref[...]
        y_ref[...] = y
        yb_ref[...] = y.astype(BF16)
        z_ref[...] = z

    row = pl.BlockSpec((bt, d), lambda i: (i, 0))
    vec = pl.BlockSpec((1, d), lambda i: (0, 0))
    return pl.pallas_call(
        body, name=name, grid=(t // bt,), in_specs=[row, row, vec, vec], out_specs=[row, row, row],
        out_shape=[jax.ShapeDtypeStruct((t, d), F32), jax.ShapeDtypeStruct((t, d), BF16),
                   jax.ShapeDtypeStruct((t, d), F32)],
        compiler_params=_cparams("parallel"),
    )(x, m, g.reshape(1, d), b.reshape(1, d))


def _ln_bwd(dy, z, g, name):
    t, d = dy.shape
    bt = _pick(t, ROW_TILE)

    def body(dy_ref, z_ref, g_ref, dz_ref, dzb_ref, dg_ref, db_ref):
        @pl.when(pl.program_id(0) == 0)
        def _():
            dg_ref[...] = jnp.zeros_like(dg_ref)
            db_ref[...] = jnp.zeros_like(db_ref)

        z = z_ref[...]
        dyv = dy_ref[...]
        mu = jnp.mean(z, axis=-1, keepdims=True)
        zc = z - mu
        var = jnp.mean(zc * zc, axis=-1, keepdims=True)
        rstd = lax.rsqrt(var + LN_EPS)
        xhat = zc * rstd
        dxh = dyv * g_ref[...]
        m1 = jnp.mean(dxh, axis=-1, keepdims=True)
        m2 = jnp.mean(dxh * xhat, axis=-1, keepdims=True)
        dz = rstd * (dxh - m1 - xhat * m2)
        dz_ref[...] = dz
        dzb_ref[...] = dz.astype(BF16)
        dg_ref[...] += jnp.sum(dyv * xhat, axis=0, keepdims=True)
        db_ref[...] += jnp.sum(dyv, axis=0, keepdims=True)

    row = pl.BlockSpec((bt, d), lambda i: (i, 0))
    vec = pl.BlockSpec((1, d), lambda i: (0, 0))
    return pl.pallas_call(
        body, name=name, grid=(t // bt,), in_specs=[row, row, vec], out_specs=[row, row, vec, vec],
        out_shape=[jax.ShapeDtypeStruct((t, d), F32), jax.ShapeDtypeStruct((t, d), BF16),
                   jax.ShapeDtypeStruct((1, d), F32), jax.ShapeDtypeStruct((1, d), F32)],
        compiler_params=_cparams("arbitrary"),
    )(dy, z, g.reshape(1, d))


def _rows_iota(shape):
    return lax.broadcasted_iota(jnp.int32, shape, 0)


def _shift_down(x, halo, s):
    if s == 0:
        return x
    rolled = pltpu.roll(x, s, axis=0)
    first = jnp.where(_rows_iota((SUBLANES, x.shape[1])) < s, pltpu.roll(halo, s, axis=0), rolled[:SUBLANES])
    return jnp.concatenate([first, rolled[SUBLANES:]], axis=0)


def _shift_up(x, halo, s):
    if s == 0:
        return x
    n = x.shape[0]
    rolled = pltpu.roll(x, n - s, axis=0)
    last = jnp.where(_rows_iota((SUBLANES, x.shape[1])) < SUBLANES - s, rolled[n - SUBLANES:],
                     pltpu.roll(halo, SUBLANES - s, axis=0))
    return jnp.concatenate([rolled[:n - SUBLANES], last], axis=0)


def _prev_halo_spec(bt, cb, col_off=0):
    r = bt // SUBLANES
    return pl.BlockSpec((SUBLANES, cb), lambda i, j: (jnp.maximum(i * r - 1, 0), j + col_off))


def _next_halo_spec(bt, cb, t, col_off=0):
    r = bt // SUBLANES
    last = t // SUBLANES - 1
    return pl.BlockSpec((SUBLANES, cb), lambda i, j: (jnp.minimum((i + 1) * r, last), j + col_off))


def _causal_conv(x, halo, w_ref, ksize):
    acc = None
    for j in range(ksize):
        term = w_ref[j:j + 1, :] * _shift_down(x, halo, ksize - 1 - j)
        acc = term if acc is None else acc + term
    return acc


def _anticausal_conv(y, halo, w_ref, ksize):
    acc = None
    for j in range(ksize):
        term = w_ref[j:j + 1, :] * _shift_up(y, halo, ksize - 1 - j)
        acc = term if acc is None else acc + term
    return acc


def _ffn_mid_fwd(h, cw, cb_, name):
    t = h.shape[0]
    bt, cb = _pick(t, ROW_TILE), _pick(D_FF, 1408)
    nc = D_FF // cb

    def body(hv_ref, hvh_ref, hg_ref, hgh_ref, wv_ref, wg_ref, bv_ref, bg_ref, y_ref):
        keep = (pl.program_id(0) > 0).astype(F32)
        val = _causal_conv(hv_ref[...], hvh_ref[...] * keep, wv_ref, CONV_F) + bv_ref[...]
        gate = _causal_conv(hg_ref[...], hgh_ref[...] * keep, wg_ref, CONV_F) + bg_ref[...]
        y_ref[...] = (_gelu(gate) * val).astype(BF16)

    tile_v = pl.BlockSpec((bt, cb), lambda i, j: (i, j))
    tile_g = pl.BlockSpec((bt, cb), lambda i, j: (i, j + nc))
    wv = pl.BlockSpec((CONV_F, cb), lambda i, j: (0, j))
    wg = pl.BlockSpec((CONV_F, cb), lambda i, j: (0, j + nc))
    bv = pl.BlockSpec((1, cb), lambda i, j: (0, j))
    bg = pl.BlockSpec((1, cb), lambda i, j: (0, j + nc))
    cb2 = cb_.reshape(1, 2 * D_FF)
    return pl.pallas_call(
        body, name=name, grid=(t // bt, nc),
        in_specs=[tile_v, _prev_halo_spec(bt, cb), tile_g, _prev_halo_spec(bt, cb, nc), wv, wg, bv, bg],
        out_specs=tile_v, out_shape=jax.ShapeDtypeStruct((t, D_FF), BF16),
        compiler_params=_cparams("parallel", "parallel"),
    )(h, h, h, h, cw, cw, cb2, cb2)


def _ffn_mid_bwd_a(h, dy, cw, cb_, name):
    t = h.shape[0]
    bt, cb = _pick(t, ROW_TILE), _pick(D_FF, 1408)
    nc = D_FF // cb

    def body(hv_ref, hvh_ref, hg_ref, hgh_ref, dy_ref, wv_ref, wg_ref, bv_ref, bg_ref,
             dv_ref, dg_ref, dwv_ref, dwg_ref, dbv_ref, dbg_ref):
        @pl.when(pl.program_id(1) == 0)
        def _():
            for r in (dwv_ref, dwg_ref, dbv_ref, dbg_ref):
                r[...] = jnp.zeros_like(r)

        keep = (pl.program_id(1) > 0).astype(F32)
        hv, hvh = hv_ref[...], hvh_ref[...] * keep
        hg, hgh = hg_ref[...], hgh_ref[...] * keep
        val = _causal_conv(hv, hvh, wv_ref, CONV_F) + bv_ref[...]
        gate = _causal_conv(hg, hgh, wg_ref, CONV_F) + bg_ref[...]
        gl, glg = _gelu_and_grad(gate)
        dyv = dy_ref[...]
        dval = dyv * gl
        dgate = dyv * val * glg
        dv_ref[...] = dval
        dg_ref[...] = dgate
        dbv_ref[...] += jnp.sum(dval, axis=0, keepdims=True)
        dbg_ref[...] += jnp.sum(dgate, axis=0, keepdims=True)
        for j in range(CONV_F):
            s = CONV_F - 1 - j
            dwv_ref[j:j + 1, :] += jnp.sum(dval * _shift_down(hv, hvh, s), axis=0, keepdims=True)
            dwg_ref[j:j + 1, :] += jnp.sum(dgate * _shift_down(hg, hgh, s), axis=0, keepdims=True)

    tile_v = pl.BlockSpec((bt, cb), lambda j, i: (i, j))
    tile_g = pl.BlockSpec((bt, cb), lambda j, i: (i, j + nc))
    r = bt // SUBLANES
    halo_v = pl.BlockSpec((SUBLANES, cb), lambda j, i: (jnp.maximum(i * r - 1, 0), j))
    halo_g = pl.BlockSpec((SUBLANES, cb), lambda j, i: (jnp.maximum(i * r - 1, 0), j + nc))
    wv = pl.BlockSpec((CONV_F, cb), lambda j, i: (0, j))
    wg = pl.BlockSpec((CONV_F, cb), lambda j, i: (0, j + nc))
    bv = pl.BlockSpec((1, cb), lambda j, i: (0, j))
    bg = pl.BlockSpec((1, cb), lambda j, i: (0, j + nc))
    cb2 = cb_.reshape(1, 2 * D_FF)
    dv, dg, dwv, dwg, dbv, dbg = pl.pallas_call(
        body, name=name, grid=(nc, t // bt),
        in_specs=[tile_v, halo_v, tile_g, halo_g, tile_v, wv, wg, bv, bg],
        out_specs=[tile_v, tile_v, wv, wv, bv, bv],
        out_shape=[jax.ShapeDtypeStruct((t, D_FF), F32), jax.ShapeDtypeStruct((t, D_FF), F32),
                   jax.ShapeDtypeStruct((CONV_F, D_FF), F32), jax.ShapeDtypeStruct((CONV_F, D_FF), F32),
                   jax.ShapeDtypeStruct((1, D_FF), F32), jax.ShapeDtypeStruct((1, D_FF), F32)],
        compiler_params=_cparams("parallel", "arbitrary"),
    )(h, h, h, h, dy, cw, cw, cb2, cb2)
    return dv, dg, jnp.concatenate([dwv, dwg], axis=1), jnp.concatenate([dbv, dbg], axis=1)


def _conv_bwd_x(dy, cw, name, out_dtype):
    t, c = dy.shape
    ksize = cw.shape[0]
    bt, cb = _pick(t, ROW_TILE), _pick(c, 1408)

    def body(dy_ref, halo_ref, w_ref, o_ref):
        keep = (pl.program_id(0) < t // bt - 1).astype(F32)
        o_ref[...] = _anticausal_conv(dy_ref[...], halo_ref[...] * keep, w_ref, ksize).astype(out_dtype)

    tile = pl.BlockSpec((bt, cb), lambda i, j: (i, j))
    return pl.pallas_call(
        body, name=name, grid=(t // bt, c // cb),
        in_specs=[tile, _next_halo_spec(bt, cb, t), pl.BlockSpec((ksize, cb), lambda i, j: (0, j))],
        out_specs=tile, out_shape=jax.ShapeDtypeStruct((t, c), out_dtype),
        compiler_params=_cparams("parallel", "parallel"),
    )(dy, dy, cw)


def _ple_fwd(x2, g, e, bg, name):
    t, d = x2.shape
    bt = _pick(t, ROW_TILE)

    def body(x_ref, g_ref, e_ref, b_ref, y_ref, yb_ref):
        y = x_ref[...] + _sigmoid(g_ref[...] + b_ref[...]) * e_ref[...]
        y_ref[...] = y
        yb_ref[...] = y.astype(BF16)

    row = pl.BlockSpec((bt, d), lambda i: (i, 0))
    vec = pl.BlockSpec((1, d), lambda i: (0, 0))
    return pl.pallas_call(
        body, name=name, grid=(t // bt,), in_specs=[row, row, row, vec], out_specs=[row, row],
        out_shape=[jax.ShapeDtypeStruct((t, d), F32), jax.ShapeDtypeStruct((t, d), BF16)],
        compiler_params=_cparams("parallel"),
    )(x2, g, e, bg.reshape(1, d))


def _ple_bwd(dx3, g, e, bg, name):
    t, d = dx3.shape
    bt = _pick(t, ROW_TILE)

    def body(dx_ref, g_ref, e_ref, b_ref, dg_ref, de_ref, db_ref):
        @pl.when(pl.program_id(0) == 0)
        def _():
            db_ref[...] = jnp.zeros_like(db_ref)

        dx = dx_ref[...]
        gate = _sigmoid(g_ref[...] + b_ref[...])
        dg = dx * e_ref[...] * gate * (1.0 - gate)
        dg_ref[...] = dg.astype(BF16)
        de_ref[...] = (dx * gate).astype(BF16)
        db_ref[...] += jnp.sum(dg, axis=0, keepdims=True)

    row = pl.BlockSpec((bt, d), lambda i: (i, 0))
    vec = pl.BlockSpec((1, d), lambda i: (0, 0))
    return pl.pallas_call(
        body, name=name, grid=(t // bt,), in_specs=[row, row, row, vec], out_specs=[row, row, vec],
        out_shape=[jax.ShapeDtypeStruct((t, d), BF16), jax.ShapeDtypeStruct((t, d), BF16),
                   jax.ShapeDtypeStruct((1, d), F32)],
        compiler_params=_cparams("arbitrary"),
    )(dx3, g, e, bg.reshape(1, d))


def _loss_head(y, target, name):
    t, d = y.shape
    bt = _pick(t, ROW_TILE)

    def body(y_ref, t_ref, dy_ref, l_ref, acc_ref):
        @pl.when(pl.program_id(0) == 0)
        def _():
            acc_ref[...] = jnp.zeros_like(acc_ref)

        err = y_ref[...] - t_ref[...]
        dy_ref[...] = err * (1.0 / d)
        acc_ref[...] += jnp.sum(err * err, axis=0, keepdims=True)

        @pl.when(pl.program_id(0) == t // bt - 1)
        def _():
            l_ref[...] = jnp.full(l_ref.shape, (0.5 / d) * jnp.sum(acc_ref[...]), F32)

    row = pl.BlockSpec((bt, d), lambda i: (i, 0))
    dy, l = pl.pallas_call(
        body, name=name, grid=(t // bt,), in_specs=[row, row],
        out_specs=[row, pl.BlockSpec((SUBLANES, LANES), lambda i: (0, 0))],
        out_shape=[jax.ShapeDtypeStruct((t, d), F32), jax.ShapeDtypeStruct((SUBLANES, LANES), F32)],
        scratch_shapes=[pltpu.VMEM((1, d), F32)],
        compiler_params=_cparams("arbitrary"),
    )(y, target)
    return dy, l[0, 0]


def _split3(x):
    hi = x.astype(BF16)
    r1 = x - hi.astype(F32)
    mid = r1.astype(BF16)
    lo = (r1 - mid.astype(F32)).astype(BF16)
    return hi, mid, lo


def _tri_dot(x, tri):
    hi, mid, lo = _split3(x)
    dims = (((1,), (0,)), ((), ()))
    return (lax.dot_general(hi, tri, dims, preferred_element_type=F32)
            + lax.dot_general(mid, tri, dims, preferred_element_type=F32)
            + lax.dot_general(lo, tri, dims, preferred_element_type=F32))


def _fgate_fwd(fg_rows, b_f, name):
    hh, t = fg_rows.shape
    bt = _pick(t, 512)

    def body(fg_ref, b_ref, c_ref, carry_ref):
        @pl.when(pl.program_id(0) == 0)
        def _():
            carry_ref[...] = jnp.zeros_like(carry_ref)

        xx = fg_ref[...] + b_ref[...]
        logf = jnp.minimum(xx, 0.0) - _log1p(jnp.exp(-jnp.abs(xx)))
        r = lax.broadcasted_iota(jnp.int32, (bt, bt), 0)
        c = lax.broadcasted_iota(jnp.int32, (bt, bt), 1)
        tri = (r <= c).astype(BF16)
        cs = _tri_dot(logf, tri) + carry_ref[...]
        c_ref[...] = cs
        carry_ref[...] = cs[:, bt - 1:bt]

    return pl.pallas_call(
        body, name=name, grid=(t // bt,),
        in_specs=[pl.BlockSpec((hh, bt), lambda i: (0, i)), pl.BlockSpec((hh, 1), lambda i: (0, 0))],
        out_specs=pl.BlockSpec((hh, bt), lambda i: (0, i)),
        out_shape=jax.ShapeDtypeStruct((hh, t), F32),
        scratch_shapes=[pltpu.VMEM((hh, 1), F32)],
        compiler_params=_cparams("arbitrary"),
    )(fg_rows, b_f.reshape(hh, 1))


def _fgate_bwd(dck_rows, dcq_rows, fg_rows, b_f, name):
    hh, t = fg_rows.shape
    bt = _pick(t, 512)
    nb = t // bt

    def body(dc_ref, dcq_ref, fg_ref, b_ref, dfg_ref, db_ref, carry_ref):
        @pl.when(pl.program_id(0) == 0)
        def _():
            carry_ref[...] = jnp.zeros_like(carry_ref)
            db_ref[...] = jnp.zeros_like(db_ref)

        r = lax.broadcasted_iota(jnp.int32, (bt, bt), 0)
        c = lax.broadcasted_iota(jnp.int32, (bt, bt), 1)
        tri = (r >= c).astype(BF16)
        dlogf = _tri_dot(dc_ref[...] + dcq_ref[...], tri) + carry_ref[...]
        carry_ref[...] = dlogf[:, 0:1]
        xx = fg_ref[...] + b_ref[...]
        dfg = dlogf * _sigmoid(-xx)
        dfg_ref[...] = dfg
        db_ref[...] += jnp.sum(dfg, axis=1, keepdims=True)

    blk = pl.BlockSpec((hh, bt), lambda i: (0, nb - 1 - i))
    vec = pl.BlockSpec((hh, 1), lambda i: (0, 0))
    return pl.pallas_call(
        body, name=name, grid=(nb,), in_specs=[blk, blk, blk, vec], out_specs=[blk, vec],
        out_shape=[jax.ShapeDtypeStruct((hh, t), F32), jax.ShapeDtypeStruct((hh, 1), F32)],
        scratch_shapes=[pltpu.VMEM((hh, 1), F32)],
        compiler_params=_cparams("arbitrary"),
    )(dck_rows, dcq_rows, fg_rows, b_f.reshape(hh, 1))


def _rows_to_cols(r):
    hh, t = r.shape
    return jnp.repeat(r.reshape(hh // 2, 2, t).transpose(0, 2, 1), HEAD_DIM, axis=-1)


def _rows_to_pairs(r):
    hh, t = r.shape
    return jnp.pad(r.reshape(hh // 2, 2, t), ((0, 0), (0, SUBLANES - 2), (0, 0)))


AUG_C = HEAD_DIM
AUG_ONE = HEAD_DIM + 3


def _attn_prep(qkv, c_cols, name):
    t = qkv.shape[0]
    bt = _pick(t, ATTN_FWD_BLOCK)
    scale = 1.0 / math.sqrt(HEAD_DIM)

    def body(q_ref, k_ref, v_ref, c_ref, qh_ref, kh_ref, qt_ref, kt_ref, vt_ref):
        vt_ref[...] = v_ref[...].astype(F32).T.astype(BF16)
        lane = lax.broadcasted_iota(jnp.int32, (bt, LANES), 1)
        q2 = q_ref[...].astype(F32)
        k2 = k_ref[...].astype(F32) * scale
        c2 = c_ref[...]
        parts = [p.astype(F32) for p in _split3(c2 - c2[0:1, :])]
        swapped = [pltpu.roll(p, HEAD_DIM, axis=1) for p in parts]
        for a in (0, 1):
            qa = q2 if a == 0 else pltpu.roll(q2, HEAD_DIM, axis=1)
            ka = k2 if a == 0 else pltpu.roll(k2, HEAD_DIM, axis=1)
            hi, mid, lo = swapped if a == 0 else parts
            kaug = jnp.where(lane < HEAD_DIM, ka,
                             jnp.where(lane == AUG_C, hi,
                                       jnp.where(lane == AUG_C + 1, mid,
                                                 jnp.where(lane == AUG_C + 2, lo,
                                                           jnp.where(lane == AUG_ONE, 1.0, 0.0)))))
            qaug = jnp.where(lane < HEAD_DIM, qa, jnp.where(lane < AUG_ONE, -1.0, 0.0))
            qh_ref[:, a * LANES:(a + 1) * LANES] = qaug.astype(BF16)
            kh_ref[:, a * LANES:(a + 1) * LANES] = kaug.astype(BF16)
            qt_ref[a * LANES:(a + 1) * LANES, :] = qaug.T.astype(BF16)
            kt_ref[a * LANES:(a + 1) * LANES, :] = kaug.T.astype(BF16)

    out = pl.BlockSpec((bt, 2 * LANES), lambda i, hp: (i, hp))
    out_t = pl.BlockSpec((2 * LANES, bt), lambda i, hp: (hp, i))
    shape = jax.ShapeDtypeStruct((t, N_HEADS * LANES), BF16)
    shape_t = jax.ShapeDtypeStruct((N_HEADS * LANES, t), BF16)
    return pl.pallas_call(
        body, name=name, grid=(t // bt, N_PAIRS),
        in_specs=[pl.BlockSpec((bt, LANES), lambda i, hp: (i, hp)),
                  pl.BlockSpec((bt, LANES), lambda i, hp: (i, N_PAIRS + hp)),
                  pl.BlockSpec((bt, LANES), lambda i, hp: (i, 2 * N_PAIRS + hp)),
                  pl.BlockSpec((None, bt, LANES), lambda i, hp: (hp, i, 0))],
        out_specs=[out, out, out_t, out_t, pl.BlockSpec((LANES, bt), lambda i, hp: (hp, i))],
        out_shape=[shape, shape, shape_t, shape_t, jax.ShapeDtypeStruct((D_MODEL, t), BF16)],
        compiler_params=_cparams("parallel", "parallel"),
    )(qkv, qkv, qkv, c_cols)


def _block_scalar(c_ref, a, start):
    return c_ref[a:a + 1, pl.ds(start, LANES)][:, 0:1]


def _attn_fwd_t(qh, kh, vt, c_pairs, name):
    t = qh.shape[0]
    bq = _pick(t, ATTN_FWD_BLOCK)
    nq = t // bq

    def body(q_ref, k_ref, vt_ref, c_ref, ot_ref, otb_ref, lse_ref, acc_ref):
        i = pl.program_id(1)
        q0 = pl.multiple_of(i * bq, bq)
        qs = (q_ref[:, 0:LANES], q_ref[:, LANES:2 * LANES])
        cq = [_block_scalar(c_ref, a, q0) for a in (0, 1)]
        acc_ref[...] = jnp.zeros_like(acc_ref)
        keep = _rows_iota((bq, bq)) <= lax.broadcasted_iota(jnp.int32, (bq, bq), 1)

        def step(j, carry, masked):
            k0 = pl.multiple_of(j * bq, bq)
            kb = k_ref[pl.ds(k0, bq), :]
            new = []
            for a in (0, 1):
                m_old, l_old = carry[2 * a], carry[2 * a + 1]
                st = lax.dot_general(kb[:, a * LANES:(a + 1) * LANES], qs[a], _DOT_DIMS["nt"],
                                     preferred_element_type=F32)
                if masked:
                    st = jnp.where(keep, st, NEG_BIG)
                sigma = cq[a] - _block_scalar(c_ref, a, k0)
                m_new = jnp.maximum(m_old, jnp.max(st, axis=0, keepdims=True) + sigma)
                pt = jnp.exp(st - (m_new - sigma))
                alpha = jnp.exp(m_old - m_new)
                l_new = alpha * l_old + jnp.sum(pt, axis=0, keepdims=True)
                vta = vt_ref[a * HEAD_DIM:(a + 1) * HEAD_DIM, pl.ds(k0, bq)]
                acc_ref[a] = alpha * acc_ref[a] + lax.dot_general(
                    vta, pt.astype(BF16), _DOT_DIMS["nn"], preferred_element_type=F32)
                new += [m_new, l_new]
            return tuple(new)

        neg = jnp.full((1, bq), NEG_BIG, F32)
        zero = jnp.zeros((1, bq), F32)
        carry = lax.fori_loop(0, i, lambda j, c: step(j, c, False), (neg, zero, neg, zero))
        m_a, l_a, m_b, l_b = step(i, carry, True)
        ot = jnp.concatenate([acc_ref[0] / l_a, acc_ref[1] / l_b], axis=0)
        ot_ref[...] = ot
        otb_ref[...] = ot.astype(BF16)
        lse_ref[...] = jnp.zeros_like(lse_ref)
        lse_ref[0:1, :] = m_a + jnp.log(l_a)
        lse_ref[1:2, :] = m_b + jnp.log(l_b)

    rows = pl.BlockSpec((None, SUBLANES, bq), lambda hp, i: (hp, 0, i))
    otile = pl.BlockSpec((LANES, bq), lambda hp, i: (hp, i))
    return pl.pallas_call(
        body, name=name, grid=(N_PAIRS, nq),
        in_specs=[pl.BlockSpec((bq, 2 * LANES), lambda hp, i: (i, hp)),
                  pl.BlockSpec((t, 2 * LANES), lambda hp, i: (0, hp)),
                  pl.BlockSpec((LANES, t), lambda hp, i: (hp, 0)),
                  pl.BlockSpec((None, SUBLANES, t), lambda hp, i: (hp, 0, 0))],
        out_specs=[otile, otile, rows],
        out_shape=[jax.ShapeDtypeStruct((D_MODEL, t), F32), jax.ShapeDtypeStruct((D_MODEL, t), BF16),
                   jax.ShapeDtypeStruct((N_PAIRS, SUBLANES, t), F32)],
        scratch_shapes=[pltpu.VMEM((2, HEAD_DIM, bq), F32)],
        compiler_params=_cparams("parallel", "arbitrary"),
    )(qh, kh, vt, c_pairs)


def _attn_delta_t(dot, ot, name):
    t = dot.shape[1]
    bt = _pick(t, 512)

    def body(do_ref, o_ref, d_ref, dob_ref):
        dob = do_ref[...].astype(BF16)
        prod = dob.astype(F32) * o_ref[...]
        d_ref[...] = jnp.zeros_like(d_ref)
        d_ref[0:1, :] = jnp.sum(prod[0:HEAD_DIM], axis=0, keepdims=True)
        d_ref[1:2, :] = jnp.sum(prod[HEAD_DIM:], axis=0, keepdims=True)
        dob_ref[...] = dob

    tile = pl.BlockSpec((LANES, bt), lambda hp, i: (hp, i))
    return pl.pallas_call(
        body, name=name, grid=(N_PAIRS, t // bt), in_specs=[tile, tile],
        out_specs=[pl.BlockSpec((None, SUBLANES, bt), lambda hp, i: (hp, 0, i)), tile],
        out_shape=[jax.ShapeDtypeStruct((N_PAIRS, SUBLANES, t), F32), jax.ShapeDtypeStruct((D_MODEL, t), BF16)],
        compiler_params=_cparams("parallel", "parallel"),
    )(dot, ot)


def _attn_bwd_t(kh, kt, qt, qkv, dotb, c_pairs, lse_pairs, d_pairs, name):
    t = kh.shape[0]
    bk = _pick(t, ATTN_BWD_BLOCK)
    nk = t // bk
    ref_tile = _pick(t, ATTN_FWD_BLOCK)
    assert ref_tile % bk == 0
    scale = 1.0 / math.sqrt(HEAD_DIM)

    def ref_start(start):
        return pl.multiple_of((start // ref_tile) * ref_tile, ref_tile)

    def body(k_ref, kt_ref, v_ref, qt_ref, dot_ref, c_ref, lse_ref, d_ref,
             dk_ref, dv_ref, dc_ref, dq_hbm, dka_ref, dva_ref, dqa_ref, sem):
        hp, j = pl.program_id(0), pl.program_id(1)
        k0 = pl.multiple_of(j * bk, bk)
        ks = (k_ref[:, 0:LANES], k_ref[:, LANES:2 * LANES])
        vb = v_ref[...]
        ck = [_block_scalar(c_ref, a, ref_start(k0)) for a in (0, 1)]
        dka_ref[...] = jnp.zeros_like(dka_ref)
        dva_ref[...] = jnp.zeros_like(dva_ref)

        @pl.when(j == 0)
        def _():
            dqa_ref[...] = jnp.zeros_like(dqa_ref)

        keep = _rows_iota((bk, bk)) <= lax.broadcasted_iota(jnp.int32, (bk, bk), 1)
        top = _rows_iota((LANES, bk)) < HEAD_DIM

        def step(i, masked):
            q0 = pl.multiple_of(i * bk, bk)
            dot2 = dot_ref[:, pl.ds(q0, bk)]
            zero = jnp.zeros_like(dot2)
            for a in (0, 1):
                qta = qt_ref[a * LANES:(a + 1) * LANES, pl.ds(q0, bk)]
                st = lax.dot_general(ks[a], qta, _DOT_DIMS["nn"], preferred_element_type=F32)
                sigma = _block_scalar(c_ref, a, ref_start(q0)) - ck[a]
                pt = jnp.exp(st - (lse_ref[a:a + 1, pl.ds(q0, bk)] - sigma))
                if masked:
                    pt = jnp.where(keep, pt, 0.0)
                dota = jnp.where(top, dot2, zero) if a == 0 else jnp.where(top, zero, dot2)
                dpt = lax.dot_general(vb, dota, _DOT_DIMS["nn"], preferred_element_type=F32)
                dstb = (pt * (dpt - d_ref[a:a + 1, pl.ds(q0, bk)])).astype(BF16)
                dva_ref[a] += lax.dot_general(dot2[a * HEAD_DIM:(a + 1) * HEAD_DIM, :], pt.astype(BF16),
                                              _DOT_DIMS["nt"], preferred_element_type=F32)
                dka_ref[a] += lax.dot_general(qta, dstb, _DOT_DIMS["nt"], preferred_element_type=F32)
                dqa_ref[a, :, pl.ds(q0, bk)] += lax.dot_general(
                    kt_ref[a * LANES:(a + 1) * LANES, :], dstb, _DOT_DIMS["nn"], preferred_element_type=F32)

        def loop_body(i, carry):
            step(i, False)
            return carry

        step(j, True)
        lax.fori_loop(j + 1, nk, loop_body, 0)
        dk_ref[...] = (jnp.concatenate([dka_ref[0, 0:HEAD_DIM, :], dka_ref[1, 0:HEAD_DIM, :]], axis=0)
                       * scale).astype(BF16)
        dv_ref[...] = jnp.concatenate([dva_ref[0], dva_ref[1]], axis=0).astype(BF16)
        dc_ref[...] = jnp.zeros_like(dc_ref)
        dc_ref[0:1, :] = dka_ref[0, AUG_C:AUG_C + 1, :]
        dc_ref[1:2, :] = dka_ref[1, AUG_C:AUG_C + 1, :]

        @pl.when(j == nk - 1)
        def _():
            for a in (0, 1):
                row0 = pl.multiple_of((2 * hp + a) * LANES, LANES)
                cp = pltpu.make_async_copy(dqa_ref.at[a], dq_hbm.at[pl.ds(row0, LANES), :], sem)
                cp.start()
                cp.wait()

    once = pl.Buffered(1)
    rows = pl.BlockSpec((None, SUBLANES, t), lambda hp, j: (hp, 0, 0))
    kv_out = pl.BlockSpec((LANES, bk), lambda hp, j: (hp, j))
    return pl.pallas_call(
        body, name=name, grid=(N_PAIRS, nk),
        in_specs=[pl.BlockSpec((bk, 2 * LANES), lambda hp, j: (j, hp)),
                  pl.BlockSpec((2 * LANES, bk), lambda hp, j: (hp, j)),
                  pl.BlockSpec((bk, LANES), lambda hp, j: (j, 2 * N_PAIRS + hp)),
                  pl.BlockSpec((2 * LANES, t), lambda hp, j: (hp, 0), pipeline_mode=once),
                  pl.BlockSpec((LANES, t), lambda hp, j: (hp, 0), pipeline_mode=once),
                  rows, rows, rows],
        out_specs=[kv_out, kv_out, pl.BlockSpec((None, SUBLANES, bk), lambda hp, j: (hp, 0, j)),
                   pl.BlockSpec(memory_space=pltpu.HBM)],
        out_shape=[jax.ShapeDtypeStruct((D_MODEL, t), BF16), jax.ShapeDtypeStruct((D_MODEL, t), BF16),
                   jax.ShapeDtypeStruct((N_PAIRS, SUBLANES, t), F32),
                   jax.ShapeDtypeStruct((N_HEADS * LANES, t), F32)],
        scratch_shapes=[pltpu.VMEM((2, LANES, bk), F32), pltpu.VMEM((2, HEAD_DIM, bk), F32),
                        pltpu.VMEM((2, LANES, t), F32), pltpu.SemaphoreType.DMA],
        compiler_params=_cparams("arbitrary", "arbitrary"),
    )(kh, kt, qkv, qt, dotb, c_pairs, lse_pairs, d_pairs)


def _scan(a, u, name, reverse=False):
    t, c = a.shape
    bt, cb = _pick(t, ROW_TILE), _pick(c, 1024)
    nt = t // bt
    ngroups = bt // SUBLANES

    def body(a_ref, u_ref, h_ref, carry_ref, as_ref, us_ref):
        @pl.when(pl.program_id(1) == 0)
        def _():
            carry_ref[...] = jnp.zeros_like(carry_ref)

        av, uv = a_ref[...], u_ref[...]
        sub = _rows_iota((bt, cb)) % SUBLANES
        for s in (1, 2, 4):
            if reverse:
                a_sh, u_sh = pltpu.roll(av, bt - s, axis=0), pltpu.roll(uv, bt - s, axis=0)
                valid = sub < SUBLANES - s
            else:
                a_sh, u_sh = pltpu.roll(av, s, axis=0), pltpu.roll(uv, s, axis=0)
                valid = sub >= s
            uv = jnp.where(valid, uv + av * u_sh, uv)
            av = jnp.where(valid, av * a_sh, av)
        as_ref[...] = av
        us_ref[...] = uv
        edge = 0 if reverse else SUBLANES - 1
        pick = _rows_iota((SUBLANES, cb)) == edge

        def group(gi, carry):
            g = (ngroups - 1 - gi) if reverse else gi
            r0 = pl.multiple_of(g * SUBLANES, SUBLANES)
            h8 = us_ref[pl.ds(r0, SUBLANES), :] + as_ref[pl.ds(r0, SUBLANES), :] * carry
            h_ref[pl.ds(r0, SUBLANES), :] = h8
            return jnp.sum(jnp.where(pick, h8, 0.0), axis=0, keepdims=True)

        carry_ref[...] = lax.fori_loop(0, ngroups, group, carry_ref[...])

    if reverse:
        tile = pl.BlockSpec((bt, cb), lambda j, i: (nt - 1 - i, j))
    else:
        tile = pl.BlockSpec((bt, cb), lambda j, i: (i, j))
    return pl.pallas_call(
        body, name=name, grid=(c // cb, nt), in_specs=[tile, tile], out_specs=tile,
        out_shape=jax.ShapeDtypeStruct((t, c), F32),
        scratch_shapes=[pltpu.VMEM((1, cb), F32), pltpu.VMEM((bt, cb), F32), pltpu.VMEM((bt, cb), F32)],
        compiler_params=_cparams("parallel", "arbitrary"),
    )(a, u)


def _rg_conv_fwd(proj, cw, cb_, name):
    t = proj.shape[0]
    bt, cb = _pick(t, ROW_TILE), D_MODEL

    def body(x_ref, halo_ref, w_ref, b_ref, o_ref):
        keep = (pl.program_id(0) > 0).astype(F32)
        o_ref[...] = _causal_conv(x_ref[...], halo_ref[...] * keep, w_ref, CONV_B) + b_ref[...]

    tile = pl.BlockSpec((bt, cb), lambda i, j: (i, j))
    return pl.pallas_call(
        body, name=name, grid=(t // bt, 1),
        in_specs=[tile, _prev_halo_spec(bt, cb), pl.BlockSpec((CONV_B, cb), lambda i, j: (0, 0)),
                  pl.BlockSpec((1, cb), lambda i, j: (0, 0))],
        out_specs=tile, out_shape=jax.ShapeDtypeStruct((t, D_MODEL), F32),
        compiler_params=_cparams("parallel", "parallel"),
    )(proj, proj, cw, cb_.reshape(1, D_MODEL))


def _conv_bwd_w(x, dy, ksize, name):
    t, c = dy.shape
    bt = _pick(t, ROW_TILE)
    r = bt // SUBLANES

    def body(x_ref, halo_ref, dy_ref, dw_ref, db_ref):
        @pl.when(pl.program_id(0) == 0)
        def _():
            dw_ref[...] = jnp.zeros_like(dw_ref)
            db_ref[...] = jnp.zeros_like(db_ref)

        keep = (pl.program_id(0) > 0).astype(F32)
        xv, halo, dyv = x_ref[...], halo_ref[...] * keep, dy_ref[...]
        db_ref[...] += jnp.sum(dyv, axis=0, keepdims=True)
        for j in range(ksize):
            dw_ref[j:j + 1, :] += jnp.sum(dyv * _shift_down(xv, halo, ksize - 1 - j), axis=0, keepdims=True)

    tile = pl.BlockSpec((bt, c), lambda i: (i, 0))
    return pl.pallas_call(
        body, name=name, grid=(t // bt,),
        in_specs=[tile, pl.BlockSpec((SUBLANES, c), lambda i: (jnp.maximum(i * r - 1, 0), 0)), tile],
        out_specs=[pl.BlockSpec((ksize, c), lambda i: (0, 0)), pl.BlockSpec((1, c), lambda i: (0, 0))],
        out_shape=[jax.ShapeDtypeStruct((ksize, c), F32), jax.ShapeDtypeStruct((1, c), F32)],
        compiler_params=_cparams("arbitrary"),
    )(x, x, dy)


def _rg_gate_math(xc, wa_ref, wi_ref, ba_ref, bi_ref, lam_ref):
    xb = xc.astype(BF16)
    ra = lax.dot_general(xb, wa_ref[...], _DOT_DIMS["nn"], preferred_element_type=F32) + ba_ref[...]
    ia = lax.dot_general(xb, wi_ref[...], _DOT_DIMS["nn"], preferred_element_type=F32) + bi_ref[...]
    r, ig = _sigmoid(ra), _sigmoid(ia)
    sp = _softplus(-lam_ref[...])
    log_a = -LRU_C * r * sp
    a = jnp.exp(log_a)
    mult = jnp.sqrt(-_expm1(2.0 * log_a))
    return xb, r, ig, sp, a, mult


def _rg_gate_specs(bt, time_first):
    if time_first:
        tile = pl.BlockSpec((bt, BLOCK_B), lambda i, n: (i, n))
        w = pl.BlockSpec((None, BLOCK_B, BLOCK_B), lambda i, n: (n, 0, 0))
        v = pl.BlockSpec((None, 1, BLOCK_B), lambda i, n: (n, 0, 0))
    else:
        tile = pl.BlockSpec((bt, BLOCK_B), lambda n, i: (i, n))
        w = pl.BlockSpec((None, BLOCK_B, BLOCK_B), lambda n, i: (n, 0, 0))
        v = pl.BlockSpec((None, 1, BLOCK_B), lambda n, i: (n, 0, 0))
    return tile, w, v


def _rg_gate_fwd(xc, wa, ba, wi, bi, lam, name):
    t = xc.shape[0]
    bt = _pick(t, 512)

    def body(x_ref, wa_ref, wi_ref, ba_ref, bi_ref, lam_ref, a_ref, u_ref):
        xcv = x_ref[...]
        _, _, ig, _, a, mult = _rg_gate_math(xcv, wa_ref, wi_ref, ba_ref, bi_ref, lam_ref)
        a_ref[...] = a
        u_ref[...] = mult * (ig * xcv)

    tile, w, v = _rg_gate_specs(bt, True)
    return pl.pallas_call(
        body, name=name, grid=(t // bt, N_BLOCKS_B), in_specs=[tile, w, w, v, v, v], out_specs=[tile, tile],
        out_shape=[jax.ShapeDtypeStruct((t, D_MODEL), F32), jax.ShapeDtypeStruct((t, D_MODEL), F32)],
        compiler_params=_cparams("parallel", "parallel"),
    )(xc, wa, wi, ba, bi, lam)


def _rg_gate_bwd(xc, g, h, wa, ba, wi, bi, lam, name):
    t = xc.shape[0]
    bt = _pick(t, 512)
    rr = bt // SUBLANES

    def body(x_ref, g_ref, h_ref, hh_ref, wa_ref, wi_ref, ba_ref, bi_ref, lam_ref,
             dx_ref, dwa_ref, dwi_ref, dba_ref, dbi_ref, dlam_ref):
        @pl.when(pl.program_id(1) == 0)
        def _():
            for ref in (dwa_ref, dwi_ref, dba_ref, dbi_ref, dlam_ref):
                ref[...] = jnp.zeros_like(ref)

        keep = (pl.program_id(1) > 0).astype(F32)
        xcv, gv = x_ref[...], g_ref[...]
        xb, r, ig, sp, a, mult = _rg_gate_math(xcv, wa_ref, wi_ref, ba_ref, bi_ref, lam_ref)
        h_prev = _shift_down(h_ref[...], hh_ref[...] * keep, 1)
        da = gv * h_prev
        dmult = gv * ig * xcv
        dig = gv * mult * xcv
        dxc = gv * mult * ig
        dlog_a = da * a - dmult * (a * a / mult)
        dr = dlog_a * (-LRU_C * sp)
        dsp = jnp.sum(dlog_a * (-LRU_C * r), axis=0, keepdims=True)
        dlam_ref[...] += dsp * (-_sigmoid(-lam_ref[...]))
        dra = dr * r * (1.0 - r)
        dia = dig * ig * (1.0 - ig)
        dba_ref[...] += jnp.sum(dra, axis=0, keepdims=True)
        dbi_ref[...] += jnp.sum(dia, axis=0, keepdims=True)
        drab, diab = dra.astype(BF16), dia.astype(BF16)
        dwa_ref[...] += lax.dot_general(xb, drab, _DOT_DIMS["tn"], preferred_element_type=F32)
        dwi_ref[...] += lax.dot_general(xb, diab, _DOT_DIMS["tn"], preferred_element_type=F32)
        dxc = dxc + lax.dot_general(drab, wa_ref[...], _DOT_DIMS["nt"], preferred_element_type=F32)
        dxc = dxc + lax.dot_general(diab, wi_ref[...], _DOT_DIMS["nt"], preferred_element_type=F32)
        dx_ref[...] = dxc

    tile, w, v = _rg_gate_specs(bt, False)
    halo = pl.BlockSpec((SUBLANES, BLOCK_B), lambda n, i: (jnp.maximum(i * rr - 1, 0), n))
    wshape = jax.ShapeDtypeStruct((N_BLOCKS_B, BLOCK_B, BLOCK_B), F32)
    vshape = jax.ShapeDtypeStruct((N_BLOCKS_B, 1, BLOCK_B), F32)
    return pl.pallas_call(
        body, name=name, grid=(N_BLOCKS_B, t // bt),
        in_specs=[tile, tile, tile, halo, w, w, v, v, v],
        out_specs=[tile, w, w, v, v, v],
        out_shape=[jax.ShapeDtypeStruct((t, D_MODEL), F32), wshape, wshape, vshape, vshape, vshape],
        compiler_params=_cparams("parallel", "arbitrary"),
    )(xc, g, h, h, wa, wi, ba, bi, lam)


def _rg_out_fwd(h, proj, name):
    t = h.shape[0]
    bt = _pick(t, ROW_TILE)

    def body(h_ref, g_ref, y_ref):
        y_ref[...] = (h_ref[...] * _gelu(g_ref[...])).astype(BF16)

    tile = pl.BlockSpec((bt, D_MODEL), lambda i: (i, 0))
    return pl.pallas_call(
        body, name=name, grid=(t // bt,), in_specs=[tile, pl.BlockSpec((bt, D_MODEL), lambda i: (i, 1))],
        out_specs=tile, out_shape=jax.ShapeDtypeStruct((t, D_MODEL), BF16),
        compiler_params=_cparams("parallel"),
    )(h, proj)


def _rg_out_bwd(dy, h, proj, name):
    t = h.shape[0]
    bt = _pick(t, ROW_TILE)

    def body(dy_ref, h_ref, g_ref, dh_ref, dg_ref):
        gl, glg = _gelu_and_grad(g_ref[...])
        dyv = dy_ref[...]
        dh_ref[...] = dyv * gl
        dg_ref[...] = (dyv * h_ref[...] * glg).astype(BF16)

    tile = pl.BlockSpec((bt, D_MODEL), lambda i: (i, 0))
    return pl.pallas_call(
        body, name=name, grid=(t // bt,),
        in_specs=[tile, tile, pl.BlockSpec((bt, D_MODEL), lambda i: (i, 1))], out_specs=[tile, tile],
        out_shape=[jax.ShapeDtypeStruct((t, D_MODEL), F32), jax.ShapeDtypeStruct((t, D_MODEL), BF16)],
        compiler_params=_cparams("parallel"),
    )(dy, h, proj)


def _shift_up_one(a, name):
    t, c = a.shape
    bt = _pick(t, ROW_TILE)

    def body(a_ref, halo_ref, o_ref):
        o_ref[...] = _shift_up(a_ref[...], halo_ref[...], 1)

    tile = pl.BlockSpec((bt, c), lambda i, j: (i, j))
    return pl.pallas_call(
        body, name=name, grid=(t // bt, 1), in_specs=[tile, _next_halo_spec(bt, c, t)], out_specs=tile,
        out_shape=jax.ShapeDtypeStruct((t, c), F32), compiler_params=_cparams("parallel", "parallel"),
    )(a, a)


ADAM_ROWS = 64


def _adamw(recv, w, m, v, name):
    _, r, c = recv.shape
    br = ADAM_ROWS
    assert r % br == 0

    def body(r_ref, w_ref, m_ref, v_ref, g_ref, d_ref, nm_ref, nv_ref):
        g = r_ref[0].astype(F32)
        for s in range(1, N_DEV):
            g = g + r_ref[s].astype(F32)
        m_new = ADAM_B1 * m_ref[...] + (1.0 - ADAM_B1) * g
        v_new = ADAM_B2 * v_ref[...] + (1.0 - ADAM_B2) * (g * g)
        m_hat = m_new / (1.0 - ADAM_B1 ** ADAM_STEP)
        v_hat = v_new / (1.0 - ADAM_B2 ** ADAM_STEP)
        g_ref[...] = g
        d_ref[...] = -ADAM_LR * (m_hat / (jnp.sqrt(v_hat) + ADAM_EPS) + ADAM_WD * w_ref[...])
        nm_ref[...] = m_new
        nv_ref[...] = v_new

    tile = pl.BlockSpec((br, c), lambda i: (i, 0))
    shape = jax.ShapeDtypeStruct((r, c), F32)
    return pl.pallas_call(
        body, name=name, grid=(r // br,),
        in_specs=[pl.BlockSpec((N_DEV, br, c), lambda i: (0, i, 0)), tile, tile, tile],
        out_specs=[tile] * 4, out_shape=[shape] * 4, compiler_params=_cparams("parallel"),
    )(recv, w, m, v)


def _all_to_all(src, name):
    def body(src_ref, out_ref, send_sems, recv_sems, local_sem):
        pos = [lax.axis_index(ax) for ax in MESH_AXES]
        me = 4 * pos[0] + 2 * pos[1] + pos[2]

        def peer_of(k):
            p = [(1 - pos[b]) if (k >> (2 - b)) & 1 else pos[b] for b in range(3)]
            return tuple(p), 4 * p[0] + 2 * p[1] + p[2]

        def copy(k):
            peer, peer_idx = peer_of(k)
            return pltpu.make_async_remote_copy(
                src_ref=src_ref.at[peer_idx], dst_ref=out_ref.at[me],
                send_sem=send_sems.at[k - 1], recv_sem=recv_sems.at[k - 1],
                device_id=peer, device_id_type=pl.DeviceIdType.MESH)

        def arrival(k):
            peer, peer_idx = peer_of(k)
            return pltpu.make_async_remote_copy(
                src_ref=src_ref.at[me], dst_ref=out_ref.at[peer_idx],
                send_sem=send_sems.at[k - 1], recv_sem=recv_sems.at[k - 1],
                device_id=peer, device_id_type=pl.DeviceIdType.MESH)

        mine = pltpu.make_async_copy(src_ref.at[me], out_ref.at[me], local_sem)
        mine.start()
        sends = [copy(k) for k in range(1, N_DEV)]
        for cp in sends:
            cp.start()
        for k in range(1, N_DEV):
            arrival(k).wait_recv()
        for cp in sends:
            cp.wait_send()
        mine.wait()

    hbm = pl.BlockSpec(memory_space=pltpu.HBM)
    return pl.pallas_call(
        body, name=name, in_specs=[hbm], out_specs=hbm,
        out_shape=jax.ShapeDtypeStruct(src.shape, src.dtype),
        scratch_shapes=[pltpu.SemaphoreType.DMA((N_DEV - 1,)), pltpu.SemaphoreType.DMA((N_DEV - 1,)),
                        pltpu.SemaphoreType.DMA],
        compiler_params=pltpu.CompilerParams(has_side_effects=True),
    )(src)


def _all_gather(src, name):
    def body(src_ref, out_ref, send_sems, recv_sems, local_sem):
        x, y, c = (lax.axis_index(ax) for ax in MESH_AXES)
        me, sibling = (x, y, c), (x, y, 1 - c)
        chips = [(1 - x, y), (x, 1 - y), (1 - x, 1 - y)]

        def slot(px, py, pc):
            return out_ref.at[4 * px + 2 * py + pc]

        def copy(k, block, to, from_src=False):
            return pltpu.make_async_remote_copy(
                src_ref=src_ref if from_src else slot(*block), dst_ref=slot(*block),
                send_sem=send_sems.at[k], recv_sem=recv_sems.at[k],
                device_id=to, device_id_type=pl.DeviceIdType.MESH)

        mine = pltpu.make_async_copy(src_ref, slot(*me), local_sem)
        mine.start()
        first = [copy(0, me, sibling, True)] + [copy(1 + j, me, (*chip, c), True) for j, chip in enumerate(chips)]
        for cp in first:
            cp.start()
        passed = [copy(4 + j, (*chip, c), sibling) for j, chip in enumerate(chips)]
        for j, chip in enumerate(chips):
            copy(1 + j, (*chip, c), me).wait_recv()
            passed[j].start()
        copy(0, sibling, me).wait_recv()
        for j, chip in enumerate(chips):
            copy(4 + j, (*chip, 1 - c), me).wait_recv()
        for cp in first + passed:
            cp.wait_send()
        mine.wait()

    hbm = pl.BlockSpec(memory_space=pltpu.HBM)
    return pl.pallas_call(
        body, name=name, in_specs=[hbm], out_specs=hbm,
        out_shape=jax.ShapeDtypeStruct((N_DEV,) + tuple(src.shape), src.dtype),
        scratch_shapes=[pltpu.SemaphoreType.DMA((N_DEV - 1,)), pltpu.SemaphoreType.DMA((N_DEV - 1,)),
                        pltpu.SemaphoreType.DMA],
        compiler_params=pltpu.CompilerParams(has_side_effects=True),
    )(src)


WEIGHTS = ['a_w_in', 'a_b_f', 'a_w_out', 'b_w_in', 'b_conv_w', 'b_conv_b', 'b_w_a', 'b_b_a', 'b_w_i', 'b_b_i',
           'b_lam', 'b_w_out', 'f_w_up', 'f_conv_w', 'f_conv_b', 'f_w_down', 'ln1_g', 'ln1_b', 'ln2_g', 'ln2_b',
           'ple_w', 'ple_gate_w', 'ple_gate_b']
SHARD_AXIS = {'a_w_in': 2, 'a_b_f': None, 'a_w_out': 1, 'b_w_in': 2, 'b_conv_w': 2, 'b_conv_b': 1, 'b_w_a': None,
              'b_b_a': None, 'b_w_i': None, 'b_b_i': None, 'b_lam': 1, 'b_w_out': 1, 'f_w_up': 2, 'f_conv_w': 2,
              'f_conv_b': None, 'f_w_down': 1, 'ln1_g': None, 'ln1_b': None, 'ln2_g': None, 'ln2_b': None,
              'ple_w': 2, 'ple_gate_w': 1, 'ple_gate_b': None}
MATMUL_WEIGHTS = ['a_w_in', 'a_w_out', 'b_w_in', 'b_w_out', 'f_w_up', 'f_w_down', 'ple_w', 'ple_gate_w']
SMALL_SHARDED = ['b_conv_w', 'b_conv_b', 'b_lam', 'f_conv_w']
GRADS_AS_BF16 = MATMUL_WEIGHTS + ['b_w_a', 'b_w_i']
PACK_COLS = 1024


def _to_shards(full, axis):
    return jnp.stack(jnp.split(full, N_DEV, axis=axis))


def _from_shards(pieces, axis):
    return jnp.concatenate([pieces[d] for d in range(N_DEV)], axis=axis)


PIECE_ROWS = 16


def _piece_rows(size):
    rows = -(-size // PACK_COLS)
    return -(-rows // PIECE_ROWS) * PIECE_ROWS


def _pack_pieces(pieces, total_mult=PIECE_ROWS):
    lead = pieces[0].shape[:-1]
    blocks = []
    for pc in pieces:
        n = pc.shape[-1]
        rows = _piece_rows(n)
        pad = [(0, 0)] * len(lead) + [(0, rows * PACK_COLS - n)]
        blocks.append(jnp.pad(pc, pad).reshape(lead + (rows, PACK_COLS)))
    total = sum(b.shape[-2] for b in blocks)
    extra = -total % total_mult
    if extra:
        blocks.append(jnp.zeros(lead + (extra, PACK_COLS), pieces[0].dtype))
    return jnp.concatenate(blocks, axis=len(lead))


def _unpack_pieces(packed, shapes):
    lead = packed.shape[:-2]
    out, row = [], 0
    for shp in shapes:
        size = math.prod(shp)
        rows = _piece_rows(size)
        block = lax.slice_in_dim(packed, row, row + rows, axis=len(lead))
        flat = block.reshape(lead + (rows * PACK_COLS,))
        out.append(lax.slice_in_dim(flat, 0, size, axis=len(lead)).reshape(lead + tuple(shp)))
        row += rows
    return out


def _gather_weights(local, names, dtype, name):
    packed = _pack_pieces([local[n].astype(dtype).reshape(-1) for n in names])
    gathered = _all_gather(packed, name)
    pieces = _unpack_pieces(gathered, [local[n].shape for n in names])
    return {n: _from_shards(pc, SHARD_AXIS[n]) for n, pc in zip(names, pieces)}


def _mixer_a_fwd(tag, xb, w):
    qkv = _mm(xb, w["wqkv"], "nn", BF16, f"{tag}_qkv")
    fg = _mm(xb, w["wf"], "nn", F32, f"{tag}_fgproj")
    fg_rows = fg[:, :N_HEADS].T
    c_rows = _fgate_fwd(fg_rows, w["b_f"], f"{tag}_fgate")
    c_pairs = _rows_to_pairs(c_rows)
    qh, kh, qt, kt, vt = _attn_prep(qkv, _rows_to_cols(c_rows), f"{tag}_attn_prep")
    ot, otb, lse_pairs = _attn_fwd_t(qh, kh, vt, c_pairs, f"{tag}_attn")
    m = _mm(otb, w["wout"], "tn", F32, f"{tag}_oproj")
    return m, dict(qkv=qkv, kh=kh, qt=qt, kt=kt, fg_rows=fg_rows, c_pairs=c_pairs, ot=ot, otb=otb,
                   lse_pairs=lse_pairs)


def _mixer_a_bwd(tag, dz, dzb, xb, w, s):
    t = xb.shape[0]
    dot = _mm(w["wout"], dzb, "nt", F32, f"{tag}_b_do")
    g_wout = _mm(s["otb"], dzb, "nn", F32, f"{tag}_b_dwout")
    d_pairs, dotb = _attn_delta_t(dot, s["ot"], f"{tag}_b_delta")
    dkt, dvt, dck_pairs, dq_aug = _attn_bwd_t(s["kh"], s["kt"], s["qt"], s["qkv"], dotb, s["c_pairs"],
                                              s["lse_pairs"], d_pairs, f"{tag}_b_attn")
    dq_aug = dq_aug.reshape(N_HEADS, LANES, t)
    dfg_rows, db_f = _fgate_bwd(dck_pairs[:, :2, :].reshape(N_HEADS, t), dq_aug[:, AUG_ONE, :], s["fg_rows"],
                                w["b_f"], f"{tag}_b_fgate")
    dqt = dq_aug[:, :HEAD_DIM, :].reshape(D_MODEL, t).astype(BF16)
    dqkv = jnp.concatenate([dqt, dkt, dvt], axis=0).T
    dfg = jnp.pad(dfg_rows.T, ((0, 0), (0, LANES - N_HEADS))).astype(BF16)
    dx = _mm(dqkv, w["wqkv"], "nt", F32, f"{tag}_b_dx_qkv", add=dz, add_scale=ALPHA, tk=3 * D_MODEL)
    dx = _mm(dfg, w["wf"], "nt", F32, f"{tag}_b_dx_fg", add=dx)
    g_wqkv = _mm(xb, dqkv, "tn", F32, f"{tag}_b_dwqkv")
    g_wf = _mm(xb, dfg, "tn", F32, f"{tag}_b_dwf")[:, :N_HEADS]
    grads = dict(a_w_in=jnp.concatenate([g_wqkv, g_wf], axis=1), a_b_f=db_f.reshape(N_HEADS), a_w_out=g_wout)
    return dx, grads


def _mixer_b_fwd(tag, xb, w):
    proj = _mm(xb, w["win"], "nn", F32, f"{tag}_proj")
    xc = _rg_conv_fwd(proj, w["conv_w"], w["conv_b"], f"{tag}_conv")
    a, u = _rg_gate_fwd(xc, w["wa"], w["ba"], w["wi"], w["bi"], w["lam"], f"{tag}_gate")
    h = _scan(a, u, f"{tag}_scan")
    y = _rg_out_fwd(h, proj, f"{tag}_out")
    m = _mm(y, w["wout"], "nn", F32, f"{tag}_oproj")
    return m, dict(proj=proj, xc=xc, a=a, h=h, y=y)


def _mixer_b_bwd(tag, dz, dzb, xb, w, s):
    dy = _mm(dzb, w["wout"], "nt", F32, f"{tag}_b_dy")
    g_wout = _mm(s["y"], dzb, "tn", F32, f"{tag}_b_dwout")
    dh, dgate = _rg_out_bwd(dy, s["h"], s["proj"], f"{tag}_b_out")
    g = _scan(_shift_up_one(s["a"], f"{tag}_b_shift"), dh, f"{tag}_b_scan", reverse=True)
    dxc, g_wa, g_wi, g_ba, g_bi, g_lam = _rg_gate_bwd(
        s["xc"], g, s["h"], w["wa"], w["ba"], w["wi"], w["bi"], w["lam"], f"{tag}_b_gate")
    dxp = _conv_bwd_x(dxc, w["conv_w"], f"{tag}_b_convx", BF16)
    g_cw, g_cb = _conv_bwd_w(s["proj"], dxc, CONV_B, f"{tag}_b_convw")
    dproj = jnp.concatenate([dxp, dgate], axis=1)
    dx = _mm(dproj, w["win"], "nt", F32, f"{tag}_b_dx", add=dz, add_scale=ALPHA, tk=2 * D_MODEL)
    g_win = _mm(xb, dproj, "tn", F32, f"{tag}_b_dwin")
    grads = dict(b_w_in=g_win, b_conv_w=g_cw, b_conv_b=g_cb.reshape(D_MODEL), b_w_a=g_wa,
                 b_b_a=g_ba.reshape(N_BLOCKS_B, BLOCK_B), b_w_i=g_wi, b_b_i=g_bi.reshape(N_BLOCKS_B, BLOCK_B),
                 b_lam=g_lam.reshape(D_MODEL), b_w_out=g_wout)
    return dx, grads


def _layer_fwd(i, x, xb, pb, w):
    tag = f"L{i}"
    mix = _mixer_a_fwd if i % 2 == 0 else _mixer_b_fwd
    m, sm = mix(tag, xb, w)
    x1, x1b, z1 = _ln_fwd(x, m, w["ln1_g"], w["ln1_b"], f"{tag}_ln1")
    h = _mm(x1b, w["wup"], "nn", F32, f"{tag}_ffn_up")
    y = _ffn_mid_fwd(h, w["fconv_w"], w["fconv_b"], f"{tag}_ffn_mid")
    ff = _mm(y, w["wdown"], "nn", F32, f"{tag}_ffn_down", tk=D_FF)
    x2, x2b, z2 = _ln_fwd(x1, ff, w["ln2_g"], w["ln2_b"], f"{tag}_ln2")
    gl = _mm(x2b, w["wg"], "nn", F32, f"{tag}_ple_gate")
    e = _mm(pb, w["wp"], "nn", F32, f"{tag}_ple_emb")
    x3, x3b = _ple_fwd(x2, gl, e, w["bg"], f"{tag}_ple")
    saved = dict(mixer=sm, xb=xb, x1b=x1b, z1=z1, h=h, y=y, x2b=x2b, z2=z2, gl=gl, e=e, pb=pb)
    return x3, x3b, saved


def _layer_bwd(i, dx3, w, s):
    tag = f"L{i}"
    dgl, de, g_bg = _ple_bwd(dx3, s["gl"], s["e"], w["bg"], f"{tag}_b_ple")
    g_wg = _mm(s["x2b"], dgl, "tn", F32, f"{tag}_b_dwg")
    g_wp = _mm(s["pb"], de, "tn", F32, f"{tag}_b_dwp")
    dx2 = _mm(dgl, w["wg"], "nt", F32, f"{tag}_b_dx2", add=dx3)
    dz2, dz2b, g_ln2g, g_ln2b = _ln_bwd(dx2, s["z2"], w["ln2_g"], f"{tag}_b_ln2")
    dy = _mm(dz2b, w["wdown"], "nt", F32, f"{tag}_b_dy")
    g_wdown = _mm(s["y"], dz2b, "tn", F32, f"{tag}_b_dwdown", tm=1408)
    dval, dgate, g_fcw, g_fcb = _ffn_mid_bwd_a(s["h"], dy, w["fconv_w"], w["fconv_b"], f"{tag}_b_ffn_mid")
    dhv = _conv_bwd_x(dval, w["fconv_w"][:, :D_FF], f"{tag}_b_convx_v", BF16)
    dhg = _conv_bwd_x(dgate, w["fconv_w"][:, D_FF:], f"{tag}_b_convx_g", BF16)
    dx1 = _mm(dhv, w["wup"][:, :D_FF], "nt", F32, f"{tag}_b_dx1_v", add=dz2, add_scale=ALPHA, tk=D_FF)
    dx1 = _mm(dhg, w["wup"][:, D_FF:], "nt", F32, f"{tag}_b_dx1_g", add=dx1, tk=D_FF)
    g_wup = jnp.concatenate([_mm(dhv, s["x1b"], "tn", F32, f"{tag}_b_dwup_v", tm=1408),
                             _mm(dhg, s["x1b"], "tn", F32, f"{tag}_b_dwup_g", tm=1408)], axis=0).T
    dz1, dz1b, g_ln1g, g_ln1b = _ln_bwd(dx1, s["z1"], w["ln1_g"], f"{tag}_b_ln1")
    mix_bwd = _mixer_a_bwd if i % 2 == 0 else _mixer_b_bwd
    dx, g_mix = mix_bwd(tag, dz1, dz1b, s["xb"], w, s["mixer"])
    grads = dict(f_w_up=g_wup, f_conv_w=g_fcw, f_conv_b=g_fcb.reshape(2 * D_FF), f_w_down=g_wdown,
                 ln1_g=g_ln1g.reshape(D_MODEL), ln1_b=g_ln1b.reshape(D_MODEL), ln2_g=g_ln2g.reshape(D_MODEL),
                 ln2_b=g_ln2b.reshape(D_MODEL), ple_w=g_wp, ple_gate_w=g_wg, ple_gate_b=g_bg.reshape(D_MODEL))
    return dx, g_mix, grads


def _layer_weights(i, full, rep):
    j = i // 2
    w = dict(ln1_g=rep["ln1_g"][i], ln1_b=rep["ln1_b"][i], ln2_g=rep["ln2_g"][i], ln2_b=rep["ln2_b"][i],
             wup=full["f_w_up"][i], fconv_w=full["f_conv_w"][i], fconv_b=rep["f_conv_b"][i],
             wdown=full["f_w_down"][i], wp=full["ple_w"][i], wg=full["ple_gate_w"][i], bg=rep["ple_gate_b"][i])
    if i % 2 == 0:
        w_in = full["a_w_in"][j]
        w.update(wqkv=w_in[:, :3 * D_MODEL],
                 wf=jnp.pad(w_in[:, 3 * D_MODEL:], ((0, 0), (0, LANES - N_HEADS))),
                 b_f=rep["a_b_f"][j], wout=full["a_w_out"][j])
    else:
        w.update(win=full["b_w_in"][j], conv_w=full["b_conv_w"][j], conv_b=full["b_conv_b"][j],
                 wa=rep["b_w_a"][j].astype(BF16), wi=rep["b_w_i"][j].astype(BF16),
                 ba=rep["b_b_a"][j].reshape(N_BLOCKS_B, 1, BLOCK_B), bi=rep["b_b_i"][j].reshape(N_BLOCKS_B, 1, BLOCK_B),
                 lam=full["b_lam"][j].reshape(N_BLOCKS_B, 1, BLOCK_B), wout=full["b_w_out"][j])
    return w


def _fwd_bwd(x, p, target, full, rep):
    weights = [_layer_weights(i, full, rep) for i in range(DEPTH)]
    xb = x.astype(BF16)
    pb = p.astype(BF16)
    saved = []
    for i in range(DEPTH):
        x, xb, s = _layer_fwd(i, x, xb, pb[i], weights[i])
        saved.append(s)
    dx, loss_local = _loss_head(x, target, "loss_head")

    per_layer = {n: [None] * (DEPTH if n.startswith(("f_", "ln", "ple")) else DEPTH // 2) for n in WEIGHTS}
    for i in reversed(range(DEPTH)):
        dx, g_mix, g_layer = _layer_bwd(i, dx, weights[i], saved[i])
        for n, g in g_layer.items():
            per_layer[n][i] = g
        for n, g in g_mix.items():
            per_layer[n][i // 2] = g
    return loss_local, dx, per_layer


def _train_step(x, p, target, local, moments_m, moments_v):
    full = _gather_weights(local, MATMUL_WEIGHTS, BF16, "gather_matmul_weights")
    full.update(_gather_weights(local, SMALL_SHARDED, F32, "gather_small_weights"))
    rep = {n: local[n] for n in WEIGHTS if SHARD_AXIS[n] is None}
    loss_local, dx, grads_layers = _fwd_bwd(x, p, target, full, rep)

    unpacked = [{} for _ in range(4)]
    for group, dtype, tag in ((GRADS_AS_BF16, BF16, "big"), ([n for n in WEIGHTS if n not in GRADS_AS_BF16], F32, "small")):
        pieces = []
        for n in group:
            g = jnp.stack(grads_layers[n]).astype(dtype)
            if SHARD_AXIS[n] is None:
                pieces.append(jnp.broadcast_to(g.reshape(1, -1), (N_DEV, g.size)))
            else:
                pieces.append(_to_shards(g, SHARD_AXIS[n]).reshape(N_DEV, -1))
        recv = _all_to_all(_pack_pieces(pieces, ADAM_ROWS), f"reduce_scatter_grads_{tag}")

        def pack_local(d):
            return _pack_pieces([d[n].astype(F32).reshape(-1) for n in group], ADAM_ROWS)

        outs = _adamw(recv, pack_local(local), pack_local(moments_m), pack_local(moments_v), f"adamw_{tag}")
        shapes = [local[n].shape for n in group]
        for dst, packed in zip(unpacked, outs):
            dst.update(zip(group, _unpack_pieces(packed, shapes)))
    return loss_local, dx, unpacked


def kernel(x, p, a_w_in, a_b_f, a_w_out, b_w_in, b_conv_w, b_conv_b, b_w_a, b_b_a, b_w_i, b_b_i, b_lam, b_w_out, f_w_up, f_conv_w, f_conv_b, f_w_down, ln1_g, ln1_b, ln2_g, ln2_b, ple_w, ple_gate_w, ple_gate_b, loss_target, m_a_w_in, m_a_b_f, m_a_w_out, m_b_w_in, m_b_conv_w, m_b_conv_b, m_b_w_a, m_b_b_a, m_b_w_i, m_b_b_i, m_b_lam, m_b_w_out, m_f_w_up, m_f_conv_w, m_f_conv_b, m_f_w_down, m_ln1_g, m_ln1_b, m_ln2_g, m_ln2_b, m_ple_w, m_ple_gate_w, m_ple_gate_b, v_a_w_in, v_a_b_f, v_a_w_out, v_b_w_in, v_b_conv_w, v_b_conv_b, v_b_w_a, v_b_b_a, v_b_w_i, v_b_b_i, v_b_lam, v_b_w_out, v_f_w_up, v_f_conv_w, v_f_conv_b, v_f_w_down, v_ln1_g, v_ln1_b, v_ln2_g, v_ln2_b, v_ple_w, v_ple_gate_w, v_ple_gate_b):
    given = dict(locals())
    local = {n: given[n] for n in WEIGHTS}
    mom_m = {n: given["m_" + n] for n in WEIGHTS}
    mom_v = {n: given["v_" + n] for n in WEIGHTS}
    t = x.shape[1]
    loss_local, dx, (grad, delta, new_m, new_v) = _train_step(
        x.reshape(t, D_MODEL), p.reshape(DEPTH, t, D_PLE), loss_target.reshape(t, D_MODEL), local, mom_m, mom_v)
    loss = lax.psum(loss_local, MESH_AXES)
    return (loss, dx.reshape(1, t, D_MODEL), *[grad[n] for n in WEIGHTS], *[delta[n] for n in WEIGHTS],
            *[new_m[n] for n in WEIGHTS], *[new_v[n] for n in WEIGHTS])
```

```python
import math

import jax
import jax.numpy as jnp
from jax import lax
from jax.experimental import pallas as pl
from jax.experimental.pallas import tpu as pltpu

F32 = jnp.float32
BF16 = jnp.bfloat16

D_MODEL = 1024
DEPTH = 4
N_HEADS = 16
HEAD_DIM = 64
N_PAIRS = N_HEADS // 2
N_BLOCKS_B = 8
BLOCK_B = 128
CONV_B = 4
LRU_C = 8.0
D_FF = 2816
CONV_F = 3
D_PLE = 256
LN_EPS = 1e-5
ALPHA = (2.0 * DEPTH) ** 0.25
ADAM_LR, ADAM_B1, ADAM_B2, ADAM_EPS, ADAM_WD, ADAM_STEP = 0.001, 0.9, 0.999, 1e-08, 0.01, 10
N_DEV = 8
MESH_AXES = ("x", "y", "c")

LANES = 128
SUBLANES = 8
VMEM_LIMIT_BYTES = 56 * 1024 * 1024
ATTN_FWD_BLOCK = 1024
ATTN_BWD_BLOCK = 512
ROW_TILE = 256
NEG_BIG = -1e30


def _cparams(*sem):
    return pltpu.CompilerParams(dimension_semantics=sem if sem else None, vmem_limit_bytes=VMEM_LIMIT_BYTES)


def _pick(n, pref):
    if n <= pref:
        return n
    best = None
    for t in range(LANES, pref + 1, LANES):
        if n % t == 0:
            best = t
    assert best is not None, (n, pref)
    return best


def _sigmoid(x):
    return 1.0 / (1.0 + jnp.exp(-x))


def _log1p(x):
    u = 1.0 + x
    d = u - 1.0
    return jnp.where(d == 0.0, x, jnp.log(u) * (x / jnp.where(d == 0.0, 1.0, d)))


def _expm1(x):
    u = jnp.exp(x)
    lu = jnp.log(u)
    return jnp.where(u == 1.0, x, (u - 1.0) * (x / jnp.where(u == 1.0, 1.0, lu)))


def _softplus(x):
    return jnp.maximum(x, 0.0) + _log1p(jnp.exp(-jnp.abs(x)))


_GELU_C = math.sqrt(2.0 / math.pi)


def _gelu(x):
    return 0.5 * x * (1.0 + jnp.tanh(_GELU_C * (x + 0.044715 * x * x * x)))


def _gelu_and_grad(x):
    t = jnp.tanh(_GELU_C * (x + 0.044715 * x * x * x))
    du = _GELU_C * (1.0 + 3.0 * 0.044715 * x * x)
    return 0.5 * x * (1.0 + t), 0.5 * (1.0 + t) + 0.5 * x * (1.0 - t * t) * du


_DOT_DIMS = {"nn": (((1,), (0,)), ((), ())), "nt": (((1,), (1,)), ((), ())), "tn": (((0,), (0,)), ((), ()))}


def _mm(a, b, mode, out_dtype, name, add=None, add_scale=1.0, tm=512, tn=1408, tk=1408):
    if mode == "nn":
        (m, k), (k2, n) = a.shape, b.shape
    elif mode == "nt":
        (m, k), (n, k2) = a.shape, b.shape
    else:
        (k, m), (k2, n) = a.shape, b.shape
    assert k == k2 and a.dtype == BF16 and b.dtype == BF16, (a.shape, b.shape, a.dtype, b.dtype)
    tm, tn, tk = _pick(m, tm), _pick(n, tn), _pick(k, tk)
    nk = k // tk
    dims = _DOT_DIMS[mode]

    def body(*refs):
        if add is None:
            a_ref, b_ref, o_ref, acc_ref = refs
        else:
            a_ref, b_ref, add_ref, o_ref, acc_ref = refs
        kk = pl.program_id(2)

        @pl.when(kk == 0)
        def _():
            acc_ref[...] = jnp.zeros_like(acc_ref)

        acc_ref[...] += lax.dot_general(a_ref[...], b_ref[...], dims, preferred_element_type=F32)

        @pl.when(kk == nk - 1)
        def _():
            r = acc_ref[...]
            if add is not None:
                r = r + add_scale * add_ref[...]
            o_ref[...] = r.astype(out_dtype)

    a_spec = (pl.BlockSpec((tk, tm), lambda j, i, kk: (kk, i)) if mode == "tn"
              else pl.BlockSpec((tm, tk), lambda j, i, kk: (i, kk)))
    b_spec = (pl.BlockSpec((tn, tk), lambda j, i, kk: (j, kk)) if mode == "nt"
              else pl.BlockSpec((tk, tn), lambda j, i, kk: (kk, j)))
    o_spec = pl.BlockSpec((tm, tn), lambda j, i, kk: (i, j))
    in_specs, args = [a_spec, b_spec], [a, b]
    if add is not None:
        assert add.shape == (m, n) and add.dtype == F32
        in_specs.append(o_spec)
        args.append(add)
    return pl.pallas_call(
        body, name=name, grid=(n // tn, m // tm, nk),
        in_specs=in_specs, out_specs=o_spec,
        out_shape=jax.ShapeDtypeStruct((m, n), out_dtype),
        scratch_shapes=[pltpu.VMEM((tm, tn), F32)],
        compiler_params=_cparams("parallel", "parallel", "arbitrary"),
    )(*args)


def _ln_fwd(x, m, g, b, name):
    t, d = x.shape
    bt = _pick(t, ROW_TILE)

    def body(x_ref, m_ref, g_ref, b_ref, y_ref, yb_ref, z_ref):
        z = ALPHA * x_ref[...] + m_ref[...]
        mu = jnp.mean(z, axis=-1, keepdims=True)
        zc = z - mu
        var = jnp.mean(zc * zc, axis=-1, keepdims=True)
        y = zc * lax.rsqrt(var + LN_EPS) * g_ref[...] + b_ref[...]
        y_ref[...] = y
        yb_ref[...] = y.astype(BF16)
        z_ref[...] = z

    row = pl.BlockSpec((bt, d), lambda i: (i, 0))
    vec = pl.BlockSpec((1, d), lambda i: (0, 0))
    return pl.pallas_call(
        body, name=name, grid=(t // bt,), in_specs=[row, row, vec, vec], out_specs=[row, row, row],
        out_shape=[jax.ShapeDtypeStruct((t, d), F32), jax.ShapeDtypeStruct((t, d), BF16),
                   jax.ShapeDtypeStruct((t, d), F32)],
        compiler_params=_cparams("parallel"),
    )(x, m, g.reshape(1, d), b.reshape(1, d))


def _ln_bwd(dy, z, g, name):
    t, d = dy.shape
    bt = _pick(t, ROW_TILE)

    def body(dy_ref, z_ref, g_ref, dz_ref, dzb_ref, dg_ref, db_ref):
        @pl.when(pl.program_id(0) == 0)
        def _():
            dg_ref[...] = jnp.zeros_like(dg_ref)
            db_ref[...] = jnp.zeros_like(db_ref)

        z = z_ref[...]
        dyv = dy_ref[...]
        mu = jnp.mean(z, axis=-1, keepdims=True)
        zc = z - mu
        var = jnp.mean(zc * zc, axis=-1, keepdims=True)
        rstd = lax.rsqrt(var + LN_EPS)
        xhat = zc * rstd
        dxh = dyv * g_ref[...]
        m1 = jnp.mean(dxh, axis=-1, keepdims=True)
        m2 = jnp.mean(dxh * xhat, axis=-1, keepdims=True)
        dz = rstd * (dxh - m1 - xhat * m2)
        dz_ref[...] = dz
        dzb_ref[...] = dz.astype(BF16)
        dg_ref[...] += jnp.sum(dyv * xhat, axis=0, keepdims=True)
        db_ref[...] += jnp.sum(dyv, axis=0, keepdims=True)

    row = pl.BlockSpec((bt, d), lambda i: (i, 0))
    vec = pl.BlockSpec((1, d), lambda i: (0, 0))
    return pl.pallas_call(
        body, name=name, grid=(t // bt,), in_specs=[row, row, vec], out_specs=[row, row, vec, vec],
        out_shape=[jax.ShapeDtypeStruct((t, d), F32), jax.ShapeDtypeStruct((t, d), BF16),
                   jax.ShapeDtypeStruct((1, d), F32), jax.ShapeDtypeStruct((1, d), F32)],
        compiler_params=_cparams("arbitrary"),
    )(dy, z, g.reshape(1, d))


def _rows_iota(shape):
    return lax.broadcasted_iota(jnp.int32, shape, 0)


def _shift_down(x, halo, s):
    if s == 0:
        return x
    rolled = pltpu.roll(x, s, axis=0)
    first = jnp.where(_rows_iota((SUBLANES, x.shape[1])) < s, pltpu.roll(halo, s, axis=0), rolled[:SUBLANES])
    return jnp.concatenate([first, rolled[SUBLANES:]], axis=0)


def _shift_up(x, halo, s):
    if s == 0:
        return x
    n = x.shape[0]
    rolled = pltpu.roll(x, n - s, axis=0)
    last = jnp.where(_rows_iota((SUBLANES, x.shape[1])) < SUBLANES - s, rolled[n - SUBLANES:],
                     pltpu.roll(halo, SUBLANES - s, axis=0))
    return jnp.concatenate([rolled[:n - SUBLANES], last], axis=0)


def _prev_halo_spec(bt, cb, col_off=0):
    r = bt // SUBLANES
    return pl.BlockSpec((SUBLANES, cb), lambda i, j: (jnp.maximum(i * r - 1, 0), j + col_off))


def _next_halo_spec(bt, cb, t, col_off=0):
    r = bt // SUBLANES
    last = t // SUBLANES - 1
    return pl.BlockSpec((SUBLANES, cb), lambda i, j: (jnp.minimum((i + 1) * r, last), j + col_off))


def _causal_conv(x, halo, w_ref, ksize):
    acc = None
    for j in range(ksize):
        term = w_ref[j:j + 1, :] * _shift_down(x, halo, ksize - 1 - j)
        acc = term if acc is None else acc + term
    return acc


def _anticausal_conv(y, halo, w_ref, ksize):
    acc = None
    for j in range(ksize):
        term = w_ref[j:j + 1, :] * _shift_up(y, halo, ksize - 1 - j)
        acc = term if acc is None else acc + term
    return acc


def _ffn_mid_fwd(h, cw, cb_, name):
    t = h.shape[0]
    bt, cb = _pick(t, ROW_TILE), _pick(D_FF, 1408)
    nc = D_FF // cb

    def body(hv_ref, hvh_ref, hg_ref, hgh_ref, wv_ref, wg_ref, bv_ref, bg_ref, y_ref):
        keep = (pl.program_id(0) > 0).astype(F32)
        val = _causal_conv(hv_ref[...], hvh_ref[...] * keep, wv_ref, CONV_F) + bv_ref[...]
        gate = _causal_conv(hg_ref[...], hgh_ref[...] * keep, wg_ref, CONV_F) + bg_ref[...]
        y_ref[...] = (_gelu(gate) * val).astype(BF16)

    tile_v = pl.BlockSpec((bt, cb), lambda i, j: (i, j))
    tile_g = pl.BlockSpec((bt, cb), lambda i, j: (i, j + nc))
    wv = pl.BlockSpec((CONV_F, cb), lambda i, j: (0, j))
    wg = pl.BlockSpec((CONV_F, cb), lambda i, j: (0, j + nc))
    bv = pl.BlockSpec((1, cb), lambda i, j: (0, j))
    bg = pl.BlockSpec((1, cb), lambda i, j: (0, j + nc))
    cb2 = cb_.reshape(1, 2 * D_FF)
    return pl.pallas_call(
        body, name=name, grid=(t // bt, nc),
        in_specs=[tile_v, _prev_halo_spec(bt, cb), tile_g, _prev_halo_spec(bt, cb, nc), wv, wg, bv, bg],
        out_specs=tile_v, out_shape=jax.ShapeDtypeStruct((t, D_FF), BF16),
        compiler_params=_cparams("parallel", "parallel"),
    )(h, h, h, h, cw, cw, cb2, cb2)


def _ffn_mid_bwd_a(h, dy, cw, cb_, name):
    t = h.shape[0]
    bt, cb = _pick(t, ROW_TILE), _pick(D_FF, 1408)
    nc = D_FF // cb

    def body(hv_ref, hvh_ref, hg_ref, hgh_ref, dy_ref, wv_ref, wg_ref, bv_ref, bg_ref,
             dv_ref, dg_ref, dwv_ref, dwg_ref, dbv_ref, dbg_ref):
        @pl.when(pl.program_id(1) == 0)
        def _():
            for r in (dwv_ref, dwg_ref, dbv_ref, dbg_ref):
                r[...] = jnp.zeros_like(r)

        keep = (pl.program_id(1) > 0).astype(F32)
        hv, hvh = hv_ref[...], hvh_ref[...] * keep
        hg, hgh = hg_ref[...], hgh_ref[...] * keep
        val = _causal_conv(hv, hvh, wv_ref, CONV_F) + bv_ref[...]
        gate = _causal_conv(hg, hgh, wg_ref, CONV_F) + bg_ref[...]
        gl, glg = _gelu_and_grad(gate)
        dyv = dy_ref[...]
        dval = dyv * gl
        dgate = dyv * val * glg
        dv_ref[...] = dval.astype(BF16)
        dg_ref[...] = dgate.astype(BF16)
        dbv_ref[...] += jnp.sum(dval, axis=0, keepdims=True)
        dbg_ref[...] += jnp.sum(dgate, axis=0, keepdims=True)
        for j in range(CONV_F):
            s = CONV_F - 1 - j
            dwv_ref[j:j + 1, :] += jnp.sum(dval * _shift_down(hv, hvh, s), axis=0, keepdims=True)
            dwg_ref[j:j + 1, :] += jnp.sum(dgate * _shift_down(hg, hgh, s), axis=0, keepdims=True)

    tile_v = pl.BlockSpec((bt, cb), lambda j, i: (i, j))
    tile_g = pl.BlockSpec((bt, cb), lambda j, i: (i, j + nc))
    r = bt // SUBLANES
    halo_v = pl.BlockSpec((SUBLANES, cb), lambda j, i: (jnp.maximum(i * r - 1, 0), j))
    halo_g = pl.BlockSpec((SUBLANES, cb), lambda j, i: (jnp.maximum(i * r - 1, 0), j + nc))
    wv = pl.BlockSpec((CONV_F, cb), lambda j, i: (0, j))
    wg = pl.BlockSpec((CONV_F, cb), lambda j, i: (0, j + nc))
    bv = pl.BlockSpec((1, cb), lambda j, i: (0, j))
    bg = pl.BlockSpec((1, cb), lambda j, i: (0, j + nc))
    cb2 = cb_.reshape(1, 2 * D_FF)
    dv, dg, dwv, dwg, dbv, dbg = pl.pallas_call(
        body, name=name, grid=(nc, t // bt),
        in_specs=[tile_v, halo_v, tile_g, halo_g, tile_v, wv, wg, bv, bg],
        out_specs=[tile_v, tile_v, wv, wv, bv, bv],
        out_shape=[jax.ShapeDtypeStruct((t, D_FF), BF16), jax.ShapeDtypeStruct((t, D_FF), BF16),
                   jax.ShapeDtypeStruct((CONV_F, D_FF), F32), jax.ShapeDtypeStruct((CONV_F, D_FF), F32),
                   jax.ShapeDtypeStruct((1, D_FF), F32), jax.ShapeDtypeStruct((1, D_FF), F32)],
        compiler_params=_cparams("parallel", "arbitrary"),
    )(h, h, h, h, dy, cw, cw, cb2, cb2)
    return dv, dg, jnp.concatenate([dwv, dwg], axis=1), jnp.concatenate([dbv, dbg], axis=1)


def _conv_bwd_x(dy, cw, name, out_dtype):
    t, c = dy.shape
    ksize = cw.shape[0]
    bt, cb = _pick(t, ROW_TILE), _pick(c, 1408)
    halo_rows = SUBLANES if dy.dtype == F32 else 2 * SUBLANES
    r, last = bt // halo_rows, t // halo_rows - 1

    def body(dy_ref, halo_ref, w_ref, o_ref):
        keep = (pl.program_id(0) < t // bt - 1).astype(F32)
        halo = halo_ref[...].astype(F32)[:SUBLANES] * keep
        o_ref[...] = _anticausal_conv(dy_ref[...].astype(F32), halo, w_ref, ksize).astype(out_dtype)

    tile = pl.BlockSpec((bt, cb), lambda i, j: (i, j))
    halo_spec = pl.BlockSpec((halo_rows, cb), lambda i, j: (jnp.minimum((i + 1) * r, last), j))
    return pl.pallas_call(
        body, name=name, grid=(t // bt, c // cb),
        in_specs=[tile, halo_spec, pl.BlockSpec((ksize, cb), lambda i, j: (0, j))],
        out_specs=tile, out_shape=jax.ShapeDtypeStruct((t, c), out_dtype),
        compiler_params=_cparams("parallel", "parallel"),
    )(dy, dy, cw)


def _ple_fwd(x2, g, e, bg, name):
    t, d = x2.shape
    bt = _pick(t, ROW_TILE)

    def body(x_ref, g_ref, e_ref, b_ref, y_ref, yb_ref):
        y = x_ref[...] + _sigmoid(g_ref[...] + b_ref[...]) * e_ref[...]
        y_ref[...] = y
        yb_ref[...] = y.astype(BF16)

    row = pl.BlockSpec((bt, d), lambda i: (i, 0))
    vec = pl.BlockSpec((1, d), lambda i: (0, 0))
    return pl.pallas_call(
        body, name=name, grid=(t // bt,), in_specs=[row, row, row, vec], out_specs=[row, row],
        out_shape=[jax.ShapeDtypeStruct((t, d), F32), jax.ShapeDtypeStruct((t, d), BF16)],
        compiler_params=_cparams("parallel"),
    )(x2, g, e, bg.reshape(1, d))


def _ple_bwd(dx3, g, e, bg, name):
    t, d = dx3.shape
    bt = _pick(t, ROW_TILE)

    def body(dx_ref, g_ref, e_ref, b_ref, dg_ref, de_ref, db_ref):
        @pl.when(pl.program_id(0) == 0)
        def _():
            db_ref[...] = jnp.zeros_like(db_ref)

        dx = dx_ref[...]
        gate = _sigmoid(g_ref[...] + b_ref[...])
        dg = dx * e_ref[...] * gate * (1.0 - gate)
        dg_ref[...] = dg.astype(BF16)
        de_ref[...] = (dx * gate).astype(BF16)
        db_ref[...] += jnp.sum(dg, axis=0, keepdims=True)

    row = pl.BlockSpec((bt, d), lambda i: (i, 0))
    vec = pl.BlockSpec((1, d), lambda i: (0, 0))
    return pl.pallas_call(
        body, name=name, grid=(t // bt,), in_specs=[row, row, row, vec], out_specs=[row, row, vec],
        out_shape=[jax.ShapeDtypeStruct((t, d), BF16), jax.ShapeDtypeStruct((t, d), BF16),
                   jax.ShapeDtypeStruct((1, d), F32)],
        compiler_params=_cparams("arbitrary"),
    )(dx3, g, e, bg.reshape(1, d))


def _loss_head(y, target, name):
    t, d = y.shape
    bt = _pick(t, ROW_TILE)

    def body(y_ref, t_ref, dy_ref, l_ref, acc_ref):
        @pl.when(pl.program_id(0) == 0)
        def _():
            acc_ref[...] = jnp.zeros_like(acc_ref)

        err = y_ref[...] - t_ref[...]
        dy_ref[...] = err * (1.0 / d)
        acc_ref[...] += jnp.sum(err * err, axis=0, keepdims=True)

        @pl.when(pl.program_id(0) == t // bt - 1)
        def _():
            l_ref[...] = jnp.full(l_ref.shape, (0.5 / d) * jnp.sum(acc_ref[...]), F32)

    row = pl.BlockSpec((bt, d), lambda i: (i, 0))
    dy, l = pl.pallas_call(
        body, name=name, grid=(t // bt,), in_specs=[row, row],
        out_specs=[row, pl.BlockSpec((SUBLANES, LANES), lambda i: (0, 0))],
        out_shape=[jax.ShapeDtypeStruct((t, d), F32), jax.ShapeDtypeStruct((SUBLANES, LANES), F32)],
        scratch_shapes=[pltpu.VMEM((1, d), F32)],
        compiler_params=_cparams("arbitrary"),
    )(y, target)
    return dy, l[0, 0]


def _split3(x):
    hi = x.astype(BF16)
    r1 = x - hi.astype(F32)
    mid = r1.astype(BF16)
    lo = (r1 - mid.astype(F32)).astype(BF16)
    return hi, mid, lo


def _tri_dot(x, tri):
    hi, mid, lo = _split3(x)
    dims = (((1,), (0,)), ((), ()))
    return (lax.dot_general(hi, tri, dims, preferred_element_type=F32)
            + lax.dot_general(mid, tri, dims, preferred_element_type=F32)
            + lax.dot_general(lo, tri, dims, preferred_element_type=F32))


def _fgate_fwd(fg_rows, b_f, name):
    hh, t = fg_rows.shape
    bt = _pick(t, 512)

    def body(fg_ref, b_ref, c_ref, carry_ref):
        @pl.when(pl.program_id(0) == 0)
        def _():
            carry_ref[...] = jnp.zeros_like(carry_ref)

        xx = fg_ref[...] + b_ref[...]
        logf = jnp.minimum(xx, 0.0) - _log1p(jnp.exp(-jnp.abs(xx)))
        r = lax.broadcasted_iota(jnp.int32, (bt, bt), 0)
        c = lax.broadcasted_iota(jnp.int32, (bt, bt), 1)
        tri = (r <= c).astype(BF16)
        cs = _tri_dot(logf, tri) + carry_ref[...]
        c_ref[...] = cs
        carry_ref[...] = cs[:, bt - 1:bt]

    return pl.pallas_call(
        body, name=name, grid=(t // bt,),
        in_specs=[pl.BlockSpec((hh, bt), lambda i: (0, i)), pl.BlockSpec((hh, 1), lambda i: (0, 0))],
        out_specs=pl.BlockSpec((hh, bt), lambda i: (0, i)),
        out_shape=jax.ShapeDtypeStruct((hh, t), F32),
        scratch_shapes=[pltpu.VMEM((hh, 1), F32)],
        compiler_params=_cparams("arbitrary"),
    )(fg_rows, b_f.reshape(hh, 1))


def _fgate_bwd(dck_rows, dcq_rows, fg_rows, b_f, name):
    hh, t = fg_rows.shape
    bt = _pick(t, 512)
    nb = t // bt

    def body(dc_ref, dcq_ref, fg_ref, b_ref, dfg_ref, db_ref, carry_ref):
        @pl.when(pl.program_id(0) == 0)
        def _():
            carry_ref[...] = jnp.zeros_like(carry_ref)
            db_ref[...] = jnp.zeros_like(db_ref)

        r = lax.broadcasted_iota(jnp.int32, (bt, bt), 0)
        c = lax.broadcasted_iota(jnp.int32, (bt, bt), 1)
        tri = (r >= c).astype(BF16)
        dlogf = _tri_dot(dc_ref[...] + dcq_ref[...], tri) + carry_ref[...]
        carry_ref[...] = dlogf[:, 0:1]
        xx = fg_ref[...] + b_ref[...]
        dfg = dlogf * _sigmoid(-xx)
        dfg_ref[...] = dfg
        db_ref[...] += jnp.sum(dfg, axis=1, keepdims=True)

    blk = pl.BlockSpec((hh, bt), lambda i: (0, nb - 1 - i))
    vec = pl.BlockSpec((hh, 1), lambda i: (0, 0))
    return pl.pallas_call(
        body, name=name, grid=(nb,), in_specs=[blk, blk, blk, vec], out_specs=[blk, vec],
        out_shape=[jax.ShapeDtypeStruct((hh, t), F32), jax.ShapeDtypeStruct((hh, 1), F32)],
        scratch_shapes=[pltpu.VMEM((hh, 1), F32)],
        compiler_params=_cparams("arbitrary"),
    )(dck_rows, dcq_rows, fg_rows, b_f.reshape(hh, 1))


def _rows_to_cols(r):
    hh, t = r.shape
    return jnp.repeat(r.reshape(hh // 2, 2, t).transpose(0, 2, 1), HEAD_DIM, axis=-1)


def _rows_to_pairs(r):
    hh, t = r.shape
    return jnp.pad(r.reshape(hh // 2, 2, t), ((0, 0), (0, SUBLANES - 2), (0, 0)))


AUG_C = HEAD_DIM
AUG_ONE = HEAD_DIM + 3


def _attn_prep(qkv, c_cols, name):
    t = qkv.shape[0]
    bt = _pick(t, ATTN_FWD_BLOCK)
    scale = 1.0 / math.sqrt(HEAD_DIM)

    def body(q_ref, k_ref, v_ref, c_ref, qh_ref, kh_ref, qt_ref, kt_ref, vt_ref):
        vt_ref[...] = v_ref[...].astype(F32).T.astype(BF16)
        lane = lax.broadcasted_iota(jnp.int32, (bt, LANES), 1)
        q2 = q_ref[...].astype(F32)
        k2 = k_ref[...].astype(F32) * scale
        c2 = c_ref[...]
        parts = [p.astype(F32) for p in _split3(c2 - c2[0:1, :])]
        swapped = [pltpu.roll(p, HEAD_DIM, axis=1) for p in parts]
        for a in (0, 1):
            qa = q2 if a == 0 else pltpu.roll(q2, HEAD_DIM, axis=1)
            ka = k2 if a == 0 else pltpu.roll(k2, HEAD_DIM, axis=1)
            hi, mid, lo = swapped if a == 0 else parts
            kaug = jnp.where(lane < HEAD_DIM, ka,
                             jnp.where(lane == AUG_C, hi,
                                       jnp.where(lane == AUG_C + 1, mid,
                                                 jnp.where(lane == AUG_C + 2, lo,
                                                           jnp.where(lane == AUG_ONE, 1.0, 0.0)))))
            qaug = jnp.where(lane < HEAD_DIM, qa, jnp.where(lane < AUG_ONE, -1.0, 0.0))
            qh_ref[:, a * LANES:(a + 1) * LANES] = qaug.astype(BF16)
            kh_ref[:, a * LANES:(a + 1) * LANES] = kaug.astype(BF16)
            qt_ref[a * LANES:(a + 1) * LANES, :] = qaug.T.astype(BF16)
            kt_ref[a * LANES:(a + 1) * LANES, :] = kaug.T.astype(BF16)

    out = pl.BlockSpec((bt, 2 * LANES), lambda i, hp: (i, hp))
    out_t = pl.BlockSpec((2 * LANES, bt), lambda i, hp: (hp, i))
    shape = jax.ShapeDtypeStruct((t, N_HEADS * LANES), BF16)
    shape_t = jax.ShapeDtypeStruct((N_HEADS * LANES, t), BF16)
    return pl.pallas_call(
        body, name=name, grid=(t // bt, N_PAIRS),
        in_specs=[pl.BlockSpec((bt, LANES), lambda i, hp: (i, hp)),
                  pl.BlockSpec((bt, LANES), lambda i, hp: (i, N_PAIRS + hp)),
                  pl.BlockSpec((bt, LANES), lambda i, hp: (i, 2 * N_PAIRS + hp)),
                  pl.BlockSpec((None, bt, LANES), lambda i, hp: (hp, i, 0))],
        out_specs=[out, out, out_t, out_t, pl.BlockSpec((LANES, bt), lambda i, hp: (hp, i))],
        out_shape=[shape, shape, shape_t, shape_t, jax.ShapeDtypeStruct((D_MODEL, t), BF16)],
        compiler_params=_cparams("parallel", "parallel"),
    )(qkv, qkv, qkv, c_cols)


def _block_scalar(c_ref, a, start):
    return c_ref[a:a + 1, pl.ds(start, LANES)][:, 0:1]


def _attn_fwd_t(qh, kh, vt, c_pairs, name):
    t = qh.shape[0]
    bq = _pick(t, ATTN_FWD_BLOCK)
    nq = t // bq

    def body(q_ref, k_ref, vt_ref, c_ref, ot_ref, otb_ref, lse_ref, acc_ref):
        i = pl.program_id(1)
        q0 = pl.multiple_of(i * bq, bq)
        qs = (q_ref[:, 0:LANES], q_ref[:, LANES:2 * LANES])
        cq = [_block_scalar(c_ref, a, q0) for a in (0, 1)]
        acc_ref[...] = jnp.zeros_like(acc_ref)
        keep = _rows_iota((bq, bq)) <= lax.broadcasted_iota(jnp.int32, (bq, bq), 1)

        def step(j, carry, masked):
            k0 = pl.multiple_of(j * bq, bq)
            kb = k_ref[pl.ds(k0, bq), :]
            new = []
            for a in (0, 1):
                m_old, l_old = carry[2 * a], carry[2 * a + 1]
                st = lax.dot_general(kb[:, a * LANES:(a + 1) * LANES], qs[a], _DOT_DIMS["nt"],
                                     preferred_element_type=F32)
                if masked:
                    st = jnp.where(keep, st, NEG_BIG)
                sigma = cq[a] - _block_scalar(c_ref, a, k0)
                m_new = jnp.maximum(m_old, jnp.max(st, axis=0, keepdims=True) + sigma)
                pt = jnp.exp(st - (m_new - sigma))
                alpha = jnp.exp(m_old - m_new)
                l_new = alpha * l_old + jnp.sum(pt, axis=0, keepdims=True)
                vta = vt_ref[a * HEAD_DIM:(a + 1) * HEAD_DIM, pl.ds(k0, bq)]
                acc_ref[a] = alpha * acc_ref[a] + lax.dot_general(
                    vta, pt.astype(BF16), _DOT_DIMS["nn"], preferred_element_type=F32)
                new += [m_new, l_new]
            return tuple(new)

        neg = jnp.full((1, bq), NEG_BIG, F32)
        zero = jnp.zeros((1, bq), F32)
        carry = lax.fori_loop(0, i, lambda j, c: step(j, c, False), (neg, zero, neg, zero))
        m_a, l_a, m_b, l_b = step(i, carry, True)
        ot = jnp.concatenate([acc_ref[0] / l_a, acc_ref[1] / l_b], axis=0)
        ot_ref[...] = ot
        otb_ref[...] = ot.astype(BF16)
        lse_ref[...] = jnp.zeros_like(lse_ref)
        lse_ref[0:1, :] = m_a + jnp.log(l_a)
        lse_ref[1:2, :] = m_b + jnp.log(l_b)

    rows = pl.BlockSpec((None, SUBLANES, bq), lambda hp, i: (hp, 0, i))
    otile = pl.BlockSpec((LANES, bq), lambda hp, i: (hp, i))
    return pl.pallas_call(
        body, name=name, grid=(N_PAIRS, nq),
        in_specs=[pl.BlockSpec((bq, 2 * LANES), lambda hp, i: (i, hp)),
                  pl.BlockSpec((t, 2 * LANES), lambda hp, i: (0, hp)),
                  pl.BlockSpec((LANES, t), lambda hp, i: (hp, 0)),
                  pl.BlockSpec((None, SUBLANES, t), lambda hp, i: (hp, 0, 0))],
        out_specs=[otile, otile, rows],
        out_shape=[jax.ShapeDtypeStruct((D_MODEL, t), F32), jax.ShapeDtypeStruct((D_MODEL, t), BF16),
                   jax.ShapeDtypeStruct((N_PAIRS, SUBLANES, t), F32)],
        scratch_shapes=[pltpu.VMEM((2, HEAD_DIM, bq), F32)],
        compiler_params=_cparams("parallel", "arbitrary"),
    )(qh, kh, vt, c_pairs)


def _attn_delta_t(dot, ot, name):
    t = dot.shape[1]
    bt = _pick(t, 512)

    def body(do_ref, o_ref, d_ref, dob_ref):
        dob = do_ref[...].astype(BF16)
        prod = dob.astype(F32) * o_ref[...]
        d_ref[...] = jnp.zeros_like(d_ref)
        d_ref[0:1, :] = jnp.sum(prod[0:HEAD_DIM], axis=0, keepdims=True)
        d_ref[1:2, :] = jnp.sum(prod[HEAD_DIM:], axis=0, keepdims=True)
        dob_ref[...] = dob

    tile = pl.BlockSpec((LANES, bt), lambda hp, i: (hp, i))
    return pl.pallas_call(
        body, name=name, grid=(N_PAIRS, t // bt), in_specs=[tile, tile],
        out_specs=[pl.BlockSpec((None, SUBLANES, bt), lambda hp, i: (hp, 0, i)), tile],
        out_shape=[jax.ShapeDtypeStruct((N_PAIRS, SUBLANES, t), F32), jax.ShapeDtypeStruct((D_MODEL, t), BF16)],
        compiler_params=_cparams("parallel", "parallel"),
    )(dot, ot)


def _attn_bwd_t(kh, kt, qt, qkv, dotb, c_pairs, lse_pairs, d_pairs, name):
    t = kh.shape[0]
    bk = _pick(t, ATTN_BWD_BLOCK)
    nk = t // bk
    ref_tile = _pick(t, ATTN_FWD_BLOCK)
    assert ref_tile % bk == 0
    scale = 1.0 / math.sqrt(HEAD_DIM)

    def ref_start(start):
        return pl.multiple_of((start // ref_tile) * ref_tile, ref_tile)

    def body(k_ref, kt_ref, v_ref, qt_ref, dot_ref, c_ref, lse_ref, d_ref,
             dk_ref, dv_ref, dc_ref, dq_hbm, dka_ref, dva_ref, dqa_ref, sem):
        hp, j = pl.program_id(0), pl.program_id(1)
        k0 = pl.multiple_of(j * bk, bk)
        ks = (k_ref[:, 0:LANES], k_ref[:, LANES:2 * LANES])
        vb = v_ref[...]
        ck = [_block_scalar(c_ref, a, ref_start(k0)) for a in (0, 1)]
        dka_ref[...] = jnp.zeros_like(dka_ref)
        dva_ref[...] = jnp.zeros_like(dva_ref)

        @pl.when(j == 0)
        def _():
            dqa_ref[...] = jnp.zeros_like(dqa_ref)

        keep = _rows_iota((bk, bk)) <= lax.broadcasted_iota(jnp.int32, (bk, bk), 1)
        top = _rows_iota((LANES, bk)) < HEAD_DIM

        def step(i, masked):
            q0 = pl.multiple_of(i * bk, bk)
            dot2 = dot_ref[:, pl.ds(q0, bk)]
            zero = jnp.zeros_like(dot2)
            for a in (0, 1):
                qta = qt_ref[a * LANES:(a + 1) * LANES, pl.ds(q0, bk)]
                st = lax.dot_general(ks[a], qta, _DOT_DIMS["nn"], preferred_element_type=F32)
                sigma = _block_scalar(c_ref, a, ref_start(q0)) - ck[a]
                pt = jnp.exp(st - (lse_ref[a:a + 1, pl.ds(q0, bk)] - sigma))
                if masked:
                    pt = jnp.where(keep, pt, 0.0)
                dota = jnp.where(top, dot2, zero) if a == 0 else jnp.where(top, zero, dot2)
                dpt = lax.dot_general(vb, dota, _DOT_DIMS["nn"], preferred_element_type=F32)
                dstb = (pt * (dpt - d_ref[a:a + 1, pl.ds(q0, bk)])).astype(BF16)
                dva_ref[a] += lax.dot_general(dot2[a * HEAD_DIM:(a + 1) * HEAD_DIM, :], pt.astype(BF16),
                                              _DOT_DIMS["nt"], preferred_element_type=F32)
                dka_ref[a] += lax.dot_general(qta, dstb, _DOT_DIMS["nt"], preferred_element_type=F32)
                dqa_ref[a, :, pl.ds(q0, bk)] += lax.dot_general(
                    kt_ref[a * LANES:(a + 1) * LANES, :], dstb, _DOT_DIMS["nn"], preferred_element_type=F32)

        def loop_body(i, carry):
            step(i, False)
            return carry

        step(j, True)
        lax.fori_loop(j + 1, nk, loop_body, 0)
        dk_ref[...] = (jnp.concatenate([dka_ref[0, 0:HEAD_DIM, :], dka_ref[1, 0:HEAD_DIM, :]], axis=0)
                       * scale).astype(BF16)
        dv_ref[...] = jnp.concatenate([dva_ref[0], dva_ref[1]], axis=0).astype(BF16)
        dc_ref[...] = jnp.zeros_like(dc_ref)
        dc_ref[0:1, :] = dka_ref[0, AUG_C:AUG_C + 1, :]
        dc_ref[1:2, :] = dka_ref[1, AUG_C:AUG_C + 1, :]

        @pl.when(j == nk - 1)
        def _():
            for a in (0, 1):
                row0 = pl.multiple_of((2 * hp + a) * LANES, LANES)
                cp = pltpu.make_async_copy(dqa_ref.at[a], dq_hbm.at[pl.ds(row0, LANES), :], sem)
                cp.start()
                cp.wait()

    once = pl.Buffered(1)
    rows = pl.BlockSpec((None, SUBLANES, t), lambda hp, j: (hp, 0, 0))
    kv_out = pl.BlockSpec((LANES, bk), lambda hp, j: (hp, j))
    return pl.pallas_call(
        body, name=name, grid=(N_PAIRS, nk),
        in_specs=[pl.BlockSpec((bk, 2 * LANES), lambda hp, j: (j, hp)),
                  pl.BlockSpec((2 * LANES, bk), lambda hp, j: (hp, j)),
                  pl.BlockSpec((bk, LANES), lambda hp, j: (j, 2 * N_PAIRS + hp)),
                  pl.BlockSpec((2 * LANES, t), lambda hp, j: (hp, 0), pipeline_mode=once),
                  pl.BlockSpec((LANES, t), lambda hp, j: (hp, 0), pipeline_mode=once),
                  rows, rows, rows],
        out_specs=[kv_out, kv_out, pl.BlockSpec((None, SUBLANES, bk), lambda hp, j: (hp, 0, j)),
                   pl.BlockSpec(memory_space=pltpu.HBM)],
        out_shape=[jax.ShapeDtypeStruct((D_MODEL, t), BF16), jax.ShapeDtypeStruct((D_MODEL, t), BF16),
                   jax.ShapeDtypeStruct((N_PAIRS, SUBLANES, t), F32),
                   jax.ShapeDtypeStruct((N_HEADS * LANES, t), F32)],
        scratch_shapes=[pltpu.VMEM((2, LANES, bk), F32), pltpu.VMEM((2, HEAD_DIM, bk), F32),
                        pltpu.VMEM((2, LANES, t), F32), pltpu.SemaphoreType.DMA],
        compiler_params=_cparams("arbitrary", "arbitrary"),
    )(kh, kt, qkv, qt, dotb, c_pairs, lse_pairs, d_pairs)


def _scan(a, u, name, reverse=False):
    t, c = a.shape
    bt, cb = _pick(t, ROW_TILE), _pick(c, 1024)
    nt = t // bt
    ngroups = bt // SUBLANES

    def body(a_ref, u_ref, h_ref, carry_ref, as_ref, us_ref):
        @pl.when(pl.program_id(1) == 0)
        def _():
            carry_ref[...] = jnp.zeros_like(carry_ref)

        av, uv = a_ref[...], u_ref[...]
        sub = _rows_iota((bt, cb)) % SUBLANES
        for s in (1, 2, 4):
            if reverse:
                a_sh, u_sh = pltpu.roll(av, bt - s, axis=0), pltpu.roll(uv, bt - s, axis=0)
                valid = sub < SUBLANES - s
            else:
                a_sh, u_sh = pltpu.roll(av, s, axis=0), pltpu.roll(uv, s, axis=0)
                valid = sub >= s
            uv = jnp.where(valid, uv + av * u_sh, uv)
            av = jnp.where(valid, av * a_sh, av)
        as_ref[...] = av
        us_ref[...] = uv
        edge = 0 if reverse else SUBLANES - 1
        pick = _rows_iota((SUBLANES, cb)) == edge

        def group(gi, carry):
            g = (ngroups - 1 - gi) if reverse else gi
            r0 = pl.multiple_of(g * SUBLANES, SUBLANES)
            h8 = us_ref[pl.ds(r0, SUBLANES), :] + as_ref[pl.ds(r0, SUBLANES), :] * carry
            h_ref[pl.ds(r0, SUBLANES), :] = h8
            return jnp.sum(jnp.where(pick, h8, 0.0), axis=0, keepdims=True)

        carry_ref[...] = lax.fori_loop(0, ngroups, group, carry_ref[...])

    if reverse:
        tile = pl.BlockSpec((bt, cb), lambda j, i: (nt - 1 - i, j))
    else:
        tile = pl.BlockSpec((bt, cb), lambda j, i: (i, j))
    return pl.pallas_call(
        body, name=name, grid=(c // cb, nt), in_specs=[tile, tile], out_specs=tile,
        out_shape=jax.ShapeDtypeStruct((t, c), F32),
        scratch_shapes=[pltpu.VMEM((1, cb), F32), pltpu.VMEM((bt, cb), F32), pltpu.VMEM((bt, cb), F32)],
        compiler_params=_cparams("parallel", "arbitrary"),
    )(a, u)


def _rg_conv_fwd(proj, cw, cb_, name):
    t = proj.shape[0]
    bt, cb = _pick(t, ROW_TILE), D_MODEL

    def body(x_ref, halo_ref, w_ref, b_ref, o_ref):
        keep = (pl.program_id(0) > 0).astype(F32)
        o_ref[...] = _causal_conv(x_ref[...], halo_ref[...] * keep, w_ref, CONV_B) + b_ref[...]

    tile = pl.BlockSpec((bt, cb), lambda i, j: (i, j))
    return pl.pallas_call(
        body, name=name, grid=(t // bt, 1),
        in_specs=[tile, _prev_halo_spec(bt, cb), pl.BlockSpec((CONV_B, cb), lambda i, j: (0, 0)),
                  pl.BlockSpec((1, cb), lambda i, j: (0, 0))],
        out_specs=tile, out_shape=jax.ShapeDtypeStruct((t, D_MODEL), F32),
        compiler_params=_cparams("parallel", "parallel"),
    )(proj, proj, cw, cb_.reshape(1, D_MODEL))


def _conv_bwd_w(x, dy, ksize, name):
    t, c = dy.shape
    bt = _pick(t, ROW_TILE)
    r = bt // SUBLANES

    def body(x_ref, halo_ref, dy_ref, dw_ref, db_ref):
        @pl.when(pl.program_id(0) == 0)
        def _():
            dw_ref[...] = jnp.zeros_like(dw_ref)
            db_ref[...] = jnp.zeros_like(db_ref)

        keep = (pl.program_id(0) > 0).astype(F32)
        xv, halo, dyv = x_ref[...], halo_ref[...] * keep, dy_ref[...]
        db_ref[...] += jnp.sum(dyv, axis=0, keepdims=True)
        for j in range(ksize):
            dw_ref[j:j + 1, :] += jnp.sum(dyv * _shift_down(xv, halo, ksize - 1 - j), axis=0, keepdims=True)

    tile = pl.BlockSpec((bt, c), lambda i: (i, 0))
    return pl.pallas_call(
        body, name=name, grid=(t // bt,),
        in_specs=[tile, pl.BlockSpec((SUBLANES, c), lambda i: (jnp.maximum(i * r - 1, 0), 0)), tile],
        out_specs=[pl.BlockSpec((ksize, c), lambda i: (0, 0)), pl.BlockSpec((1, c), lambda i: (0, 0))],
        out_shape=[jax.ShapeDtypeStruct((ksize, c), F32), jax.ShapeDtypeStruct((1, c), F32)],
        compiler_params=_cparams("arbitrary"),
    )(x, x, dy)


def _rg_gate_math(xc, wa_ref, wi_ref, ba_ref, bi_ref, lam_ref):
    xb = xc.astype(BF16)
    ra = lax.dot_general(xb, wa_ref[...], _DOT_DIMS["nn"], preferred_element_type=F32) + ba_ref[...]
    ia = lax.dot_general(xb, wi_ref[...], _DOT_DIMS["nn"], preferred_element_type=F32) + bi_ref[...]
    r, ig = _sigmoid(ra), _sigmoid(ia)
    sp = _softplus(-lam_ref[...])
    log_a = -LRU_C * r * sp
    a = jnp.exp(log_a)
    mult = jnp.sqrt(-_expm1(2.0 * log_a))
    return xb, r, ig, sp, a, mult


def _rg_gate_specs(bt, time_first):
    if time_first:
        tile = pl.BlockSpec((bt, BLOCK_B), lambda i, n: (i, n))
        w = pl.BlockSpec((None, BLOCK_B, BLOCK_B), lambda i, n: (n, 0, 0))
        v = pl.BlockSpec((None, 1, BLOCK_B), lambda i, n: (n, 0, 0))
    else:
        tile = pl.BlockSpec((bt, BLOCK_B), lambda n, i: (i, n))
        w = pl.BlockSpec((None, BLOCK_B, BLOCK_B), lambda n, i: (n, 0, 0))
        v = pl.BlockSpec((None, 1, BLOCK_B), lambda n, i: (n, 0, 0))
    return tile, w, v


def _rg_gate_fwd(xc, wa, ba, wi, bi, lam, name):
    t = xc.shape[0]
    bt = _pick(t, 512)

    def body(x_ref, wa_ref, wi_ref, ba_ref, bi_ref, lam_ref, a_ref, u_ref):
        xcv = x_ref[...]
        _, _, ig, _, a, mult = _rg_gate_math(xcv, wa_ref, wi_ref, ba_ref, bi_ref, lam_ref)
        a_ref[...] = a
        u_ref[...] = mult * (ig * xcv)

    tile, w, v = _rg_gate_specs(bt, True)
    return pl.pallas_call(
        body, name=name, grid=(t // bt, N_BLOCKS_B), in_specs=[tile, w, w, v, v, v], out_specs=[tile, tile],
        out_shape=[jax.ShapeDtypeStruct((t, D_MODEL), F32), jax.ShapeDtypeStruct((t, D_MODEL), F32)],
        compiler_params=_cparams("parallel", "parallel"),
    )(xc, wa, wi, ba, bi, lam)


def _rg_gate_bwd(xc, g, h, wa, ba, wi, bi, lam, name):
    t = xc.shape[0]
    bt = _pick(t, 512)
    rr = bt // SUBLANES

    def body(x_ref, g_ref, h_ref, hh_ref, wa_ref, wi_ref, ba_ref, bi_ref, lam_ref,
             dx_ref, dwa_ref, dwi_ref, dba_ref, dbi_ref, dlam_ref):
        @pl.when(pl.program_id(1) == 0)
        def _():
            for ref in (dwa_ref, dwi_ref, dba_ref, dbi_ref, dlam_ref):
                ref[...] = jnp.zeros_like(ref)

        keep = (pl.program_id(1) > 0).astype(F32)
        xcv, gv = x_ref[...], g_ref[...]
        xb, r, ig, sp, a, mult = _rg_gate_math(xcv, wa_ref, wi_ref, ba_ref, bi_ref, lam_ref)
        h_prev = _shift_down(h_ref[...], hh_ref[...] * keep, 1)
        da = gv * h_prev
        dmult = gv * ig * xcv
        dig = gv * mult * xcv
        dxc = gv * mult * ig
        dlog_a = da * a - dmult * (a * a / mult)
        dr = dlog_a * (-LRU_C * sp)
        dsp = jnp.sum(dlog_a * (-LRU_C * r), axis=0, keepdims=True)
        dlam_ref[...] += dsp * (-_sigmoid(-lam_ref[...]))
        dra = dr * r * (1.0 - r)
        dia = dig * ig * (1.0 - ig)
        dba_ref[...] += jnp.sum(dra, axis=0, keepdims=True)
        dbi_ref[...] += jnp.sum(dia, axis=0, keepdims=True)
        drab, diab = dra.astype(BF16), dia.astype(BF16)
        dwa_ref[...] += lax.dot_general(xb, drab, _DOT_DIMS["tn"], preferred_element_type=F32)
        dwi_ref[...] += lax.dot_general(xb, diab, _DOT_DIMS["tn"], preferred_element_type=F32)
        dxc = dxc + lax.dot_general(drab, wa_ref[...], _DOT_DIMS["nt"], preferred_element_type=F32)
        dxc = dxc + lax.dot_general(diab, wi_ref[...], _DOT_DIMS["nt"], preferred_element_type=F32)
        dx_ref[...] = dxc

    tile, w, v = _rg_gate_specs(bt, False)
    halo = pl.BlockSpec((SUBLANES, BLOCK_B), lambda n, i: (jnp.maximum(i * rr - 1, 0), n))
    wshape = jax.ShapeDtypeStruct((N_BLOCKS_B, BLOCK_B, BLOCK_B), F32)
    vshape = jax.ShapeDtypeStruct((N_BLOCKS_B, 1, BLOCK_B), F32)
    return pl.pallas_call(
        body, name=name, grid=(N_BLOCKS_B, t // bt),
        in_specs=[tile, tile, tile, halo, w, w, v, v, v],
        out_specs=[tile, w, w, v, v, v],
        out_shape=[jax.ShapeDtypeStruct((t, D_MODEL), F32), wshape, wshape, vshape, vshape, vshape],
        compiler_params=_cparams("parallel", "arbitrary"),
    )(xc, g, h, h, wa, wi, ba, bi, lam)


def _rg_out_fwd(h, proj, name):
    t = h.shape[0]
    bt = _pick(t, ROW_TILE)

    def body(h_ref, g_ref, y_ref):
        y_ref[...] = (h_ref[...] * _gelu(g_ref[...])).astype(BF16)

    tile = pl.BlockSpec((bt, D_MODEL), lambda i: (i, 0))
    return pl.pallas_call(
        body, name=name, grid=(t // bt,), in_specs=[tile, pl.BlockSpec((bt, D_MODEL), lambda i: (i, 1))],
        out_specs=tile, out_shape=jax.ShapeDtypeStruct((t, D_MODEL), BF16),
        compiler_params=_cparams("parallel"),
    )(h, proj)


def _rg_out_bwd(dy, h, proj, name):
    t = h.shape[0]
    bt = _pick(t, ROW_TILE)

    def body(dy_ref, h_ref, g_ref, dh_ref, dg_ref):
        gl, glg = _gelu_and_grad(g_ref[...])
        dyv = dy_ref[...]
        dh_ref[...] = dyv * gl
        dg_ref[...] = (dyv * h_ref[...] * glg).astype(BF16)

    tile = pl.BlockSpec((bt, D_MODEL), lambda i: (i, 0))
    return pl.pallas_call(
        body, name=name, grid=(t // bt,),
        in_specs=[tile, tile, pl.BlockSpec((bt, D_MODEL), lambda i: (i, 1))], out_specs=[tile, tile],
        out_shape=[jax.ShapeDtypeStruct((t, D_MODEL), F32), jax.ShapeDtypeStruct((t, D_MODEL), BF16)],
        compiler_params=_cparams("parallel"),
    )(dy, h, proj)


def _shift_up_one(a, name):
    t, c = a.shape
    bt = _pick(t, ROW_TILE)

    def body(a_ref, halo_ref, o_ref):
        o_ref[...] = _shift_up(a_ref[...], halo_ref[...], 1)

    tile = pl.BlockSpec((bt, c), lambda i, j: (i, j))
    return pl.pallas_call(
        body, name=name, grid=(t // bt, 1), in_specs=[tile, _next_halo_spec(bt, c, t)], out_specs=tile,
        out_shape=jax.ShapeDtypeStruct((t, c), F32), compiler_params=_cparams("parallel", "parallel"),
    )(a, a)


ADAM_ROWS = 64


def _adamw(recv, w, m, v, name):
    _, r, c = recv.shape
    br = ADAM_ROWS
    assert r % br == 0

    def body(r_ref, w_ref, m_ref, v_ref, g_ref, d_ref, nm_ref, nv_ref):
        g = r_ref[0].astype(F32)
        for s in range(1, N_DEV):
            g = g + r_ref[s].astype(F32)
        m_new = ADAM_B1 * m_ref[...] + (1.0 - ADAM_B1) * g
        v_new = ADAM_B2 * v_ref[...] + (1.0 - ADAM_B2) * (g * g)
        m_hat = m_new / (1.0 - ADAM_B1 ** ADAM_STEP)
        v_hat = v_new / (1.0 - ADAM_B2 ** ADAM_STEP)
        g_ref[...] = g
        d_ref[...] = -ADAM_LR * (m_hat / (jnp.sqrt(v_hat) + ADAM_EPS) + ADAM_WD * w_ref[...])
        nm_ref[...] = m_new
        nv_ref[...] = v_new

    tile = pl.BlockSpec((br, c), lambda i: (i, 0))
    shape = jax.ShapeDtypeStruct((r, c), F32)
    return pl.pallas_call(
        body, name=name, grid=(r // br,),
        in_specs=[pl.BlockSpec((N_DEV, br, c), lambda i: (0, i, 0)), tile, tile, tile],
        out_specs=[tile] * 4, out_shape=[shape] * 4, compiler_params=_cparams("parallel"),
    )(recv, w, m, v)


def _all_to_all(src, name):
    def body(src_ref, out_ref, send_sems, recv_sems, local_sem):
        pos = [lax.axis_index(ax) for ax in MESH_AXES]
        me = 4 * pos[0] + 2 * pos[1] + pos[2]

        def peer_of(k):
            p = [(1 - pos[b]) if (k >> (2 - b)) & 1 else pos[b] for b in range(3)]
            return tuple(p), 4 * p[0] + 2 * p[1] + p[2]

        def copy(k):
            peer, peer_idx = peer_of(k)
            return pltpu.make_async_remote_copy(
                src_ref=src_ref.at[peer_idx], dst_ref=out_ref.at[me],
                send_sem=send_sems.at[k - 1], recv_sem=recv_sems.at[k - 1],
                device_id=peer, device_id_type=pl.DeviceIdType.MESH)

        def arrival(k):
            peer, peer_idx = peer_of(k)
            return pltpu.make_async_remote_copy(
                src_ref=src_ref.at[me], dst_ref=out_ref.at[peer_idx],
                send_sem=send_sems.at[k - 1], recv_sem=recv_sems.at[k - 1],
                device_id=peer, device_id_type=pl.DeviceIdType.MESH)

        mine = pltpu.make_async_copy(src_ref.at[me], out_ref.at[me], local_sem)
        mine.start()
        sends = [copy(k) for k in range(1, N_DEV)]
        for cp in sends:
            cp.start()
        for k in range(1, N_DEV):
            arrival(k).wait_recv()
        for cp in sends:
            cp.wait_send()
        mine.wait()

    hbm = pl.BlockSpec(memory_space=pltpu.HBM)
    return pl.pallas_call(
        body, name=name, in_specs=[hbm], out_specs=hbm,
        out_shape=jax.ShapeDtypeStruct(src.shape, src.dtype),
        scratch_shapes=[pltpu.SemaphoreType.DMA((N_DEV - 1,)), pltpu.SemaphoreType.DMA((N_DEV - 1,)),
                        pltpu.SemaphoreType.DMA],
        compiler_params=pltpu.CompilerParams(has_side_effects=True),
    )(src)


def _all_gather(src, name):
    def body(src_ref, out_ref, send_sems, recv_sems, local_sem):
        x, y, c = (lax.axis_index(ax) for ax in MESH_AXES)
        me, sibling = (x, y, c), (x, y, 1 - c)
        chips = [(1 - x, y), (x, 1 - y), (1 - x, 1 - y)]

        def slot(px, py, pc):
            return out_ref.at[4 * px + 2 * py + pc]

        def copy(k, block, to, from_src=False):
            return pltpu.make_async_remote_copy(
                src_ref=src_ref if from_src else slot(*block), dst_ref=slot(*block),
                send_sem=send_sems.at[k], recv_sem=recv_sems.at[k],
                device_id=to, device_id_type=pl.DeviceIdType.MESH)

        mine = pltpu.make_async_copy(src_ref, slot(*me), local_sem)
        mine.start()
        first = [copy(0, me, sibling, True)] + [copy(1 + j, me, (*chip, c), True) for j, chip in enumerate(chips)]
        for cp in first:
            cp.start()
        passed = [copy(4 + j, (*chip, c), sibling) for j, chip in enumerate(chips)]
        for j, chip in enumerate(chips):
            copy(1 + j, (*chip, c), me).wait_recv()
            passed[j].start()
        copy(0, sibling, me).wait_recv()
        for j, chip in enumerate(chips):
            copy(4 + j, (*chip, 1 - c), me).wait_recv()
        for cp in first + passed:
            cp.wait_send()
        mine.wait()

    hbm = pl.BlockSpec(memory_space=pltpu.HBM)
    return pl.pallas_call(
        body, name=name, in_specs=[hbm], out_specs=hbm,
        out_shape=jax.ShapeDtypeStruct((N_DEV,) + tuple(src.shape), src.dtype),
        scratch_shapes=[pltpu.SemaphoreType.DMA((N_DEV - 1,)), pltpu.SemaphoreType.DMA((N_DEV - 1,)),
                        pltpu.SemaphoreType.DMA],
        compiler_params=pltpu.CompilerParams(has_side_effects=True),
    )(src)


WEIGHTS = ['a_w_in', 'a_b_f', 'a_w_out', 'b_w_in', 'b_conv_w', 'b_conv_b', 'b_w_a', 'b_b_a', 'b_w_i', 'b_b_i',
           'b_lam', 'b_w_out', 'f_w_up', 'f_conv_w', 'f_conv_b', 'f_w_down', 'ln1_g', 'ln1_b', 'ln2_g', 'ln2_b',
           'ple_w', 'ple_gate_w', 'ple_gate_b']
SHARD_AXIS = {'a_w_in': 2, 'a_b_f': None, 'a_w_out': 1, 'b_w_in': 2, 'b_conv_w': 2, 'b_conv_b': 1, 'b_w_a': None,
              'b_b_a': None, 'b_w_i': None, 'b_b_i': None, 'b_lam': 1, 'b_w_out': 1, 'f_w_up': 2, 'f_conv_w': 2,
              'f_conv_b': None, 'f_w_down': 1, 'ln1_g': None, 'ln1_b': None, 'ln2_g': None, 'ln2_b': None,
              'ple_w': 2, 'ple_gate_w': 1, 'ple_gate_b': None}
MATMUL_WEIGHTS = ['a_w_in', 'a_w_out', 'b_w_in', 'b_w_out', 'f_w_up', 'f_w_down', 'ple_w', 'ple_gate_w']
SMALL_SHARDED = ['b_conv_w', 'b_conv_b', 'b_lam', 'f_conv_w']
GRADS_AS_BF16 = MATMUL_WEIGHTS + ['b_w_a', 'b_w_i']
PACK_COLS = 1024


def _to_shards(full, axis):
    return jnp.stack(jnp.split(full, N_DEV, axis=axis))


def _from_shards(pieces, axis):
    return jnp.concatenate([pieces[d] for d in range(N_DEV)], axis=axis)


PIECE_ROWS = 16


def _piece_rows(size):
    rows = -(-size // PACK_COLS)
    return -(-rows // PIECE_ROWS) * PIECE_ROWS


def _pack_pieces(pieces, total_mult=PIECE_ROWS):
    lead = pieces[0].shape[:-1]
    blocks = []
    for pc in pieces:
        n = pc.shape[-1]
        rows = _piece_rows(n)
        pad = [(0, 0)] * len(lead) + [(0, rows * PACK_COLS - n)]
        blocks.append(jnp.pad(pc, pad).reshape(lead + (rows, PACK_COLS)))
    total = sum(b.shape[-2] for b in blocks)
    extra = -total % total_mult
    if extra:
        blocks.append(jnp.zeros(lead + (extra, PACK_COLS), pieces[0].dtype))
    return jnp.concatenate(blocks, axis=len(lead))


def _unpack_pieces(packed, shapes):
    lead = packed.shape[:-2]
    out, row = [], 0
    for shp in shapes:
        size = math.prod(shp)
        rows = _piece_rows(size)
        block = lax.slice_in_dim(packed, row, row + rows, axis=len(lead))
        flat = block.reshape(lead + (rows * PACK_COLS,))
        out.append(lax.slice_in_dim(flat, 0, size, axis=len(lead)).reshape(lead + tuple(shp)))
        row += rows
    return out


def _gather_weights(local, names, dtype, name):
    packed = _pack_pieces([local[n].astype(dtype).reshape(-1) for n in names])
    gathered = _all_gather(packed, name)
    pieces = _unpack_pieces(gathered, [local[n].shape for n in names])
    return {n: _from_shards(pc, SHARD_AXIS[n]) for n, pc in zip(names, pieces)}


def _mixer_a_fwd(tag, xb, w):
    qkv = _mm(xb, w["wqkv"], "nn", BF16, f"{tag}_qkv")
    fg = _mm(xb, w["wf"], "nn", F32, f"{tag}_fgproj")
    fg_rows = fg[:, :N_HEADS].T
    c_rows = _fgate_fwd(fg_rows, w["b_f"], f"{tag}_fgate")
    c_pairs = _rows_to_pairs(c_rows)
    qh, kh, qt, kt, vt = _attn_prep(qkv, _rows_to_cols(c_rows), f"{tag}_attn_prep")
    ot, otb, lse_pairs = _attn_fwd_t(qh, kh, vt, c_pairs, f"{tag}_attn")
    m = _mm(otb, w["wout"], "tn", F32, f"{tag}_oproj")
    return m, dict(qkv=qkv, kh=kh, qt=qt, kt=kt, fg_rows=fg_rows, c_pairs=c_pairs, ot=ot, otb=otb,
                   lse_pairs=lse_pairs)


def _mixer_a_bwd(tag, dz, dzb, xb, w, s):
    t = xb.shape[0]
    dot = _mm(w["wout"], dzb, "nt", F32, f"{tag}_b_do")
    g_wout = _mm(s["otb"], dzb, "nn", F32, f"{tag}_b_dwout")
    d_pairs, dotb = _attn_delta_t(dot, s["ot"], f"{tag}_b_delta")
    dkt, dvt, dck_pairs, dq_aug = _attn_bwd_t(s["kh"], s["kt"], s["qt"], s["qkv"], dotb, s["c_pairs"],
                                              s["lse_pairs"], d_pairs, f"{tag}_b_attn")
    dq_aug = dq_aug.reshape(N_HEADS, LANES, t)
    dfg_rows, db_f = _fgate_bwd(dck_pairs[:, :2, :].reshape(N_HEADS, t), dq_aug[:, AUG_ONE, :], s["fg_rows"],
                                w["b_f"], f"{tag}_b_fgate")
    dqt = dq_aug[:, :HEAD_DIM, :].reshape(D_MODEL, t).astype(BF16)
    dqkv = jnp.concatenate([dqt, dkt, dvt], axis=0).T
    dfg = jnp.pad(dfg_rows.T, ((0, 0), (0, LANES - N_HEADS))).astype(BF16)
    dx = _mm(dqkv, w["wqkv"], "nt", F32, f"{tag}_b_dx_qkv", add=dz, add_scale=ALPHA, tk=3 * D_MODEL)
    dx = _mm(dfg, w["wf"], "nt", F32, f"{tag}_b_dx_fg", add=dx)
    g_wqkv = _mm(xb, dqkv, "tn", F32, f"{tag}_b_dwqkv")
    g_wf = _mm(xb, dfg, "tn", F32, f"{tag}_b_dwf")[:, :N_HEADS]
    grads = dict(a_w_in=jnp.concatenate([g_wqkv, g_wf], axis=1), a_b_f=db_f.reshape(N_HEADS), a_w_out=g_wout)
    return dx, grads


def _mixer_b_fwd(tag, xb, w):
    proj = _mm(xb, w["win"], "nn", F32, f"{tag}_proj")
    xc = _rg_conv_fwd(proj, w["conv_w"], w["conv_b"], f"{tag}_conv")
    a, u = _rg_gate_fwd(xc, w["wa"], w["ba"], w["wi"], w["bi"], w["lam"], f"{tag}_gate")
    h = _scan(a, u, f"{tag}_scan")
    y = _rg_out_fwd(h, proj, f"{tag}_out")
    m = _mm(y, w["wout"], "nn", F32, f"{tag}_oproj")
    return m, dict(proj=proj, xc=xc, a=a, h=h, y=y)


def _mixer_b_bwd(tag, dz, dzb, xb, w, s):
    dy = _mm(dzb, w["wout"], "nt", F32, f"{tag}_b_dy")
    g_wout = _mm(s["y"], dzb, "tn", F32, f"{tag}_b_dwout")
    dh, dgate = _rg_out_bwd(dy, s["h"], s["proj"], f"{tag}_b_out")
    g = _scan(_shift_up_one(s["a"], f"{tag}_b_shift"), dh, f"{tag}_b_scan", reverse=True)
    dxc, g_wa, g_wi, g_ba, g_bi, g_lam = _rg_gate_bwd(
        s["xc"], g, s["h"], w["wa"], w["ba"], w["wi"], w["bi"], w["lam"], f"{tag}_b_gate")
    dxp = _conv_bwd_x(dxc, w["conv_w"], f"{tag}_b_convx", BF16)
    g_cw, g_cb = _conv_bwd_w(s["proj"], dxc, CONV_B, f"{tag}_b_convw")
    dproj = jnp.concatenate([dxp, dgate], axis=1)
    dx = _mm(dproj, w["win"], "nt", F32, f"{tag}_b_dx", add=dz, add_scale=ALPHA, tk=2 * D_MODEL)
    g_win = _mm(xb, dproj, "tn", F32, f"{tag}_b_dwin")
    grads = dict(b_w_in=g_win, b_conv_w=g_cw, b_conv_b=g_cb.reshape(D_MODEL), b_w_a=g_wa,
                 b_b_a=g_ba.reshape(N_BLOCKS_B, BLOCK_B), b_w_i=g_wi, b_b_i=g_bi.reshape(N_BLOCKS_B, BLOCK_B),
                 b_lam=g_lam.reshape(D_MODEL), b_w_out=g_wout)
    return dx, grads


def _layer_fwd(i, x, xb, pb, w):
    tag = f"L{i}"
    mix = _mixer_a_fwd if i % 2 == 0 else _mixer_b_fwd
    m, sm = mix(tag, xb, w)
    x1, x1b, z1 = _ln_fwd(x, m, w["ln1_g"], w["ln1_b"], f"{tag}_ln1")
    h = _mm(x1b, w["wup"], "nn", F32, f"{tag}_ffn_up")
    y = _ffn_mid_fwd(h, w["fconv_w"], w["fconv_b"], f"{tag}_ffn_mid")
    ff = _mm(y, w["wdown"], "nn", F32, f"{tag}_ffn_down", tk=D_FF)
    x2, x2b, z2 = _ln_fwd(x1, ff, w["ln2_g"], w["ln2_b"], f"{tag}_ln2")
    gl = _mm(x2b, w["wg"], "nn", F32, f"{tag}_ple_gate")
    e = _mm(pb, w["wp"], "nn", F32, f"{tag}_ple_emb")
    x3, x3b = _ple_fwd(x2, gl, e, w["bg"], f"{tag}_ple")
    saved = dict(mixer=sm, xb=xb, x1b=x1b, z1=z1, h=h, y=y, x2b=x2b, z2=z2, gl=gl, e=e, pb=pb)
    return x3, x3b, saved


def _layer_bwd(i, dx3, w, s):
    tag = f"L{i}"
    dgl, de, g_bg = _ple_bwd(dx3, s["gl"], s["e"], w["bg"], f"{tag}_b_ple")
    g_wg = _mm(s["x2b"], dgl, "tn", F32, f"{tag}_b_dwg")
    g_wp = _mm(s["pb"], de, "tn", F32, f"{tag}_b_dwp")
    dx2 = _mm(dgl, w["wg"], "nt", F32, f"{tag}_b_dx2", add=dx3)
    dz2, dz2b, g_ln2g, g_ln2b = _ln_bwd(dx2, s["z2"], w["ln2_g"], f"{tag}_b_ln2")
    dy = _mm(dz2b, w["wdown"], "nt", F32, f"{tag}_b_dy")
    g_wdown = _mm(s["y"], dz2b, "tn", F32, f"{tag}_b_dwdown", tm=1408)
    dval, dgate, g_fcw, g_fcb = _ffn_mid_bwd_a(s["h"], dy, w["fconv_w"], w["fconv_b"], f"{tag}_b_ffn_mid")
    dhv = _conv_bwd_x(dval, w["fconv_w"][:, :D_FF], f"{tag}_b_convx_v", BF16)
    dhg = _conv_bwd_x(dgate, w["fconv_w"][:, D_FF:], f"{tag}_b_convx_g", BF16)
    dx1 = _mm(dhv, w["wup"][:, :D_FF], "nt", F32, f"{tag}_b_dx1_v", add=dz2, add_scale=ALPHA, tk=D_FF)
    dx1 = _mm(dhg, w["wup"][:, D_FF:], "nt", F32, f"{tag}_b_dx1_g", add=dx1, tk=D_FF)
    g_wup = jnp.concatenate([_mm(dhv, s["x1b"], "tn", F32, f"{tag}_b_dwup_v", tm=1408),
                             _mm(dhg, s["x1b"], "tn", F32, f"{tag}_b_dwup_g", tm=1408)], axis=0).T
    dz1, dz1b, g_ln1g, g_ln1b = _ln_bwd(dx1, s["z1"], w["ln1_g"], f"{tag}_b_ln1")
    mix_bwd = _mixer_a_bwd if i % 2 == 0 else _mixer_b_bwd
    dx, g_mix = mix_bwd(tag, dz1, dz1b, s["xb"], w, s["mixer"])
    grads = dict(f_w_up=g_wup, f_conv_w=g_fcw, f_conv_b=g_fcb.reshape(2 * D_FF), f_w_down=g_wdown,
                 ln1_g=g_ln1g.reshape(D_MODEL), ln1_b=g_ln1b.reshape(D_MODEL), ln2_g=g_ln2g.reshape(D_MODEL),
                 ln2_b=g_ln2b.reshape(D_MODEL), ple_w=g_wp, ple_gate_w=g_wg, ple_gate_b=g_bg.reshape(D_MODEL))
    return dx, g_mix, grads


def _layer_weights(i, full, rep):
    j = i // 2
    w = dict(ln1_g=rep["ln1_g"][i], ln1_b=rep["ln1_b"][i], ln2_g=rep["ln2_g"][i], ln2_b=rep["ln2_b"][i],
             wup=full["f_w_up"][i], fconv_w=full["f_conv_w"][i], fconv_b=rep["f_conv_b"][i],
             wdown=full["f_w_down"][i], wp=full["ple_w"][i], wg=full["ple_gate_w"][i], bg=rep["ple_gate_b"][i])
    if i % 2 == 0:
        w_in = full["a_w_in"][j]
        w.update(wqkv=w_in[:, :3 * D_MODEL],
                 wf=jnp.pad(w_in[:, 3 * D_MODEL:], ((0, 0), (0, LANES - N_HEADS))),
                 b_f=rep["a_b_f"][j], wout=full["a_w_out"][j])
    else:
        w.update(win=full["b_w_in"][j], conv_w=full["b_conv_w"][j], conv_b=full["b_conv_b"][j],
                 wa=rep["b_w_a"][j].astype(BF16), wi=rep["b_w_i"][j].astype(BF16),
                 ba=rep["b_b_a"][j].reshape(N_BLOCKS_B, 1, BLOCK_B), bi=rep["b_b_i"][j].reshape(N_BLOCKS_B, 1, BLOCK_B),
                 lam=full["b_lam"][j].reshape(N_BLOCKS_B, 1, BLOCK_B), wout=full["b_w_out"][j])
    return w


def _fwd_bwd(x, p, target, full, rep):
    weights = [_layer_weights(i, full, rep) for i in range(DEPTH)]
    xb = x.astype(BF16)
    pb = p.astype(BF16)
    saved = []
    for i in range(DEPTH):
        x, xb, s = _layer_fwd(i, x, xb, pb[i], weights[i])
        saved.append(s)
    dx, loss_local = _loss_head(x, target, "loss_head")

    per_layer = {n: [None] * (DEPTH if n.startswith(("f_", "ln", "ple")) else DEPTH // 2) for n in WEIGHTS}
    for i in reversed(range(DEPTH)):
        dx, g_mix, g_layer = _layer_bwd(i, dx, weights[i], saved[i])
        for n, g in g_layer.items():
            per_layer[n][i] = g
        for n, g in g_mix.items():
            per_layer[n][i // 2] = g
    return loss_local, dx, per_layer


def _train_step(x, p, target, local, moments_m, moments_v):
    full = _gather_weights(local, MATMUL_WEIGHTS, BF16, "gather_matmul_weights")
    full.update(_gather_weights(local, SMALL_SHARDED, F32, "gather_small_weights"))
    rep = {n: local[n] for n in WEIGHTS if SHARD_AXIS[n] is None}
    loss_local, dx, grads_layers = _fwd_bwd(x, p, target, full, rep)

    unpacked = [{} for _ in range(4)]
    for group, dtype, tag in ((GRADS_AS_BF16, BF16, "big"), ([n for n in WEIGHTS if n not in GRADS_AS_BF16], F32, "small")):
        pieces = []
        for n in group:
            g = jnp.stack(grads_layers[n]).astype(dtype)
            if SHARD_AXIS[n] is None:
                pieces.append(jnp.broadcast_to(g.reshape(1, -1), (N_DEV, g.size)))
            else:
                pieces.append(_to_shards(g, SHARD_AXIS[n]).reshape(N_DEV, -1))
        recv = _all_to_all(_pack_pieces(pieces, ADAM_ROWS), f"reduce_scatter_grads_{tag}")

        def pack_local(d):
            return _pack_pieces([d[n].astype(F32).reshape(-1) for n in group], ADAM_ROWS)

        outs = _adamw(recv, pack_local(local), pack_local(moments_m), pack_local(moments_v), f"adamw_{tag}")
        shapes = [local[n].shape for n in group]
        for dst, packed in zip(unpacked, outs):
            dst.update(zip(group, _unpack_pieces(packed, shapes)))
    return loss_local, dx, unpacked


def kernel(x, p, a_w_in, a_b_f, a_w_out, b_w_in, b_conv_w, b_conv_b, b_w_a, b_b_a, b_w_i, b_b_i, b_lam, b_w_out, f_w_up, f_conv_w, f_conv_b, f_w_down, ln1_g, ln1_b, ln2_g, ln2_b, ple_w, ple_gate_w, ple_gate_b, loss_target, m_a_w_in, m_a_b_f, m_a_w_out, m_b_w_in, m_b_conv_w, m_b_conv_b, m_b_w_a, m_b_b_a, m_b_w_i, m_b_b_i, m_b_lam, m_b_w_out, m_f_w_up, m_f_conv_w, m_f_conv_b, m_f_w_down, m_ln1_g, m_ln1_b, m_ln2_g, m_ln2_b, m_ple_w, m_ple_gate_w, m_ple_gate_b, v_a_w_in, v_a_b_f, v_a_w_out, v_b_w_in, v_b_conv_w, v_b_conv_b, v_b_w_a, v_b_b_a, v_b_w_i, v_b_b_i, v_b_lam, v_b_w_out, v_f_w_up, v_f_conv_w, v_f_conv_b, v_f_w_down, v_ln1_g, v_ln1_b, v_ln2_g, v_ln2_b, v_ple_w, v_ple_gate_w, v_ple_gate_b):
    given = dict(locals())
    local = {n: given[n] for n in WEIGHTS}
    mom_m = {n: given["m_" + n] for n in WEIGHTS}
    mom_v = {n: given["v_" + n] for n in WEIGHTS}
    t = x.shape[1]
    loss_local, dx, (grad, delta, new_m, new_v) = _train_step(
        x.reshape(t, D_MODEL), p.reshape(DEPTH, t, D_PLE), loss_target.reshape(t, D_MODEL), local, mom_m, mom_v)
    loss = lax.psum(loss_local, MESH_AXES)
    return (loss, dx.reshape(1, t, D_MODEL), *[grad[n] for n in WEIGHTS], *[delta[n] for n in WEIGHTS],
            *[new_m[n] for n in WEIGHTS], *[new_v[n] for n in WEIGHTS])
```

```python
import math

import jax
import jax.numpy as jnp
from jax import lax
from jax.experimental import pallas as pl
from jax.experimental.pallas import tpu as pltpu

F32 = jnp.float32
BF16 = jnp.bfloat16

D_MODEL = 1024
DEPTH = 4
N_HEADS = 16
HEAD_DIM = 64
N_PAIRS = N_HEADS // 2
N_BLOCKS_B = 8
BLOCK_B = 128
CONV_B = 4
LRU_C = 8.0
D_FF = 2816
CONV_F = 3
D_PLE = 256
LN_EPS = 1e-5
ALPHA = (2.0 * DEPTH) ** 0.25
ADAM_LR, ADAM_B1, ADAM_B2, ADAM_EPS, ADAM_WD, ADAM_STEP = 0.001, 0.9, 0.999, 1e-08, 0.01, 10
N_DEV = 8
MESH_AXES = ("x", "y", "c")

LANES = 128
SUBLANES = 8
VMEM_LIMIT_BYTES = 56 * 1024 * 1024
ATTN_FWD_BLOCK = 1024
ATTN_BWD_BLOCK = 512
ROW_TILE = 512
NEG_BIG = -1e30


def _cparams(*sem):
    return pltpu.CompilerParams(dimension_semantics=sem if sem else None, vmem_limit_bytes=VMEM_LIMIT_BYTES)


def _pick(n, pref):
    if n <= pref:
        return n
    best = None
    for t in range(LANES, pref + 1, LANES):
        if n % t == 0:
            best = t
    assert best is not None, (n, pref)
    return best


def _sigmoid(x):
    return 1.0 / (1.0 + jnp.exp(-x))


def _log1p(x):
    u = 1.0 + x
    d = u - 1.0
    return jnp.where(d == 0.0, x, jnp.log(u) * (x / jnp.where(d == 0.0, 1.0, d)))


def _expm1(x):
    u = jnp.exp(x)
    lu = jnp.log(u)
    return jnp.where(u == 1.0, x, (u - 1.0) * (x / jnp.where(u == 1.0, 1.0, lu)))


def _softplus(x):
    return jnp.maximum(x, 0.0) + _log1p(jnp.exp(-jnp.abs(x)))


_GELU_C = math.sqrt(2.0 / math.pi)


def _gelu(x):
    return 0.5 * x * (1.0 + jnp.tanh(_GELU_C * (x + 0.044715 * x * x * x)))


def _gelu_and_grad(x):
    t = jnp.tanh(_GELU_C * (x + 0.044715 * x * x * x))
    du = _GELU_C * (1.0 + 3.0 * 0.044715 * x * x)
    return 0.5 * x * (1.0 + t), 0.5 * (1.0 + t) + 0.5 * x * (1.0 - t * t) * du


_DOT_DIMS = {"nn": (((1,), (0,)), ((), ())), "nt": (((1,), (1,)), ((), ())), "tn": (((0,), (0,)), ((), ()))}


def _mm(a, b, mode, out_dtype, name, add=None, add_scale=1.0, tm=512, tn=1408, tk=1408):
    if mode == "nn":
        (m, k), (k2, n) = a.shape, b.shape
    elif mode == "nt":
        (m, k), (n, k2) = a.shape, b.shape
    else:
        (k, m), (k2, n) = a.shape, b.shape
    assert k == k2 and a.dtype == BF16 and b.dtype == BF16, (a.shape, b.shape, a.dtype, b.dtype)
    tm, tn, tk = _pick(m, tm), _pick(n, tn), _pick(k, tk)
    nk = k // tk
    dims = _DOT_DIMS[mode]

    def body(*refs):
        if add is None:
            a_ref, b_ref, o_ref, acc_ref = refs
        else:
            a_ref, b_ref, add_ref, o_ref, acc_ref = refs
        kk = pl.program_id(2)

        @pl.when(kk == 0)
        def _():
            acc_ref[...] = jnp.zeros_like(acc_ref)

        acc_ref[...] += lax.dot_general(a_ref[...], b_ref[...], dims, preferred_element_type=F32)

        @pl.when(kk == nk - 1)
        def _():
            r = acc_ref[...]
            if add is not None:
                r = r + add_scale * add_ref[...]
            o_ref[...] = r.astype(out_dtype)

    a_spec = (pl.BlockSpec((tk, tm), lambda j, i, kk: (kk, i)) if mode == "tn"
              else pl.BlockSpec((tm, tk), lambda j, i, kk: (i, kk)))
    b_spec = (pl.BlockSpec((tn, tk), lambda j, i, kk: (j, kk)) if mode == "nt"
              else pl.BlockSpec((tk, tn), lambda j, i, kk: (kk, j)))
    o_spec = pl.BlockSpec((tm, tn), lambda j, i, kk: (i, j))
    in_specs, args = [a_spec, b_spec], [a, b]
    if add is not None:
        assert add.shape == (m, n) and add.dtype == F32
        in_specs.append(o_spec)
        args.append(add)
    return pl.pallas_call(
        body, name=name, grid=(n // tn, m // tm, nk),
        in_specs=in_specs, out_specs=o_spec,
        out_shape=jax.ShapeDtypeStruct((m, n), out_dtype),
        scratch_shapes=[pltpu.VMEM((tm, tn), F32)],
        compiler_params=_cparams("parallel", "parallel", "arbitrary"),
    )(*args)


def _ln_fwd(x, m, g, b, name):
    t, d = x.shape
    bt = _pick(t, ROW_TILE)

    def body(x_ref, m_ref, g_ref, b_ref, y_ref, yb_ref, z_ref):
        z = ALPHA * x_ref[...] + m_ref[...]
        mu = jnp.mean(z, axis=-1, keepdims=True)
        zc = z - mu
        var = jnp.mean(zc * zc, axis=-1, keepdims=True)
        y = zc * lax.rsqrt(var + LN_EPS) * g_ref[...] + b_ref[...]
        y_ref[...] = y
        yb_ref[...] = y.astype(BF16)
        z_ref[...] = z

    row = pl.BlockSpec((bt, d), lambda i: (i, 0))
    vec = pl.BlockSpec((1, d), lambda i: (0, 0))
    return pl.pallas_call(
        body, name=name, grid=(t // bt,), in_specs=[row, row, vec, vec], out_specs=[row, row, row],
        out_shape=[jax.ShapeDtypeStruct((t, d), F32), jax.ShapeDtypeStruct((t, d), BF16),
                   jax.ShapeDtypeStruct((t, d), F32)],
        compiler_params=_cparams("parallel"),
    )(x, m, g.reshape(1, d), b.reshape(1, d))


def _ln_bwd(dy, z, g, name):
    t, d = dy.shape
    bt = _pick(t, ROW_TILE)

    def body(dy_ref, z_ref, g_ref, dz_ref, dzb_ref, dg_ref, db_ref):
        @pl.when(pl.program_id(0) == 0)
        def _():
            dg_ref[...] = jnp.zeros_like(dg_ref)
            db_ref[...] = jnp.zeros_like(db_ref)

        z = z_ref[...]
        dyv = dy_ref[...]
        mu = jnp.mean(z, axis=-1, keepdims=True)
        zc = z - mu
        var = jnp.mean(zc * zc, axis=-1, keepdims=True)
        rstd = lax.rsqrt(var + LN_EPS)
        xhat = zc * rstd
        dxh = dyv * g_ref[...]
        m1 = jnp.mean(dxh, axis=-1, keepdims=True)
        m2 = jnp.mean(dxh * xhat, axis=-1, keepdims=True)
        dz = rstd * (dxh - m1 - xhat * m2)
        dz_ref[...] = dz
        dzb_ref[...] = dz.astype(BF16)
        dg_ref[...] += jnp.sum(dyv * xhat, axis=0, keepdims=True)
        db_ref[...] += jnp.sum(dyv, axis=0, keepdims=True)

    row = pl.BlockSpec((bt, d), lambda i: (i, 0))
    vec = pl.BlockSpec((1, d), lambda i: (0, 0))
    return pl.pallas_call(
        body, name=name, grid=(t // bt,), in_specs=[row, row, vec], out_specs=[row, row, vec, vec],
        out_shape=[jax.ShapeDtypeStruct((t, d), F32), jax.ShapeDtypeStruct((t, d), BF16),
                   jax.ShapeDtypeStruct((1, d), F32), jax.ShapeDtypeStruct((1, d), F32)],
        compiler_params=_cparams("arbitrary"),
    )(dy, z, g.reshape(1, d))


def _rows_iota(shape):
    return lax.broadcasted_iota(jnp.int32, shape, 0)


def _shift_down(x, halo, s):
    if s == 0:
        return x
    rolled = pltpu.roll(x, s, axis=0)
    first = jnp.where(_rows_iota((SUBLANES, x.shape[1])) < s, pltpu.roll(halo, s, axis=0), rolled[:SUBLANES])
    return jnp.concatenate([first, rolled[SUBLANES:]], axis=0)


def _shift_up(x, halo, s):
    if s == 0:
        return x
    n = x.shape[0]
    rolled = pltpu.roll(x, n - s, axis=0)
    last = jnp.where(_rows_iota((SUBLANES, x.shape[1])) < SUBLANES - s, rolled[n - SUBLANES:],
                     pltpu.roll(halo, SUBLANES - s, axis=0))
    return jnp.concatenate([rolled[:n - SUBLANES], last], axis=0)


def _prev_halo_spec(bt, cb, col_off=0):
    r = bt // SUBLANES
    return pl.BlockSpec((SUBLANES, cb), lambda i, j: (jnp.maximum(i * r - 1, 0), j + col_off))


def _next_halo_spec(bt, cb, t, col_off=0):
    r = bt // SUBLANES
    last = t // SUBLANES - 1
    return pl.BlockSpec((SUBLANES, cb), lambda i, j: (jnp.minimum((i + 1) * r, last), j + col_off))


def _causal_conv(x, halo, w_ref, ksize):
    acc = None
    for j in range(ksize):
        term = w_ref[j:j + 1, :] * _shift_down(x, halo, ksize - 1 - j)
        acc = term if acc is None else acc + term
    return acc


def _anticausal_conv(y, halo, w_ref, ksize):
    acc = None
    for j in range(ksize):
        term = w_ref[j:j + 1, :] * _shift_up(y, halo, ksize - 1 - j)
        acc = term if acc is None else acc + term
    return acc


def _ffn_mid_fwd(h, cw, cb_, name):
    t = h.shape[0]
    bt, cb = _pick(t, ROW_TILE), _pick(D_FF, 1408)
    nc = D_FF // cb

    def body(hv_ref, hvh_ref, hg_ref, hgh_ref, wv_ref, wg_ref, bv_ref, bg_ref, y_ref):
        keep = (pl.program_id(0) > 0).astype(F32)
        val = _causal_conv(hv_ref[...], hvh_ref[...] * keep, wv_ref, CONV_F) + bv_ref[...]
        gate = _causal_conv(hg_ref[...], hgh_ref[...] * keep, wg_ref, CONV_F) + bg_ref[...]
        y_ref[...] = (_gelu(gate) * val).astype(BF16)

    tile_v = pl.BlockSpec((bt, cb), lambda i, j: (i, j))
    tile_g = pl.BlockSpec((bt, cb), lambda i, j: (i, j + nc))
    wv = pl.BlockSpec((CONV_F, cb), lambda i, j: (0, j))
    wg = pl.BlockSpec((CONV_F, cb), lambda i, j: (0, j + nc))
    bv = pl.BlockSpec((1, cb), lambda i, j: (0, j))
    bg = pl.BlockSpec((1, cb), lambda i, j: (0, j + nc))
    cb2 = cb_.reshape(1, 2 * D_FF)
    return pl.pallas_call(
        body, name=name, grid=(t // bt, nc),
        in_specs=[tile_v, _prev_halo_spec(bt, cb), tile_g, _prev_halo_spec(bt, cb, nc), wv, wg, bv, bg],
        out_specs=tile_v, out_shape=jax.ShapeDtypeStruct((t, D_FF), BF16),
        compiler_params=_cparams("parallel", "parallel"),
    )(h, h, h, h, cw, cw, cb2, cb2)


def _ffn_mid_bwd_a(h, dy, cw, cb_, name):
    t = h.shape[0]
    bt, cb = _pick(t, ROW_TILE), _pick(D_FF, 1408)
    nc = D_FF // cb

    def body(hv_ref, hvh_ref, hg_ref, hgh_ref, dy_ref, wv_ref, wg_ref, bv_ref, bg_ref,
             dv_ref, dg_ref, dwv_ref, dwg_ref, dbv_ref, dbg_ref):
        @pl.when(pl.program_id(1) == 0)
        def _():
            for r in (dwv_ref, dwg_ref, dbv_ref, dbg_ref):
                r[...] = jnp.zeros_like(r)

        keep = (pl.program_id(1) > 0).astype(F32)
        hv, hvh = hv_ref[...], hvh_ref[...] * keep
        hg, hgh = hg_ref[...], hgh_ref[...] * keep
        val = _causal_conv(hv, hvh, wv_ref, CONV_F) + bv_ref[...]
        gate = _causal_conv(hg, hgh, wg_ref, CONV_F) + bg_ref[...]
        gl, glg = _gelu_and_grad(gate)
        dyv = dy_ref[...]
        dval = dyv * gl
        dgate = dyv * val * glg
        dv_ref[...] = dval.astype(BF16)
        dg_ref[...] = dgate.astype(BF16)
        dbv_ref[...] += jnp.sum(dval, axis=0, keepdims=True)
        dbg_ref[...] += jnp.sum(dgate, axis=0, keepdims=True)
        for j in range(CONV_F):
            s = CONV_F - 1 - j
            dwv_ref[j:j + 1, :] += jnp.sum(dval * _shift_down(hv, hvh, s), axis=0, keepdims=True)
            dwg_ref[j:j + 1, :] += jnp.sum(dgate * _shift_down(hg, hgh, s), axis=0, keepdims=True)

    tile_v = pl.BlockSpec((bt, cb), lambda j, i: (i, j))
    tile_g = pl.BlockSpec((bt, cb), lambda j, i: (i, j + nc))
    r = bt // SUBLANES
    halo_v = pl.BlockSpec((SUBLANES, cb), lambda j, i: (jnp.maximum(i * r - 1, 0), j))
    halo_g = pl.BlockSpec((SUBLANES, cb), lambda j, i: (jnp.maximum(i * r - 1, 0), j + nc))
    wv = pl.BlockSpec((CONV_F, cb), lambda j, i: (0, j))
    wg = pl.BlockSpec((CONV_F, cb), lambda j, i: (0, j + nc))
    bv = pl.BlockSpec((1, cb), lambda j, i: (0, j))
    bg = pl.BlockSpec((1, cb), lambda j, i: (0, j + nc))
    cb2 = cb_.reshape(1, 2 * D_FF)
    dv, dg, dwv, dwg, dbv, dbg = pl.pallas_call(
        body, name=name, grid=(nc, t // bt),
        in_specs=[tile_v, halo_v, tile_g, halo_g, tile_v, wv, wg, bv, bg],
        out_specs=[tile_v, tile_v, wv, wv, bv, bv],
        out_shape=[jax.ShapeDtypeStruct((t, D_FF), BF16), jax.ShapeDtypeStruct((t, D_FF), BF16),
                   jax.ShapeDtypeStruct((CONV_F, D_FF), F32), jax.ShapeDtypeStruct((CONV_F, D_FF), F32),
                   jax.ShapeDtypeStruct((1, D_FF), F32), jax.ShapeDtypeStruct((1, D_FF), F32)],
        compiler_params=_cparams("parallel", "arbitrary"),
    )(h, h, h, h, dy, cw, cw, cb2, cb2)
    return dv, dg, jnp.concatenate([dwv, dwg], axis=1), jnp.concatenate([dbv, dbg], axis=1)


def _conv_bwd_x(dy, cw, name, out_dtype):
    t, c = dy.shape
    ksize = cw.shape[0]
    bt, cb = _pick(t, ROW_TILE), _pick(c, 1408)
    halo_rows = SUBLANES if dy.dtype == F32 else 2 * SUBLANES
    r, last = bt // halo_rows, t // halo_rows - 1

    def body(dy_ref, halo_ref, w_ref, o_ref):
        keep = (pl.program_id(0) < t // bt - 1).astype(F32)
        halo = halo_ref[...].astype(F32)[:SUBLANES] * keep
        o_ref[...] = _anticausal_conv(dy_ref[...].astype(F32), halo, w_ref, ksize).astype(out_dtype)

    tile = pl.BlockSpec((bt, cb), lambda i, j: (i, j))
    halo_spec = pl.BlockSpec((halo_rows, cb), lambda i, j: (jnp.minimum((i + 1) * r, last), j))
    return pl.pallas_call(
        body, name=name, grid=(t // bt, c // cb),
        in_specs=[tile, halo_spec, pl.BlockSpec((ksize, cb), lambda i, j: (0, j))],
        out_specs=tile, out_shape=jax.ShapeDtypeStruct((t, c), out_dtype),
        compiler_params=_cparams("parallel", "parallel"),
    )(dy, dy, cw)


def _ple_fwd(x2, g, e, bg, name):
    t, d = x2.shape
    bt = _pick(t, ROW_TILE)

    def body(x_ref, g_ref, e_ref, b_ref, y_ref, yb_ref):
        y = x_ref[...] + _sigmoid(g_ref[...] + b_ref[...]) * e_ref[...]
        y_ref[...] = y
        yb_ref[...] = y.astype(BF16)

    row = pl.BlockSpec((bt, d), lambda i: (i, 0))
    vec = pl.BlockSpec((1, d), lambda i: (0, 0))
    return pl.pallas_call(
        body, name=name, grid=(t // bt,), in_specs=[row, row, row, vec], out_specs=[row, row],
        out_shape=[jax.ShapeDtypeStruct((t, d), F32), jax.ShapeDtypeStruct((t, d), BF16)],
        compiler_params=_cparams("parallel"),
    )(x2, g, e, bg.reshape(1, d))


def _ple_bwd(dx3, g, e, bg, name):
    t, d = dx3.shape
    bt = _pick(t, ROW_TILE)

    def body(dx_ref, g_ref, e_ref, b_ref, dg_ref, de_ref, db_ref):
        @pl.when(pl.program_id(0) == 0)
        def _():
            db_ref[...] = jnp.zeros_like(db_ref)

        dx = dx_ref[...]
        gate = _sigmoid(g_ref[...] + b_ref[...])
        dg = dx * e_ref[...] * gate * (1.0 - gate)
        dg_ref[...] = dg.astype(BF16)
        de_ref[...] = (dx * gate).astype(BF16)
        db_ref[...] += jnp.sum(dg, axis=0, keepdims=True)

    row = pl.BlockSpec((bt, d), lambda i: (i, 0))
    vec = pl.BlockSpec((1, d), lambda i: (0, 0))
    return pl.pallas_call(
        body, name=name, grid=(t // bt,), in_specs=[row, row, row, vec], out_specs=[row, row, vec],
        out_shape=[jax.ShapeDtypeStruct((t, d), BF16), jax.ShapeDtypeStruct((t, d), BF16),
                   jax.ShapeDtypeStruct((1, d), F32)],
        compiler_params=_cparams("arbitrary"),
    )(dx3, g, e, bg.reshape(1, d))


def _loss_head(y, target, name):
    t, d = y.shape
    bt = _pick(t, ROW_TILE)

    def body(y_ref, t_ref, dy_ref, l_ref, acc_ref):
        @pl.when(pl.program_id(0) == 0)
        def _():
            acc_ref[...] = jnp.zeros_like(acc_ref)

        err = y_ref[...] - t_ref[...]
        dy_ref[...] = err * (1.0 / d)
        acc_ref[...] += jnp.sum(err * err, axis=0, keepdims=True)

        @pl.when(pl.program_id(0) == t // bt - 1)
        def _():
            l_ref[...] = jnp.full(l_ref.shape, (0.5 / d) * jnp.sum(acc_ref[...]), F32)

    row = pl.BlockSpec((bt, d), lambda i: (i, 0))
    dy, l = pl.pallas_call(
        body, name=name, grid=(t // bt,), in_specs=[row, row],
        out_specs=[row, pl.BlockSpec((SUBLANES, LANES), lambda i: (0, 0))],
        out_shape=[jax.ShapeDtypeStruct((t, d), F32), jax.ShapeDtypeStruct((SUBLANES, LANES), F32)],
        scratch_shapes=[pltpu.VMEM((1, d), F32)],
        compiler_params=_cparams("arbitrary"),
    )(y, target)
    return dy, l[0, 0]


def _split3(x):
    hi = x.astype(BF16)
    r1 = x - hi.astype(F32)
    mid = r1.astype(BF16)
    lo = (r1 - mid.astype(F32)).astype(BF16)
    return hi, mid, lo


def _tri_dot(x, tri):
    hi, mid, lo = _split3(x)
    dims = (((1,), (0,)), ((), ()))
    return (lax.dot_general(hi, tri, dims, preferred_element_type=F32)
            + lax.dot_general(mid, tri, dims, preferred_element_type=F32)
            + lax.dot_general(lo, tri, dims, preferred_element_type=F32))


def _fgate_fwd(fg_rows, b_f, name):
    hh, t = fg_rows.shape
    bt = _pick(t, 512)

    def body(fg_ref, b_ref, c_ref, carry_ref):
        @pl.when(pl.program_id(0) == 0)
        def _():
            carry_ref[...] = jnp.zeros_like(carry_ref)

        xx = fg_ref[...] + b_ref[...]
        logf = jnp.minimum(xx, 0.0) - _log1p(jnp.exp(-jnp.abs(xx)))
        r = lax.broadcasted_iota(jnp.int32, (bt, bt), 0)
        c = lax.broadcasted_iota(jnp.int32, (bt, bt), 1)
        tri = (r <= c).astype(BF16)
        cs = _tri_dot(logf, tri) + carry_ref[...]
        c_ref[...] = cs
        carry_ref[...] = cs[:, bt - 1:bt]

    return pl.pallas_call(
        body, name=name, grid=(t // bt,),
        in_specs=[pl.BlockSpec((hh, bt), lambda i: (0, i)), pl.BlockSpec((hh, 1), lambda i: (0, 0))],
        out_specs=pl.BlockSpec((hh, bt), lambda i: (0, i)),
        out_shape=jax.ShapeDtypeStruct((hh, t), F32),
        scratch_shapes=[pltpu.VMEM((hh, 1), F32)],
        compiler_params=_cparams("arbitrary"),
    )(fg_rows, b_f.reshape(hh, 1))


def _fgate_bwd(dck_rows, dcq_rows, fg_rows, b_f, name):
    hh, t = fg_rows.shape
    bt = _pick(t, 512)
    nb = t // bt

    def body(dc_ref, dcq_ref, fg_ref, b_ref, dfg_ref, db_ref, carry_ref):
        @pl.when(pl.program_id(0) == 0)
        def _():
            carry_ref[...] = jnp.zeros_like(carry_ref)
            db_ref[...] = jnp.zeros_like(db_ref)

        r = lax.broadcasted_iota(jnp.int32, (bt, bt), 0)
        c = lax.broadcasted_iota(jnp.int32, (bt, bt), 1)
        tri = (r >= c).astype(BF16)
        dlogf = _tri_dot(dc_ref[...] + dcq_ref[...], tri) + carry_ref[...]
        carry_ref[...] = dlogf[:, 0:1]
        xx = fg_ref[...] + b_ref[...]
        dfg = dlogf * _sigmoid(-xx)
        dfg_ref[...] = dfg
        db_ref[...] += jnp.sum(dfg, axis=1, keepdims=True)

    blk = pl.BlockSpec((hh, bt), lambda i: (0, nb - 1 - i))
    vec = pl.BlockSpec((hh, 1), lambda i: (0, 0))
    return pl.pallas_call(
        body, name=name, grid=(nb,), in_specs=[blk, blk, blk, vec], out_specs=[blk, vec],
        out_shape=[jax.ShapeDtypeStruct((hh, t), F32), jax.ShapeDtypeStruct((hh, 1), F32)],
        scratch_shapes=[pltpu.VMEM((hh, 1), F32)],
        compiler_params=_cparams("arbitrary"),
    )(dck_rows, dcq_rows, fg_rows, b_f.reshape(hh, 1))


def _rows_to_cols(r):
    hh, t = r.shape
    return jnp.repeat(r.reshape(hh // 2, 2, t).transpose(0, 2, 1), HEAD_DIM, axis=-1)


def _rows_to_pairs(r):
    hh, t = r.shape
    return jnp.pad(r.reshape(hh // 2, 2, t), ((0, 0), (0, SUBLANES - 2), (0, 0)))


AUG_C = HEAD_DIM
AUG_ONE = HEAD_DIM + 3


def _attn_prep(qkv, c_cols, name):
    t = qkv.shape[0]
    bt = _pick(t, ATTN_FWD_BLOCK)
    scale = 1.0 / math.sqrt(HEAD_DIM)

    def body(q_ref, k_ref, v_ref, c_ref, qh_ref, kh_ref, qt_ref, kt_ref, vt_ref):
        vt_ref[...] = v_ref[...].astype(F32).T.astype(BF16)
        lane = lax.broadcasted_iota(jnp.int32, (bt, LANES), 1)
        q2 = q_ref[...].astype(F32)
        k2 = k_ref[...].astype(F32) * scale
        c2 = c_ref[...]
        parts = [p.astype(F32) for p in _split3(c2 - c2[0:1, :])]
        swapped = [pltpu.roll(p, HEAD_DIM, axis=1) for p in parts]
        for a in (0, 1):
            qa = q2 if a == 0 else pltpu.roll(q2, HEAD_DIM, axis=1)
            ka = k2 if a == 0 else pltpu.roll(k2, HEAD_DIM, axis=1)
            hi, mid, lo = swapped if a == 0 else parts
            kaug = jnp.where(lane < HEAD_DIM, ka,
                             jnp.where(lane == AUG_C, hi,
                                       jnp.where(lane == AUG_C + 1, mid,
                                                 jnp.where(lane == AUG_C + 2, lo,
                                                           jnp.where(lane == AUG_ONE, 1.0, 0.0)))))
            qaug = jnp.where(lane < HEAD_DIM, qa, jnp.where(lane < AUG_ONE, -1.0, 0.0))
            qh_ref[:, a * LANES:(a + 1) * LANES] = qaug.astype(BF16)
            kh_ref[:, a * LANES:(a + 1) * LANES] = kaug.astype(BF16)
            qt_ref[a * LANES:(a + 1) * LANES, :] = qaug.T.astype(BF16)
            kt_ref[a * LANES:(a + 1) * LANES, :] = kaug.T.astype(BF16)

    out = pl.BlockSpec((bt, 2 * LANES), lambda i, hp: (i, hp))
    out_t = pl.BlockSpec((2 * LANES, bt), lambda i, hp: (hp, i))
    shape = jax.ShapeDtypeStruct((t, N_HEADS * LANES), BF16)
    shape_t = jax.ShapeDtypeStruct((N_HEADS * LANES, t), BF16)
    return pl.pallas_call(
        body, name=name, grid=(t // bt, N_PAIRS),
        in_specs=[pl.BlockSpec((bt, LANES), lambda i, hp: (i, hp)),
                  pl.BlockSpec((bt, LANES), lambda i, hp: (i, N_PAIRS + hp)),
                  pl.BlockSpec((bt, LANES), lambda i, hp: (i, 2 * N_PAIRS + hp)),
                  pl.BlockSpec((None, bt, LANES), lambda i, hp: (hp, i, 0))],
        out_specs=[out, out, out_t, out_t, pl.BlockSpec((LANES, bt), lambda i, hp: (hp, i))],
        out_shape=[shape, shape, shape_t, shape_t, jax.ShapeDtypeStruct((D_MODEL, t), BF16)],
        compiler_params=_cparams("parallel", "parallel"),
    )(qkv, qkv, qkv, c_cols)


def _block_scalar(c_ref, a, start):
    return c_ref[a:a + 1, pl.ds(start, LANES)][:, 0:1]


def _attn_fwd_t(qh, kh, vt, c_pairs, name):
    t = qh.shape[0]
    bq = _pick(t, ATTN_FWD_BLOCK)
    nq = t // bq

    def body(q_ref, k_ref, vt_ref, c_ref, ot_ref, otb_ref, lse_ref, acc_ref):
        i = pl.program_id(1)
        q0 = pl.multiple_of(i * bq, bq)
        qs = (q_ref[:, 0:LANES], q_ref[:, LANES:2 * LANES])
        cq = [_block_scalar(c_ref, a, q0) for a in (0, 1)]
        acc_ref[...] = jnp.zeros_like(acc_ref)
        keep = _rows_iota((bq, bq)) <= lax.broadcasted_iota(jnp.int32, (bq, bq), 1)

        def step(j, carry, masked):
            k0 = pl.multiple_of(j * bq, bq)
            kb = k_ref[pl.ds(k0, bq), :]
            new = []
            for a in (0, 1):
                m_old, l_old = carry[2 * a], carry[2 * a + 1]
                st = lax.dot_general(kb[:, a * LANES:(a + 1) * LANES], qs[a], _DOT_DIMS["nt"],
                                     preferred_element_type=F32)
                if masked:
                    st = jnp.where(keep, st, NEG_BIG)
                sigma = cq[a] - _block_scalar(c_ref, a, k0)
                m_new = jnp.maximum(m_old, jnp.max(st, axis=0, keepdims=True) + sigma)
                pt = jnp.exp(st - (m_new - sigma))
                alpha = jnp.exp(m_old - m_new)
                l_new = alpha * l_old + jnp.sum(pt, axis=0, keepdims=True)
                vta = vt_ref[a * HEAD_DIM:(a + 1) * HEAD_DIM, pl.ds(k0, bq)]
                acc_ref[a] = alpha * acc_ref[a] + lax.dot_general(
                    vta, pt.astype(BF16), _DOT_DIMS["nn"], preferred_element_type=F32)
                new += [m_new, l_new]
            return tuple(new)

        neg = jnp.full((1, bq), NEG_BIG, F32)
        zero = jnp.zeros((1, bq), F32)
        carry = lax.fori_loop(0, i, lambda j, c: step(j, c, False), (neg, zero, neg, zero))
        m_a, l_a, m_b, l_b = step(i, carry, True)
        ot = jnp.concatenate([acc_ref[0] / l_a, acc_ref[1] / l_b], axis=0)
        ot_ref[...] = ot
        otb_ref[...] = ot.astype(BF16)
        lse_ref[...] = jnp.zeros_like(lse_ref)
        lse_ref[0:1, :] = m_a + jnp.log(l_a)
        lse_ref[1:2, :] = m_b + jnp.log(l_b)

    rows = pl.BlockSpec((None, SUBLANES, bq), lambda hp, i: (hp, 0, i))
    otile = pl.BlockSpec((LANES, bq), lambda hp, i: (hp, i))
    return pl.pallas_call(
        body, name=name, grid=(N_PAIRS, nq),
        in_specs=[pl.BlockSpec((bq, 2 * LANES), lambda hp, i: (i, hp)),
                  pl.BlockSpec((t, 2 * LANES), lambda hp, i: (0, hp)),
                  pl.BlockSpec((LANES, t), lambda hp, i: (hp, 0)),
                  pl.BlockSpec((None, SUBLANES, t), lambda hp, i: (hp, 0, 0))],
        out_specs=[otile, otile, rows],
        out_shape=[jax.ShapeDtypeStruct((D_MODEL, t), F32), jax.ShapeDtypeStruct((D_MODEL, t), BF16),
                   jax.ShapeDtypeStruct((N_PAIRS, SUBLANES, t), F32)],
        scratch_shapes=[pltpu.VMEM((2, HEAD_DIM, bq), F32)],
        compiler_params=_cparams("parallel", "arbitrary"),
    )(qh, kh, vt, c_pairs)


def _attn_delta_t(dot, ot, name):
    t = dot.shape[1]
    bt = _pick(t, 512)

    def body(do_ref, o_ref, d_ref, dob_ref):
        dob = do_ref[...].astype(BF16)
        prod = dob.astype(F32) * o_ref[...]
        d_ref[...] = jnp.zeros_like(d_ref)
        d_ref[0:1, :] = jnp.sum(prod[0:HEAD_DIM], axis=0, keepdims=True)
        d_ref[1:2, :] = jnp.sum(prod[HEAD_DIM:], axis=0, keepdims=True)
        dob_ref[...] = dob

    tile = pl.BlockSpec((LANES, bt), lambda hp, i: (hp, i))
    return pl.pallas_call(
        body, name=name, grid=(N_PAIRS, t // bt), in_specs=[tile, tile],
        out_specs=[pl.BlockSpec((None, SUBLANES, bt), lambda hp, i: (hp, 0, i)), tile],
        out_shape=[jax.ShapeDtypeStruct((N_PAIRS, SUBLANES, t), F32), jax.ShapeDtypeStruct((D_MODEL, t), BF16)],
        compiler_params=_cparams("parallel", "parallel"),
    )(dot, ot)


def _attn_bwd_t(kh, kt, qt, qkv, dotb, c_pairs, lse_pairs, d_pairs, name):
    t = kh.shape[0]
    bk = _pick(t, ATTN_BWD_BLOCK)
    nk = t // bk
    ref_tile = _pick(t, ATTN_FWD_BLOCK)
    assert ref_tile % bk == 0
    scale = 1.0 / math.sqrt(HEAD_DIM)

    def ref_start(start):
        return pl.multiple_of((start // ref_tile) * ref_tile, ref_tile)

    def body(k_ref, kt_ref, v_ref, qt_ref, dot_ref, c_ref, lse_ref, d_ref,
             dk_ref, dv_ref, dc_ref, dq_hbm, dka_ref, dva_ref, dqa_ref, sem):
        hp, j = pl.program_id(0), pl.program_id(1)
        k0 = pl.multiple_of(j * bk, bk)
        ks = (k_ref[:, 0:LANES], k_ref[:, LANES:2 * LANES])
        vb = v_ref[...]
        ck = [_block_scalar(c_ref, a, ref_start(k0)) for a in (0, 1)]
        dka_ref[...] = jnp.zeros_like(dka_ref)
        dva_ref[...] = jnp.zeros_like(dva_ref)

        @pl.when(j == 0)
        def _():
            dqa_ref[...] = jnp.zeros_like(dqa_ref)

        keep = _rows_iota((bk, bk)) <= lax.broadcasted_iota(jnp.int32, (bk, bk), 1)
        top = _rows_iota((LANES, bk)) < HEAD_DIM

        def step(i, masked):
            q0 = pl.multiple_of(i * bk, bk)
            dot2 = dot_ref[:, pl.ds(q0, bk)]
            zero = jnp.zeros_like(dot2)
            for a in (0, 1):
                qta = qt_ref[a * LANES:(a + 1) * LANES, pl.ds(q0, bk)]
                st = lax.dot_general(ks[a], qta, _DOT_DIMS["nn"], preferred_element_type=F32)
                sigma = _block_scalar(c_ref, a, ref_start(q0)) - ck[a]
                pt = jnp.exp(st - (lse_ref[a:a + 1, pl.ds(q0, bk)] - sigma))
                if masked:
                    pt = jnp.where(keep, pt, 0.0)
                dota = jnp.where(top, dot2, zero) if a == 0 else jnp.where(top, zero, dot2)
                dpt = lax.dot_general(vb, dota, _DOT_DIMS["nn"], preferred_element_type=F32)
                dstb = (pt * (dpt - d_ref[a:a + 1, pl.ds(q0, bk)])).astype(BF16)
                dva_ref[a] += lax.dot_general(dot2[a * HEAD_DIM:(a + 1) * HEAD_DIM, :], pt.astype(BF16),
                                              _DOT_DIMS["nt"], preferred_element_type=F32)
                dka_ref[a] += lax.dot_general(qta, dstb, _DOT_DIMS["nt"], preferred_element_type=F32)
                dqa_ref[a, :, pl.ds(q0, bk)] += lax.dot_general(
                    kt_ref[a * LANES:(a + 1) * LANES, :], dstb, _DOT_DIMS["nn"], preferred_element_type=F32)

        def loop_body(i, carry):
            step(i, False)
            return carry

        step(j, True)
        lax.fori_loop(j + 1, nk, loop_body, 0)
        dk_ref[...] = (jnp.concatenate([dka_ref[0, 0:HEAD_DIM, :], dka_ref[1, 0:HEAD_DIM, :]], axis=0)
                       * scale).astype(BF16)
        dv_ref[...] = jnp.concatenate([dva_ref[0], dva_ref[1]], axis=0).astype(BF16)
        dc_ref[...] = jnp.zeros_like(dc_ref)
        dc_ref[0:1, :] = dka_ref[0, AUG_C:AUG_C + 1, :]
        dc_ref[1:2, :] = dka_ref[1, AUG_C:AUG_C + 1, :]

        @pl.when(j == nk - 1)
        def _():
            for a in (0, 1):
                row0 = pl.multiple_of((2 * hp + a) * LANES, LANES)
                cp = pltpu.make_async_copy(dqa_ref.at[a], dq_hbm.at[pl.ds(row0, LANES), :], sem)
                cp.start()
                cp.wait()

    once = pl.Buffered(1)
    rows = pl.BlockSpec((None, SUBLANES, t), lambda hp, j: (hp, 0, 0))
    kv_out = pl.BlockSpec((LANES, bk), lambda hp, j: (hp, j))
    return pl.pallas_call(
        body, name=name, grid=(N_PAIRS, nk),
        in_specs=[pl.BlockSpec((bk, 2 * LANES), lambda hp, j: (j, hp)),
                  pl.BlockSpec((2 * LANES, bk), lambda hp, j: (hp, j)),
                  pl.BlockSpec((bk, LANES), lambda hp, j: (j, 2 * N_PAIRS + hp)),
                  pl.BlockSpec((2 * LANES, t), lambda hp, j: (hp, 0), pipeline_mode=once),
                  pl.BlockSpec((LANES, t), lambda hp, j: (hp, 0), pipeline_mode=once),
                  rows, rows, rows],
        out_specs=[kv_out, kv_out, pl.BlockSpec((None, SUBLANES, bk), lambda hp, j: (hp, 0, j)),
                   pl.BlockSpec(memory_space=pltpu.HBM)],
        out_shape=[jax.ShapeDtypeStruct((D_MODEL, t), BF16), jax.ShapeDtypeStruct((D_MODEL, t), BF16),
                   jax.ShapeDtypeStruct((N_PAIRS, SUBLANES, t), F32),
                   jax.ShapeDtypeStruct((N_HEADS * LANES, t), F32)],
        scratch_shapes=[pltpu.VMEM((2, LANES, bk), F32), pltpu.VMEM((2, HEAD_DIM, bk), F32),
                        pltpu.VMEM((2, LANES, t), F32), pltpu.SemaphoreType.DMA],
        compiler_params=_cparams("arbitrary", "arbitrary"),
    )(kh, kt, qkv, qt, dotb, c_pairs, lse_pairs, d_pairs)


def _scan(a, u, name, reverse=False):
    t, c = a.shape
    bt, cb = _pick(t, ROW_TILE), _pick(c, 1024)
    nt = t // bt
    ngroups = bt // SUBLANES

    def body(a_ref, u_ref, h_ref, carry_ref, as_ref, us_ref):
        @pl.when(pl.program_id(1) == 0)
        def _():
            carry_ref[...] = jnp.zeros_like(carry_ref)

        av, uv = a_ref[...], u_ref[...]
        sub = _rows_iota((bt, cb)) % SUBLANES
        for s in (1, 2, 4):
            if reverse:
                a_sh, u_sh = pltpu.roll(av, bt - s, axis=0), pltpu.roll(uv, bt - s, axis=0)
                valid = sub < SUBLANES - s
            else:
                a_sh, u_sh = pltpu.roll(av, s, axis=0), pltpu.roll(uv, s, axis=0)
                valid = sub >= s
            uv = jnp.where(valid, uv + av * u_sh, uv)
            av = jnp.where(valid, av * a_sh, av)
        as_ref[...] = av
        us_ref[...] = uv
        edge = 0 if reverse else SUBLANES - 1
        pick = _rows_iota((SUBLANES, cb)) == edge

        def group(gi, carry):
            g = (ngroups - 1 - gi) if reverse else gi
            r0 = pl.multiple_of(g * SUBLANES, SUBLANES)
            h8 = us_ref[pl.ds(r0, SUBLANES), :] + as_ref[pl.ds(r0, SUBLANES), :] * carry
            h_ref[pl.ds(r0, SUBLANES), :] = h8
            return jnp.sum(jnp.where(pick, h8, 0.0), axis=0, keepdims=True)

        carry_ref[...] = lax.fori_loop(0, ngroups, group, carry_ref[...])

    if reverse:
        tile = pl.BlockSpec((bt, cb), lambda j, i: (nt - 1 - i, j))
    else:
        tile = pl.BlockSpec((bt, cb), lambda j, i: (i, j))
    return pl.pallas_call(
        body, name=name, grid=(c // cb, nt), in_specs=[tile, tile], out_specs=tile,
        out_shape=jax.ShapeDtypeStruct((t, c), F32),
        scratch_shapes=[pltpu.VMEM((1, cb), F32), pltpu.VMEM((bt, cb), F32), pltpu.VMEM((bt, cb), F32)],
        compiler_params=_cparams("parallel", "arbitrary"),
    )(a, u)


def _rg_conv_fwd(proj, cw, cb_, name):
    t = proj.shape[0]
    bt, cb = _pick(t, ROW_TILE), D_MODEL

    def body(x_ref, halo_ref, w_ref, b_ref, o_ref):
        keep = (pl.program_id(0) > 0).astype(F32)
        o_ref[...] = _causal_conv(x_ref[...], halo_ref[...] * keep, w_ref, CONV_B) + b_ref[...]

    tile = pl.BlockSpec((bt, cb), lambda i, j: (i, j))
    return pl.pallas_call(
        body, name=name, grid=(t // bt, 1),
        in_specs=[tile, _prev_halo_spec(bt, cb), pl.BlockSpec((CONV_B, cb), lambda i, j: (0, 0)),
                  pl.BlockSpec((1, cb), lambda i, j: (0, 0))],
        out_specs=tile, out_shape=jax.ShapeDtypeStruct((t, D_MODEL), F32),
        compiler_params=_cparams("parallel", "parallel"),
    )(proj, proj, cw, cb_.reshape(1, D_MODEL))


def _conv_bwd_w(x, dy, ksize, name):
    t, c = dy.shape
    bt = _pick(t, ROW_TILE)
    r = bt // SUBLANES

    def body(x_ref, halo_ref, dy_ref, dw_ref, db_ref):
        @pl.when(pl.program_id(0) == 0)
        def _():
            dw_ref[...] = jnp.zeros_like(dw_ref)
            db_ref[...] = jnp.zeros_like(db_ref)

        keep = (pl.program_id(0) > 0).astype(F32)
        xv, halo, dyv = x_ref[...], halo_ref[...] * keep, dy_ref[...]
        db_ref[...] += jnp.sum(dyv, axis=0, keepdims=True)
        for j in range(ksize):
            dw_ref[j:j + 1, :] += jnp.sum(dyv * _shift_down(xv, halo, ksize - 1 - j), axis=0, keepdims=True)

    tile = pl.BlockSpec((bt, c), lambda i: (i, 0))
    return pl.pallas_call(
        body, name=name, grid=(t // bt,),
        in_specs=[tile, pl.BlockSpec((SUBLANES, c), lambda i: (jnp.maximum(i * r - 1, 0), 0)), tile],
        out_specs=[pl.BlockSpec((ksize, c), lambda i: (0, 0)), pl.BlockSpec((1, c), lambda i: (0, 0))],
        out_shape=[jax.ShapeDtypeStruct((ksize, c), F32), jax.ShapeDtypeStruct((1, c), F32)],
        compiler_params=_cparams("arbitrary"),
    )(x, x, dy)


def _rg_gate_math(xc, wa_ref, wi_ref, ba_ref, bi_ref, lam_ref):
    xb = xc.astype(BF16)
    ra = lax.dot_general(xb, wa_ref[...], _DOT_DIMS["nn"], preferred_element_type=F32) + ba_ref[...]
    ia = lax.dot_general(xb, wi_ref[...], _DOT_DIMS["nn"], preferred_element_type=F32) + bi_ref[...]
    r, ig = _sigmoid(ra), _sigmoid(ia)
    sp = _softplus(-lam_ref[...])
    log_a = -LRU_C * r * sp
    a = jnp.exp(log_a)
    mult = jnp.sqrt(-_expm1(2.0 * log_a))
    return xb, r, ig, sp, a, mult


def _rg_gate_specs(bt, time_first):
    if time_first:
        tile = pl.BlockSpec((bt, BLOCK_B), lambda i, n: (i, n))
        w = pl.BlockSpec((None, BLOCK_B, BLOCK_B), lambda i, n: (n, 0, 0))
        v = pl.BlockSpec((None, 1, BLOCK_B), lambda i, n: (n, 0, 0))
    else:
        tile = pl.BlockSpec((bt, BLOCK_B), lambda n, i: (i, n))
        w = pl.BlockSpec((None, BLOCK_B, BLOCK_B), lambda n, i: (n, 0, 0))
        v = pl.BlockSpec((None, 1, BLOCK_B), lambda n, i: (n, 0, 0))
    return tile, w, v


def _rg_gate_fwd(xc, wa, ba, wi, bi, lam, name):
    t = xc.shape[0]
    bt = _pick(t, 512)

    def body(x_ref, wa_ref, wi_ref, ba_ref, bi_ref, lam_ref, a_ref, u_ref):
        xcv = x_ref[...]
        _, _, ig, _, a, mult = _rg_gate_math(xcv, wa_ref, wi_ref, ba_ref, bi_ref, lam_ref)
        a_ref[...] = a
        u_ref[...] = mult * (ig * xcv)

    tile, w, v = _rg_gate_specs(bt, True)
    return pl.pallas_call(
        body, name=name, grid=(t // bt, N_BLOCKS_B), in_specs=[tile, w, w, v, v, v], out_specs=[tile, tile],
        out_shape=[jax.ShapeDtypeStruct((t, D_MODEL), F32), jax.ShapeDtypeStruct((t, D_MODEL), F32)],
        compiler_params=_cparams("parallel", "parallel"),
    )(xc, wa, wi, ba, bi, lam)


def _rg_gate_bwd(xc, g, h, wa, ba, wi, bi, lam, name):
    t = xc.shape[0]
    bt = _pick(t, 512)
    rr = bt // SUBLANES

    def body(x_ref, g_ref, h_ref, hh_ref, wa_ref, wi_ref, ba_ref, bi_ref, lam_ref,
             dx_ref, dwa_ref, dwi_ref, dba_ref, dbi_ref, dlam_ref):
        @pl.when(pl.program_id(1) == 0)
        def _():
            for ref in (dwa_ref, dwi_ref, dba_ref, dbi_ref, dlam_ref):
                ref[...] = jnp.zeros_like(ref)

        keep = (pl.program_id(1) > 0).astype(F32)
        xcv, gv = x_ref[...], g_ref[...]
        xb, r, ig, sp, a, mult = _rg_gate_math(xcv, wa_ref, wi_ref, ba_ref, bi_ref, lam_ref)
        h_prev = _shift_down(h_ref[...], hh_ref[...] * keep, 1)
        da = gv * h_prev
        dmult = gv * ig * xcv
        dig = gv * mult * xcv
        dxc = gv * mult * ig
        dlog_a = da * a - dmult * (a * a / mult)
        dr = dlog_a * (-LRU_C * sp)
        dsp = jnp.sum(dlog_a * (-LRU_C * r), axis=0, keepdims=True)
        dlam_ref[...] += dsp * (-_sigmoid(-lam_ref[...]))
        dra = dr * r * (1.0 - r)
        dia = dig * ig * (1.0 - ig)
        dba_ref[...] += jnp.sum(dra, axis=0, keepdims=True)
        dbi_ref[...] += jnp.sum(dia, axis=0, keepdims=True)
        drab, diab = dra.astype(BF16), dia.astype(BF16)
        dwa_ref[...] += lax.dot_general(xb, drab, _DOT_DIMS["tn"], preferred_element_type=F32)
        dwi_ref[...] += lax.dot_general(xb, diab, _DOT_DIMS["tn"], preferred_element_type=F32)
        dxc = dxc + lax.dot_general(drab, wa_ref[...], _DOT_DIMS["nt"], preferred_element_type=F32)
        dxc = dxc + lax.dot_general(diab, wi_ref[...], _DOT_DIMS["nt"], preferred_element_type=F32)
        dx_ref[...] = dxc

    tile, w, v = _rg_gate_specs(bt, False)
    halo = pl.BlockSpec((SUBLANES, BLOCK_B), lambda n, i: (jnp.maximum(i * rr - 1, 0), n))
    wshape = jax.ShapeDtypeStruct((N_BLOCKS_B, BLOCK_B, BLOCK_B), F32)
    vshape = jax.ShapeDtypeStruct((N_BLOCKS_B, 1, BLOCK_B), F32)
    return pl.pallas_call(
        body, name=name, grid=(N_BLOCKS_B, t // bt),
        in_specs=[tile, tile, tile, halo, w, w, v, v, v],
        out_specs=[tile, w, w, v, v, v],
        out_shape=[jax.ShapeDtypeStruct((t, D_MODEL), F32), wshape, wshape, vshape, vshape, vshape],
        compiler_params=_cparams("parallel", "arbitrary"),
    )(xc, g, h, h, wa, wi, ba, bi, lam)


def _rg_out_fwd(h, proj, name):
    t = h.shape[0]
    bt = _pick(t, ROW_TILE)

    def body(h_ref, g_ref, y_ref):
        y_ref[...] = (h_ref[...] * _gelu(g_ref[...])).astype(BF16)

    tile = pl.BlockSpec((bt, D_MODEL), lambda i: (i, 0))
    return pl.pallas_call(
        body, name=name, grid=(t // bt,), in_specs=[tile, pl.BlockSpec((bt, D_MODEL), lambda i: (i, 1))],
        out_specs=tile, out_shape=jax.ShapeDtypeStruct((t, D_MODEL), BF16),
        compiler_params=_cparams("parallel"),
    )(h, proj)


def _rg_out_bwd(dy, h, proj, name):
    t = h.shape[0]
    bt = _pick(t, ROW_TILE)

    def body(dy_ref, h_ref, g_ref, dh_ref, dg_ref):
        gl, glg = _gelu_and_grad(g_ref[...])
        dyv = dy_ref[...]
        dh_ref[...] = dyv * gl
        dg_ref[...] = (dyv * h_ref[...] * glg).astype(BF16)

    tile = pl.BlockSpec((bt, D_MODEL), lambda i: (i, 0))
    return pl.pallas_call(
        body, name=name, grid=(t // bt,),
        in_specs=[tile, tile, pl.BlockSpec((bt, D_MODEL), lambda i: (i, 1))], out_specs=[tile, tile],
        out_shape=[jax.ShapeDtypeStruct((t, D_MODEL), F32), jax.ShapeDtypeStruct((t, D_MODEL), BF16)],
        compiler_params=_cparams("parallel"),
    )(dy, h, proj)


def _shift_up_one(a, name):
    t, c = a.shape
    bt = _pick(t, ROW_TILE)

    def body(a_ref, halo_ref, o_ref):
        o_ref[...] = _shift_up(a_ref[...], halo_ref[...], 1)

    tile = pl.BlockSpec((bt, c), lambda i, j: (i, j))
    return pl.pallas_call(
        body, name=name, grid=(t // bt, 1), in_specs=[tile, _next_halo_spec(bt, c, t)], out_specs=tile,
        out_shape=jax.ShapeDtypeStruct((t, c), F32), compiler_params=_cparams("parallel", "parallel"),
    )(a, a)


ADAM_ROWS = 64


def _adamw(recv, w, m, v, name):
    _, r, c = recv.shape
    br = ADAM_ROWS
    assert r % br == 0

    def body(r_ref, w_ref, m_ref, v_ref, g_ref, d_ref, nm_ref, nv_ref):
        g = r_ref[0].astype(F32)
        for s in range(1, N_DEV):
            g = g + r_ref[s].astype(F32)
        m_new = ADAM_B1 * m_ref[...] + (1.0 - ADAM_B1) * g
        v_new = ADAM_B2 * v_ref[...] + (1.0 - ADAM_B2) * (g * g)
        m_hat = m_new / (1.0 - ADAM_B1 ** ADAM_STEP)
        v_hat = v_new / (1.0 - ADAM_B2 ** ADAM_STEP)
        g_ref[...] = g
        d_ref[...] = -ADAM_LR * (m_hat / (jnp.sqrt(v_hat) + ADAM_EPS) + ADAM_WD * w_ref[...])
        nm_ref[...] = m_new
        nv_ref[...] = v_new

    tile = pl.BlockSpec((br, c), lambda i: (i, 0))
    shape = jax.ShapeDtypeStruct((r, c), F32)
    return pl.pallas_call(
        body, name=name, grid=(r // br,),
        in_specs=[pl.BlockSpec((N_DEV, br, c), lambda i: (0, i, 0)), tile, tile, tile],
        out_specs=[tile] * 4, out_shape=[shape] * 4, compiler_params=_cparams("parallel"),
    )(recv, w, m, v)


def _all_to_all(src, name):
    def body(src_ref, out_ref, send_sems, recv_sems, local_sem):
        pos = [lax.axis_index(ax) for ax in MESH_AXES]
        me = 4 * pos[0] + 2 * pos[1] + pos[2]

        def peer_of(k):
            p = [(1 - pos[b]) if (k >> (2 - b)) & 1 else pos[b] for b in range(3)]
            return tuple(p), 4 * p[0] + 2 * p[1] + p[2]

        def copy(k):
            peer, peer_idx = peer_of(k)
            return pltpu.make_async_remote_copy(
                src_ref=src_ref.at[peer_idx], dst_ref=out_ref.at[me],
                send_sem=send_sems.at[k - 1], recv_sem=recv_sems.at[k - 1],
                device_id=peer, device_id_type=pl.DeviceIdType.MESH)

        def arrival(k):
            peer, peer_idx = peer_of(k)
            return pltpu.make_async_remote_copy(
                src_ref=src_ref.at[me], dst_ref=out_ref.at[peer_idx],
                send_sem=send_sems.at[k - 1], recv_sem=recv_sems.at[k - 1],
                device_id=peer, device_id_type=pl.DeviceIdType.MESH)

        mine = pltpu.make_async_copy(src_ref.at[me], out_ref.at[me], local_sem)
        mine.start()
        sends = [copy(k) for k in range(1, N_DEV)]
        for cp in sends:
            cp.start()
        for k in range(1, N_DEV):
            arrival(k).wait_recv()
        for cp in sends:
            cp.wait_send()
        mine.wait()

    hbm = pl.BlockSpec(memory_space=pltpu.HBM)
    return pl.pallas_call(
        body, name=name, in_specs=[hbm], out_specs=hbm,
        out_shape=jax.ShapeDtypeStruct(src.shape, src.dtype),
        scratch_shapes=[pltpu.SemaphoreType.DMA((N_DEV - 1,)), pltpu.SemaphoreType.DMA((N_DEV - 1,)),
                        pltpu.SemaphoreType.DMA],
        compiler_params=pltpu.CompilerParams(has_side_effects=True),
    )(src)


def _all_gather(src, name):
    def body(src_ref, out_ref, send_sems, recv_sems, local_sem):
        x, y, c = (lax.axis_index(ax) for ax in MESH_AXES)
        me, sibling = (x, y, c), (x, y, 1 - c)
        chips = [(1 - x, y), (x, 1 - y), (1 - x, 1 - y)]

        def slot(px, py, pc):
            return out_ref.at[4 * px + 2 * py + pc]

        def copy(k, block, to, from_src=False):
            return pltpu.make_async_remote_copy(
                src_ref=src_ref if from_src else slot(*block), dst_ref=slot(*block),
                send_sem=send_sems.at[k], recv_sem=recv_sems.at[k],
                device_id=to, device_id_type=pl.DeviceIdType.MESH)

        mine = pltpu.make_async_copy(src_ref, slot(*me), local_sem)
        mine.start()
        first = [copy(0, me, sibling, True)] + [copy(1 + j, me, (*chip, c), True) for j, chip in enumerate(chips)]
        for cp in first:
            cp.start()
        passed = [copy(4 + j, (*chip, c), sibling) for j, chip in enumerate(chips)]
        for j, chip in enumerate(chips):
            copy(1 + j, (*chip, c), me).wait_recv()
            passed[j].start()
        copy(0, sibling, me).wait_recv()
        for j, chip in enumerate(chips):
            copy(4 + j, (*chip, 1 - c), me).wait_recv()
        for cp in first + passed:
            cp.wait_send()
        mine.wait()

    hbm = pl.BlockSpec(memory_space=pltpu.HBM)
    return pl.pallas_call(
        body, name=name, in_specs=[hbm], out_specs=hbm,
        out_shape=jax.ShapeDtypeStruct((N_DEV,) + tuple(src.shape), src.dtype),
        scratch_shapes=[pltpu.SemaphoreType.DMA((N_DEV - 1,)), pltpu.SemaphoreType.DMA((N_DEV - 1,)),
                        pltpu.SemaphoreType.DMA],
        compiler_params=pltpu.CompilerParams(has_side_effects=True),
    )(src)


WEIGHTS = ['a_w_in', 'a_b_f', 'a_w_out', 'b_w_in', 'b_conv_w', 'b_conv_b', 'b_w_a', 'b_b_a', 'b_w_i', 'b_b_i',
           'b_lam', 'b_w_out', 'f_w_up', 'f_conv_w', 'f_conv_b', 'f_w_down', 'ln1_g', 'ln1_b', 'ln2_g', 'ln2_b',
           'ple_w', 'ple_gate_w', 'ple_gate_b']
SHARD_AXIS = {'a_w_in': 2, 'a_b_f': None, 'a_w_out': 1, 'b_w_in': 2, 'b_conv_w': 2, 'b_conv_b': 1, 'b_w_a': None,
              'b_b_a': None, 'b_w_i': None, 'b_b_i': None, 'b_lam': 1, 'b_w_out': 1, 'f_w_up': 2, 'f_conv_w': 2,
              'f_conv_b': None, 'f_w_down': 1, 'ln1_g': None, 'ln1_b': None, 'ln2_g': None, 'ln2_b': None,
              'ple_w': 2, 'ple_gate_w': 1, 'ple_gate_b': None}
MATMUL_WEIGHTS = ['a_w_in', 'a_w_out', 'b_w_in', 'b_w_out', 'f_w_up', 'f_w_down', 'ple_w', 'ple_gate_w']
SMALL_SHARDED = ['b_conv_w', 'b_conv_b', 'b_lam', 'f_conv_w']
GRADS_AS_BF16 = MATMUL_WEIGHTS + ['b_w_a', 'b_w_i']
PACK_COLS = 1024


def _to_shards(full, axis):
    return jnp.stack(jnp.split(full, N_DEV, axis=axis))


def _from_shards(pieces, axis):
    return jnp.concatenate([pieces[d] for d in range(N_DEV)], axis=axis)


PIECE_ROWS = 16


def _piece_rows(size):
    rows = -(-size // PACK_COLS)
    return -(-rows // PIECE_ROWS) * PIECE_ROWS


def _pack_pieces(pieces, total_mult=PIECE_ROWS):
    lead = pieces[0].shape[:-1]
    blocks = []
    for pc in pieces:
        n = pc.shape[-1]
        rows = _piece_rows(n)
        pad = [(0, 0)] * len(lead) + [(0, rows * PACK_COLS - n)]
        blocks.append(jnp.pad(pc, pad).reshape(lead + (rows, PACK_COLS)))
    total = sum(b.shape[-2] for b in blocks)
    extra = -total % total_mult
    if extra:
        blocks.append(jnp.zeros(lead + (extra, PACK_COLS), pieces[0].dtype))
    return jnp.concatenate(blocks, axis=len(lead))


def _unpack_pieces(packed, shapes):
    lead = packed.shape[:-2]
    out, row = [], 0
    for shp in shapes:
        size = math.prod(shp)
        rows = _piece_rows(size)
        block = lax.slice_in_dim(packed, row, row + rows, axis=len(lead))
        flat = block.reshape(lead + (rows * PACK_COLS,))
        out.append(lax.slice_in_dim(flat, 0, size, axis=len(lead)).reshape(lead + tuple(shp)))
        row += rows
    return out


def _gather_weights(local, names, dtype, name):
    packed = _pack_pieces([local[n].astype(dtype).reshape(-1) for n in names])
    gathered = _all_gather(packed, name)
    pieces = _unpack_pieces(gathered, [local[n].shape for n in names])
    return {n: _from_shards(pc, SHARD_AXIS[n]) for n, pc in zip(names, pieces)}


def _mixer_a_fwd(tag, xb, w):
    qkv = _mm(xb, w["wqkv"], "nn", BF16, f"{tag}_qkv")
    fg = _mm(xb, w["wf"], "nn", F32, f"{tag}_fgproj")
    fg_rows = fg[:, :N_HEADS].T
    c_rows = _fgate_fwd(fg_rows, w["b_f"], f"{tag}_fgate")
    c_pairs = _rows_to_pairs(c_rows)
    qh, kh, qt, kt, vt = _attn_prep(qkv, _rows_to_cols(c_rows), f"{tag}_attn_prep")
    ot, otb, lse_pairs = _attn_fwd_t(qh, kh, vt, c_pairs, f"{tag}_attn")
    m = _mm(otb, w["wout"], "tn", F32, f"{tag}_oproj")
    return m, dict(qkv=qkv, kh=kh, qt=qt, kt=kt, fg_rows=fg_rows, c_pairs=c_pairs, ot=ot, otb=otb,
                   lse_pairs=lse_pairs)


def _mixer_a_bwd(tag, dz, dzb, xb, w, s):
    t = xb.shape[0]
    dot = _mm(w["wout"], dzb, "nt", F32, f"{tag}_b_do")
    g_wout = _mm(s["otb"], dzb, "nn", F32, f"{tag}_b_dwout")
    d_pairs, dotb = _attn_delta_t(dot, s["ot"], f"{tag}_b_delta")
    dkt, dvt, dck_pairs, dq_aug = _attn_bwd_t(s["kh"], s["kt"], s["qt"], s["qkv"], dotb, s["c_pairs"],
                                              s["lse_pairs"], d_pairs, f"{tag}_b_attn")
    dq_aug = dq_aug.reshape(N_HEADS, LANES, t)
    dfg_rows, db_f = _fgate_bwd(dck_pairs[:, :2, :].reshape(N_HEADS, t), dq_aug[:, AUG_ONE, :], s["fg_rows"],
                                w["b_f"], f"{tag}_b_fgate")
    dqt = dq_aug[:, :HEAD_DIM, :].reshape(D_MODEL, t).astype(BF16)
    dqkv = jnp.concatenate([dqt, dkt, dvt], axis=0).T
    dfg = jnp.pad(dfg_rows.T, ((0, 0), (0, LANES - N_HEADS))).astype(BF16)
    dx = _mm(dqkv, w["wqkv"], "nt", F32, f"{tag}_b_dx_qkv", add=dz, add_scale=ALPHA, tk=3 * D_MODEL)
    dx = _mm(dfg, w["wf"], "nt", F32, f"{tag}_b_dx_fg", add=dx)
    g_wqkv = _mm(xb, dqkv, "tn", F32, f"{tag}_b_dwqkv")
    g_wf = _mm(xb, dfg, "tn", F32, f"{tag}_b_dwf")[:, :N_HEADS]
    grads = dict(a_w_in=jnp.concatenate([g_wqkv, g_wf], axis=1), a_b_f=db_f.reshape(N_HEADS), a_w_out=g_wout)
    return dx, grads


def _mixer_b_fwd(tag, xb, w):
    proj = _mm(xb, w["win"], "nn", F32, f"{tag}_proj")
    xc = _rg_conv_fwd(proj, w["conv_w"], w["conv_b"], f"{tag}_conv")
    a, u = _rg_gate_fwd(xc, w["wa"], w["ba"], w["wi"], w["bi"], w["lam"], f"{tag}_gate")
    h = _scan(a, u, f"{tag}_scan")
    y = _rg_out_fwd(h, proj, f"{tag}_out")
    m = _mm(y, w["wout"], "nn", F32, f"{tag}_oproj")
    return m, dict(proj=proj, xc=xc, a=a, h=h, y=y)


def _mixer_b_bwd(tag, dz, dzb, xb, w, s):
    dy = _mm(dzb, w["wout"], "nt", F32, f"{tag}_b_dy")
    g_wout = _mm(s["y"], dzb, "tn", F32, f"{tag}_b_dwout")
    dh, dgate = _rg_out_bwd(dy, s["h"], s["proj"], f"{tag}_b_out")
    g = _scan(_shift_up_one(s["a"], f"{tag}_b_shift"), dh, f"{tag}_b_scan", reverse=True)
    dxc, g_wa, g_wi, g_ba, g_bi, g_lam = _rg_gate_bwd(
        s["xc"], g, s["h"], w["wa"], w["ba"], w["wi"], w["bi"], w["lam"], f"{tag}_b_gate")
    dxp = _conv_bwd_x(dxc, w["conv_w"], f"{tag}_b_convx", BF16)
    g_cw, g_cb = _conv_bwd_w(s["proj"], dxc, CONV_B, f"{tag}_b_convw")
    dproj = jnp.concatenate([dxp, dgate], axis=1)
    dx = _mm(dproj, w["win"], "nt", F32, f"{tag}_b_dx", add=dz, add_scale=ALPHA, tk=2 * D_MODEL)
    g_win = _mm(xb, dproj, "tn", F32, f"{tag}_b_dwin")
    grads = dict(b_w_in=g_win, b_conv_w=g_cw, b_conv_b=g_cb.reshape(D_MODEL), b_w_a=g_wa,
                 b_b_a=g_ba.reshape(N_BLOCKS_B, BLOCK_B), b_w_i=g_wi, b_b_i=g_bi.reshape(N_BLOCKS_B, BLOCK_B),
                 b_lam=g_lam.reshape(D_MODEL), b_w_out=g_wout)
    return dx, grads


def _layer_fwd(i, x, xb, pb, w):
    tag = f"L{i}"
    mix = _mixer_a_fwd if i % 2 == 0 else _mixer_b_fwd
    m, sm = mix(tag, xb, w)
    x1, x1b, z1 = _ln_fwd(x, m, w["ln1_g"], w["ln1_b"], f"{tag}_ln1")
    h = _mm(x1b, w["wup"], "nn", F32, f"{tag}_ffn_up")
    y = _ffn_mid_fwd(h, w["fconv_w"], w["fconv_b"], f"{tag}_ffn_mid")
    ff = _mm(y, w["wdown"], "nn", F32, f"{tag}_ffn_down", tk=D_FF)
    x2, x2b, z2 = _ln_fwd(x1, ff, w["ln2_g"], w["ln2_b"], f"{tag}_ln2")
    gl = _mm(x2b, w["wg"], "nn", F32, f"{tag}_ple_gate")
    e = _mm(pb, w["wp"], "nn", F32, f"{tag}_ple_emb")
    x3, x3b = _ple_fwd(x2, gl, e, w["bg"], f"{tag}_ple")
    saved = dict(mixer=sm, xb=xb, x1b=x1b, z1=z1, h=h, y=y, x2b=x2b, z2=z2, gl=gl, e=e, pb=pb)
    return x3, x3b, saved


def _layer_bwd(i, dx3, w, s):
    tag = f"L{i}"
    dgl, de, g_bg = _ple_bwd(dx3, s["gl"], s["e"], w["bg"], f"{tag}_b_ple")
    g_wg = _mm(s["x2b"], dgl, "tn", F32, f"{tag}_b_dwg")
    g_wp = _mm(s["pb"], de, "tn", F32, f"{tag}_b_dwp")
    dx2 = _mm(dgl, w["wg"], "nt", F32, f"{tag}_b_dx2", add=dx3)
    dz2, dz2b, g_ln2g, g_ln2b = _ln_bwd(dx2, s["z2"], w["ln2_g"], f"{tag}_b_ln2")
    dy = _mm(dz2b, w["wdown"], "nt", F32, f"{tag}_b_dy")
    g_wdown = _mm(s["y"], dz2b, "tn", F32, f"{tag}_b_dwdown", tm=1408)
    dval, dgate, g_fcw, g_fcb = _ffn_mid_bwd_a(s["h"], dy, w["fconv_w"], w["fconv_b"], f"{tag}_b_ffn_mid")
    dhv = _conv_bwd_x(dval, w["fconv_w"][:, :D_FF], f"{tag}_b_convx_v", BF16)
    dhg = _conv_bwd_x(dgate, w["fconv_w"][:, D_FF:], f"{tag}_b_convx_g", BF16)
    dx1 = _mm(dhv, w["wup"][:, :D_FF], "nt", F32, f"{tag}_b_dx1_v", add=dz2, add_scale=ALPHA, tk=D_FF)
    dx1 = _mm(dhg, w["wup"][:, D_FF:], "nt", F32, f"{tag}_b_dx1_g", add=dx1, tk=D_FF)
    g_wup = jnp.concatenate([_mm(dhv, s["x1b"], "tn", F32, f"{tag}_b_dwup_v", tm=1408),
                             _mm(dhg, s["x1b"], "tn", F32, f"{tag}_b_dwup_g", tm=1408)], axis=0).T
    dz1, dz1b, g_ln1g, g_ln1b = _ln_bwd(dx1, s["z1"], w["ln1_g"], f"{tag}_b_ln1")
    mix_bwd = _mixer_a_bwd if i % 2 == 0 else _mixer_b_bwd
    dx, g_mix = mix_bwd(tag, dz1, dz1b, s["xb"], w, s["mixer"])
    grads = dict(f_w_up=g_wup, f_conv_w=g_fcw, f_conv_b=g_fcb.reshape(2 * D_FF), f_w_down=g_wdown,
                 ln1_g=g_ln1g.reshape(D_MODEL), ln1_b=g_ln1b.reshape(D_MODEL), ln2_g=g_ln2g.reshape(D_MODEL),
                 ln2_b=g_ln2b.reshape(D_MODEL), ple_w=g_wp, ple_gate_w=g_wg, ple_gate_b=g_bg.reshape(D_MODEL))
    return dx, g_mix, grads


def _layer_weights(i, full, rep):
    j = i // 2
    w = dict(ln1_g=rep["ln1_g"][i], ln1_b=rep["ln1_b"][i], ln2_g=rep["ln2_g"][i], ln2_b=rep["ln2_b"][i],
             wup=full["f_w_up"][i], fconv_w=full["f_conv_w"][i], fconv_b=rep["f_conv_b"][i],
             wdown=full["f_w_down"][i], wp=full["ple_w"][i], wg=full["ple_gate_w"][i], bg=rep["ple_gate_b"][i])
    if i % 2 == 0:
        w_in = full["a_w_in"][j]
        w.update(wqkv=w_in[:, :3 * D_MODEL],
                 wf=jnp.pad(w_in[:, 3 * D_MODEL:], ((0, 0), (0, LANES - N_HEADS))),
                 b_f=rep["a_b_f"][j], wout=full["a_w_out"][j])
    else:
        w.update(win=full["b_w_in"][j], conv_w=full["b_conv_w"][j], conv_b=full["b_conv_b"][j],
                 wa=rep["b_w_a"][j].astype(BF16), wi=rep["b_w_i"][j].astype(BF16),
                 ba=rep["b_b_a"][j].reshape(N_BLOCKS_B, 1, BLOCK_B), bi=rep["b_b_i"][j].reshape(N_BLOCKS_B, 1, BLOCK_B),
                 lam=full["b_lam"][j].reshape(N_BLOCKS_B, 1, BLOCK_B), wout=full["b_w_out"][j])
    return w


def _fwd_bwd(x, p, target, full, rep):
    weights = [_layer_weights(i, full, rep) for i in range(DEPTH)]
    xb = x.astype(BF16)
    pb = p.astype(BF16)
    saved = []
    for i in range(DEPTH):
        x, xb, s = _layer_fwd(i, x, xb, pb[i], weights[i])
        saved.append(s)
    dx, loss_local = _loss_head(x, target, "loss_head")

    per_layer = {n: [None] * (DEPTH if n.startswith(("f_", "ln", "ple")) else DEPTH // 2) for n in WEIGHTS}
    for i in reversed(range(DEPTH)):
        dx, g_mix, g_layer = _layer_bwd(i, dx, weights[i], saved[i])
        for n, g in g_layer.items():
            per_layer[n][i] = g
        for n, g in g_mix.items():
            per_layer[n][i // 2] = g
    return loss_local, dx, per_layer


def _train_step(x, p, target, local, moments_m, moments_v):
    full = _gather_weights(local, MATMUL_WEIGHTS, BF16, "gather_matmul_weights")
    full.update(_gather_weights(local, SMALL_SHARDED, F32, "gather_small_weights"))
    rep = {n: local[n] for n in WEIGHTS if SHARD_AXIS[n] is None}
    loss_local, dx, grads_layers = _fwd_bwd(x, p, target, full, rep)

    unpacked = [{} for _ in range(4)]
    for group, dtype, tag in ((GRADS_AS_BF16, BF16, "big"), ([n for n in WEIGHTS if n not in GRADS_AS_BF16], F32, "small")):
        pieces = []
        for n in group:
            g = jnp.stack(grads_layers[n]).astype(dtype)
            if SHARD_AXIS[n] is None:
                pieces.append(jnp.broadcast_to(g.reshape(1, -1), (N_DEV, g.size)))
            else:
                pieces.append(_to_shards(g, SHARD_AXIS[n]).reshape(N_DEV, -1))
        recv = _all_to_all(_pack_pieces(pieces, ADAM_ROWS), f"reduce_scatter_grads_{tag}")

        def pack_local(d):
            return _pack_pieces([d[n].astype(F32).reshape(-1) for n in group], ADAM_ROWS)

        outs = _adamw(recv, pack_local(local), pack_local(moments_m), pack_local(moments_v), f"adamw_{tag}")
        shapes = [local[n].shape for n in group]
        for dst, packed in zip(unpacked, outs):
            dst.update(zip(group, _unpack_pieces(packed, shapes)))
    return loss_local, dx, unpacked


def kernel(x, p, a_w_in, a_b_f, a_w_out, b_w_in, b_conv_w, b_conv_b, b_w_a, b_b_a, b_w_i, b_b_i, b_lam, b_w_out, f_w_up, f_conv_w, f_conv_b, f_w_down, ln1_g, ln1_b, ln2_g, ln2_b, ple_w, ple_gate_w, ple_gate_b, loss_target, m_a_w_in, m_a_b_f, m_a_w_out, m_b_w_in, m_b_conv_w, m_b_conv_b, m_b_w_a, m_b_b_a, m_b_w_i, m_b_b_i, m_b_lam, m_b_w_out, m_f_w_up, m_f_conv_w, m_f_conv_b, m_f_w_down, m_ln1_g, m_ln1_b, m_ln2_g, m_ln2_b, m_ple_w, m_ple_gate_w, m_ple_gate_b, v_a_w_in, v_a_b_f, v_a_w_out, v_b_w_in, v_b_conv_w, v_b_conv_b, v_b_w_a, v_b_b_a, v_b_w_i, v_b_b_i, v_b_lam, v_b_w_out, v_f_w_up, v_f_conv_w, v_f_conv_b, v_f_w_down, v_ln1_g, v_ln1_b, v_ln2_g, v_ln2_b, v_ple_w, v_ple_gate_w, v_ple_gate_b):
    given = dict(locals())
    local = {n: given[n] for n in WEIGHTS}
    mom_m = {n: given["m_" + n] for n in WEIGHTS}
    mom_v = {n: given["v_" + n] for n in WEIGHTS}
    t = x.shape[1]
    loss_local, dx, (grad, delta, new_m, new_v) = _train_step(
        x.reshape(t, D_MODEL), p.reshape(DEPTH, t, D_PLE), loss_target.reshape(t, D_MODEL), local, mom_m, mom_v)
    loss = lax.psum(loss_local, MESH_AXES)
    return (loss, dx.reshape(1, t, D_MODEL), *[grad[n] for n in WEIGHTS], *[delta[n] for n in WEIGHTS],
            *[new_m[n] for n in WEIGHTS], *[new_v[n] for n in WEIGHTS])
```

```python
import math

import jax
import jax.numpy as jnp
from jax import lax
from jax.experimental import pallas as pl
from jax.experimental.pallas import tpu as pltpu

F32 = jnp.float32
BF16 = jnp.bfloat16

D_MODEL = 1024
DEPTH = 4
N_HEADS = 16
HEAD_DIM = 64
N_PAIRS = N_HEADS // 2
N_BLOCKS_B = 8
BLOCK_B = 128
CONV_B = 4
LRU_C = 8.0
D_FF = 2816
CONV_F = 3
D_PLE = 256
LN_EPS = 1e-5
ALPHA = (2.0 * DEPTH) ** 0.25
ADAM_LR, ADAM_B1, ADAM_B2, ADAM_EPS, ADAM_WD, ADAM_STEP = 0.001, 0.9, 0.999, 1e-08, 0.01, 10
N_DEV = 8
MESH_AXES = ("x", "y", "c")

LANES = 128
SUBLANES = 8
VMEM_LIMIT_BYTES = 56 * 1024 * 1024
ATTN_FWD_BLOCK = 1024
ATTN_BWD_BLOCK = 512
ROW_TILE = 1024
WIDE_ROW_TILE = 512
NEG_BIG = -1e30


def _cparams(*sem):
    return pltpu.CompilerParams(dimension_semantics=sem if sem else None, vmem_limit_bytes=VMEM_LIMIT_BYTES)


def _pick(n, pref):
    if n <= pref:
        return n
    best = None
    for t in range(LANES, pref + 1, LANES):
        if n % t == 0:
            best = t
    assert best is not None, (n, pref)
    return best


def _sigmoid(x):
    return 1.0 / (1.0 + jnp.exp(-x))


def _log1p(x):
    u = 1.0 + x
    d = u - 1.0
    return jnp.where(d == 0.0, x, jnp.log(u) * (x / jnp.where(d == 0.0, 1.0, d)))


def _expm1(x):
    u = jnp.exp(x)
    lu = jnp.log(u)
    return jnp.where(u == 1.0, x, (u - 1.0) * (x / jnp.where(u == 1.0, 1.0, lu)))


def _softplus(x):
    return jnp.maximum(x, 0.0) + _log1p(jnp.exp(-jnp.abs(x)))


_GELU_C = math.sqrt(2.0 / math.pi)


def _gelu(x):
    return 0.5 * x * (1.0 + jnp.tanh(_GELU_C * (x + 0.044715 * x * x * x)))


def _gelu_and_grad(x):
    t = jnp.tanh(_GELU_C * (x + 0.044715 * x * x * x))
    du = _GELU_C * (1.0 + 3.0 * 0.044715 * x * x)
    return 0.5 * x * (1.0 + t), 0.5 * (1.0 + t) + 0.5 * x * (1.0 - t * t) * du


_DOT_DIMS = {"nn": (((1,), (0,)), ((), ())), "nt": (((1,), (1,)), ((), ())), "tn": (((0,), (0,)), ((), ()))}


def _mm(a, b, mode, out_dtype, name, add=None, add_scale=1.0, tm=512, tn=1408, tk=1408):
    if mode == "nn":
        (m, k), (k2, n) = a.shape, b.shape
    elif mode == "nt":
        (m, k), (n, k2) = a.shape, b.shape
    else:
        (k, m), (k2, n) = a.shape, b.shape
    assert k == k2 and a.dtype == BF16 and b.dtype == BF16, (a.shape, b.shape, a.dtype, b.dtype)
    tm, tn, tk = _pick(m, tm), _pick(n, tn), _pick(k, tk)
    nk = k // tk
    dims = _DOT_DIMS[mode]

    def body(*refs):
        if add is None:
            a_ref, b_ref, o_ref, acc_ref = refs
        else:
            a_ref, b_ref, add_ref, o_ref, acc_ref = refs
        kk = pl.program_id(2)

        @pl.when(kk == 0)
        def _():
            acc_ref[...] = jnp.zeros_like(acc_ref)

        acc_ref[...] += lax.dot_general(a_ref[...], b_ref[...], dims, preferred_element_type=F32)

        @pl.when(kk == nk - 1)
        def _():
            r = acc_ref[...]
            if add is not None:
                r = r + add_scale * add_ref[...]
            o_ref[...] = r.astype(out_dtype)

    a_spec = (pl.BlockSpec((tk, tm), lambda j, i, kk: (kk, i)) if mode == "tn"
              else pl.BlockSpec((tm, tk), lambda j, i, kk: (i, kk)))
    b_spec = (pl.BlockSpec((tn, tk), lambda j, i, kk: (j, kk)) if mode == "nt"
              else pl.BlockSpec((tk, tn), lambda j, i, kk: (kk, j)))
    o_spec = pl.BlockSpec((tm, tn), lambda j, i, kk: (i, j))
    in_specs, args = [a_spec, b_spec], [a, b]
    if add is not None:
        assert add.shape == (m, n) and add.dtype == F32
        in_specs.append(o_spec)
        args.append(add)
    return pl.pallas_call(
        body, name=name, grid=(n // tn, m // tm, nk),
        in_specs=in_specs, out_specs=o_spec,
        out_shape=jax.ShapeDtypeStruct((m, n), out_dtype),
        scratch_shapes=[pltpu.VMEM((tm, tn), F32)],
        compiler_params=_cparams("parallel", "parallel", "arbitrary"),
    )(*args)


def _ln_fwd(x, m, g, b, name):
    t, d = x.shape
    bt = _pick(t, ROW_TILE)

    def body(x_ref, m_ref, g_ref, b_ref, y_ref, yb_ref, z_ref):
        z = ALPHA * x_ref[...] + m_ref[...]
        mu = jnp.mean(z, axis=-1, keepdims=True)
        zc = z - mu
        var = jnp.mean(zc * zc, axis=-1, keepdims=True)
        y = zc * lax.rsqrt(var + LN_EPS) * g_ref[...] + b_ref[...]
        y_ref[...] = y
        yb_ref[...] = y.astype(BF16)
        z_ref[...] = z

    row = pl.BlockSpec((bt, d), lambda i: (i, 0))
    vec = pl.BlockSpec((1, d), lambda i: (0, 0))
    return pl.pallas_call(
        body, name=name, grid=(t // bt,), in_specs=[row, row, vec, vec], out_specs=[row, row, row],
        out_shape=[jax.ShapeDtypeStruct((t, d), F32), jax.ShapeDtypeStruct((t, d), BF16),
                   jax.ShapeDtypeStruct((t, d), F32)],
        compiler_params=_cparams("parallel"),
    )(x, m, g.reshape(1, d), b.reshape(1, d))


def _ln_bwd(dy, z, g, name):
    t, d = dy.shape
    bt = _pick(t, ROW_TILE)

    def body(dy_ref, z_ref, g_ref, dz_ref, dzb_ref, dg_ref, db_ref):
        @pl.when(pl.program_id(0) == 0)
        def _():
            dg_ref[...] = jnp.zeros_like(dg_ref)
            db_ref[...] = jnp.zeros_like(db_ref)

        z = z_ref[...]
        dyv = dy_ref[...]
        mu = jnp.mean(z, axis=-1, keepdims=True)
        zc = z - mu
        var = jnp.mean(zc * zc, axis=-1, keepdims=True)
        rstd = lax.rsqrt(var + LN_EPS)
        xhat = zc * rstd
        dxh = dyv * g_ref[...]
        m1 = jnp.mean(dxh, axis=-1, keepdims=True)
        m2 = jnp.mean(dxh * xhat, axis=-1, keepdims=True)
        dz = rstd * (dxh - m1 - xhat * m2)
        dz_ref[...] = dz
        dzb_ref[...] = dz.astype(BF16)
        dg_ref[...] += jnp.sum(dyv * xhat, axis=0, keepdims=True)
        db_ref[...] += jnp.sum(dyv, axis=0, keepdims=True)

    row = pl.BlockSpec((bt, d), lambda i: (i, 0))
    vec = pl.BlockSpec((1, d), lambda i: (0, 0))
    return pl.pallas_call(
        body, name=name, grid=(t // bt,), in_specs=[row, row, vec], out_specs=[row, row, vec, vec],
        out_shape=[jax.ShapeDtypeStruct((t, d), F32), jax.ShapeDtypeStruct((t, d), BF16),
                   jax.ShapeDtypeStruct((1, d), F32), jax.ShapeDtypeStruct((1, d), F32)],
        compiler_params=_cparams("arbitrary"),
    )(dy, z, g.reshape(1, d))


def _rows_iota(shape):
    return lax.broadcasted_iota(jnp.int32, shape, 0)


def _shift_down(x, halo, s):
    if s == 0:
        return x
    rolled = pltpu.roll(x, s, axis=0)
    first = jnp.where(_rows_iota((SUBLANES, x.shape[1])) < s, pltpu.roll(halo, s, axis=0), rolled[:SUBLANES])
    return jnp.concatenate([first, rolled[SUBLANES:]], axis=0)


def _shift_up(x, halo, s):
    if s == 0:
        return x
    n = x.shape[0]
    rolled = pltpu.roll(x, n - s, axis=0)
    last = jnp.where(_rows_iota((SUBLANES, x.shape[1])) < SUBLANES - s, rolled[n - SUBLANES:],
                     pltpu.roll(halo, SUBLANES - s, axis=0))
    return jnp.concatenate([rolled[:n - SUBLANES], last], axis=0)


def _prev_halo_spec(bt, cb, col_off=0):
    r = bt // SUBLANES
    return pl.BlockSpec((SUBLANES, cb), lambda i, j: (jnp.maximum(i * r - 1, 0), j + col_off))


def _next_halo_spec(bt, cb, t, col_off=0):
    r = bt // SUBLANES
    last = t // SUBLANES - 1
    return pl.BlockSpec((SUBLANES, cb), lambda i, j: (jnp.minimum((i + 1) * r, last), j + col_off))


def _causal_conv(x, halo, w_ref, ksize):
    acc = None
    for j in range(ksize):
        term = w_ref[j:j + 1, :] * _shift_down(x, halo, ksize - 1 - j)
        acc = term if acc is None else acc + term
    return acc


def _anticausal_conv(y, halo, w_ref, ksize):
    acc = None
    for j in range(ksize):
        term = w_ref[j:j + 1, :] * _shift_up(y, halo, ksize - 1 - j)
        acc = term if acc is None else acc + term
    return acc


def _ffn_mid_fwd(h, cw, cb_, name):
    t = h.shape[0]
    bt, cb = _pick(t, WIDE_ROW_TILE), _pick(D_FF, 1408)
    nc = D_FF // cb

    def body(hv_ref, hvh_ref, hg_ref, hgh_ref, wv_ref, wg_ref, bv_ref, bg_ref, y_ref):
        keep = (pl.program_id(0) > 0).astype(F32)
        val = _causal_conv(hv_ref[...], hvh_ref[...] * keep, wv_ref, CONV_F) + bv_ref[...]
        gate = _causal_conv(hg_ref[...], hgh_ref[...] * keep, wg_ref, CONV_F) + bg_ref[...]
        y_ref[...] = (_gelu(gate) * val).astype(BF16)

    tile_v = pl.BlockSpec((bt, cb), lambda i, j: (i, j))
    tile_g = pl.BlockSpec((bt, cb), lambda i, j: (i, j + nc))
    wv = pl.BlockSpec((CONV_F, cb), lambda i, j: (0, j))
    wg = pl.BlockSpec((CONV_F, cb), lambda i, j: (0, j + nc))
    bv = pl.BlockSpec((1, cb), lambda i, j: (0, j))
    bg = pl.BlockSpec((1, cb), lambda i, j: (0, j + nc))
    cb2 = cb_.reshape(1, 2 * D_FF)
    return pl.pallas_call(
        body, name=name, grid=(t // bt, nc),
        in_specs=[tile_v, _prev_halo_spec(bt, cb), tile_g, _prev_halo_spec(bt, cb, nc), wv, wg, bv, bg],
        out_specs=tile_v, out_shape=jax.ShapeDtypeStruct((t, D_FF), BF16),
        compiler_params=_cparams("parallel", "parallel"),
    )(h, h, h, h, cw, cw, cb2, cb2)


def _ffn_mid_bwd_a(h, dy, cw, cb_, name):
    t = h.shape[0]
    bt, cb = _pick(t, WIDE_ROW_TILE), _pick(D_FF, 1408)
    nc = D_FF // cb

    def body(hv_ref, hvh_ref, hg_ref, hgh_ref, dy_ref, wv_ref, wg_ref, bv_ref, bg_ref,
             dv_ref, dg_ref, dwv_ref, dwg_ref, dbv_ref, dbg_ref):
        @pl.when(pl.program_id(1) == 0)
        def _():
            for r in (dwv_ref, dwg_ref, dbv_ref, dbg_ref):
                r[...] = jnp.zeros_like(r)

        keep = (pl.program_id(1) > 0).astype(F32)
        hv, hvh = hv_ref[...], hvh_ref[...] * keep
        hg, hgh = hg_ref[...], hgh_ref[...] * keep
        val = _causal_conv(hv, hvh, wv_ref, CONV_F) + bv_ref[...]
        gate = _causal_conv(hg, hgh, wg_ref, CONV_F) + bg_ref[...]
        gl, glg = _gelu_and_grad(gate)
        dyv = dy_ref[...]
        dval = dyv * gl
        dgate = dyv * val * glg
        dv_ref[...] = dval.astype(BF16)
        dg_ref[...] = dgate.astype(BF16)
        dbv_ref[...] += jnp.sum(dval, axis=0, keepdims=True)
        dbg_ref[...] += jnp.sum(dgate, axis=0, keepdims=True)
        for j in range(CONV_F):
            s = CONV_F - 1 - j
            dwv_ref[j:j + 1, :] += jnp.sum(dval * _shift_down(hv, hvh, s), axis=0, keepdims=True)
            dwg_ref[j:j + 1, :] += jnp.sum(dgate * _shift_down(hg, hgh, s), axis=0, keepdims=True)

    tile_v = pl.BlockSpec((bt, cb), lambda j, i: (i, j))
    tile_g = pl.BlockSpec((bt, cb), lambda j, i: (i, j + nc))
    r = bt // SUBLANES
    halo_v = pl.BlockSpec((SUBLANES, cb), lambda j, i: (jnp.maximum(i * r - 1, 0), j))
    halo_g = pl.BlockSpec((SUBLANES, cb), lambda j, i: (jnp.maximum(i * r - 1, 0), j + nc))
    wv = pl.BlockSpec((CONV_F, cb), lambda j, i: (0, j))
    wg = pl.BlockSpec((CONV_F, cb), lambda j, i: (0, j + nc))
    bv = pl.BlockSpec((1, cb), lambda j, i: (0, j))
    bg = pl.BlockSpec((1, cb), lambda j, i: (0, j + nc))
    cb2 = cb_.reshape(1, 2 * D_FF)
    dv, dg, dwv, dwg, dbv, dbg = pl.pallas_call(
        body, name=name, grid=(nc, t // bt),
        in_specs=[tile_v, halo_v, tile_g, halo_g, tile_v, wv, wg, bv, bg],
        out_specs=[tile_v, tile_v, wv, wv, bv, bv],
        out_shape=[jax.ShapeDtypeStruct((t, D_FF), BF16), jax.ShapeDtypeStruct((t, D_FF), BF16),
                   jax.ShapeDtypeStruct((CONV_F, D_FF), F32), jax.ShapeDtypeStruct((CONV_F, D_FF), F32),
                   jax.ShapeDtypeStruct((1, D_FF), F32), jax.ShapeDtypeStruct((1, D_FF), F32)],
        compiler_params=_cparams("parallel", "arbitrary"),
    )(h, h, h, h, dy, cw, cw, cb2, cb2)
    return dv, dg, jnp.concatenate([dwv, dwg], axis=1), jnp.concatenate([dbv, dbg], axis=1)


def _conv_bwd_x(dy, cw, name, out_dtype):
    t, c = dy.shape
    ksize = cw.shape[0]
    bt, cb = _pick(t, WIDE_ROW_TILE), _pick(c, 1408)
    halo_rows = SUBLANES if dy.dtype == F32 else 2 * SUBLANES
    r, last = bt // halo_rows, t // halo_rows - 1

    def body(dy_ref, halo_ref, w_ref, o_ref):
        keep = (pl.program_id(0) < t // bt - 1).astype(F32)
        halo = halo_ref[...].astype(F32)[:SUBLANES] * keep
        o_ref[...] = _anticausal_conv(dy_ref[...].astype(F32), halo, w_ref, ksize).astype(out_dtype)

    tile = pl.BlockSpec((bt, cb), lambda i, j: (i, j))
    halo_spec = pl.BlockSpec((halo_rows, cb), lambda i, j: (jnp.minimum((i + 1) * r, last), j))
    return pl.pallas_call(
        body, name=name, grid=(t // bt, c // cb),
        in_specs=[tile, halo_spec, pl.BlockSpec((ksize, cb), lambda i, j: (0, j))],
        out_specs=tile, out_shape=jax.ShapeDtypeStruct((t, c), out_dtype),
        compiler_params=_cparams("parallel", "parallel"),
    )(dy, dy, cw)


def _ple_fwd(x2, g, e, bg, name):
    t, d = x2.shape
    bt = _pick(t, ROW_TILE)

    def body(x_ref, g_ref, e_ref, b_ref, y_ref, yb_ref):
        y = x_ref[...] + _sigmoid(g_ref[...] + b_ref[...]) * e_ref[...]
        y_ref[...] = y
        yb_ref[...] = y.astype(BF16)

    row = pl.BlockSpec((bt, d), lambda i: (i, 0))
    vec = pl.BlockSpec((1, d), lambda i: (0, 0))
    return pl.pallas_call(
        body, name=name, grid=(t // bt,), in_specs=[row, row, row, vec], out_specs=[row, row],
        out_shape=[jax.ShapeDtypeStruct((t, d), F32), jax.ShapeDtypeStruct((t, d), BF16)],
        compiler_params=_cparams("parallel"),
    )(x2, g, e, bg.reshape(1, d))


def _ple_bwd(dx3, g, e, bg, name):
    t, d = dx3.shape
    bt = _pick(t, ROW_TILE)

    def body(dx_ref, g_ref, e_ref, b_ref, dg_ref, de_ref, db_ref):
        @pl.when(pl.program_id(0) == 0)
        def _():
            db_ref[...] = jnp.zeros_like(db_ref)

        dx = dx_ref[...]
        gate = _sigmoid(g_ref[...] + b_ref[...])
        dg = dx * e_ref[...] * gate * (1.0 - gate)
        dg_ref[...] = dg.astype(BF16)
        de_ref[...] = (dx * gate).astype(BF16)
        db_ref[...] += jnp.sum(dg, axis=0, keepdims=True)

    row = pl.BlockSpec((bt, d), lambda i: (i, 0))
    vec = pl.BlockSpec((1, d), lambda i: (0, 0))
    return pl.pallas_call(
        body, name=name, grid=(t // bt,), in_specs=[row, row, row, vec], out_specs=[row, row, vec],
        out_shape=[jax.ShapeDtypeStruct((t, d), BF16), jax.ShapeDtypeStruct((t, d), BF16),
                   jax.ShapeDtypeStruct((1, d), F32)],
        compiler_params=_cparams("arbitrary"),
    )(dx3, g, e, bg.reshape(1, d))


def _loss_head(y, target, name):
    t, d = y.shape
    bt = _pick(t, ROW_TILE)

    def body(y_ref, t_ref, dy_ref, l_ref, acc_ref):
        @pl.when(pl.program_id(0) == 0)
        def _():
            acc_ref[...] = jnp.zeros_like(acc_ref)

        err = y_ref[...] - t_ref[...]
        dy_ref[...] = err * (1.0 / d)
        acc_ref[...] += jnp.sum(err * err, axis=0, keepdims=True)

        @pl.when(pl.program_id(0) == t // bt - 1)
        def _():
            l_ref[...] = jnp.full(l_ref.shape, (0.5 / d) * jnp.sum(acc_ref[...]), F32)

    row = pl.BlockSpec((bt, d), lambda i: (i, 0))
    dy, l = pl.pallas_call(
        body, name=name, grid=(t // bt,), in_specs=[row, row],
        out_specs=[row, pl.BlockSpec((SUBLANES, LANES), lambda i: (0, 0))],
        out_shape=[jax.ShapeDtypeStruct((t, d), F32), jax.ShapeDtypeStruct((SUBLANES, LANES), F32)],
        scratch_shapes=[pltpu.VMEM((1, d), F32)],
        compiler_params=_cparams("arbitrary"),
    )(y, target)
    return dy, l[0, 0]


def _split3(x):
    hi = x.astype(BF16)
    r1 = x - hi.astype(F32)
    mid = r1.astype(BF16)
    lo = (r1 - mid.astype(F32)).astype(BF16)
    return hi, mid, lo


def _tri_dot(x, tri):
    hi, mid, lo = _split3(x)
    dims = (((1,), (0,)), ((), ()))
    return (lax.dot_general(hi, tri, dims, preferred_element_type=F32)
            + lax.dot_general(mid, tri, dims, preferred_element_type=F32)
            + lax.dot_general(lo, tri, dims, preferred_element_type=F32))


def _fgate_fwd(fg_rows, b_f, name):
    hh, t = fg_rows.shape
    bt = _pick(t, 512)

    def body(fg_ref, b_ref, c_ref, carry_ref):
        @pl.when(pl.program_id(0) == 0)
        def _():
            carry_ref[...] = jnp.zeros_like(carry_ref)

        xx = fg_ref[...] + b_ref[...]
        logf = jnp.minimum(xx, 0.0) - _log1p(jnp.exp(-jnp.abs(xx)))
        r = lax.broadcasted_iota(jnp.int32, (bt, bt), 0)
        c = lax.broadcasted_iota(jnp.int32, (bt, bt), 1)
        tri = (r <= c).astype(BF16)
        cs = _tri_dot(logf, tri) + carry_ref[...]
        c_ref[...] = cs
        carry_ref[...] = cs[:, bt - 1:bt]

    return pl.pallas_call(
        body, name=name, grid=(t // bt,),
        in_specs=[pl.BlockSpec((hh, bt), lambda i: (0, i)), pl.BlockSpec((hh, 1), lambda i: (0, 0))],
        out_specs=pl.BlockSpec((hh, bt), lambda i: (0, i)),
        out_shape=jax.ShapeDtypeStruct((hh, t), F32),
        scratch_shapes=[pltpu.VMEM((hh, 1), F32)],
        compiler_params=_cparams("arbitrary"),
    )(fg_rows, b_f.reshape(hh, 1))


def _fgate_bwd(dck_rows, dcq_rows, fg_rows, b_f, name):
    hh, t = fg_rows.shape
    bt = _pick(t, 512)
    nb = t // bt

    def body(dc_ref, dcq_ref, fg_ref, b_ref, dfg_ref, db_ref, carry_ref):
        @pl.when(pl.program_id(0) == 0)
        def _():
            carry_ref[...] = jnp.zeros_like(carry_ref)
            db_ref[...] = jnp.zeros_like(db_ref)

        r = lax.broadcasted_iota(jnp.int32, (bt, bt), 0)
        c = lax.broadcasted_iota(jnp.int32, (bt, bt), 1)
        tri = (r >= c).astype(BF16)
        dlogf = _tri_dot(dc_ref[...] + dcq_ref[...], tri) + carry_ref[...]
        carry_ref[...] = dlogf[:, 0:1]
        xx = fg_ref[...] + b_ref[...]
        dfg = dlogf * _sigmoid(-xx)
        dfg_ref[...] = dfg
        db_ref[...] += jnp.sum(dfg, axis=1, keepdims=True)

    blk = pl.BlockSpec((hh, bt), lambda i: (0, nb - 1 - i))
    vec = pl.BlockSpec((hh, 1), lambda i: (0, 0))
    return pl.pallas_call(
        body, name=name, grid=(nb,), in_specs=[blk, blk, blk, vec], out_specs=[blk, vec],
        out_shape=[jax.ShapeDtypeStruct((hh, t), F32), jax.ShapeDtypeStruct((hh, 1), F32)],
        scratch_shapes=[pltpu.VMEM((hh, 1), F32)],
        compiler_params=_cparams("arbitrary"),
    )(dck_rows, dcq_rows, fg_rows, b_f.reshape(hh, 1))


def _rows_to_cols(r):
    hh, t = r.shape
    return jnp.repeat(r.reshape(hh // 2, 2, t).transpose(0, 2, 1), HEAD_DIM, axis=-1)


def _rows_to_pairs(r):
    hh, t = r.shape
    return jnp.pad(r.reshape(hh // 2, 2, t), ((0, 0), (0, SUBLANES - 2), (0, 0)))


AUG_C = HEAD_DIM
AUG_ONE = HEAD_DIM + 3


def _attn_prep(qkv, c_cols, name):
    t = qkv.shape[0]
    bt = _pick(t, ATTN_FWD_BLOCK)
    scale = 1.0 / math.sqrt(HEAD_DIM)

    def body(q_ref, k_ref, v_ref, c_ref, qh_ref, kh_ref, qt_ref, kt_ref, vt_ref):
        vt_ref[...] = v_ref[...].astype(F32).T.astype(BF16)
        lane = lax.broadcasted_iota(jnp.int32, (bt, LANES), 1)
        q2 = q_ref[...].astype(F32)
        k2 = k_ref[...].astype(F32) * scale
        c2 = c_ref[...]
        parts = [p.astype(F32) for p in _split3(c2 - c2[0:1, :])]
        swapped = [pltpu.roll(p, HEAD_DIM, axis=1) for p in parts]
        for a in (0, 1):
            qa = q2 if a == 0 else pltpu.roll(q2, HEAD_DIM, axis=1)
            ka = k2 if a == 0 else pltpu.roll(k2, HEAD_DIM, axis=1)
            hi, mid, lo = swapped if a == 0 else parts
            kaug = jnp.where(lane < HEAD_DIM, ka,
                             jnp.where(lane == AUG_C, hi,
                                       jnp.where(lane == AUG_C + 1, mid,
                                                 jnp.where(lane == AUG_C + 2, lo,
                                                           jnp.where(lane == AUG_ONE, 1.0, 0.0)))))
            qaug = jnp.where(lane < HEAD_DIM, qa, jnp.where(lane < AUG_ONE, -1.0, 0.0))
            qh_ref[:, a * LANES:(a + 1) * LANES] = qaug.astype(BF16)
            kh_ref[:, a * LANES:(a + 1) * LANES] = kaug.astype(BF16)
            qt_ref[a * LANES:(a + 1) * LANES, :] = qaug.T.astype(BF16)
            kt_ref[a * LANES:(a + 1) * LANES, :] = kaug.T.astype(BF16)

    out = pl.BlockSpec((bt, 2 * LANES), lambda i, hp: (i, hp))
    out_t = pl.BlockSpec((2 * LANES, bt), lambda i, hp: (hp, i))
    shape = jax.ShapeDtypeStruct((t, N_HEADS * LANES), BF16)
    shape_t = jax.ShapeDtypeStruct((N_HEADS * LANES, t), BF16)
    return pl.pallas_call(
        body, name=name, grid=(t // bt, N_PAIRS),
        in_specs=[pl.BlockSpec((bt, LANES), lambda i, hp: (i, hp)),
                  pl.BlockSpec((bt, LANES), lambda i, hp: (i, N_PAIRS + hp)),
                  pl.BlockSpec((bt, LANES), lambda i, hp: (i, 2 * N_PAIRS + hp)),
                  pl.BlockSpec((None, bt, LANES), lambda i, hp: (hp, i, 0))],
        out_specs=[out, out, out_t, out_t, pl.BlockSpec((LANES, bt), lambda i, hp: (hp, i))],
        out_shape=[shape, shape, shape_t, shape_t, jax.ShapeDtypeStruct((D_MODEL, t), BF16)],
        compiler_params=_cparams("parallel", "parallel"),
    )(qkv, qkv, qkv, c_cols)


def _block_scalar(c_ref, a, start):
    return c_ref[a:a + 1, pl.ds(start, LANES)][:, 0:1]


def _attn_fwd_t(qh, kh, vt, c_pairs, name):
    t = qh.shape[0]
    bq = _pick(t, ATTN_FWD_BLOCK)
    nq = t // bq

    def body(q_ref, k_ref, vt_ref, c_ref, ot_ref, otb_ref, lse_ref, acc_ref):
        i = pl.program_id(1)
        q0 = pl.multiple_of(i * bq, bq)
        qs = (q_ref[:, 0:LANES], q_ref[:, LANES:2 * LANES])
        cq = [_block_scalar(c_ref, a, q0) for a in (0, 1)]
        acc_ref[...] = jnp.zeros_like(acc_ref)
        keep = _rows_iota((bq, bq)) <= lax.broadcasted_iota(jnp.int32, (bq, bq), 1)

        def step(j, carry, masked):
            k0 = pl.multiple_of(j * bq, bq)
            kb = k_ref[pl.ds(k0, bq), :]
            new = []
            for a in (0, 1):
                m_old, l_old = carry[2 * a], carry[2 * a + 1]
                st = lax.dot_general(kb[:, a * LANES:(a + 1) * LANES], qs[a], _DOT_DIMS["nt"],
                                     preferred_element_type=F32)
                if masked:
                    st = jnp.where(keep, st, NEG_BIG)
                sigma = cq[a] - _block_scalar(c_ref, a, k0)
                m_new = jnp.maximum(m_old, jnp.max(st, axis=0, keepdims=True) + sigma)
                pt = jnp.exp(st - (m_new - sigma))
                alpha = jnp.exp(m_old - m_new)
                l_new = alpha * l_old + jnp.sum(pt, axis=0, keepdims=True)
                vta = vt_ref[a * HEAD_DIM:(a + 1) * HEAD_DIM, pl.ds(k0, bq)]
                acc_ref[a] = alpha * acc_ref[a] + lax.dot_general(
                    vta, pt.astype(BF16), _DOT_DIMS["nn"], preferred_element_type=F32)
                new += [m_new, l_new]
            return tuple(new)

        neg = jnp.full((1, bq), NEG_BIG, F32)
        zero = jnp.zeros((1, bq), F32)
        carry = lax.fori_loop(0, i, lambda j, c: step(j, c, False), (neg, zero, neg, zero))
        m_a, l_a, m_b, l_b = step(i, carry, True)
        ot = jnp.concatenate([acc_ref[0] / l_a, acc_ref[1] / l_b], axis=0)
        ot_ref[...] = ot
        otb_ref[...] = ot.astype(BF16)
        lse_ref[...] = jnp.zeros_like(lse_ref)
        lse_ref[0:1, :] = m_a + jnp.log(l_a)
        lse_ref[1:2, :] = m_b + jnp.log(l_b)

    rows = pl.BlockSpec((None, SUBLANES, bq), lambda hp, i: (hp, 0, i))
    otile = pl.BlockSpec((LANES, bq), lambda hp, i: (hp, i))
    return pl.pallas_call(
        body, name=name, grid=(N_PAIRS, nq),
        in_specs=[pl.BlockSpec((bq, 2 * LANES), lambda hp, i: (i, hp)),
                  pl.BlockSpec((t, 2 * LANES), lambda hp, i: (0, hp)),
                  pl.BlockSpec((LANES, t), lambda hp, i: (hp, 0)),
                  pl.BlockSpec((None, SUBLANES, t), lambda hp, i: (hp, 0, 0))],
        out_specs=[otile, otile, rows],
        out_shape=[jax.ShapeDtypeStruct((D_MODEL, t), F32), jax.ShapeDtypeStruct((D_MODEL, t), BF16),
                   jax.ShapeDtypeStruct((N_PAIRS, SUBLANES, t), F32)],
        scratch_shapes=[pltpu.VMEM((2, HEAD_DIM, bq), F32)],
        compiler_params=_cparams("parallel", "arbitrary"),
    )(qh, kh, vt, c_pairs)


def _attn_delta_t(dot, ot, name):
    t = dot.shape[1]
    bt = _pick(t, 512)

    def body(do_ref, o_ref, d_ref, dob_ref):
        dob = do_ref[...].astype(BF16)
        prod = dob.astype(F32) * o_ref[...]
        d_ref[...] = jnp.zeros_like(d_ref)
        d_ref[0:1, :] = jnp.sum(prod[0:HEAD_DIM], axis=0, keepdims=True)
        d_ref[1:2, :] = jnp.sum(prod[HEAD_DIM:], axis=0, keepdims=True)
        dob_ref[...] = dob

    tile = pl.BlockSpec((LANES, bt), lambda hp, i: (hp, i))
    return pl.pallas_call(
        body, name=name, grid=(N_PAIRS, t // bt), in_specs=[tile, tile],
        out_specs=[pl.BlockSpec((None, SUBLANES, bt), lambda hp, i: (hp, 0, i)), tile],
        out_shape=[jax.ShapeDtypeStruct((N_PAIRS, SUBLANES, t), F32), jax.ShapeDtypeStruct((D_MODEL, t), BF16)],
        compiler_params=_cparams("parallel", "parallel"),
    )(dot, ot)


def _attn_bwd_t(kh, kt, qt, qkv, dotb, c_pairs, lse_pairs, d_pairs, name):
    t = kh.shape[0]
    bk = _pick(t, ATTN_BWD_BLOCK)
    nk = t // bk
    ref_tile = _pick(t, ATTN_FWD_BLOCK)
    assert ref_tile % bk == 0
    scale = 1.0 / math.sqrt(HEAD_DIM)

    def ref_start(start):
        return pl.multiple_of((start // ref_tile) * ref_tile, ref_tile)

    def body(k_ref, kt_ref, v_ref, qt_ref, dot_ref, c_ref, lse_ref, d_ref,
             dk_ref, dv_ref, dc_ref, dq_hbm, dka_ref, dva_ref, dqa_ref, sem):
        hp, j = pl.program_id(0), pl.program_id(1)
        k0 = pl.multiple_of(j * bk, bk)
        ks = (k_ref[:, 0:LANES], k_ref[:, LANES:2 * LANES])
        vb = v_ref[...]
        ck = [_block_scalar(c_ref, a, ref_start(k0)) for a in (0, 1)]
        dka_ref[...] = jnp.zeros_like(dka_ref)
        dva_ref[...] = jnp.zeros_like(dva_ref)

        @pl.when(j == 0)
        def _():
            dqa_ref[...] = jnp.zeros_like(dqa_ref)

        keep = _rows_iota((bk, bk)) <= lax.broadcasted_iota(jnp.int32, (bk, bk), 1)
        top = _rows_iota((LANES, bk)) < HEAD_DIM

        def step(i, masked):
            q0 = pl.multiple_of(i * bk, bk)
            dot2 = dot_ref[:, pl.ds(q0, bk)]
            zero = jnp.zeros_like(dot2)
            for a in (0, 1):
                qta = qt_ref[a * LANES:(a + 1) * LANES, pl.ds(q0, bk)]
                st = lax.dot_general(ks[a], qta, _DOT_DIMS["nn"], preferred_element_type=F32)
                sigma = _block_scalar(c_ref, a, ref_start(q0)) - ck[a]
                pt = jnp.exp(st - (lse_ref[a:a + 1, pl.ds(q0, bk)] - sigma))
                if masked:
                    pt = jnp.where(keep, pt, 0.0)
                dota = jnp.where(top, dot2, zero) if a == 0 else jnp.where(top, zero, dot2)
                dpt = lax.dot_general(vb, dota, _DOT_DIMS["nn"], preferred_element_type=F32)
                dstb = (pt * (dpt - d_ref[a:a + 1, pl.ds(q0, bk)])).astype(BF16)
                dva_ref[a] += lax.dot_general(dot2[a * HEAD_DIM:(a + 1) * HEAD_DIM, :], pt.astype(BF16),
                                              _DOT_DIMS["nt"], preferred_element_type=F32)
                dka_ref[a] += lax.dot_general(qta, dstb, _DOT_DIMS["nt"], preferred_element_type=F32)
                dqa_ref[a, :, pl.ds(q0, bk)] += lax.dot_general(
                    kt_ref[a * LANES:(a + 1) * LANES, :], dstb, _DOT_DIMS["nn"], preferred_element_type=F32)

        def loop_body(i, carry):
            step(i, False)
            return carry

        step(j, True)
        lax.fori_loop(j + 1, nk, loop_body, 0)
        dk_ref[...] = (jnp.concatenate([dka_ref[0, 0:HEAD_DIM, :], dka_ref[1, 0:HEAD_DIM, :]], axis=0)
                       * scale).astype(BF16)
        dv_ref[...] = jnp.concatenate([dva_ref[0], dva_ref[1]], axis=0).astype(BF16)
        dc_ref[...] = jnp.zeros_like(dc_ref)
        dc_ref[0:1, :] = dka_ref[0, AUG_C:AUG_C + 1, :]
        dc_ref[1:2, :] = dka_ref[1, AUG_C:AUG_C + 1, :]

        @pl.when(j == nk - 1)
        def _():
            for a in (0, 1):
                row0 = pl.multiple_of((2 * hp + a) * LANES, LANES)
                cp = pltpu.make_async_copy(dqa_ref.at[a], dq_hbm.at[pl.ds(row0, LANES), :], sem)
                cp.start()
                cp.wait()

    once = pl.Buffered(1)
    rows = pl.BlockSpec((None, SUBLANES, t), lambda hp, j: (hp, 0, 0))
    kv_out = pl.BlockSpec((LANES, bk), lambda hp, j: (hp, j))
    return pl.pallas_call(
        body, name=name, grid=(N_PAIRS, nk),
        in_specs=[pl.BlockSpec((bk, 2 * LANES), lambda hp, j: (j, hp)),
                  pl.BlockSpec((2 * LANES, bk), lambda hp, j: (hp, j)),
                  pl.BlockSpec((bk, LANES), lambda hp, j: (j, 2 * N_PAIRS + hp)),
                  pl.BlockSpec((2 * LANES, t), lambda hp, j: (hp, 0), pipeline_mode=once),
                  pl.BlockSpec((LANES, t), lambda hp, j: (hp, 0), pipeline_mode=once),
                  rows, rows, rows],
        out_specs=[kv_out, kv_out, pl.BlockSpec((None, SUBLANES, bk), lambda hp, j: (hp, 0, j)),
                   pl.BlockSpec(memory_space=pltpu.HBM)],
        out_shape=[jax.ShapeDtypeStruct((D_MODEL, t), BF16), jax.ShapeDtypeStruct((D_MODEL, t), BF16),
                   jax.ShapeDtypeStruct((N_PAIRS, SUBLANES, t), F32),
                   jax.ShapeDtypeStruct((N_HEADS * LANES, t), F32)],
        scratch_shapes=[pltpu.VMEM((2, LANES, bk), F32), pltpu.VMEM((2, HEAD_DIM, bk), F32),
                        pltpu.VMEM((2, LANES, t), F32), pltpu.SemaphoreType.DMA],
        compiler_params=_cparams("arbitrary", "arbitrary"),
    )(kh, kt, qkv, qt, dotb, c_pairs, lse_pairs, d_pairs)


def _scan(a, u, name, reverse=False):
    t, c = a.shape
    bt, cb = _pick(t, ROW_TILE), _pick(c, 1024)
    nt = t // bt
    ngroups = bt // SUBLANES

    def body(a_ref, u_ref, h_ref, carry_ref, as_ref, us_ref):
        @pl.when(pl.program_id(1) == 0)
        def _():
            carry_ref[...] = jnp.zeros_like(carry_ref)

        av, uv = a_ref[...], u_ref[...]
        sub = _rows_iota((bt, cb)) % SUBLANES
        for s in (1, 2, 4):
            if reverse:
                a_sh, u_sh = pltpu.roll(av, bt - s, axis=0), pltpu.roll(uv, bt - s, axis=0)
                valid = sub < SUBLANES - s
            else:
                a_sh, u_sh = pltpu.roll(av, s, axis=0), pltpu.roll(uv, s, axis=0)
                valid = sub >= s
            uv = jnp.where(valid, uv + av * u_sh, uv)
            av = jnp.where(valid, av * a_sh, av)
        as_ref[...] = av
        us_ref[...] = uv
        edge = 0 if reverse else SUBLANES - 1
        pick = _rows_iota((SUBLANES, cb)) == edge

        def group(gi, carry):
            g = (ngroups - 1 - gi) if reverse else gi
            r0 = pl.multiple_of(g * SUBLANES, SUBLANES)
            h8 = us_ref[pl.ds(r0, SUBLANES), :] + as_ref[pl.ds(r0, SUBLANES), :] * carry
            h_ref[pl.ds(r0, SUBLANES), :] = h8
            return jnp.sum(jnp.where(pick, h8, 0.0), axis=0, keepdims=True)

        carry_ref[...] = lax.fori_loop(0, ngroups, group, carry_ref[...])

    if reverse:
        tile = pl.BlockSpec((bt, cb), lambda j, i: (nt - 1 - i, j))
    else:
        tile = pl.BlockSpec((bt, cb), lambda j, i: (i, j))
    return pl.pallas_call(
        body, name=name, grid=(c // cb, nt), in_specs=[tile, tile], out_specs=tile,
        out_shape=jax.ShapeDtypeStruct((t, c), F32),
        scratch_shapes=[pltpu.VMEM((1, cb), F32), pltpu.VMEM((bt, cb), F32), pltpu.VMEM((bt, cb), F32)],
        compiler_params=_cparams("parallel", "arbitrary"),
    )(a, u)


def _rg_conv_fwd(proj, cw, cb_, name):
    t = proj.shape[0]
    bt, cb = _pick(t, ROW_TILE), D_MODEL

    def body(x_ref, halo_ref, w_ref, b_ref, o_ref):
        keep = (pl.program_id(0) > 0).astype(F32)
        o_ref[...] = _causal_conv(x_ref[...], halo_ref[...] * keep, w_ref, CONV_B) + b_ref[...]

    tile = pl.BlockSpec((bt, cb), lambda i, j: (i, j))
    return pl.pallas_call(
        body, name=name, grid=(t // bt, 1),
        in_specs=[tile, _prev_halo_spec(bt, cb), pl.BlockSpec((CONV_B, cb), lambda i, j: (0, 0)),
                  pl.BlockSpec((1, cb), lambda i, j: (0, 0))],
        out_specs=tile, out_shape=jax.ShapeDtypeStruct((t, D_MODEL), F32),
        compiler_params=_cparams("parallel", "parallel"),
    )(proj, proj, cw, cb_.reshape(1, D_MODEL))


def _conv_bwd_w(x, dy, ksize, name):
    t, c = dy.shape
    bt = _pick(t, ROW_TILE)
    r = bt // SUBLANES

    def body(x_ref, halo_ref, dy_ref, dw_ref, db_ref):
        @pl.when(pl.program_id(0) == 0)
        def _():
            dw_ref[...] = jnp.zeros_like(dw_ref)
            db_ref[...] = jnp.zeros_like(db_ref)

        keep = (pl.program_id(0) > 0).astype(F32)
        xv, halo, dyv = x_ref[...], halo_ref[...] * keep, dy_ref[...]
        db_ref[...] += jnp.sum(dyv, axis=0, keepdims=True)
        for j in range(ksize):
            dw_ref[j:j + 1, :] += jnp.sum(dyv * _shift_down(xv, halo, ksize - 1 - j), axis=0, keepdims=True)

    tile = pl.BlockSpec((bt, c), lambda i: (i, 0))
    return pl.pallas_call(
        body, name=name, grid=(t // bt,),
        in_specs=[tile, pl.BlockSpec((SUBLANES, c), lambda i: (jnp.maximum(i * r - 1, 0), 0)), tile],
        out_specs=[pl.BlockSpec((ksize, c), lambda i: (0, 0)), pl.BlockSpec((1, c), lambda i: (0, 0))],
        out_shape=[jax.ShapeDtypeStruct((ksize, c), F32), jax.ShapeDtypeStruct((1, c), F32)],
        compiler_params=_cparams("arbitrary"),
    )(x, x, dy)


def _rg_gate_math(xc, wa_ref, wi_ref, ba_ref, bi_ref, lam_ref):
    xb = xc.astype(BF16)
    ra = lax.dot_general(xb, wa_ref[...], _DOT_DIMS["nn"], preferred_element_type=F32) + ba_ref[...]
    ia = lax.dot_general(xb, wi_ref[...], _DOT_DIMS["nn"], preferred_element_type=F32) + bi_ref[...]
    r, ig = _sigmoid(ra), _sigmoid(ia)
    sp = _softplus(-lam_ref[...])
    log_a = -LRU_C * r * sp
    a = jnp.exp(log_a)
    mult = jnp.sqrt(-_expm1(2.0 * log_a))
    return xb, r, ig, sp, a, mult


def _rg_gate_specs(bt, time_first):
    if time_first:
        tile = pl.BlockSpec((bt, BLOCK_B), lambda i, n: (i, n))
        w = pl.BlockSpec((None, BLOCK_B, BLOCK_B), lambda i, n: (n, 0, 0))
        v = pl.BlockSpec((None, 1, BLOCK_B), lambda i, n: (n, 0, 0))
    else:
        tile = pl.BlockSpec((bt, BLOCK_B), lambda n, i: (i, n))
        w = pl.BlockSpec((None, BLOCK_B, BLOCK_B), lambda n, i: (n, 0, 0))
        v = pl.BlockSpec((None, 1, BLOCK_B), lambda n, i: (n, 0, 0))
    return tile, w, v


def _rg_gate_fwd(xc, wa, ba, wi, bi, lam, name):
    t = xc.shape[0]
    bt = _pick(t, 512)

    def body(x_ref, wa_ref, wi_ref, ba_ref, bi_ref, lam_ref, a_ref, u_ref):
        xcv = x_ref[...]
        _, _, ig, _, a, mult = _rg_gate_math(xcv, wa_ref, wi_ref, ba_ref, bi_ref, lam_ref)
        a_ref[...] = a
        u_ref[...] = mult * (ig * xcv)

    tile, w, v = _rg_gate_specs(bt, True)
    return pl.pallas_call(
        body, name=name, grid=(t // bt, N_BLOCKS_B), in_specs=[tile, w, w, v, v, v], out_specs=[tile, tile],
        out_shape=[jax.ShapeDtypeStruct((t, D_MODEL), F32), jax.ShapeDtypeStruct((t, D_MODEL), F32)],
        compiler_params=_cparams("parallel", "parallel"),
    )(xc, wa, wi, ba, bi, lam)


def _rg_gate_bwd(xc, g, h, wa, ba, wi, bi, lam, name):
    t = xc.shape[0]
    bt = _pick(t, 512)
    rr = bt // SUBLANES

    def body(x_ref, g_ref, h_ref, hh_ref, wa_ref, wi_ref, ba_ref, bi_ref, lam_ref,
             dx_ref, dwa_ref, dwi_ref, dba_ref, dbi_ref, dlam_ref):
        @pl.when(pl.program_id(1) == 0)
        def _():
            for ref in (dwa_ref, dwi_ref, dba_ref, dbi_ref, dlam_ref):
                ref[...] = jnp.zeros_like(ref)

        keep = (pl.program_id(1) > 0).astype(F32)
        xcv, gv = x_ref[...], g_ref[...]
        xb, r, ig, sp, a, mult = _rg_gate_math(xcv, wa_ref, wi_ref, ba_ref, bi_ref, lam_ref)
        h_prev = _shift_down(h_ref[...], hh_ref[...] * keep, 1)
        da = gv * h_prev
        dmult = gv * ig * xcv
        dig = gv * mult * xcv
        dxc = gv * mult * ig
        dlog_a = da * a - dmult * (a * a / mult)
        dr = dlog_a * (-LRU_C * sp)
        dsp = jnp.sum(dlog_a * (-LRU_C * r), axis=0, keepdims=True)
        dlam_ref[...] += dsp * (-_sigmoid(-lam_ref[...]))
        dra = dr * r * (1.0 - r)
        dia = dig * ig * (1.0 - ig)
        dba_ref[...] += jnp.sum(dra, axis=0, keepdims=True)
        dbi_ref[...] += jnp.sum(dia, axis=0, keepdims=True)
        drab, diab = dra.astype(BF16), dia.astype(BF16)
        dwa_ref[...] += lax.dot_general(xb, drab, _DOT_DIMS["tn"], preferred_element_type=F32)
        dwi_ref[...] += lax.dot_general(xb, diab, _DOT_DIMS["tn"], preferred_element_type=F32)
        dxc = dxc + lax.dot_general(drab, wa_ref[...], _DOT_DIMS["nt"], preferred_element_type=F32)
        dxc = dxc + lax.dot_general(diab, wi_ref[...], _DOT_DIMS["nt"], preferred_element_type=F32)
        dx_ref[...] = dxc

    tile, w, v = _rg_gate_specs(bt, False)
    halo = pl.BlockSpec((SUBLANES, BLOCK_B), lambda n, i: (jnp.maximum(i * rr - 1, 0), n))
    wshape = jax.ShapeDtypeStruct((N_BLOCKS_B, BLOCK_B, BLOCK_B), F32)
    vshape = jax.ShapeDtypeStruct((N_BLOCKS_B, 1, BLOCK_B), F32)
    return pl.pallas_call(
        body, name=name, grid=(N_BLOCKS_B, t // bt),
        in_specs=[tile, tile, tile, halo, w, w, v, v, v],
        out_specs=[tile, w, w, v, v, v],
        out_shape=[jax.ShapeDtypeStruct((t, D_MODEL), F32), wshape, wshape, vshape, vshape, vshape],
        compiler_params=_cparams("parallel", "arbitrary"),
    )(xc, g, h, h, wa, wi, ba, bi, lam)


def _rg_out_fwd(h, proj, name):
    t = h.shape[0]
    bt = _pick(t, ROW_TILE)

    def body(h_ref, g_ref, y_ref):
        y_ref[...] = (h_ref[...] * _gelu(g_ref[...])).astype(BF16)

    tile = pl.BlockSpec((bt, D_MODEL), lambda i: (i, 0))
    return pl.pallas_call(
        body, name=name, grid=(t // bt,), in_specs=[tile, pl.BlockSpec((bt, D_MODEL), lambda i: (i, 1))],
        out_specs=tile, out_shape=jax.ShapeDtypeStruct((t, D_MODEL), BF16),
        compiler_params=_cparams("parallel"),
    )(h, proj)


def _rg_out_bwd(dy, h, proj, name):
    t = h.shape[0]
    bt = _pick(t, ROW_TILE)

    def body(dy_ref, h_ref, g_ref, dh_ref, dg_ref):
        gl, glg = _gelu_and_grad(g_ref[...])
        dyv = dy_ref[...]
        dh_ref[...] = dyv * gl
        dg_ref[...] = (dyv * h_ref[...] * glg).astype(BF16)

    tile = pl.BlockSpec((bt, D_MODEL), lambda i: (i, 0))
    return pl.pallas_call(
        body, name=name, grid=(t // bt,),
        in_specs=[tile, tile, pl.BlockSpec((bt, D_MODEL), lambda i: (i, 1))], out_specs=[tile, tile],
        out_shape=[jax.ShapeDtypeStruct((t, D_MODEL), F32), jax.ShapeDtypeStruct((t, D_MODEL), BF16)],
        compiler_params=_cparams("parallel"),
    )(dy, h, proj)


def _shift_up_one(a, name):
    t, c = a.shape
    bt = _pick(t, ROW_TILE)

    def body(a_ref, halo_ref, o_ref):
        o_ref[...] = _shift_up(a_ref[...], halo_ref[...], 1)

    tile = pl.BlockSpec((bt, c), lambda i, j: (i, j))
    return pl.pallas_call(
        body, name=name, grid=(t // bt, 1), in_specs=[tile, _next_halo_spec(bt, c, t)], out_specs=tile,
        out_shape=jax.ShapeDtypeStruct((t, c), F32), compiler_params=_cparams("parallel", "parallel"),
    )(a, a)


ADAM_ROWS = 64


def _adamw(recv, w, m, v, name):
    _, r, c = recv.shape
    br = ADAM_ROWS
    assert r % br == 0

    def body(r_ref, w_ref, m_ref, v_ref, g_ref, d_ref, nm_ref, nv_ref):
        g = r_ref[0].astype(F32)
        for s in range(1, N_DEV):
            g = g + r_ref[s].astype(F32)
        m_new = ADAM_B1 * m_ref[...] + (1.0 - ADAM_B1) * g
        v_new = ADAM_B2 * v_ref[...] + (1.0 - ADAM_B2) * (g * g)
        m_hat = m_new / (1.0 - ADAM_B1 ** ADAM_STEP)
        v_hat = v_new / (1.0 - ADAM_B2 ** ADAM_STEP)
        g_ref[...] = g
        d_ref[...] = -ADAM_LR * (m_hat / (jnp.sqrt(v_hat) + ADAM_EPS) + ADAM_WD * w_ref[...])
        nm_ref[...] = m_new
        nv_ref[...] = v_new

    tile = pl.BlockSpec((br, c), lambda i: (i, 0))
    shape = jax.ShapeDtypeStruct((r, c), F32)
    return pl.pallas_call(
        body, name=name, grid=(r // br,),
        in_specs=[pl.BlockSpec((N_DEV, br, c), lambda i: (0, i, 0)), tile, tile, tile],
        out_specs=[tile] * 4, out_shape=[shape] * 4, compiler_params=_cparams("parallel"),
    )(recv, w, m, v)


def _all_to_all(src, name):
    def body(src_ref, out_ref, send_sems, recv_sems, local_sem):
        pos = [lax.axis_index(ax) for ax in MESH_AXES]
        me = 4 * pos[0] + 2 * pos[1] + pos[2]

        def peer_of(k):
            p = [(1 - pos[b]) if (k >> (2 - b)) & 1 else pos[b] for b in range(3)]
            return tuple(p), 4 * p[0] + 2 * p[1] + p[2]

        def copy(k):
            peer, peer_idx = peer_of(k)
            return pltpu.make_async_remote_copy(
                src_ref=src_ref.at[peer_idx], dst_ref=out_ref.at[me],
                send_sem=send_sems.at[k - 1], recv_sem=recv_sems.at[k - 1],
                device_id=peer, device_id_type=pl.DeviceIdType.MESH)

        def arrival(k):
            peer, peer_idx = peer_of(k)
            return pltpu.make_async_remote_copy(
                src_ref=src_ref.at[me], dst_ref=out_ref.at[peer_idx],
                send_sem=send_sems.at[k - 1], recv_sem=recv_sems.at[k - 1],
                device_id=peer, device_id_type=pl.DeviceIdType.MESH)

        mine = pltpu.make_async_copy(src_ref.at[me], out_ref.at[me], local_sem)
        mine.start()
        sends = [copy(k) for k in range(1, N_DEV)]
        for cp in sends:
            cp.start()
        for k in range(1, N_DEV):
            arrival(k).wait_recv()
        for cp in sends:
            cp.wait_send()
        mine.wait()

    hbm = pl.BlockSpec(memory_space=pltpu.HBM)
    return pl.pallas_call(
        body, name=name, in_specs=[hbm], out_specs=hbm,
        out_shape=jax.ShapeDtypeStruct(src.shape, src.dtype),
        scratch_shapes=[pltpu.SemaphoreType.DMA((N_DEV - 1,)), pltpu.SemaphoreType.DMA((N_DEV - 1,)),
                        pltpu.SemaphoreType.DMA],
        compiler_params=pltpu.CompilerParams(has_side_effects=True),
    )(src)


def _all_gather(src, name):
    def body(src_ref, out_ref, send_sems, recv_sems, local_sem):
        x, y, c = (lax.axis_index(ax) for ax in MESH_AXES)
        me, sibling = (x, y, c), (x, y, 1 - c)
        chips = [(1 - x, y), (x, 1 - y), (1 - x, 1 - y)]

        def slot(px, py, pc):
            return out_ref.at[4 * px + 2 * py + pc]

        def copy(k, block, to, from_src=False):
            return pltpu.make_async_remote_copy(
                src_ref=src_ref if from_src else slot(*block), dst_ref=slot(*block),
                send_sem=send_sems.at[k], recv_sem=recv_sems.at[k],
                device_id=to, device_id_type=pl.DeviceIdType.MESH)

        mine = pltpu.make_async_copy(src_ref, slot(*me), local_sem)
        mine.start()
        first = [copy(0, me, sibling, True)] + [copy(1 + j, me, (*chip, c), True) for j, chip in enumerate(chips)]
        for cp in first:
            cp.start()
        passed = [copy(4 + j, (*chip, c), sibling) for j, chip in enumerate(chips)]
        for j, chip in enumerate(chips):
            copy(1 + j, (*chip, c), me).wait_recv()
            passed[j].start()
        copy(0, sibling, me).wait_recv()
        for j, chip in enumerate(chips):
            copy(4 + j, (*chip, 1 - c), me).wait_recv()
        for cp in first + passed:
            cp.wait_send()
        mine.wait()

    hbm = pl.BlockSpec(memory_space=pltpu.HBM)
    return pl.pallas_call(
        body, name=name, in_specs=[hbm], out_specs=hbm,
        out_shape=jax.ShapeDtypeStruct((N_DEV,) + tuple(src.shape), src.dtype),
        scratch_shapes=[pltpu.SemaphoreType.DMA((N_DEV - 1,)), pltpu.SemaphoreType.DMA((N_DEV - 1,)),
                        pltpu.SemaphoreType.DMA],
        compiler_params=pltpu.CompilerParams(has_side_effects=True),
    )(src)


WEIGHTS = ['a_w_in', 'a_b_f', 'a_w_out', 'b_w_in', 'b_conv_w', 'b_conv_b', 'b_w_a', 'b_b_a', 'b_w_i', 'b_b_i',
           'b_lam', 'b_w_out', 'f_w_up', 'f_conv_w', 'f_conv_b', 'f_w_down', 'ln1_g', 'ln1_b', 'ln2_g', 'ln2_b',
           'ple_w', 'ple_gate_w', 'ple_gate_b']
SHARD_AXIS = {'a_w_in': 2, 'a_b_f': None, 'a_w_out': 1, 'b_w_in': 2, 'b_conv_w': 2, 'b_conv_b': 1, 'b_w_a': None,
              'b_b_a': None, 'b_w_i': None, 'b_b_i': None, 'b_lam': 1, 'b_w_out': 1, 'f_w_up': 2, 'f_conv_w': 2,
              'f_conv_b': None, 'f_w_down': 1, 'ln1_g': None, 'ln1_b': None, 'ln2_g': None, 'ln2_b': None,
              'ple_w': 2, 'ple_gate_w': 1, 'ple_gate_b': None}
MATMUL_WEIGHTS = ['a_w_in', 'a_w_out', 'b_w_in', 'b_w_out', 'f_w_up', 'f_w_down', 'ple_w', 'ple_gate_w']
SMALL_SHARDED = ['b_conv_w', 'b_conv_b', 'b_lam', 'f_conv_w']
GRADS_AS_BF16 = MATMUL_WEIGHTS + ['b_w_a', 'b_w_i']
PACK_COLS = 1024


def _to_shards(full, axis):
    return jnp.stack(jnp.split(full, N_DEV, axis=axis))


def _from_shards(pieces, axis):
    return jnp.concatenate([pieces[d] for d in range(N_DEV)], axis=axis)


PIECE_ROWS = 16


def _piece_rows(size):
    rows = -(-size // PACK_COLS)
    return -(-rows // PIECE_ROWS) * PIECE_ROWS


def _pack_pieces(pieces, total_mult=PIECE_ROWS):
    lead = pieces[0].shape[:-1]
    blocks = []
    for pc in pieces:
        n = pc.shape[-1]
        rows = _piece_rows(n)
        pad = [(0, 0)] * len(lead) + [(0, rows * PACK_COLS - n)]
        blocks.append(jnp.pad(pc, pad).reshape(lead + (rows, PACK_COLS)))
    total = sum(b.shape[-2] for b in blocks)
    extra = -total % total_mult
    if extra:
        blocks.append(jnp.zeros(lead + (extra, PACK_COLS), pieces[0].dtype))
    return jnp.concatenate(blocks, axis=len(lead))


def _unpack_pieces(packed, shapes):
    lead = packed.shape[:-2]
    out, row = [], 0
    for shp in shapes:
        size = math.prod(shp)
        rows = _piece_rows(size)
        block = lax.slice_in_dim(packed, row, row + rows, axis=len(lead))
        flat = block.reshape(lead + (rows * PACK_COLS,))
        out.append(lax.slice_in_dim(flat, 0, size, axis=len(lead)).reshape(lead + tuple(shp)))
        row += rows
    return out


def _gather_weights(local, names, dtype, name):
    packed = _pack_pieces([local[n].astype(dtype).reshape(-1) for n in names])
    gathered = _all_gather(packed, name)
    pieces = _unpack_pieces(gathered, [local[n].shape for n in names])
    return {n: _from_shards(pc, SHARD_AXIS[n]) for n, pc in zip(names, pieces)}


def _mixer_a_fwd(tag, xb, w):
    qkv = _mm(xb, w["wqkv"], "nn", BF16, f"{tag}_qkv")
    fg = _mm(xb, w["wf"], "nn", F32, f"{tag}_fgproj")
    fg_rows = fg[:, :N_HEADS].T
    c_rows = _fgate_fwd(fg_rows, w["b_f"], f"{tag}_fgate")
    c_pairs = _rows_to_pairs(c_rows)
    qh, kh, qt, kt, vt = _attn_prep(qkv, _rows_to_cols(c_rows), f"{tag}_attn_prep")
    ot, otb, lse_pairs = _attn_fwd_t(qh, kh, vt, c_pairs, f"{tag}_attn")
    m = _mm(otb, w["wout"], "tn", F32, f"{tag}_oproj")
    return m, dict(qkv=qkv, kh=kh, qt=qt, kt=kt, fg_rows=fg_rows, c_pairs=c_pairs, ot=ot, otb=otb,
                   lse_pairs=lse_pairs)


def _mixer_a_bwd(tag, dz, dzb, xb, w, s):
    t = xb.shape[0]
    dot = _mm(w["wout"], dzb, "nt", F32, f"{tag}_b_do")
    g_wout = _mm(s["otb"], dzb, "nn", F32, f"{tag}_b_dwout")
    d_pairs, dotb = _attn_delta_t(dot, s["ot"], f"{tag}_b_delta")
    dkt, dvt, dck_pairs, dq_aug = _attn_bwd_t(s["kh"], s["kt"], s["qt"], s["qkv"], dotb, s["c_pairs"],
                                              s["lse_pairs"], d_pairs, f"{tag}_b_attn")
    dq_aug = dq_aug.reshape(N_HEADS, LANES, t)
    dfg_rows, db_f = _fgate_bwd(dck_pairs[:, :2, :].reshape(N_HEADS, t), dq_aug[:, AUG_ONE, :], s["fg_rows"],
                                w["b_f"], f"{tag}_b_fgate")
    dqt = dq_aug[:, :HEAD_DIM, :].reshape(D_MODEL, t).astype(BF16)
    dqkv = jnp.concatenate([dqt, dkt, dvt], axis=0).T
    dfg = jnp.pad(dfg_rows.T, ((0, 0), (0, LANES - N_HEADS))).astype(BF16)
    dx = _mm(dqkv, w["wqkv"], "nt", F32, f"{tag}_b_dx_qkv", add=dz, add_scale=ALPHA, tk=3 * D_MODEL)
    dx = _mm(dfg, w["wf"], "nt", F32, f"{tag}_b_dx_fg", add=dx)
    g_wqkv = _mm(xb, dqkv, "tn", F32, f"{tag}_b_dwqkv")
    g_wf = _mm(xb, dfg, "tn", F32, f"{tag}_b_dwf")[:, :N_HEADS]
    grads = dict(a_w_in=jnp.concatenate([g_wqkv, g_wf], axis=1), a_b_f=db_f.reshape(N_HEADS), a_w_out=g_wout)
    return dx, grads


def _mixer_b_fwd(tag, xb, w):
    proj = _mm(xb, w["win"], "nn", F32, f"{tag}_proj")
    xc = _rg_conv_fwd(proj, w["conv_w"], w["conv_b"], f"{tag}_conv")
    a, u = _rg_gate_fwd(xc, w["wa"], w["ba"], w["wi"], w["bi"], w["lam"], f"{tag}_gate")
    h = _scan(a, u, f"{tag}_scan")
    y = _rg_out_fwd(h, proj, f"{tag}_out")
    m = _mm(y, w["wout"], "nn", F32, f"{tag}_oproj")
    return m, dict(proj=proj, xc=xc, a=a, h=h, y=y)


def _mixer_b_bwd(tag, dz, dzb, xb, w, s):
    dy = _mm(dzb, w["wout"], "nt", F32, f"{tag}_b_dy")
    g_wout = _mm(s["y"], dzb, "tn", F32, f"{tag}_b_dwout")
    dh, dgate = _rg_out_bwd(dy, s["h"], s["proj"], f"{tag}_b_out")
    g = _scan(_shift_up_one(s["a"], f"{tag}_b_shift"), dh, f"{tag}_b_scan", reverse=True)
    dxc, g_wa, g_wi, g_ba, g_bi, g_lam = _rg_gate_bwd(
        s["xc"], g, s["h"], w["wa"], w["ba"], w["wi"], w["bi"], w["lam"], f"{tag}_b_gate")
    dxp = _conv_bwd_x(dxc, w["conv_w"], f"{tag}_b_convx", BF16)
    g_cw, g_cb = _conv_bwd_w(s["proj"], dxc, CONV_B, f"{tag}_b_convw")
    dproj = jnp.concatenate([dxp, dgate], axis=1)
    dx = _mm(dproj, w["win"], "nt", F32, f"{tag}_b_dx", add=dz, add_scale=ALPHA, tk=2 * D_MODEL)
    g_win = _mm(xb, dproj, "tn", F32, f"{tag}_b_dwin")
    grads = dict(b_w_in=g_win, b_conv_w=g_cw, b_conv_b=g_cb.reshape(D_MODEL), b_w_a=g_wa,
                 b_b_a=g_ba.reshape(N_BLOCKS_B, BLOCK_B), b_w_i=g_wi, b_b_i=g_bi.reshape(N_BLOCKS_B, BLOCK_B),
                 b_lam=g_lam.reshape(D_MODEL), b_w_out=g_wout)
    return dx, grads


def _layer_fwd(i, x, xb, pb, w):
    tag = f"L{i}"
    mix = _mixer_a_fwd if i % 2 == 0 else _mixer_b_fwd
    m, sm = mix(tag, xb, w)
    x1, x1b, z1 = _ln_fwd(x, m, w["ln1_g"], w["ln1_b"], f"{tag}_ln1")
    h = _mm(x1b, w["wup"], "nn", F32, f"{tag}_ffn_up")
    y = _ffn_mid_fwd(h, w["fconv_w"], w["fconv_b"], f"{tag}_ffn_mid")
    ff = _mm(y, w["wdown"], "nn", F32, f"{tag}_ffn_down", tk=D_FF)
    x2, x2b, z2 = _ln_fwd(x1, ff, w["ln2_g"], w["ln2_b"], f"{tag}_ln2")
    gl = _mm(x2b, w["wg"], "nn", F32, f"{tag}_ple_gate")
    e = _mm(pb, w["wp"], "nn", F32, f"{tag}_ple_emb")
    x3, x3b = _ple_fwd(x2, gl, e, w["bg"], f"{tag}_ple")
    saved = dict(mixer=sm, xb=xb, x1b=x1b, z1=z1, h=h, y=y, x2b=x2b, z2=z2, gl=gl, e=e, pb=pb)
    return x3, x3b, saved


def _layer_bwd(i, dx3, w, s):
    tag = f"L{i}"
    dgl, de, g_bg = _ple_bwd(dx3, s["gl"], s["e"], w["bg"], f"{tag}_b_ple")
    g_wg = _mm(s["x2b"], dgl, "tn", F32, f"{tag}_b_dwg")
    g_wp = _mm(s["pb"], de, "tn", F32, f"{tag}_b_dwp")
    dx2 = _mm(dgl, w["wg"], "nt", F32, f"{tag}_b_dx2", add=dx3)
    dz2, dz2b, g_ln2g, g_ln2b = _ln_bwd(dx2, s["z2"], w["ln2_g"], f"{tag}_b_ln2")
    dy = _mm(dz2b, w["wdown"], "nt", F32, f"{tag}_b_dy")
    g_wdown = _mm(s["y"], dz2b, "tn", F32, f"{tag}_b_dwdown", tm=1408)
    dval, dgate, g_fcw, g_fcb = _ffn_mid_bwd_a(s["h"], dy, w["fconv_w"], w["fconv_b"], f"{tag}_b_ffn_mid")
    dhv = _conv_bwd_x(dval, w["fconv_w"][:, :D_FF], f"{tag}_b_convx_v", BF16)
    dhg = _conv_bwd_x(dgate, w["fconv_w"][:, D_FF:], f"{tag}_b_convx_g", BF16)
    dx1 = _mm(dhv, w["wup"][:, :D_FF], "nt", F32, f"{tag}_b_dx1_v", add=dz2, add_scale=ALPHA, tk=D_FF)
    dx1 = _mm(dhg, w["wup"][:, D_FF:], "nt", F32, f"{tag}_b_dx1_g", add=dx1, tk=D_FF)
    g_wup = jnp.concatenate([_mm(dhv, s["x1b"], "tn", F32, f"{tag}_b_dwup_v", tm=1408),
                             _mm(dhg, s["x1b"], "tn", F32, f"{tag}_b_dwup_g", tm=1408)], axis=0).T
    dz1, dz1b, g_ln1g, g_ln1b = _ln_bwd(dx1, s["z1"], w["ln1_g"], f"{tag}_b_ln1")
    mix_bwd = _mixer_a_bwd if i % 2 == 0 else _mixer_b_bwd
    dx, g_mix = mix_bwd(tag, dz1, dz1b, s["xb"], w, s["mixer"])
    grads = dict(f_w_up=g_wup, f_conv_w=g_fcw, f_conv_b=g_fcb.reshape(2 * D_FF), f_w_down=g_wdown,
                 ln1_g=g_ln1g.reshape(D_MODEL), ln1_b=g_ln1b.reshape(D_MODEL), ln2_g=g_ln2g.reshape(D_MODEL),
                 ln2_b=g_ln2b.reshape(D_MODEL), ple_w=g_wp, ple_gate_w=g_wg, ple_gate_b=g_bg.reshape(D_MODEL))
    return dx, g_mix, grads


def _layer_weights(i, full, rep):
    j = i // 2
    w = dict(ln1_g=rep["ln1_g"][i], ln1_b=rep["ln1_b"][i], ln2_g=rep["ln2_g"][i], ln2_b=rep["ln2_b"][i],
             wup=full["f_w_up"][i], fconv_w=full["f_conv_w"][i], fconv_b=rep["f_conv_b"][i],
             wdown=full["f_w_down"][i], wp=full["ple_w"][i], wg=full["ple_gate_w"][i], bg=rep["ple_gate_b"][i])
    if i % 2 == 0:
        w_in = full["a_w_in"][j]
        w.update(wqkv=w_in[:, :3 * D_MODEL],
                 wf=jnp.pad(w_in[:, 3 * D_MODEL:], ((0, 0), (0, LANES - N_HEADS))),
                 b_f=rep["a_b_f"][j], wout=full["a_w_out"][j])
    else:
        w.update(win=full["b_w_in"][j], conv_w=full["b_conv_w"][j], conv_b=full["b_conv_b"][j],
                 wa=rep["b_w_a"][j].astype(BF16), wi=rep["b_w_i"][j].astype(BF16),
                 ba=rep["b_b_a"][j].reshape(N_BLOCKS_B, 1, BLOCK_B), bi=rep["b_b_i"][j].reshape(N_BLOCKS_B, 1, BLOCK_B),
                 lam=full["b_lam"][j].reshape(N_BLOCKS_B, 1, BLOCK_B), wout=full["b_w_out"][j])
    return w


def _fwd_bwd(x, p, target, full, rep):
    weights = [_layer_weights(i, full, rep) for i in range(DEPTH)]
    xb = x.astype(BF16)
    pb = p.astype(BF16)
    saved = []
    for i in range(DEPTH):
        x, xb, s = _layer_fwd(i, x, xb, pb[i], weights[i])
        saved.append(s)
    dx, loss_local = _loss_head(x, target, "loss_head")

    per_layer = {n: [None] * (DEPTH if n.startswith(("f_", "ln", "ple")) else DEPTH // 2) for n in WEIGHTS}
    for i in reversed(range(DEPTH)):
        dx, g_mix, g_layer = _layer_bwd(i, dx, weights[i], saved[i])
        for n, g in g_layer.items():
            per_layer[n][i] = g
        for n, g in g_mix.items():
            per_layer[n][i // 2] = g
    return loss_local, dx, per_layer


def _train_step(x, p, target, local, moments_m, moments_v):
    full = _gather_weights(local, MATMUL_WEIGHTS, BF16, "gather_matmul_weights")
    full.update(_gather_weights(local, SMALL_SHARDED, F32, "gather_small_weights"))
    rep = {n: local[n] for n in WEIGHTS if SHARD_AXIS[n] is None}
    loss_local, dx, grads_layers = _fwd_bwd(x, p, target, full, rep)

    unpacked = [{} for _ in range(4)]
    for group, dtype, tag in ((GRADS_AS_BF16, BF16, "big"), ([n for n in WEIGHTS if n not in GRADS_AS_BF16], F32, "small")):
        pieces = []
        for n in group:
            g = jnp.stack(grads_layers[n]).astype(dtype)
            if SHARD_AXIS[n] is None:
                pieces.append(jnp.broadcast_to(g.reshape(1, -1), (N_DEV, g.size)))
            else:
                pieces.append(_to_shards(g, SHARD_AXIS[n]).reshape(N_DEV, -1))
        recv = _all_to_all(_pack_pieces(pieces, ADAM_ROWS), f"reduce_scatter_grads_{tag}")

        def pack_local(d):
            return _pack_pieces([d[n].astype(F32).reshape(-1) for n in group], ADAM_ROWS)

        outs = _adamw(recv, pack_local(local), pack_local(moments_m), pack_local(moments_v), f"adamw_{tag}")
        shapes = [local[n].shape for n in group]
        for dst, packed in zip(unpacked, outs):
            dst.update(zip(group, _unpack_pieces(packed, shapes)))
    return loss_local, dx, unpacked


def kernel(x, p, a_w_in, a_b_f, a_w_out, b_w_in, b_conv_w, b_conv_b, b_w_a, b_b_a, b_w_i, b_b_i, b_lam, b_w_out, f_w_up, f_conv_w, f_conv_b, f_w_down, ln1_g, ln1_b, ln2_g, ln2_b, ple_w, ple_gate_w, ple_gate_b, loss_target, m_a_w_in, m_a_b_f, m_a_w_out, m_b_w_in, m_b_conv_w, m_b_conv_b, m_b_w_a, m_b_b_a, m_b_w_i, m_b_b_i, m_b_lam, m_b_w_out, m_f_w_up, m_f_conv_w, m_f_conv_b, m_f_w_down, m_ln1_g, m_ln1_b, m_ln2_g, m_ln2_b, m_ple_w, m_ple_gate_w, m_ple_gate_b, v_a_w_in, v_a_b_f, v_a_w_out, v_b_w_in, v_b_conv_w, v_b_conv_b, v_b_w_a, v_b_b_a, v_b_w_i, v_b_b_i, v_b_lam, v_b_w_out, v_f_w_up, v_f_conv_w, v_f_conv_b, v_f_w_down, v_ln1_g, v_ln1_b, v_ln2_g, v_ln2_b, v_ple_w, v_ple_gate_w, v_ple_gate_b):
    given = dict(locals())
    local = {n: given[n] for n in WEIGHTS}
    mom_m = {n: given["m_" + n] for n in WEIGHTS}
    mom_v = {n: given["v_" + n] for n in WEIGHTS}
    t = x.shape[1]
    loss_local, dx, (grad, delta, new_m, new_v) = _train_step(
        x.reshape(t, D_MODEL), p.reshape(DEPTH, t, D_PLE), loss_target.reshape(t, D_MODEL), local, mom_m, mom_v)
    loss = lax.psum(loss_local, MESH_AXES)
    return (loss, dx.reshape(1, t, D_MODEL), *[grad[n] for n in WEIGHTS], *[delta[n] for n in WEIGHTS],
            *[new_m[n] for n in WEIGHTS], *[new_v[n] for n in WEIGHTS])
```
